```python
import math
import jax, jax.numpy as jnp
from jax import lax
import numpy as np

D_MODEL = 1024
BATCH = 8
SEQ = 2048
DEPTH = 1
DEC_BATCH = 32
DEC_SEQ = 4
PAST_LEN = 16384
PAGE_SIZE = 128

GLA_HEADS = 4
GLA_DK = D_MODEL // 2
GLA_DV = D_MODEL
GLA_HDK = GLA_DK // GLA_HEADS
GLA_HDV = GLA_DV // GLA_HEADS
GLA_GATE_RANK = 16
GLA_TAU = 16.0
GLA_CHUNK = 64
DIL_WINDOWS = (128, 512, 2048)
DIL_RATES = (1, 4, 16)
DIL_GROUPS = 3
DIL_HEADS = 4
DIL_HD = 64
DIL_WIDTH = DIL_GROUPS * DIL_HEADS * DIL_HD
DIL_OUT = DIL_HEADS * DIL_HD
Q_BLOCK = 128
ALIBI_MAX = 8.0
D_FF = -(-8 * D_MODEL // (3 * 256)) * 256
IN_SPLITS = (GLA_DK, GLA_DK, GLA_DV, GLA_DV, GLA_GATE_RANK, DIL_WIDTH, DIL_WIDTH, DIL_WIDTH, D_MODEL, D_MODEL)
N_IN = 2 * GLA_DK + 2 * GLA_DV + GLA_GATE_RANK + 3 * DIL_WIDTH + 2 * D_MODEL
RMS_EPS = 1e-6

kernel_name = 'gla_dilated_hybrid_step'


def rmsnorm(x, g):
    x32 = x.astype(jnp.float32)
    y = x32 * lax.rsqrt(jnp.mean(x32 * x32, axis=-1, keepdims=True) + RMS_EPS)
    return (y * g.astype(jnp.float32)).astype(x.dtype)


def split_cols(z):
    offs = np.cumsum(IN_SPLITS)[:-1].tolist()
    return jnp.split(z, offs, axis=-1)


def gla_chunked(q, k, v, log_a, s0):
    B, T, H, dk = q.shape
    dv = v.shape[-1]
    c = GLA_CHUNK if T % GLA_CHUNK == 0 else T
    n = T // c

    def to_chunks(a):
        return a.reshape(B, n, c, H, a.shape[-1]).swapaxes(0, 1)

    tril = jnp.tril(jnp.ones((c, c), dtype=bool))

    def step(S, inp):
        qc, kc, vc, ac = inp
        b = jnp.cumsum(ac, axis=1)
        o_inter = jnp.einsum('bthk,bhkv->bthv', qc * jnp.exp(b), S)
        diff = b[:, :, None] - b[:, None, :]
        decay = jnp.exp(jnp.where(tril[None, :, :, None, None], diff, -jnp.inf))
        scores = jnp.einsum('bthk,btshk,bshk->bhts', qc, decay, kc)
        o_intra = jnp.einsum('bhts,bshv->bthv', scores, vc)
        b_last = b[:, -1]
        S_new = jnp.exp(b_last)[..., None] * S + jnp.einsum(
            'bshk,bshv->bhkv', kc * jnp.exp(b_last[:, None] - b), vc)
        return S_new, o_inter + o_intra

    S_fin, o = lax.scan(step, s0, (to_chunks(q), to_chunks(k), to_chunks(v), to_chunks(log_a)))
    return o.swapaxes(0, 1).reshape(B, T, H, dv), S_fin


def alibi_slopes():
    n = DIL_GROUPS * DIL_HEADS
    s = jnp.exp2(-ALIBI_MAX * (jnp.arange(n, dtype=jnp.float32) + 1.0) / n)
    return s.reshape(DIL_GROUPS, DIL_HEADS)


def dilated_group(qb, qidx, k_ext, v_ext, rate, n_keys, slopes):
    steps = jnp.arange(n_keys, dtype=jnp.int32)
    idx = qidx[:, None] - rate * steps[None, :]
    valid = idx >= 0
    idx = jnp.maximum(idx, 0)
    kg = jnp.take(k_ext, idx, axis=1).astype(jnp.float32)
    vg = jnp.take(v_ext, idx, axis=1).astype(jnp.float32)
    dist = (rate * steps).astype(jnp.float32)
    s = jnp.einsum('bqhd,bqjhd->bhqj', qb, kg) - slopes[:, None, None] * dist
    s = jnp.where(valid[None, None], s, -jnp.inf)
    m = jnp.max(s, axis=-1, keepdims=True)
    p = jnp.exp(s - m)
    l = jnp.sum(p, axis=-1, keepdims=True)
    o = jnp.einsum('bhqj,bqjhd->bqhd', p / l, vg)
    log_den = jnp.swapaxes((m + jnp.log(l))[..., 0], 1, 2)
    return o, log_den


def dilated_block(qb, qidx_list, k_exts, v_exts, slopes):
    outs, dens = [], []
    for g in range(DIL_GROUPS):
        o, ld = dilated_group(qb[:, :, g], qidx_list[g], k_exts[g], v_exts[g],
                              DIL_RATES[g], DIL_WINDOWS[g] // DIL_RATES[g] + 1, slopes[g])
        outs.append(o)
        dens.append(ld)
    w = jax.nn.softmax(jnp.stack(dens), axis=0)
    return jnp.sum(w[..., None] * jnp.stack(outs), axis=0)


def dilated_mixer(q, k, v, k_bufs, v_bufs):
    B, T = q.shape[0], q.shape[1]
    q = q.astype(jnp.float32) * (DIL_HD ** -0.5)
    k_exts = [jnp.concatenate([k_bufs[g], k[:, :, g]], axis=1) for g in range(DIL_GROUPS)]
    v_exts = [jnp.concatenate([v_bufs[g], v[:, :, g]], axis=1) for g in range(DIL_GROUPS)]
    lens = [k_bufs[g].shape[1] for g in range(DIL_GROUPS)]
    slopes = alibi_slopes()

    def block(qb, start):
        tb = qb.shape[1]
        qidx = [lens[g] + start + jnp.arange(tb, dtype=jnp.int32) for g in range(DIL_GROUPS)]
        return dilated_block(qb, qidx, k_exts, v_exts, slopes)

    if T > Q_BLOCK and T % Q_BLOCK == 0:
        nb = T // Q_BLOCK
        qbs = q.reshape(B, nb, Q_BLOCK, DIL_GROUPS, DIL_HEADS, DIL_HD).swapaxes(0, 1)
        starts = jnp.arange(nb, dtype=jnp.int32) * Q_BLOCK
        o = lax.map(lambda a: block(a[0], a[1]), (qbs, starts))
        o = o.swapaxes(0, 1).reshape(B, T, DIL_HEADS, DIL_HD)
    else:
        o = block(q, jnp.int32(0))
    new_k = tuple(k_exts[g][:, -min(DIL_WINDOWS[g], k_exts[g].shape[1]):] for g in range(DIL_GROUPS))
    new_v = tuple(v_exts[g][:, -min(DIL_WINDOWS[g], v_exts[g].shape[1]):] for g in range(DIL_GROUPS))
    return o, new_k, new_v


def layer_forward(x, s_gla, k_bufs, v_bufs, norm1_g, w_in, gla_gate_w2, gla_gate_b, gla_norm_g,
                  proj_a, proj_b, w_out, norm2_g, w_ffn_gate, w_ffn_up, w_ffn_down):
    f32 = jnp.float32
    B, T, _ = x.shape
    h = rmsnorm(x, norm1_g)
    gq, gk, gv, gr, glr, dq, dk, dv, ga, gb = split_cols(h @ w_in)
    q_a = gq.reshape(B, T, GLA_HEADS, GLA_HDK).astype(f32) * (GLA_HDK ** -0.5)
    k_a = gk.reshape(B, T, GLA_HEADS, GLA_HDK).astype(f32)
    v_a = gv.reshape(B, T, GLA_HEADS, GLA_HDV).astype(f32)
    log_a = jax.nn.log_sigmoid((glr @ gla_gate_w2 + gla_gate_b).astype(f32)) / GLA_TAU
    log_a = log_a.reshape(B, T, GLA_HEADS, GLA_HDK)
    o_a, s_new = gla_chunked(q_a, k_a, v_a, log_a, s_gla.astype(f32))
    o_a = o_a * lax.rsqrt(jnp.mean(o_a * o_a, axis=-1, keepdims=True) + RMS_EPS) * gla_norm_g.astype(f32)
    o_a = (o_a.reshape(B, T, GLA_DV) * jax.nn.silu(gr.astype(f32))).astype(x.dtype)
    shp = (B, T, DIL_GROUPS, DIL_HEADS, DIL_HD)
    o_b, new_k, new_v = dilated_mixer(dq.reshape(shp), dk.reshape(shp), dv.reshape(shp), k_bufs, v_bufs)
    o_b = o_b.reshape(B, T, DIL_OUT).astype(x.dtype)
    merged = jax.nn.sigmoid(ga) * (o_a @ proj_a) + jax.nn.sigmoid(gb) * (o_b @ proj_b)
    x = x + merged @ w_out
    h2 = rmsnorm(x, norm2_g)
    x = x + (jax.nn.silu(h2 @ w_ffn_gate) * (h2 @ w_ffn_up)) @ w_ffn_down
    return x, s_new.astype(x.dtype), new_k, new_v


def run_trunk(x, s_gla, k_bufs, v_bufs, norm1_g, w_in, gla_gate_w2, gla_gate_b, gla_norm_g,
              proj_a, proj_b, w_out, norm2_g, w_ffn_gate, w_ffn_up, w_ffn_down, norm_f_g):
    gla_states, k_states, v_states = [], [], []
    for l in range(DEPTH):
        x, s, nk, nv = layer_forward(
            x, s_gla[l], tuple(b[l] for b in k_bufs), tuple(b[l] for b in v_bufs),
            norm1_g[l], w_in[l], gla_gate_w2[l], gla_gate_b[l], gla_norm_g[l],
            proj_a[l], proj_b[l], w_out[l], norm2_g[l], w_ffn_gate[l], w_ffn_up[l], w_ffn_down[l])
        gla_states.append(s)
        k_states.append(nk)
        v_states.append(nv)
    y = rmsnorm(x, norm_f_g)
    new_k = tuple(jnp.stack([ks[g] for ks in k_states]) for g in range(DIL_GROUPS))
    new_v = tuple(jnp.stack([vs[g] for vs in v_states]) for g in range(DIL_GROUPS))
    return y, jnp.stack(gla_states), new_k, new_v


def setup_inputs(seed: int = 0) -> dict:
    key = jax.random.key(seed)
    ks = jax.random.split(key, 24)

    def nrm(k, shape, scale):
        return jax.random.normal(k, shape, jnp.float32) * scale

    lens = [min(w, PAST_LEN) for w in DIL_WINDOWS]
    win_shape = lambda g: (DEPTH, DEC_BATCH, lens[g], DIL_HEADS, DIL_HD)
    return {
        'x_prompt': nrm(ks[0], (BATCH, SEQ, D_MODEL), 1.0),
        'x_sample': nrm(ks[1], (DEC_BATCH, DEC_SEQ, D_MODEL), 1.0),
        'state_gla': nrm(ks[2], (DEPTH, DEC_BATCH, GLA_HEADS, GLA_HDK, GLA_HDV), 0.5),
        'state_win0_k': nrm(ks[3], win_shape(0), 1.0),
        'state_win0_v': nrm(ks[4], win_shape(0), 1.0),
        'state_win1_k': nrm(ks[5], win_shape(1), 1.0),
        'state_win1_v': nrm(ks[6], win_shape(1), 1.0),
        'state_win2_k': nrm(ks[7], win_shape(2), 1.0),
        'state_win2_v': nrm(ks[8], win_shape(2), 1.0),
        'norm1_g': 1.0 + nrm(ks[9], (DEPTH, D_MODEL), 0.02),
        'w_in': nrm(ks[10], (DEPTH, D_MODEL, N_IN), D_MODEL ** -0.5),
        'gla_gate_w2': nrm(ks[11], (DEPTH, GLA_GATE_RANK, GLA_DK), GLA_GATE_RANK ** -0.5),
        'gla_gate_b': nrm(ks[12], (DEPTH, GLA_DK), 0.1),
        'gla_norm_g': 1.0 + nrm(ks[13], (DEPTH, GLA_HDV), 0.02),
        'proj_a': nrm(ks[14], (DEPTH, GLA_DV, D_MODEL), GLA_DV ** -0.5),
        'proj_b': nrm(ks[15], (DEPTH, DIL_OUT, D_MODEL), DIL_OUT ** -0.5),
        'w_out': nrm(ks[16], (DEPTH, D_MODEL, D_MODEL), D_MODEL ** -0.5),
        'norm2_g': 1.0 + nrm(ks[17], (DEPTH, D_MODEL), 0.02),
        'w_ffn_gate': nrm(ks[18], (DEPTH, D_MODEL, D_FF), D_MODEL ** -0.5),
        'w_ffn_up': nrm(ks[19], (DEPTH, D_MODEL, D_FF), D_MODEL ** -0.5),
        'w_ffn_down': nrm(ks[20], (DEPTH, D_FF, D_MODEL), D_FF ** -0.5),
        'norm_f_g': 1.0 + nrm(ks[21], (D_MODEL,), 0.02),
    }


def reference(x_prompt, x_sample, state_gla, state_win0_k, state_win0_v, state_win1_k, state_win1_v,
              state_win2_k, state_win2_v, norm1_g, w_in, gla_gate_w2, gla_gate_b, gla_norm_g,
              proj_a, proj_b, w_out, norm2_g, w_ffn_gate, w_ffn_up, w_ffn_down, norm_f_g):
    bp = x_prompt.shape[0]
    s0_prompt = jnp.zeros((DEPTH, bp, GLA_HEADS, GLA_HDK, GLA_HDV), x_prompt.dtype)
    empty = jnp.zeros((DEPTH, bp, 0, DIL_HEADS, DIL_HD), x_prompt.dtype)
    y_prompt, gla_p, (k0p, k1p, k2p), (v0p, v1p, v2p) = run_trunk(
        x_prompt, s0_prompt, (empty, empty, empty), (empty, empty, empty),
        norm1_g, w_in, gla_gate_w2, gla_gate_b, gla_norm_g, proj_a, proj_b, w_out,
        norm2_g, w_ffn_gate, w_ffn_up, w_ffn_down, norm_f_g)
    y_sample, gla_s, (k0s, k1s, k2s), (v0s, v1s, v2s) = run_trunk(
        x_sample, state_gla, (state_win0_k, state_win1_k, state_win2_k),
        (state_win0_v, state_win1_v, state_win2_v),
        norm1_g, w_in, gla_gate_w2, gla_gate_b, gla_norm_g, proj_a, proj_b, w_out,
        norm2_g, w_ffn_gate, w_ffn_up, w_ffn_down, norm_f_g)
    return (y_prompt, y_sample, gla_p, k0p, v0p, k1p, v1p, k2p, v2p,
            gla_s, k0s, v0s, k1s, v1s, k2s, v2s)
```

```python
import functools

import jax
import jax.numpy as jnp
from jax import lax
from jax.experimental import pallas as pl
from jax.experimental.pallas import tpu as pltpu

F32 = jnp.float32
BF16 = jnp.bfloat16

D_MODEL = 1024
GLA_HEADS = 4
GLA_DK = 512
GLA_DV = 1024
GLA_HDK = 128
GLA_HDV = 256
GLA_GATE_RANK = 16
GLA_TAU = 16.0
DIL_WINDOWS = (128, 512, 2048)
DIL_RATES = (1, 4, 16)
DIL_GROUPS = 3
DIL_HEADS = 4
DIL_HD = 64
DIL_WIDTH = 768
DIL_OUT = 256
ALIBI_MAX = 8.0
D_FF = 2816
RMS_EPS = 1e-6

LANES = 128
Q_TILE = 128
GLA_CHUNK = 128
SAMPLE_PAD = 16
MASK_VALUE = -1e30
VMEM_LIMIT_BYTES = 48 * 1024 * 1024
_LN2 = 0.6931471805599453

_OFF = {}
_o = 0
for _name, _w in (("gq", GLA_DK), ("gk", GLA_DK), ("gv", GLA_DV), ("gr", GLA_DV), ("glr", GLA_GATE_RANK),
                  ("dq", DIL_WIDTH), ("dk", DIL_WIDTH), ("dv", DIL_WIDTH), ("ga", D_MODEL), ("gb", D_MODEL)):
    _OFF[_name] = (_o, _o + _w)
    _o += _w
GLA_COLS = 2 * GLA_DK + 2 * GLA_DV + LANES


def _params(*sem):
    return pltpu.CompilerParams(dimension_semantics=sem, vmem_limit_bytes=VMEM_LIMIT_BYTES)


def _dot(a, b):
    return jnp.dot(a, b, preferred_element_type=F32)


def _dot_nt(a, b):
    return lax.dot_general(a, b, (((1,), (1,)), ((), ())), preferred_element_type=F32)


def _dot_tn(a, b):
    return lax.dot_general(a, b, (((0,), (0,)), ((), ())), preferred_element_type=F32)


def _rms(x, g):
    return x * lax.rsqrt(jnp.mean(x * x, axis=-1, keepdims=True) + RMS_EPS) * g


def _norm_matmul_kernel(x_ref, g_ref, w_ref, o_ref, h_ref):
    @pl.when(pl.program_id(1) == 0)
    def _():
        h_ref[...] = _rms(x_ref[...], g_ref[...]).astype(BF16)

    o_ref[...] = _dot(h_ref[...], w_ref[...]).astype(o_ref.dtype)


def _norm_matmul(x, g, w, out_dtype, tm, tn):
    m, k = x.shape
    n = w.shape[1]
    return pl.pallas_call(
        _norm_matmul_kernel,
        grid=(m // tm, n // tn),
        in_specs=[pl.BlockSpec((tm, k), lambda i, j: (i, 0)),
                  pl.BlockSpec((1, k), lambda i, j: (0, 0)),
                  pl.BlockSpec((k, tn), lambda i, j: (0, j))],
        out_specs=pl.BlockSpec((tm, tn), lambda i, j: (i, j)),
        out_shape=jax.ShapeDtypeStruct((m, n), out_dtype),
        scratch_shapes=[pltpu.VMEM((tm, k), BF16)],
        compiler_params=_params("parallel", "arbitrary"),
        name="norm_matmul",
    )(x, g, w)


def _split3(x):
    x1 = x.astype(BF16)
    r1 = x - x1.astype(F32)
    x2 = r1.astype(BF16)
    x3 = (r1 - x2.astype(F32)).astype(BF16)
    return x1, x2, x3


def _gla_kernel(q_ref, k_ref, v_ref, r_ref, lr_ref, w2_ref, gb_ref, ng_ref, s0_ref,
                o_ref, sout_ref, s_ref, *, chunk, n_valid):
    c = pl.program_id(1)

    @pl.when(c == 0)
    def _():
        s_ref[...] = s0_ref[0]

    gate = _dot(lr_ref[0], w2_ref[...]) + gb_ref[...]
    log_a = (jnp.minimum(gate, 0.0) - jnp.log(1.0 + jnp.exp(-jnp.abs(gate)))) * (1.0 / GLA_TAU)
    row = lax.broadcasted_iota(jnp.int32, (chunk, chunk), 0)
    col = lax.broadcasted_iota(jnp.int32, (chunk, chunk), 1)
    causal = row >= col
    if n_valid < chunk:
        tok = lax.broadcasted_iota(jnp.int32, log_a.shape, 0)
        log_a = jnp.where(tok < n_valid, log_a, 0.0)
    parts = _split3(log_a)
    tril = jnp.where(causal, 1.0, 0.0).astype(BF16)
    ones = jnp.ones((chunk, LANES), BF16)
    b = sum(_dot(tril, p) for p in parts)
    b_tot = sum(_dot_tn(p, ones) for p in parts)

    for h in range(GLA_HEADS):
        ks = slice(h * GLA_HDK, (h + 1) * GLA_HDK)
        vs = slice(h * GLA_HDV, (h + 1) * GLA_HDV)
        bh = b[:, ks]
        qh = q_ref[0, :, ks].astype(F32) * (GLA_HDK ** -0.5)
        kh = k_ref[0, :, ks].astype(F32)
        vh = v_ref[0, :, vs]
        if n_valid < chunk:
            tokv = lax.broadcasted_iota(jnp.int32, vh.shape, 0)
            vh = jnp.where(tokv < n_valid, vh, jnp.zeros_like(vh))
        qt = (qh * jnp.exp(bh)).astype(BF16)
        kt = (kh * jnp.exp(-bh)).astype(BF16)
        kd = (kh * jnp.exp(bh[chunk - 1:chunk, :] - bh)).astype(BF16)
        s_old = s_ref[h]
        scores = jnp.where(causal, _dot_nt(qt, kt), 0.0).astype(BF16)
        o = _dot(qt, s_old.astype(BF16)) + _dot(scores, vh)
        dec = jnp.exp(b_tot[ks, :])
        s_ref[h] = s_old * jnp.concatenate([dec, dec], axis=1) + _dot_tn(kd, vh)
        on = _rms(o, ng_ref[...])
        gr = r_ref[0, :, vs].astype(F32)
        o_ref[0, :, vs] = (on * (gr * jax.nn.sigmoid(gr))).astype(o_ref.dtype)

    @pl.when(c == pl.num_programs(1) - 1)
    def _():
        sout_ref[0] = s_ref[...]


def _gla(z, w2, gate_b, norm_g, s0, chunk, n_valid):
    bsz, t, _ = z.shape
    kern = functools.partial(_gla_kernel, chunk=chunk, n_valid=n_valid)
    return pl.pallas_call(
        kern,
        grid=(bsz, t // chunk),
        in_specs=[pl.BlockSpec((1, chunk, GLA_DK), lambda b, c: (b, c, 0)),
                  pl.BlockSpec((1, chunk, GLA_DK), lambda b, c: (b, c, 1)),
                  pl.BlockSpec((1, chunk, GLA_DV), lambda b, c: (b, c, 1)),
                  pl.BlockSpec((1, chunk, GLA_DV), lambda b, c: (b, c, 2)),
                  pl.BlockSpec((1, chunk, LANES), lambda b, c: (b, c, (2 * GLA_DK + 2 * GLA_DV) // LANES)),
                  pl.BlockSpec((LANES, GLA_DK), lambda b, c: (0, 0)),
                  pl.BlockSpec((1, GLA_DK), lambda b, c: (0, 0)),
                  pl.BlockSpec((1, GLA_HDV), lambda b, c: (0, 0)),
                  pl.BlockSpec((1, GLA_HEADS, GLA_HDK, GLA_HDV), lambda b, c: (b, 0, 0, 0))],
        out_specs=[pl.BlockSpec((1, chunk, GLA_DV), lambda b, c: (b, c, 0)),
                   pl.BlockSpec((1, GLA_HEADS, GLA_HDK, GLA_HDV), lambda b, c: (b, 0, 0, 0))],
        out_shape=[jax.ShapeDtypeStruct((bsz, t, GLA_DV), BF16),
                   jax.ShapeDtypeStruct((bsz, GLA_HEADS, GLA_HDK, GLA_HDV), F32)],
        scratch_shapes=[pltpu.VMEM((GLA_HEADS, GLA_HDK, GLA_HDV), F32)],
        compiler_params=_params("parallel", "arbitrary"),
        name="gla",
    )(z, z, z, z, z, w2, gate_b, norm_g, s0)


def _alibi_slope(g, head):
    n = DIL_GROUPS * DIL_HEADS
    return jnp.exp((-ALIBI_MAX * _LN2 / n) * (head + (g * DIL_HEADS + 1.0)))


def _pair_slopes(g, hp, rows):
    head = 2.0 * hp.astype(F32) + jnp.where(rows >= Q_TILE, 1.0, 0.0)
    return _alibi_slope(g, head)


def _dil_prompt_kernel(q0_ref, q1_ref, q2_ref, k0_ref, k1_ref, k2_ref, v0_ref, v1_ref, v2_ref,
                       o_ref, qf_ref, og_ref, ld_ref, bf_ref, br_ref, *, seq):
    hp = pl.program_id(1)
    q_refs = (q0_ref, q1_ref, q2_ref)
    k_refs = (k0_ref, k1_ref, k2_ref)
    v_refs = (v0_ref, v1_ref, v2_ref)

    for g in range(DIL_GROUPS):
        qf_ref[g] = q_refs[g][...].astype(F32) * (DIL_HD ** -0.5)

    for g in range(DIL_GROUPS):
        rate = float(DIL_RATES[g])
        rows = lax.broadcasted_iota(jnp.int32, (2 * Q_TILE, Q_TILE), 0)
        cols = lax.broadcasted_iota(jnp.int32, (2 * Q_TILE, Q_TILE), 1)
        dist = (rows & (Q_TILE - 1)) - cols
        slope = _pair_slopes(g, hp, rows)
        bf_ref[g] = jnp.where(dist >= 0, -slope * rate * dist.astype(F32), MASK_VALUE)
        if g < 2:
            rows = lax.broadcasted_iota(jnp.int32, (2 * Q_TILE, 2 * Q_TILE), 0)
            cols = lax.broadcasted_iota(jnp.int32, (2 * Q_TILE, 2 * Q_TILE), 1)
            dist = (rows & (Q_TILE - 1)) + Q_TILE - cols
            slope = _pair_slopes(g, hp, rows)
            ok = jnp.abs(dist - Q_TILE // 2) <= Q_TILE // 2
            br_ref[g] = jnp.where(ok, -slope * rate * dist.astype(F32), MASK_VALUE)

    lane = lax.broadcasted_iota(jnp.int32, (Q_TILE, LANES), 1)
    lo = lane < DIL_HD

    def rows_of(start, size, rate):
        return pl.ds(start, size) if rate == 1 else pl.ds(start, size, stride=rate)

    def tile(g, q_row, k_row, n_keys, bias):
        rate = DIL_RATES[g]
        q2 = qf_ref[g, rows_of(q_row, Q_TILE, rate), :]
        k2 = k_refs[g][rows_of(k_row, n_keys, rate), :].astype(BF16)
        v2 = v_refs[g][rows_of(k_row, n_keys, rate), :].astype(BF16)
        qs = jnp.concatenate([jnp.where(lo, q2, 0.0), jnp.where(lo, 0.0, q2)], axis=0).astype(BF16)
        s = _dot_nt(qs, k2) + bias
        m = jnp.max(s, axis=-1, keepdims=True)
        p = jnp.exp(s - m)
        l = jnp.sum(p, axis=-1, keepdims=True)
        pv = _dot(p.astype(BF16), v2)
        o2 = jnp.where(lo, pv[:Q_TILE], pv[Q_TILE:])
        l2 = jnp.where(lo, l[:Q_TILE], l[Q_TILE:])
        m2 = jnp.where(lo, m[:Q_TILE], m[Q_TILE:])
        og_ref[g, rows_of(q_row, Q_TILE, rate), :] = o2 / l2
        ld_ref[g, rows_of(q_row, Q_TILE, rate), :] = m2 + jnp.log(l2)

    for g in range(DIL_GROUPS):
        rate = DIL_RATES[g]
        n_tiles = seq // rate // Q_TILE
        span = rate * Q_TILE

        def residue(rho, carry, g=g, n_tiles=n_tiles, span=span):
            tile(g, rho, rho, Q_TILE, bf_ref[g])
            if n_tiles > 1:
                def later(n, c2):
                    tile(g, rho + n * span, rho + (n - 1) * span, 2 * Q_TILE, br_ref[g])
                    return c2
                lax.fori_loop(1, n_tiles, later, 0)
            return carry

        if rate == 1:
            residue(0, 0)
        else:
            lax.fori_loop(0, rate, residue, 0)

    def combine(i, carry):
        rs = pl.ds(pl.multiple_of(i * Q_TILE, Q_TILE), Q_TILE)
        ld = [ld_ref[g, rs, :] for g in range(DIL_GROUPS)]
        top = jnp.maximum(jnp.maximum(ld[0], ld[1]), ld[2])
        w = [jnp.exp(x - top) for x in ld]
        num = sum(w[g] * og_ref[g, rs, :] for g in range(DIL_GROUPS))
        o_ref[rs, :] = (num / (w[0] + w[1] + w[2])).astype(o_ref.dtype)
        return carry

    lax.fori_loop(0, seq // Q_TILE, combine, 0)


def _dil_prompt(dq, kv):
    bsz, seq, _ = dq.shape
    pairs = DIL_HEADS * DIL_HD // LANES
    nblk = DIL_WIDTH // LANES

    def spec(col0):
        return pl.BlockSpec((None, seq, LANES), lambda b, hp, col0=col0: (b, 0, col0 + hp))

    in_specs = ([spec(g * pairs) for g in range(DIL_GROUPS)]
                + [spec(g * pairs) for g in range(DIL_GROUPS)]
                + [spec(nblk + g * pairs) for g in range(DIL_GROUPS)])
    return pl.pallas_call(
        functools.partial(_dil_prompt_kernel, seq=seq),
        grid=(bsz, pairs),
        in_specs=in_specs,
        out_specs=pl.BlockSpec((None, seq, LANES), lambda b, hp: (b, 0, hp)),
        out_shape=jax.ShapeDtypeStruct((bsz, seq, DIL_OUT), BF16),
        scratch_shapes=[pltpu.VMEM((DIL_GROUPS, seq, LANES), F32),
                        pltpu.VMEM((DIL_GROUPS, seq, LANES), F32),
                        pltpu.VMEM((DIL_GROUPS, seq, LANES), F32),
                        pltpu.VMEM((DIL_GROUPS, 2 * Q_TILE, Q_TILE), F32),
                        pltpu.VMEM((2, 2 * Q_TILE, 2 * Q_TILE), F32)],
        compiler_params=_params("parallel", "arbitrary"),
        name="dil_prompt",
    )(dq, dq, dq, kv, kv, kv, kv, kv, kv)


def _dil_sample_kernel(q_ref, kvn_ref, kb0_ref, vb0_ref, kb1_ref, vb1_ref, kb2_ref, vb2_ref,
                       o_ref, nk0_ref, nv0_ref, nk1_ref, nv1_ref, nk2_ref, nv2_ref, *, n_new):
    kb_refs = (kb0_ref, kb1_ref, kb2_ref)
    vb_refs = (vb0_ref, vb1_ref, vb2_ref)
    nk_refs = (nk0_ref, nk1_ref, nk2_ref)
    nv_refs = (nv0_ref, nv1_ref, nv2_ref)
    pad = q_ref.shape[1]
    nrow = DIL_HEADS * pad
    width = DIL_HEADS * DIL_HD

    lane_head = jnp.right_shift(lax.broadcasted_iota(jnp.int32, (pad, width), 1), DIL_HD.bit_length() - 1)

    def bias_for(g, dist):
        rate = DIL_RATES[g]
        rows = lax.broadcasted_iota(jnp.int32, dist.shape, 0)
        head = sum(jnp.where(rows >= h * pad, 1.0, 0.0) for h in range(1, DIL_HEADS))
        ok = (dist >= 0) & (dist <= DIL_WINDOWS[g]) & ((dist & (rate - 1)) == 0)
        return jnp.where(ok, -_alibi_slope(g, head) * dist.astype(F32), MASK_VALUE)

    scores, values = [], []
    for g in range(DIL_GROUPS):
        length = kb_refs[g].shape[1]
        cs = slice(g * width, (g + 1) * width)
        vcs = slice(DIL_WIDTH + g * width, DIL_WIDTH + (g + 1) * width)
        qg = q_ref[0, :, cs].astype(F32) * (DIL_HD ** -0.5)
        qs = jnp.concatenate([jnp.where(lane_head == h, qg, 0.0) for h in range(DIL_HEADS)], axis=0).astype(BF16)
        rows = lax.broadcasted_iota(jnp.int32, (nrow, length), 0)
        cols = lax.broadcasted_iota(jnp.int32, (nrow, length), 1)
        dist = length + (rows & (pad - 1)) - cols
        scores.append(_dot_nt(qs, kb_refs[g][0].astype(BF16)) + bias_for(g, dist))
        values.append(vb_refs[g][0].astype(BF16))
        rows = lax.broadcasted_iota(jnp.int32, (nrow, pad), 0)
        cols = lax.broadcasted_iota(jnp.int32, (nrow, pad), 1)
        dist = jnp.where(cols < n_new, (rows & (pad - 1)) - cols, -1)
        scores.append(_dot_nt(qs, kvn_ref[0, :, cs].astype(BF16)) + bias_for(g, dist))
        values.append(kvn_ref[0, :, vcs].astype(BF16))
        nk_refs[g][0, 0:length - n_new, :] = kb_refs[g][0, n_new:length, :]
        nk_refs[g][0, length - n_new:length, :] = kvn_ref[0, 0:n_new, cs]
        nv_refs[g][0, 0:length - n_new, :] = vb_refs[g][0, n_new:length, :]
        nv_refs[g][0, length - n_new:length, :] = kvn_ref[0, 0:n_new, vcs]

    top = functools.reduce(jnp.maximum, [jnp.max(s, axis=-1, keepdims=True) for s in scores])
    probs = [jnp.exp(s - top) for s in scores]
    den = sum(jnp.sum(p, axis=-1, keepdims=True) for p in probs)
    acc = sum(_dot(p.astype(BF16), v) for p, v in zip(probs, values)) / den
    out = sum(jnp.where(lane_head == h, acc[h * pad:(h + 1) * pad], 0.0) for h in range(DIL_HEADS))
    o_ref[0] = out.astype(o_ref.dtype)


def _dil_sample(dq, kvn, bufs, n_new):
    bsz, pad, _ = dq.shape
    full = lambda a: pl.BlockSpec((1,) + a.shape[1:], lambda b: (b, 0, 0))
    out_shape = ([jax.ShapeDtypeStruct((bsz, pad, DIL_OUT), BF16)]
                 + [jax.ShapeDtypeStruct(a.shape, a.dtype) for a in bufs])
    return pl.pallas_call(
        functools.partial(_dil_sample_kernel, n_new=n_new),
        grid=(bsz,),
        in_specs=[full(dq), full(kvn)] + [full(a) for a in bufs],
        out_specs=[pl.BlockSpec((1, pad, DIL_OUT), lambda b: (b, 0, 0))] + [full(a) for a in bufs],
        out_shape=out_shape,
        compiler_params=_params("parallel"),
        name="dil_sample",
    )(dq, kvn, *bufs)


def _merge_out_kernel(x_ref, oa_ref, ob_ref, ga_ref, gb_ref, pa_ref, pb_ref, wo_ref, o_ref):
    pa = _dot(oa_ref[...], pa_ref[...])
    pb = _dot(ob_ref[...], pb_ref[...])
    merged = jax.nn.sigmoid(ga_ref[...].astype(F32)) * pa + jax.nn.sigmoid(gb_ref[...].astype(F32)) * pb
    o_ref[...] = x_ref[...] + _dot(merged.astype(BF16), wo_ref[...])


def _merge_out(x, oa, ob, gates, pa, pb, wo, tm):
    m = x.shape[0]
    row = lambda w: pl.BlockSpec((tm, w), lambda i: (i, 0))
    whole = lambda a: pl.BlockSpec(a.shape, lambda i: (0, 0))
    return pl.pallas_call(
        _merge_out_kernel,
        grid=(m // tm,),
        in_specs=[row(D_MODEL), row(GLA_DV), row(DIL_OUT), row(D_MODEL),
                  pl.BlockSpec((tm, D_MODEL), lambda i: (i, 1)),
                  whole(pa), whole(pb), whole(wo)],
        out_specs=row(D_MODEL),
        out_shape=jax.ShapeDtypeStruct((m, D_MODEL), F32),
        compiler_params=_params("parallel"),
        name="merge_out",
    )(x, oa, ob, gates, gates, pa, pb, wo)


def _ffn_kernel(x_ref, n2_ref, nf_ref, wg_ref, wu_ref, wd_ref, y_ref, h_ref, acc_ref):
    j = pl.program_id(1)

    @pl.when(j == 0)
    def _():
        h_ref[...] = _rms(x_ref[...], n2_ref[...]).astype(BF16)
        acc_ref[...] = jnp.zeros_like(acc_ref)

    h = h_ref[...]
    gate = _dot(h, wg_ref[...])
    up = _dot(h, wu_ref[...])
    act = (gate * jax.nn.sigmoid(gate) * up).astype(BF16)
    acc_ref[...] += _dot(act, wd_ref[...])

    @pl.when(j == pl.num_programs(1) - 1)
    def _():
        y_ref[...] = _rms(x_ref[...] + acc_ref[...], nf_ref[...])


def _ffn(x, n2, nf, wg, wu, wd, tm, tf):
    m = x.shape[0]
    return pl.pallas_call(
        _ffn_kernel,
        grid=(m // tm, D_FF // tf),
        in_specs=[pl.BlockSpec((tm, D_MODEL), lambda i, j: (i, 0)),
                  pl.BlockSpec((1, D_MODEL), lambda i, j: (0, 0)),
                  pl.BlockSpec((1, D_MODEL), lambda i, j: (0, 0)),
                  pl.BlockSpec((D_MODEL, tf), lambda i, j: (0, j)),
                  pl.BlockSpec((D_MODEL, tf), lambda i, j: (0, j)),
                  pl.BlockSpec((tf, D_MODEL), lambda i, j: (j, 0))],
        out_specs=pl.BlockSpec((tm, D_MODEL), lambda i, j: (i, 0)),
        out_shape=jax.ShapeDtypeStruct((m, D_MODEL), F32),
        scratch_shapes=[pltpu.VMEM((tm, D_MODEL), BF16), pltpu.VMEM((tm, D_MODEL), F32)],
        compiler_params=_params("parallel", "arbitrary"),
        name="ffn",
    )(x, n2, nf, wg, wu, wd)


def _prep_weights(w_in, gla_gate_w2, proj_a, proj_b, w_out, w_ffn_gate, w_ffn_up, w_ffn_down):
    w = w_in[0]
    cols = lambda name: w[:, _OFF[name][0]:_OFF[name][1]]
    pad = jnp.zeros((D_MODEL, LANES - GLA_GATE_RANK), F32)
    w_gla = jnp.concatenate([cols("gq"), cols("gk"), cols("gv"), cols("gr"), cols("glr"), pad], axis=1)
    w2 = jnp.concatenate([gla_gate_w2[0], jnp.zeros((LANES - GLA_GATE_RANK, GLA_DK), F32)], axis=0)
    return dict(
        w_gla=w_gla.astype(BF16), w_dq=cols("dq").astype(BF16),
        w_kv=jnp.concatenate([cols("dk"), cols("dv")], axis=1).astype(BF16),
        w_gates=jnp.concatenate([cols("ga"), cols("gb")], axis=1).astype(BF16),
        w2=w2.astype(BF16), pa=proj_a[0].astype(BF16), pb=proj_b[0].astype(BF16), wo=w_out[0].astype(BF16),
        wg=w_ffn_gate[0].astype(BF16), wu=w_ffn_up[0].astype(BF16), wd=w_ffn_down[0].astype(BF16))


def _project(x2d, norm1_g, wts, tm):
    z_gla = _norm_matmul(x2d, norm1_g, wts["w_gla"], BF16, tm, 640)
    z_dq = _norm_matmul(x2d, norm1_g, wts["w_dq"], BF16, tm, DIL_WIDTH)
    z_kv = _norm_matmul(x2d, norm1_g, wts["w_kv"], F32, tm, DIL_WIDTH)
    z_gates = _norm_matmul(x2d, norm1_g, wts["w_gates"], BF16, tm, D_MODEL)
    return z_gla, z_dq, z_kv, z_gates


def _tail(x2d, o_a, o_b, z_gates, wts, norm2_g, norm_f_g, tm):
    x1 = _merge_out(x2d, o_a, o_b, z_gates, wts["pa"], wts["pb"], wts["wo"], tm)
    return _ffn(x1, norm2_g, norm_f_g.reshape(1, D_MODEL), wts["wg"], wts["wu"], wts["wd"], tm, D_FF // 2)


def kernel(x_prompt, x_sample, state_gla, state_win0_k, state_win0_v, state_win1_k, state_win1_v,
           state_win2_k, state_win2_v, norm1_g, w_in, gla_gate_w2, gla_gate_b, gla_norm_g,
           proj_a, proj_b, w_out, norm2_g, w_ffn_gate, w_ffn_up, w_ffn_down, norm_f_g):
    wts = _prep_weights(w_in, gla_gate_w2, proj_a, proj_b, w_out, w_ffn_gate, w_ffn_up, w_ffn_down)
    bp, seq, _ = x_prompt.shape
    bs, n_new, _ = x_sample.shape
    width = DIL_HEADS * DIL_HD

    xp = x_prompt.reshape(bp * seq, D_MODEL)
    z_gla, z_dq, z_kv, z_gates = _project(xp, norm1_g, wts, 1024)
    s0 = jnp.zeros((bp, GLA_HEADS, GLA_HDK, GLA_HDV), F32)
    o_a, gla_p = _gla(z_gla.reshape(bp, seq, GLA_COLS), wts["w2"], gla_gate_b, gla_norm_g, s0,
                      GLA_CHUNK, GLA_CHUNK)
    kv_p = z_kv.reshape(bp, seq, 2 * DIL_WIDTH)
    o_b = _dil_prompt(z_dq.reshape(bp, seq, DIL_WIDTH), kv_p)
    y_prompt = _tail(xp, o_a.reshape(bp * seq, GLA_DV), o_b.reshape(bp * seq, DIL_OUT), z_gates, wts,
                     norm2_g, norm_f_g, 512).reshape(bp, seq, D_MODEL)
    win_p = []
    for g in range(DIL_GROUPS):
        keep = min(DIL_WINDOWS[g], seq)
        for off in (0, DIL_WIDTH):
            c0 = off + g * width
            win_p.append(kv_p[:, seq - keep:, c0:c0 + width].reshape(1, bp, keep, DIL_HEADS, DIL_HD))

    xs = jnp.pad(x_sample, ((0, 0), (0, SAMPLE_PAD - n_new), (0, 0))).reshape(bs * SAMPLE_PAD, D_MODEL)
    zs_gla, zs_dq, zs_kv, zs_gates = _project(xs, norm1_g, wts, bs * SAMPLE_PAD)
    os_a, gla_s = _gla(zs_gla.reshape(bs, SAMPLE_PAD, GLA_COLS), wts["w2"], gla_gate_b, gla_norm_g,
                       state_gla[0], SAMPLE_PAD, n_new)
    bufs = [a.reshape(bs, a.shape[2], width) for a in
            (state_win0_k, state_win0_v, state_win1_k, state_win1_v, state_win2_k, state_win2_v)]
    outs = _dil_sample(zs_dq.reshape(bs, SAMPLE_PAD, DIL_WIDTH), zs_kv.reshape(bs, SAMPLE_PAD, 2 * DIL_WIDTH),
                       bufs, n_new)
    os_b, win_s = outs[0], outs[1:]
    ys = _tail(xs, os_a.reshape(bs * SAMPLE_PAD, GLA_DV), os_b.reshape(bs * SAMPLE_PAD, DIL_OUT), zs_gates, wts,
               norm2_g, norm_f_g, bs * SAMPLE_PAD)
    y_sample = ys.reshape(bs, SAMPLE_PAD, D_MODEL)[:, :n_new]
    win_s = [a.reshape(1, bs, a.shape[1], DIL_HEADS, DIL_HD) for a in win_s]

    return (y_prompt, y_sample, gla_p[None], *win_p, gla_s[None], *win_s)
```

```python
import functools

import jax
import jax.numpy as jnp
from jax import lax
from jax.experimental import pallas as pl
from jax.experimental.pallas import tpu as pltpu

F32 = jnp.float32
BF16 = jnp.bfloat16

D_MODEL = 1024
GLA_HEADS = 4
GLA_DK = 512
GLA_DV = 1024
GLA_HDK = 128
GLA_HDV = 256
GLA_GATE_RANK = 16
GLA_TAU = 16.0
DIL_WINDOWS = (128, 512, 2048)
DIL_RATES = (1, 4, 16)
DIL_GROUPS = 3
DIL_HEADS = 4
DIL_HD = 64
DIL_WIDTH = 768
DIL_OUT = 256
ALIBI_MAX = 8.0
D_FF = 2816
RMS_EPS = 1e-6

LANES = 128
Q_TILE = 128
GLA_CHUNK = 128
SAMPLE_PAD = 16
MASK_VALUE = -1e30
VMEM_LIMIT_BYTES = 48 * 1024 * 1024
IN_PROJ_VMEM_LIMIT_BYTES = 56 * 1024 * 1024
_LN2 = 0.6931471805599453

_OFF = {}
_o = 0
for _name, _w in (("gq", GLA_DK), ("gk", GLA_DK), ("gv", GLA_DV), ("gr", GLA_DV), ("glr", GLA_GATE_RANK),
                  ("dq", DIL_WIDTH), ("dk", DIL_WIDTH), ("dv", DIL_WIDTH), ("ga", D_MODEL), ("gb", D_MODEL)):
    _OFF[_name] = (_o, _o + _w)
    _o += _w
GLA_COLS = 2 * GLA_DK + 2 * GLA_DV + LANES


def _params(*sem):
    return pltpu.CompilerParams(dimension_semantics=sem, vmem_limit_bytes=VMEM_LIMIT_BYTES)


def _dot(a, b):
    return jnp.dot(a, b, preferred_element_type=F32)


def _dot_nt(a, b):
    return lax.dot_general(a, b, (((1,), (1,)), ((), ())), preferred_element_type=F32)


def _dot_tn(a, b):
    return lax.dot_general(a, b, (((0,), (0,)), ((), ())), preferred_element_type=F32)


def _rms(x, g):
    return x * lax.rsqrt(jnp.mean(x * x, axis=-1, keepdims=True) + RMS_EPS) * g


IN_PROJ_COL_CHUNK = 1024


def _in_proj_kernel(x_ref, g_ref, *refs):
    n_out = len(refs) // 2
    h = _rms(x_ref[...], g_ref[...]).astype(BF16)
    for w_ref, o_ref in zip(refs[:n_out], refs[n_out:]):
        n = w_ref.shape[1]
        for c0 in range(0, n, IN_PROJ_COL_CHUNK):
            cs = slice(c0, min(c0 + IN_PROJ_COL_CHUNK, n))
            o_ref[:, cs] = _dot(h, w_ref[:, cs]).astype(o_ref.dtype)


def _in_proj(x, g, weights, out_dtypes, tm):
    m, k = x.shape
    resident = lambda a: pl.BlockSpec(a.shape, lambda i: (0, 0), pipeline_mode=pl.Buffered(1))
    return pl.pallas_call(
        _in_proj_kernel,
        grid=(m // tm,),
        in_specs=[pl.BlockSpec((tm, k), lambda i: (i, 0)), resident(g)] + [resident(w) for w in weights],
        out_specs=[pl.BlockSpec((tm, w.shape[1]), lambda i: (i, 0)) for w in weights],
        out_shape=[jax.ShapeDtypeStruct((m, w.shape[1]), dt) for w, dt in zip(weights, out_dtypes)],
        compiler_params=pltpu.CompilerParams(dimension_semantics=("parallel",),
                                             vmem_limit_bytes=IN_PROJ_VMEM_LIMIT_BYTES),
        name="in_proj",
    )(x, g, *weights)


def _split3(x):
    x1 = x.astype(BF16)
    r1 = x - x1.astype(F32)
    x2 = r1.astype(BF16)
    x3 = (r1 - x2.astype(F32)).astype(BF16)
    return x1, x2, x3


def _gla_kernel(q_ref, k_ref, v_ref, r_ref, lr_ref, w2_ref, gb_ref, ng_ref, s0_ref,
                o_ref, sout_ref, s_ref, *, chunk, n_valid):
    c = pl.program_id(1)

    @pl.when(c == 0)
    def _():
        s_ref[...] = s0_ref[0]

    gate = _dot(lr_ref[0], w2_ref[...]) + gb_ref[...]
    log_a = (jnp.minimum(gate, 0.0) - jnp.log(1.0 + jnp.exp(-jnp.abs(gate)))) * (1.0 / GLA_TAU)
    row = lax.broadcasted_iota(jnp.int32, (chunk, chunk), 0)
    col = lax.broadcasted_iota(jnp.int32, (chunk, chunk), 1)
    causal = row >= col
    if n_valid < chunk:
        tok = lax.broadcasted_iota(jnp.int32, log_a.shape, 0)
        log_a = jnp.where(tok < n_valid, log_a, 0.0)
    parts = _split3(log_a)
    tril = jnp.where(causal, 1.0, 0.0).astype(BF16)
    ones = jnp.ones((chunk, LANES), BF16)
    b = sum(_dot(tril, p) for p in parts)
    b_tot = sum(_dot_tn(p, ones) for p in parts)

    for h in range(GLA_HEADS):
        ks = slice(h * GLA_HDK, (h + 1) * GLA_HDK)
        vs = slice(h * GLA_HDV, (h + 1) * GLA_HDV)
        bh = b[:, ks]
        qh = q_ref[0, :, ks].astype(F32) * (GLA_HDK ** -0.5)
        kh = k_ref[0, :, ks].astype(F32)
        vh = v_ref[0, :, vs]
        if n_valid < chunk:
            tokv = lax.broadcasted_iota(jnp.int32, vh.shape, 0)
            vh = jnp.where(tokv < n_valid, vh, jnp.zeros_like(vh))
        qt = (qh * jnp.exp(bh)).astype(BF16)
        kt = (kh * jnp.exp(-bh)).astype(BF16)
        kd = (kh * jnp.exp(bh[chunk - 1:chunk, :] - bh)).astype(BF16)
        s_old = s_ref[h]
        scores = jnp.where(causal, _dot_nt(qt, kt), 0.0).astype(BF16)
        o = _dot(qt, s_old.astype(BF16)) + _dot(scores, vh)
        dec = jnp.exp(b_tot[ks, :])
        s_ref[h] = s_old * jnp.concatenate([dec, dec], axis=1) + _dot_tn(kd, vh)
        on = _rms(o, ng_ref[...])
        gr = r_ref[0, :, vs].astype(F32)
        o_ref[0, :, vs] = (on * (gr * jax.nn.sigmoid(gr))).astype(o_ref.dtype)

    @pl.when(c == pl.num_programs(1) - 1)
    def _():
        sout_ref[0] = s_ref[...]


def _gla(z, w2, gate_b, norm_g, s0, chunk, n_valid):
    bsz, t, _ = z.shape
    kern = functools.partial(_gla_kernel, chunk=chunk, n_valid=n_valid)
    return pl.pallas_call(
        kern,
        grid=(bsz, t // chunk),
        in_specs=[pl.BlockSpec((1, chunk, GLA_DK), lambda b, c: (b, c, 0)),
                  pl.BlockSpec((1, chunk, GLA_DK), lambda b, c: (b, c, 1)),
                  pl.BlockSpec((1, chunk, GLA_DV), lambda b, c: (b, c, 1)),
                  pl.BlockSpec((1, chunk, GLA_DV), lambda b, c: (b, c, 2)),
                  pl.BlockSpec((1, chunk, LANES), lambda b, c: (b, c, (2 * GLA_DK + 2 * GLA_DV) // LANES)),
                  pl.BlockSpec((LANES, GLA_DK), lambda b, c: (0, 0)),
                  pl.BlockSpec((1, GLA_DK), lambda b, c: (0, 0)),
                  pl.BlockSpec((1, GLA_HDV), lambda b, c: (0, 0)),
                  pl.BlockSpec((1, GLA_HEADS, GLA_HDK, GLA_HDV), lambda b, c: (b, 0, 0, 0))],
        out_specs=[pl.BlockSpec((1, chunk, GLA_DV), lambda b, c: (b, c, 0)),
                   pl.BlockSpec((1, GLA_HEADS, GLA_HDK, GLA_HDV), lambda b, c: (b, 0, 0, 0))],
        out_shape=[jax.ShapeDtypeStruct((bsz, t, GLA_DV), BF16),
                   jax.ShapeDtypeStruct((bsz, GLA_HEADS, GLA_HDK, GLA_HDV), F32)],
        scratch_shapes=[pltpu.VMEM((GLA_HEADS, GLA_HDK, GLA_HDV), F32)],
        compiler_params=_params("parallel", "arbitrary"),
        name="gla",
    )(z, z, z, z, z, w2, gate_b, norm_g, s0)


def _alibi_slope(g, head):
    n = DIL_GROUPS * DIL_HEADS
    return jnp.exp((-ALIBI_MAX * _LN2 / n) * (head + (g * DIL_HEADS + 1.0)))


def _pair_slopes(g, hp, rows):
    head = 2.0 * hp.astype(F32) + jnp.where(rows >= Q_TILE, 1.0, 0.0)
    return _alibi_slope(g, head)


def _dil_prompt_kernel(q0_ref, q1_ref, q2_ref, k0_ref, k1_ref, k2_ref, v0_ref, v1_ref, v2_ref,
                       o_ref, qf_ref, og_ref, ld_ref, bf_ref, br_ref, *, seq):
    hp = pl.program_id(1)
    q_refs = (q0_ref, q1_ref, q2_ref)
    k_refs = (k0_ref, k1_ref, k2_ref)
    v_refs = (v0_ref, v1_ref, v2_ref)

    for g in range(DIL_GROUPS):
        qf_ref[g] = q_refs[g][...].astype(F32) * (DIL_HD ** -0.5)

    for g in range(DIL_GROUPS):
        rate = float(DIL_RATES[g])
        rows = lax.broadcasted_iota(jnp.int32, (2 * Q_TILE, Q_TILE), 0)
        cols = lax.broadcasted_iota(jnp.int32, (2 * Q_TILE, Q_TILE), 1)
        dist = (rows & (Q_TILE - 1)) - cols
        slope = _pair_slopes(g, hp, rows)
        bf_ref[g] = jnp.where(dist >= 0, -slope * rate * dist.astype(F32), MASK_VALUE)
        if g < 2:
            rows = lax.broadcasted_iota(jnp.int32, (2 * Q_TILE, 2 * Q_TILE), 0)
            cols = lax.broadcasted_iota(jnp.int32, (2 * Q_TILE, 2 * Q_TILE), 1)
            dist = (rows & (Q_TILE - 1)) + Q_TILE - cols
            slope = _pair_slopes(g, hp, rows)
            ok = jnp.abs(dist - Q_TILE // 2) <= Q_TILE // 2
            br_ref[g] = jnp.where(ok, -slope * rate * dist.astype(F32), MASK_VALUE)

    lane = lax.broadcasted_iota(jnp.int32, (Q_TILE, LANES), 1)
    lo = lane < DIL_HD

    def rows_of(start, size, rate):
        return pl.ds(start, size) if rate == 1 else pl.ds(start, size, stride=rate)

    def tile(g, q_row, k_row, n_keys, bias):
        rate = DIL_RATES[g]
        q2 = qf_ref[g, rows_of(q_row, Q_TILE, rate), :]
        k2 = k_refs[g][rows_of(k_row, n_keys, rate), :].astype(BF16)
        v2 = v_refs[g][rows_of(k_row, n_keys, rate), :].astype(BF16)
        qs = jnp.concatenate([jnp.where(lo, q2, 0.0), jnp.where(lo, 0.0, q2)], axis=0).astype(BF16)
        s = _dot_nt(qs, k2) + bias
        m = jnp.max(s, axis=-1, keepdims=True)
        p = jnp.exp(s - m)
        l = jnp.sum(p, axis=-1, keepdims=True)
        pv = _dot(p.astype(BF16), v2)
        o2 = jnp.where(lo, pv[:Q_TILE], pv[Q_TILE:])
        l2 = jnp.where(lo, l[:Q_TILE], l[Q_TILE:])
        m2 = jnp.where(lo, m[:Q_TILE], m[Q_TILE:])
        og_ref[g, rows_of(q_row, Q_TILE, rate), :] = o2 / l2
        ld_ref[g, rows_of(q_row, Q_TILE, rate), :] = m2 + jnp.log(l2)

    for g in range(DIL_GROUPS):
        rate = DIL_RATES[g]
        n_tiles = seq // rate // Q_TILE
        span = rate * Q_TILE

        def residue(rho, carry, g=g, n_tiles=n_tiles, span=span):
            tile(g, rho, rho, Q_TILE, bf_ref[g])
            if n_tiles > 1:
                def later(n, c2):
                    tile(g, rho + n * span, rho + (n - 1) * span, 2 * Q_TILE, br_ref[g])
                    return c2
                lax.fori_loop(1, n_tiles, later, 0)
            return carry

        if rate == 1:
            residue(0, 0)
        else:
            lax.fori_loop(0, rate, residue, 0)

    def combine(i, carry):
        rs = pl.ds(pl.multiple_of(i * Q_TILE, Q_TILE), Q_TILE)
        ld = [ld_ref[g, rs, :] for g in range(DIL_GROUPS)]
        top = jnp.maximum(jnp.maximum(ld[0], ld[1]), ld[2])
        w = [jnp.exp(x - top) for x in ld]
        num = sum(w[g] * og_ref[g, rs, :] for g in range(DIL_GROUPS))
        o_ref[rs, :] = (num / (w[0] + w[1] + w[2])).astype(o_ref.dtype)
        return carry

    lax.fori_loop(0, seq // Q_TILE, combine, 0)


def _dil_prompt(dq, kv):
    bsz, seq, _ = dq.shape
    pairs = DIL_HEADS * DIL_HD // LANES
    nblk = DIL_WIDTH // LANES

    def spec(col0):
        return pl.BlockSpec((None, seq, LANES), lambda b, hp, col0=col0: (b, 0, col0 + hp))

    in_specs = ([spec(g * pairs) for g in range(DIL_GROUPS)]
                + [spec(g * pairs) for g in range(DIL_GROUPS)]
                + [spec(nblk + g * pairs) for g in range(DIL_GROUPS)])
    return pl.pallas_call(
        functools.partial(_dil_prompt_kernel, seq=seq),
        grid=(bsz, pairs),
        in_specs=in_specs,
        out_specs=pl.BlockSpec((None, seq, LANES), lambda b, hp: (b, 0, hp)),
        out_shape=jax.ShapeDtypeStruct((bsz, seq, DIL_OUT), BF16),
        scratch_shapes=[pltpu.VMEM((DIL_GROUPS, seq, LANES), F32),
                        pltpu.VMEM((DIL_GROUPS, seq, LANES), F32),
                        pltpu.VMEM((DIL_GROUPS, seq, LANES), F32),
                        pltpu.VMEM((DIL_GROUPS, 2 * Q_TILE, Q_TILE), F32),
                        pltpu.VMEM((2, 2 * Q_TILE, 2 * Q_TILE), F32)],
        compiler_params=_params("parallel", "arbitrary"),
        name="dil_prompt",
    )(dq, dq, dq, kv, kv, kv, kv, kv, kv)


def _dil_sample_kernel(q_ref, kvn_ref, kb0_ref, vb0_ref, kb1_ref, vb1_ref, kb2_ref, vb2_ref,
                       o_ref, nk0_ref, nv0_ref, nk1_ref, nv1_ref, nk2_ref, nv2_ref, *, n_new):
    kb_refs = (kb0_ref, kb1_ref, kb2_ref)
    vb_refs = (vb0_ref, vb1_ref, vb2_ref)
    nk_refs = (nk0_ref, nk1_ref, nk2_ref)
    nv_refs = (nv0_ref, nv1_ref, nv2_ref)
    pad = q_ref.shape[1]
    nrow = DIL_HEADS * pad
    width = DIL_HEADS * DIL_HD

    lane_head = jnp.right_shift(lax.broadcasted_iota(jnp.int32, (pad, width), 1), DIL_HD.bit_length() - 1)

    def bias_for(g, dist):
        rate = DIL_RATES[g]
        rows = lax.broadcasted_iota(jnp.int32, dist.shape, 0)
        head = sum(jnp.where(rows >= h * pad, 1.0, 0.0) for h in range(1, DIL_HEADS))
        ok = (dist >= 0) & (dist <= DIL_WINDOWS[g]) & ((dist & (rate - 1)) == 0)
        return jnp.where(ok, -_alibi_slope(g, head) * dist.astype(F32), MASK_VALUE)

    sel_r = lax.broadcasted_iota(jnp.int32, (pad, LANES), 0)
    sel_c = lax.broadcasted_iota(jnp.int32, (pad, LANES), 1)
    place = jnp.where((sel_r < n_new) & (sel_c == sel_r + (LANES - n_new)), 1.0, 0.0).astype(BF16)
    tail_lane = lax.broadcasted_iota(jnp.int32, (width, LANES), 1) >= LANES - n_new

    def shifted(buf_t, new_rows):
        length = buf_t.shape[1]
        rolled = pltpu.roll(buf_t, length - n_new, 1)
        new_t = sum(_dot_tn(part, place) for part in _split3(new_rows))
        last = jnp.where(tail_lane, new_t, rolled[:, length - LANES:])
        return rolled, last

    scores, values, transposed = [], [], []
    for g in range(DIL_GROUPS):
        length = kb_refs[g].shape[2]
        cs = slice(g * width, (g + 1) * width)
        vcs = slice(DIL_WIDTH + g * width, DIL_WIDTH + (g + 1) * width)
        qg = q_ref[0, :, cs].astype(F32) * (DIL_HD ** -0.5)
        qs = jnp.concatenate([jnp.where(lane_head == h, qg, 0.0) for h in range(DIL_HEADS)], axis=0).astype(BF16)
        kb, vb = kb_refs[g][0], vb_refs[g][0]
        kn, vn = kvn_ref[0, :, cs], kvn_ref[0, :, vcs]
        rows = lax.broadcasted_iota(jnp.int32, (nrow, length), 0)
        cols = lax.broadcasted_iota(jnp.int32, (nrow, length), 1)
        dist = length + (rows & (pad - 1)) - cols
        scores.append(_dot(qs, kb.astype(BF16)) + bias_for(g, dist))
        values.append(vb.astype(BF16))
        transposed.append(True)
        rows = lax.broadcasted_iota(jnp.int32, (nrow, pad), 0)
        cols = lax.broadcasted_iota(jnp.int32, (nrow, pad), 1)
        dist = jnp.where(cols < n_new, (rows & (pad - 1)) - cols, -1)
        scores.append(_dot_nt(qs, kn.astype(BF16)) + bias_for(g, dist))
        values.append(vn.astype(BF16))
        transposed.append(False)
        for buf, new, out_ref in ((kb, kn, nk_refs[g]), (vb, vn, nv_refs[g])):
            rolled, last = shifted(buf, new)
            if length > LANES:
                out_ref[0, :, 0:length - LANES] = rolled[:, 0:length - LANES]
            out_ref[0, :, length - LANES:length] = last

    top = functools.reduce(jnp.maximum, [jnp.max(s, axis=-1, keepdims=True) for s in scores])
    probs = [jnp.exp(s - top) for s in scores]
    den = sum(jnp.sum(p, axis=-1, keepdims=True) for p in probs)
    acc = sum((_dot_nt if t else _dot)(p.astype(BF16), v)
              for p, v, t in zip(probs, values, transposed)) / den
    out = sum(jnp.where(lane_head == h, acc[h * pad:(h + 1) * pad], 0.0) for h in range(DIL_HEADS))
    o_ref[0] = out.astype(o_ref.dtype)


def _dil_sample(dq, kvn, bufs, n_new):
    bsz, pad, _ = dq.shape
    full = lambda a: pl.BlockSpec((1,) + a.shape[1:], lambda b: (b, 0, 0))
    out_shape = ([jax.ShapeDtypeStruct((bsz, pad, DIL_OUT), BF16)]
                 + [jax.ShapeDtypeStruct(a.shape, a.dtype) for a in bufs])
    return pl.pallas_call(
        functools.partial(_dil_sample_kernel, n_new=n_new),
        grid=(bsz,),
        in_specs=[full(dq), full(kvn)] + [full(a) for a in bufs],
        out_specs=[pl.BlockSpec((1, pad, DIL_OUT), lambda b: (b, 0, 0))] + [full(a) for a in bufs],
        out_shape=out_shape,
        compiler_params=_params("parallel"),
        name="dil_sample",
    )(dq, kvn, *bufs)


def _merge_out_kernel(x_ref, oa_ref, ob_ref, ga_ref, gb_ref, pa_ref, pb_ref, wo_ref, o_ref):
    pa = _dot(oa_ref[...], pa_ref[...])
    pb = _dot(ob_ref[...], pb_ref[...])
    merged = jax.nn.sigmoid(ga_ref[...].astype(F32)) * pa + jax.nn.sigmoid(gb_ref[...].astype(F32)) * pb
    o_ref[...] = x_ref[...] + _dot(merged.astype(BF16), wo_ref[...])


def _merge_out(x, oa, ob, gates, pa, pb, wo, tm):
    m = x.shape[0]
    row = lambda w: pl.BlockSpec((tm, w), lambda i: (i, 0))
    whole = lambda a: pl.BlockSpec(a.shape, lambda i: (0, 0))
    return pl.pallas_call(
        _merge_out_kernel,
        grid=(m // tm,),
        in_specs=[row(D_MODEL), row(GLA_DV), row(DIL_OUT), row(D_MODEL),
                  pl.BlockSpec((tm, D_MODEL), lambda i: (i, 1)),
                  whole(pa), whole(pb), whole(wo)],
        out_specs=row(D_MODEL),
        out_shape=jax.ShapeDtypeStruct((m, D_MODEL), F32),
        compiler_params=_params("parallel"),
        name="merge_out",
    )(x, oa, ob, gates, gates, pa, pb, wo)


def _ffn_kernel(x_ref, n2_ref, nf_ref, wg_ref, wu_ref, wd_ref, y_ref, h_ref, acc_ref):
    j = pl.program_id(1)

    @pl.when(j == 0)
    def _():
        h_ref[...] = _rms(x_ref[...], n2_ref[...]).astype(BF16)
        acc_ref[...] = jnp.zeros_like(acc_ref)

    h = h_ref[...]
    gate = _dot(h, wg_ref[...])
    up = _dot(h, wu_ref[...])
    act = (gate * jax.nn.sigmoid(gate) * up).astype(BF16)
    acc_ref[...] += _dot(act, wd_ref[...])

    @pl.when(j == pl.num_programs(1) - 1)
    def _():
        y_ref[...] = _rms(x_ref[...] + acc_ref[...], nf_ref[...])


def _ffn(x, n2, nf, wg, wu, wd, tm, tf):
    m = x.shape[0]
    return pl.pallas_call(
        _ffn_kernel,
        grid=(m // tm, D_FF // tf),
        in_specs=[pl.BlockSpec((tm, D_MODEL), lambda i, j: (i, 0)),
                  pl.BlockSpec((1, D_MODEL), lambda i, j: (0, 0)),
                  pl.BlockSpec((1, D_MODEL), lambda i, j: (0, 0)),
                  pl.BlockSpec((D_MODEL, tf), lambda i, j: (0, j)),
                  pl.BlockSpec((D_MODEL, tf), lambda i, j: (0, j)),
                  pl.BlockSpec((tf, D_MODEL), lambda i, j: (j, 0))],
        out_specs=pl.BlockSpec((tm, D_MODEL), lambda i, j: (i, 0)),
        out_shape=jax.ShapeDtypeStruct((m, D_MODEL), F32),
        scratch_shapes=[pltpu.VMEM((tm, D_MODEL), BF16), pltpu.VMEM((tm, D_MODEL), F32)],
        compiler_params=_params("parallel", "arbitrary"),
        name="ffn",
    )(x, n2, nf, wg, wu, wd)


def _prep_weights(w_in, gla_gate_w2, proj_a, proj_b, w_out, w_ffn_gate, w_ffn_up, w_ffn_down):
    w = w_in[0]
    cols = lambda name: w[:, _OFF[name][0]:_OFF[name][1]]
    pad = jnp.zeros((D_MODEL, LANES - GLA_GATE_RANK), F32)
    w_gla = jnp.concatenate([cols("gq"), cols("gk"), cols("gv"), cols("gr"), cols("glr"), pad], axis=1)
    w2 = jnp.concatenate([gla_gate_w2[0], jnp.zeros((LANES - GLA_GATE_RANK, GLA_DK), F32)], axis=0)
    return dict(
        w_gla=w_gla.astype(BF16), w_dq=cols("dq").astype(BF16),
        w_kv=jnp.concatenate([cols("dk"), cols("dv")], axis=1).astype(BF16),
        w_gates=jnp.concatenate([cols("ga"), cols("gb")], axis=1).astype(BF16),
        w2=w2.astype(BF16), pa=proj_a[0].astype(BF16), pb=proj_b[0].astype(BF16), wo=w_out[0].astype(BF16),
        wg=w_ffn_gate[0].astype(BF16), wu=w_ffn_up[0].astype(BF16), wd=w_ffn_down[0].astype(BF16))


def _project(x2d, norm1_g, wts, tm):
    return _in_proj(x2d, norm1_g, [wts["w_gla"], wts["w_dq"], wts["w_kv"], wts["w_gates"]],
                    [BF16, BF16, F32, BF16], tm)


def _tail(x2d, o_a, o_b, z_gates, wts, norm2_g, norm_f_g, tm):
    x1 = _merge_out(x2d, o_a, o_b, z_gates, wts["pa"], wts["pb"], wts["wo"], tm)
    return _ffn(x1, norm2_g, norm_f_g.reshape(1, D_MODEL), wts["wg"], wts["wu"], wts["wd"], tm, D_FF // 2)


def kernel(x_prompt, x_sample, state_gla, state_win0_k, state_win0_v, state_win1_k, state_win1_v,
           state_win2_k, state_win2_v, norm1_g, w_in, gla_gate_w2, gla_gate_b, gla_norm_g,
           proj_a, proj_b, w_out, norm2_g, w_ffn_gate, w_ffn_up, w_ffn_down, norm_f_g):
    wts = _prep_weights(w_in, gla_gate_w2, proj_a, proj_b, w_out, w_ffn_gate, w_ffn_up, w_ffn_down)
    bp, seq, _ = x_prompt.shape
    bs, n_new, _ = x_sample.shape
    width = DIL_HEADS * DIL_HD

    xp = x_prompt.reshape(bp * seq, D_MODEL)
    z_gla, z_dq, z_kv, z_gates = _project(xp, norm1_g, wts, 512)
    s0 = jnp.zeros((bp, GLA_HEADS, GLA_HDK, GLA_HDV), F32)
    o_a, gla_p = _gla(z_gla.reshape(bp, seq, GLA_COLS), wts["w2"], gla_gate_b, gla_norm_g, s0,
                      GLA_CHUNK, GLA_CHUNK)
    kv_p = z_kv.reshape(bp, seq, 2 * DIL_WIDTH)
    o_b = _dil_prompt(z_dq.reshape(bp, seq, DIL_WIDTH), kv_p)
    y_prompt = _tail(xp, o_a.reshape(bp * seq, GLA_DV), o_b.reshape(bp * seq, DIL_OUT), z_gates, wts,
                     norm2_g, norm_f_g, 512).reshape(bp, seq, D_MODEL)
    win_p = []
    for g in range(DIL_GROUPS):
        keep = min(DIL_WINDOWS[g], seq)
        for off in (0, DIL_WIDTH):
            c0 = off + g * width
            win_p.append(kv_p[:, seq - keep:, c0:c0 + width].reshape(1, bp, keep, DIL_HEADS, DIL_HD))

    xs = jnp.pad(x_sample, ((0, 0), (0, SAMPLE_PAD - n_new), (0, 0))).reshape(bs * SAMPLE_PAD, D_MODEL)
    zs_gla, zs_dq, zs_kv, zs_gates = _project(xs, norm1_g, wts, bs * SAMPLE_PAD)
    os_a, gla_s = _gla(zs_gla.reshape(bs, SAMPLE_PAD, GLA_COLS), wts["w2"], gla_gate_b, gla_norm_g,
                       state_gla[0], SAMPLE_PAD, n_new)
    bufs = [jnp.transpose(a[0], (0, 2, 3, 1)).reshape(bs, width, a.shape[2]) for a in
            (state_win0_k, state_win0_v, state_win1_k, state_win1_v, state_win2_k, state_win2_v)]
    outs = _dil_sample(zs_dq.reshape(bs, SAMPLE_PAD, DIL_WIDTH), zs_kv.reshape(bs, SAMPLE_PAD, 2 * DIL_WIDTH),
                       bufs, n_new)
    os_b, win_s = outs[0], outs[1:]
    ys = _tail(xs, os_a.reshape(bs * SAMPLE_PAD, GLA_DV), os_b.reshape(bs * SAMPLE_PAD, DIL_OUT), zs_gates, wts,
               norm2_g, norm_f_g, bs * SAMPLE_PAD)
    y_sample = ys.reshape(bs, SAMPLE_PAD, D_MODEL)[:, :n_new]
    win_s = [jnp.transpose(a.reshape(bs, DIL_HEADS, DIL_HD, a.shape[2]), (0, 3, 1, 2))[None] for a in win_s]

    return (y_prompt, y_sample, gla_p[None], *win_p, gla_s[None], *win_s)
```

```python
import functools

import jax
import jax.numpy as jnp
from jax import lax
from jax.experimental import pallas as pl
from jax.experimental.pallas import tpu as pltpu

F32 = jnp.float32
BF16 = jnp.bfloat16

D_MODEL = 1024
GLA_HEADS = 4
GLA_DK = 512
GLA_DV = 1024
GLA_HDK = 128
GLA_HDV = 256
GLA_GATE_RANK = 16
GLA_TAU = 16.0
DIL_WINDOWS = (128, 512, 2048)
DIL_RATES = (1, 4, 16)
DIL_GROUPS = 3
DIL_HEADS = 4
DIL_HD = 64
DIL_WIDTH = 768
DIL_OUT = 256
ALIBI_MAX = 8.0
D_FF = 2816
RMS_EPS = 1e-6

LANES = 128
Q_TILE = 128
TILES_PER_BODY = 3
RESIDUES_PER_BODY = 4
GLA_CHUNK = 128
SAMPLE_PAD = 16
MASK_VALUE = -1e30
VMEM_LIMIT_BYTES = 48 * 1024 * 1024
IN_PROJ_VMEM_LIMIT_BYTES = 56 * 1024 * 1024
_LN2 = 0.6931471805599453

_OFF = {}
_o = 0
for _name, _w in (("gq", GLA_DK), ("gk", GLA_DK), ("gv", GLA_DV), ("gr", GLA_DV), ("glr", GLA_GATE_RANK),
                  ("dq", DIL_WIDTH), ("dk", DIL_WIDTH), ("dv", DIL_WIDTH), ("ga", D_MODEL), ("gb", D_MODEL)):
    _OFF[_name] = (_o, _o + _w)
    _o += _w
GLA_COLS = 2 * GLA_DK + 2 * GLA_DV + LANES


def _params(*sem):
    return pltpu.CompilerParams(dimension_semantics=sem, vmem_limit_bytes=VMEM_LIMIT_BYTES)


def _dot(a, b):
    return jnp.dot(a, b, preferred_element_type=F32)


def _dot_nt(a, b):
    return lax.dot_general(a, b, (((1,), (1,)), ((), ())), preferred_element_type=F32)


def _dot_tn(a, b):
    return lax.dot_general(a, b, (((0,), (0,)), ((), ())), preferred_element_type=F32)


def _rms(x, g):
    return x * lax.rsqrt(jnp.mean(x * x, axis=-1, keepdims=True) + RMS_EPS) * g


IN_PROJ_COL_CHUNK = 1024


def _in_proj_kernel(x_ref, g_ref, *refs):
    n_out = len(refs) // 2
    h = _rms(x_ref[...], g_ref[...]).astype(BF16)
    for w_ref, o_ref in zip(refs[:n_out], refs[n_out:]):
        n = w_ref.shape[1]
        for c0 in range(0, n, IN_PROJ_COL_CHUNK):
            cs = slice(c0, min(c0 + IN_PROJ_COL_CHUNK, n))
            o_ref[:, cs] = _dot(h, w_ref[:, cs]).astype(o_ref.dtype)


def _in_proj(x, g, weights, out_dtypes, tm):
    m, k = x.shape
    resident = lambda a: pl.BlockSpec(a.shape, lambda i: (0, 0), pipeline_mode=pl.Buffered(1))
    return pl.pallas_call(
        _in_proj_kernel,
        grid=(m // tm,),
        in_specs=[pl.BlockSpec((tm, k), lambda i: (i, 0)), resident(g)] + [resident(w) for w in weights],
        out_specs=[pl.BlockSpec((tm, w.shape[1]), lambda i: (i, 0)) for w in weights],
        out_shape=[jax.ShapeDtypeStruct((m, w.shape[1]), dt) for w, dt in zip(weights, out_dtypes)],
        compiler_params=pltpu.CompilerParams(dimension_semantics=("parallel",),
                                             vmem_limit_bytes=IN_PROJ_VMEM_LIMIT_BYTES),
        name="in_proj",
    )(x, g, *weights)


def _split3(x):
    x1 = x.astype(BF16)
    r1 = x - x1.astype(F32)
    x2 = r1.astype(BF16)
    x3 = (r1 - x2.astype(F32)).astype(BF16)
    return x1, x2, x3


def _gla_kernel(q_ref, k_ref, v_ref, r_ref, lr_ref, w2_ref, gb_ref, ng_ref, s0_ref,
                o_ref, sout_ref, s_ref, *, chunk, n_valid):
    c = pl.program_id(1)

    @pl.when(c == 0)
    def _():
        s_ref[...] = s0_ref[0]

    gate = _dot(lr_ref[0], w2_ref[...]) + gb_ref[...]
    log_a = (jnp.minimum(gate, 0.0) - jnp.log(1.0 + jnp.exp(-jnp.abs(gate)))) * (1.0 / GLA_TAU)
    row = lax.broadcasted_iota(jnp.int32, (chunk, chunk), 0)
    col = lax.broadcasted_iota(jnp.int32, (chunk, chunk), 1)
    causal = row >= col
    if n_valid < chunk:
        tok = lax.broadcasted_iota(jnp.int32, log_a.shape, 0)
        log_a = jnp.where(tok < n_valid, log_a, 0.0)
    parts = _split3(log_a)
    tril = jnp.where(causal, 1.0, 0.0).astype(BF16)
    b = sum(_dot(tril, p) for p in parts)

    for h in range(GLA_HEADS):
        ks = slice(h * GLA_HDK, (h + 1) * GLA_HDK)
        vs = slice(h * GLA_HDV, (h + 1) * GLA_HDV)
        bh = b[:, ks]
        qh = q_ref[0, :, ks].astype(F32) * (GLA_HDK ** -0.5)
        kh = k_ref[0, :, ks].astype(F32)
        vh = v_ref[0, :, vs]
        if n_valid < chunk:
            tokv = lax.broadcasted_iota(jnp.int32, vh.shape, 0)
            vh = jnp.where(tokv < n_valid, vh, jnp.zeros_like(vh))
        qt = (qh * jnp.exp(bh)).astype(BF16)
        kt = (kh * jnp.exp(-bh)).astype(BF16)
        kd = (kh * jnp.exp(bh[chunk - 1:chunk, :] - bh)).astype(BF16)
        s_old = s_ref[h]
        scores = jnp.where(causal, _dot_nt(qt, kt), 0.0).astype(BF16)
        o = _dot(qt, s_old.astype(BF16)) + _dot(scores, vh)
        dec = jnp.exp(jnp.broadcast_to(bh[chunk - 1:chunk, :], (GLA_HDK, GLA_HDK)).T)
        s_ref[h] = s_old * jnp.concatenate([dec, dec], axis=1) + _dot_tn(kd, vh)
        on = _rms(o, ng_ref[...])
        gr = r_ref[0, :, vs].astype(F32)
        o_ref[0, :, vs] = (on * (gr * jax.nn.sigmoid(gr))).astype(o_ref.dtype)

    @pl.when(c == pl.num_programs(1) - 1)
    def _():
        sout_ref[0] = s_ref[...]


def _gla(z, w2, gate_b, norm_g, s0, chunk, n_valid):
    bsz, t, _ = z.shape
    kern = functools.partial(_gla_kernel, chunk=chunk, n_valid=n_valid)
    return pl.pallas_call(
        kern,
        grid=(bsz, t // chunk),
        in_specs=[pl.BlockSpec((1, chunk, GLA_DK), lambda b, c: (b, c, 0)),
                  pl.BlockSpec((1, chunk, GLA_DK), lambda b, c: (b, c, 1)),
                  pl.BlockSpec((1, chunk, GLA_DV), lambda b, c: (b, c, 1)),
                  pl.BlockSpec((1, chunk, GLA_DV), lambda b, c: (b, c, 2)),
                  pl.BlockSpec((1, chunk, LANES), lambda b, c: (b, c, (2 * GLA_DK + 2 * GLA_DV) // LANES)),
                  pl.BlockSpec((LANES, GLA_DK), lambda b, c: (0, 0)),
                  pl.BlockSpec((1, GLA_DK), lambda b, c: (0, 0)),
                  pl.BlockSpec((1, GLA_HDV), lambda b, c: (0, 0)),
                  pl.BlockSpec((1, GLA_HEADS, GLA_HDK, GLA_HDV), lambda b, c: (b, 0, 0, 0))],
        out_specs=[pl.BlockSpec((1, chunk, GLA_DV), lambda b, c: (b, c, 0)),
                   pl.BlockSpec((1, GLA_HEADS, GLA_HDK, GLA_HDV), lambda b, c: (b, 0, 0, 0))],
        out_shape=[jax.ShapeDtypeStruct((bsz, t, GLA_DV), BF16),
                   jax.ShapeDtypeStruct((bsz, GLA_HEADS, GLA_HDK, GLA_HDV), F32)],
        scratch_shapes=[pltpu.VMEM((GLA_HEADS, GLA_HDK, GLA_HDV), F32)],
        compiler_params=_params("parallel", "arbitrary"),
        name="gla",
    )(z, z, z, z, z, w2, gate_b, norm_g, s0)


def _alibi_slope(g, head):
    n = DIL_GROUPS * DIL_HEADS
    return jnp.exp((-ALIBI_MAX * _LN2 / n) * (head + (g * DIL_HEADS + 1.0)))


def _pair_slopes(g, hp, rows):
    head = 2.0 * hp.astype(F32) + jnp.where(rows >= Q_TILE, 1.0, 0.0)
    return _alibi_slope(g, head)


def _dil_prompt_kernel(q0_ref, q1_ref, q2_ref, k0_ref, k1_ref, k2_ref, v0_ref, v1_ref, v2_ref,
                       o_ref, qf_ref, og_ref, ld_ref, bf_ref, br_ref, *, seq):
    hp = pl.program_id(1)
    q_refs = (q0_ref, q1_ref, q2_ref)
    k_refs = (k0_ref, k1_ref, k2_ref)
    v_refs = (v0_ref, v1_ref, v2_ref)

    for g in range(DIL_GROUPS):
        qf_ref[g] = q_refs[g][...].astype(F32) * (DIL_HD ** -0.5)

    for g in range(DIL_GROUPS):
        rate = float(DIL_RATES[g])
        rows = lax.broadcasted_iota(jnp.int32, (2 * Q_TILE, Q_TILE), 0)
        cols = lax.broadcasted_iota(jnp.int32, (2 * Q_TILE, Q_TILE), 1)
        dist = (rows & (Q_TILE - 1)) - cols
        slope = _pair_slopes(g, hp, rows)
        bf_ref[g] = jnp.where(dist >= 0, -slope * rate * dist.astype(F32), MASK_VALUE)
        if g < 2:
            rows = lax.broadcasted_iota(jnp.int32, (2 * Q_TILE, 2 * Q_TILE), 0)
            cols = lax.broadcasted_iota(jnp.int32, (2 * Q_TILE, 2 * Q_TILE), 1)
            dist = (rows & (Q_TILE - 1)) + Q_TILE - cols
            slope = _pair_slopes(g, hp, rows)
            ok = jnp.abs(dist - Q_TILE // 2) <= Q_TILE // 2
            br_ref[g] = jnp.where(ok, -slope * rate * dist.astype(F32), MASK_VALUE)

    lane = lax.broadcasted_iota(jnp.int32, (Q_TILE, LANES), 1)
    lo = lane < DIL_HD

    def rows_of(start, size, rate):
        return pl.ds(start, size) if rate == 1 else pl.ds(start, size, stride=rate)

    def tile(g, q_row, k_row, n_keys, bias):
        rate = DIL_RATES[g]
        q2 = qf_ref[g, rows_of(q_row, Q_TILE, rate), :]
        k2 = k_refs[g][rows_of(k_row, n_keys, rate), :].astype(BF16)
        v2 = v_refs[g][rows_of(k_row, n_keys, rate), :]
        lo_k = lax.broadcasted_iota(jnp.int32, (n_keys, LANES), 1) < DIL_HD
        va = jnp.where(lo_k, v2, 1.0).astype(BF16)
        vb = jnp.where(lo_k, 1.0, v2).astype(BF16)
        qs = jnp.concatenate([jnp.where(lo, q2, 0.0), jnp.where(lo, 0.0, q2)], axis=0).astype(BF16)
        s = _dot_nt(qs, k2) + bias
        m = jnp.max(s, axis=-1, keepdims=True)
        p = jnp.exp(s - m).astype(BF16)
        ra = _dot(p[:Q_TILE], va)
        rb = _dot(p[Q_TILE:], vb)
        o2 = jnp.where(lo, ra, rb)
        l2 = pltpu.roll(jnp.where(lo, rb, ra), DIL_HD, 1)
        m2 = jnp.where(lo, m[:Q_TILE], m[Q_TILE:])
        og_ref[g, rows_of(q_row, Q_TILE, rate), :] = o2 / l2
        ld_ref[g, rows_of(q_row, Q_TILE, rate), :] = m2 + jnp.log(l2)

    for g in range(DIL_GROUPS):
        rate = DIL_RATES[g]
        n_tiles = seq // rate // Q_TILE
        span = rate * Q_TILE
        group = min(rate, RESIDUES_PER_BODY)
        n_groups = rate // group

        def first_tiles(i, carry, g=g, group=group):
            for u in range(group):
                rho = i * group + u
                tile(g, rho, rho, Q_TILE, bf_ref[g])
            return carry

        def consecutive_tiles(i, carry, g=g, span=span):
            for u in range(TILES_PER_BODY):
                n = 1 + i * TILES_PER_BODY + u
                tile(g, n * span, (n - 1) * span, 2 * Q_TILE, br_ref[g])
            return carry

        def later_tiles(i, carry, g=g, group=group, n_groups=n_groups, span=span):
            n = 1 + i // n_groups
            for u in range(group):
                rho = (i % n_groups) * group + u
                tile(g, rho + n * span, rho + (n - 1) * span, 2 * Q_TILE, br_ref[g])
            return carry

        if n_groups == 1:
            first_tiles(0, 0)
        else:
            lax.fori_loop(0, n_groups, first_tiles, 0)
        if n_tiles > 1:
            if rate == 1:
                assert (n_tiles - 1) % TILES_PER_BODY == 0
                lax.fori_loop(0, (n_tiles - 1) // TILES_PER_BODY, consecutive_tiles, 0)
            else:
                lax.fori_loop(0, (n_tiles - 1) * n_groups, later_tiles, 0)

    def combine(i, carry):
        rs = pl.ds(pl.multiple_of(i * Q_TILE, Q_TILE), Q_TILE)
        ld = [ld_ref[g, rs, :] for g in range(DIL_GROUPS)]
        top = jnp.maximum(jnp.maximum(ld[0], ld[1]), ld[2])
        w = [jnp.exp(x - top) for x in ld]
        num = sum(w[g] * og_ref[g, rs, :] for g in range(DIL_GROUPS))
        o_ref[rs, :] = (num / (w[0] + w[1] + w[2])).astype(o_ref.dtype)
        return carry

    lax.fori_loop(0, seq // Q_TILE, combine, 0)


def _dil_prompt(dq, kv):
    bsz, seq, _ = dq.shape
    pairs = DIL_HEADS * DIL_HD // LANES
    nblk = DIL_WIDTH // LANES

    def spec(col0):
        return pl.BlockSpec((None, seq, LANES), lambda b, hp, col0=col0: (b, 0, col0 + hp))

    in_specs = ([spec(g * pairs) for g in range(DIL_GROUPS)]
                + [spec(g * pairs) for g in range(DIL_GROUPS)]
                + [spec(nblk + g * pairs) for g in range(DIL_GROUPS)])
    return pl.pallas_call(
        functools.partial(_dil_prompt_kernel, seq=seq),
        grid=(bsz, pairs),
        in_specs=in_specs,
        out_specs=pl.BlockSpec((None, seq, LANES), lambda b, hp: (b, 0, hp)),
        out_shape=jax.ShapeDtypeStruct((bsz, seq, DIL_OUT), BF16),
        scratch_shapes=[pltpu.VMEM((DIL_GROUPS, seq, LANES), F32),
                        pltpu.VMEM((DIL_GROUPS, seq, LANES), F32),
                        pltpu.VMEM((DIL_GROUPS, seq, LANES), F32),
                        pltpu.VMEM((DIL_GROUPS, 2 * Q_TILE, Q_TILE), F32),
                        pltpu.VMEM((2, 2 * Q_TILE, 2 * Q_TILE), F32)],
        compiler_params=_params("parallel", "arbitrary"),
        name="dil_prompt",
    )(dq, dq, dq, kv, kv, kv, kv, kv, kv)


def _dil_sample_kernel(q_ref, kvn_ref, kb0_ref, vb0_ref, kb1_ref, vb1_ref, kb2_ref, vb2_ref,
                       o_ref, nk0_ref, nv0_ref, nk1_ref, nv1_ref, nk2_ref, nv2_ref, *, n_new):
    kb_refs = (kb0_ref, kb1_ref, kb2_ref)
    vb_refs = (vb0_ref, vb1_ref, vb2_ref)
    nk_refs = (nk0_ref, nk1_ref, nk2_ref)
    nv_refs = (nv0_ref, nv1_ref, nv2_ref)
    pad = q_ref.shape[1]
    nrow = DIL_HEADS * pad
    width = DIL_HEADS * DIL_HD

    lane_head = jnp.right_shift(lax.broadcasted_iota(jnp.int32, (pad, width), 1), DIL_HD.bit_length() - 1)

    def bias_for(g, dist):
        rate = DIL_RATES[g]
        rows = lax.broadcasted_iota(jnp.int32, dist.shape, 0)
        head = sum(jnp.where(rows >= h * pad, 1.0, 0.0) for h in range(1, DIL_HEADS))
        ok = (dist >= 0) & (dist <= DIL_WINDOWS[g]) & ((dist & (rate - 1)) == 0)
        return jnp.where(ok, -_alibi_slope(g, head) * dist.astype(F32), MASK_VALUE)

    sel_r = lax.broadcasted_iota(jnp.int32, (pad, LANES), 0)
    sel_c = lax.broadcasted_iota(jnp.int32, (pad, LANES), 1)
    place = jnp.where((sel_r < n_new) & (sel_c == sel_r + (LANES - n_new)), 1.0, 0.0).astype(BF16)
    tail_lane = lax.broadcasted_iota(jnp.int32, (width, LANES), 1) >= LANES - n_new

    def shifted(buf_t, new_rows):
        length = buf_t.shape[1]
        rolled = pltpu.roll(buf_t, length - n_new, 1)
        new_t = sum(_dot_tn(part, place) for part in _split3(new_rows))
        last = jnp.where(tail_lane, new_t, rolled[:, length - LANES:])
        return rolled, last

    scores, values, transposed = [], [], []
    for g in range(DIL_GROUPS):
        length = kb_refs[g].shape[2]
        cs = slice(g * width, (g + 1) * width)
        vcs = slice(DIL_WIDTH + g * width, DIL_WIDTH + (g + 1) * width)
        qg = q_ref[0, :, cs].astype(F32) * (DIL_HD ** -0.5)
        qs = jnp.concatenate([jnp.where(lane_head == h, qg, 0.0) for h in range(DIL_HEADS)], axis=0).astype(BF16)
        kb, vb = kb_refs[g][0], vb_refs[g][0]
        kn, vn = kvn_ref[0, :, cs], kvn_ref[0, :, vcs]
        rows = lax.broadcasted_iota(jnp.int32, (nrow, length), 0)
        cols = lax.broadcasted_iota(jnp.int32, (nrow, length), 1)
        dist = length + (rows & (pad - 1)) - cols
        scores.append(_dot(qs, kb.astype(BF16)) + bias_for(g, dist))
        values.append(vb.astype(BF16))
        transposed.append(True)
        rows = lax.broadcasted_iota(jnp.int32, (nrow, pad), 0)
        cols = lax.broadcasted_iota(jnp.int32, (nrow, pad), 1)
        dist = jnp.where(cols < n_new, (rows & (pad - 1)) - cols, -1)
        scores.append(_dot_nt(qs, kn.astype(BF16)) + bias_for(g, dist))
        values.append(vn.astype(BF16))
        transposed.append(False)
        for buf, new, out_ref in ((kb, kn, nk_refs[g]), (vb, vn, nv_refs[g])):
            rolled, last = shifted(buf, new)
            if length > LANES:
                out_ref[0, :, 0:length - LANES] = rolled[:, 0:length - LANES]
            out_ref[0, :, length - LANES:length] = last

    top = functools.reduce(jnp.maximum, [jnp.max(s, axis=-1, keepdims=True) for s in scores])
    probs = [jnp.exp(s - top) for s in scores]
    den = sum(jnp.sum(p, axis=-1, keepdims=True) for p in probs)
    acc = sum((_dot_nt if t else _dot)(p.astype(BF16), v)
              for p, v, t in zip(probs, values, transposed)) / den
    out = sum(jnp.where(lane_head == h, acc[h * pad:(h + 1) * pad], 0.0) for h in range(DIL_HEADS))
    o_ref[0] = out.astype(o_ref.dtype)


def _dil_sample(dq, kvn, bufs, n_new):
    bsz, pad, _ = dq.shape
    full = lambda a: pl.BlockSpec((1,) + a.shape[1:], lambda b: (b, 0, 0))
    out_shape = ([jax.ShapeDtypeStruct((bsz, pad, DIL_OUT), BF16)]
                 + [jax.ShapeDtypeStruct(a.shape, a.dtype) for a in bufs])
    return pl.pallas_call(
        functools.partial(_dil_sample_kernel, n_new=n_new),
        grid=(bsz,),
        in_specs=[full(dq), full(kvn)] + [full(a) for a in bufs],
        out_specs=[pl.BlockSpec((1, pad, DIL_OUT), lambda b: (b, 0, 0))] + [full(a) for a in bufs],
        out_shape=out_shape,
        compiler_params=_params("parallel"),
        name="dil_sample",
    )(dq, kvn, *bufs)


def _merge_out_kernel(x_ref, oa_ref, ob_ref, ga_ref, gb_ref, pa_ref, pb_ref, wo_ref, o_ref):
    pa = _dot(oa_ref[...], pa_ref[...])
    pb = _dot(ob_ref[...], pb_ref[...])
    merged = jax.nn.sigmoid(ga_ref[...].astype(F32)) * pa + jax.nn.sigmoid(gb_ref[...].astype(F32)) * pb
    o_ref[...] = x_ref[...] + _dot(merged.astype(BF16), wo_ref[...])


def _merge_out(x, oa, ob, gates, pa, pb, wo, tm):
    m = x.shape[0]
    row = lambda w: pl.BlockSpec((tm, w), lambda i: (i, 0))
    whole = lambda a: pl.BlockSpec(a.shape, lambda i: (0, 0))
    return pl.pallas_call(
        _merge_out_kernel,
        grid=(m // tm,),
        in_specs=[row(D_MODEL), row(GLA_DV), row(DIL_OUT), row(D_MODEL),
                  pl.BlockSpec((tm, D_MODEL), lambda i: (i, 1)),
                  whole(pa), whole(pb), whole(wo)],
        out_specs=row(D_MODEL),
        out_shape=jax.ShapeDtypeStruct((m, D_MODEL), F32),
        compiler_params=_params("parallel"),
        name="merge_out",
    )(x, oa, ob, gates, gates, pa, pb, wo)


FFN_COL_CHUNK = D_FF // 2


def _ffn_kernel(x_ref, n2_ref, nf_ref, wg_ref, wu_ref, wd_ref, y_ref):
    x = x_ref[...]
    h = _rms(x, n2_ref[...]).astype(BF16)
    acc = x
    for c0 in range(0, D_FF, FFN_COL_CHUNK):
        cs = slice(c0, c0 + FFN_COL_CHUNK)
        gate = _dot(h, wg_ref[:, cs])
        up = _dot(h, wu_ref[:, cs])
        act = (gate * jax.nn.sigmoid(gate) * up).astype(BF16)
        acc = acc + _dot(act, wd_ref[cs, :])
    y_ref[...] = _rms(acc, nf_ref[...])


def _ffn(x, n2, nf, wg, wu, wd, tm):
    m = x.shape[0]
    resident = lambda a: pl.BlockSpec(a.shape, lambda i: (0, 0), pipeline_mode=pl.Buffered(1))
    return pl.pallas_call(
        _ffn_kernel,
        grid=(m // tm,),
        in_specs=[pl.BlockSpec((tm, D_MODEL), lambda i: (i, 0)),
                  resident(n2), resident(nf), resident(wg), resident(wu), resident(wd)],
        out_specs=pl.BlockSpec((tm, D_MODEL), lambda i: (i, 0)),
        out_shape=jax.ShapeDtypeStruct((m, D_MODEL), F32),
        compiler_params=_params("parallel"),
        name="ffn",
    )(x, n2, nf, wg, wu, wd)


def _prep_weights(w_in, gla_gate_w2, proj_a, proj_b, w_out, w_ffn_gate, w_ffn_up, w_ffn_down):
    w = w_in[0]
    cols = lambda name: w[:, _OFF[name][0]:_OFF[name][1]]
    pad = jnp.zeros((D_MODEL, LANES - GLA_GATE_RANK), F32)
    w_gla = jnp.concatenate([cols("gq"), cols("gk"), cols("gv"), cols("gr"), cols("glr"), pad], axis=1)
    w2 = jnp.concatenate([gla_gate_w2[0], jnp.zeros((LANES - GLA_GATE_RANK, GLA_DK), F32)], axis=0)
    return dict(
        w_gla=w_gla.astype(BF16), w_dq=cols("dq").astype(BF16),
        w_kv=jnp.concatenate([cols("dk"), cols("dv")], axis=1).astype(BF16),
        w_gates=jnp.concatenate([cols("ga"), cols("gb")], axis=1).astype(BF16),
        w2=w2.astype(BF16), pa=proj_a[0].astype(BF16), pb=proj_b[0].astype(BF16), wo=w_out[0].astype(BF16),
        wg=w_ffn_gate[0].astype(BF16), wu=w_ffn_up[0].astype(BF16), wd=w_ffn_down[0].astype(BF16))


def _project(x2d, norm1_g, wts, tm):
    return _in_proj(x2d, norm1_g, [wts["w_gla"], wts["w_dq"], wts["w_kv"], wts["w_gates"]],
                    [BF16, BF16, F32, BF16], tm)


def _tail(x2d, o_a, o_b, z_gates, wts, norm2_g, norm_f_g, tm):
    x1 = _merge_out(x2d, o_a, o_b, z_gates, wts["pa"], wts["pb"], wts["wo"], tm)
    return _ffn(x1, norm2_g, norm_f_g.reshape(1, D_MODEL), wts["wg"], wts["wu"], wts["wd"], tm)


def kernel(x_prompt, x_sample, state_gla, state_win0_k, state_win0_v, state_win1_k, state_win1_v,
           state_win2_k, state_win2_v, norm1_g, w_in, gla_gate_w2, gla_gate_b, gla_norm_g,
           proj_a, proj_b, w_out, norm2_g, w_ffn_gate, w_ffn_up, w_ffn_down, norm_f_g):
    wts = _prep_weights(w_in, gla_gate_w2, proj_a, proj_b, w_out, w_ffn_gate, w_ffn_up, w_ffn_down)
    bp, seq, _ = x_prompt.shape
    bs, n_new, _ = x_sample.shape
    width = DIL_HEADS * DIL_HD

    xp = x_prompt.reshape(bp * seq, D_MODEL)
    z_gla, z_dq, z_kv, z_gates = _project(xp, norm1_g, wts, 512)
    s0 = jnp.zeros((bp, GLA_HEADS, GLA_HDK, GLA_HDV), F32)
    o_a, gla_p = _gla(z_gla.reshape(bp, seq, GLA_COLS), wts["w2"], gla_gate_b, gla_norm_g, s0,
                      GLA_CHUNK, GLA_CHUNK)
    kv_p = z_kv.reshape(bp, seq, 2 * DIL_WIDTH)
    o_b = _dil_prompt(z_dq.reshape(bp, seq, DIL_WIDTH), kv_p)
    y_prompt = _tail(xp, o_a.reshape(bp * seq, GLA_DV), o_b.reshape(bp * seq, DIL_OUT), z_gates, wts,
                     norm2_g, norm_f_g, 512).reshape(bp, seq, D_MODEL)
    win_p = []
    for g in range(DIL_GROUPS):
        keep = min(DIL_WINDOWS[g], seq)
        for off in (0, DIL_WIDTH):
            c0 = off + g * width
            win_p.append(kv_p[:, seq - keep:, c0:c0 + width].reshape(1, bp, keep, DIL_HEADS, DIL_HD))

    xs = jnp.pad(x_sample, ((0, 0), (0, SAMPLE_PAD - n_new), (0, 0))).reshape(bs * SAMPLE_PAD, D_MODEL)
    zs_gla, zs_dq, zs_kv, zs_gates = _project(xs, norm1_g, wts, bs * SAMPLE_PAD)
    os_a, gla_s = _gla(zs_gla.reshape(bs, SAMPLE_PAD, GLA_COLS), wts["w2"], gla_gate_b, gla_norm_g,
                       state_gla[0], SAMPLE_PAD, n_new)
    bufs = [jnp.transpose(a[0], (0, 2, 3, 1)).reshape(bs, width, a.shape[2]) for a in
            (state_win0_k, state_win0_v, state_win1_k, state_win1_v, state_win2_k, state_win2_v)]
    outs = _dil_sample(zs_dq.reshape(bs, SAMPLE_PAD, DIL_WIDTH), zs_kv.reshape(bs, SAMPLE_PAD, 2 * DIL_WIDTH),
                       bufs, n_new)
    os_b, win_s = outs[0], outs[1:]
    ys = _tail(xs, os_a.reshape(bs * SAMPLE_PAD, GLA_DV), os_b.reshape(bs * SAMPLE_PAD, DIL_OUT), zs_gates, wts,
               norm2_g, norm_f_g, bs * SAMPLE_PAD)
    y_sample = ys.reshape(bs, SAMPLE_PAD, D_MODEL)[:, :n_new]
    win_s = [jnp.transpose(a.reshape(bs, DIL_HEADS, DIL_HD, a.shape[2]), (0, 3, 1, 2))[None] for a in win_s]

    return (y_prompt, y_sample, gla_p[None], *win_p, gla_s[None], *win_s)
```

```python
import functools

import jax
import jax.numpy as jnp
from jax import lax
from jax.experimental import pallas as pl
from jax.experimental.pallas import tpu as pltpu

F32 = jnp.float32
BF16 = jnp.bfloat16

D_MODEL = 1024
GLA_HEADS = 4
GLA_DK = 512
GLA_DV = 1024
GLA_HDK = 128
GLA_HDV = 256
GLA_GATE_RANK = 16
GLA_TAU = 16.0
DIL_WINDOWS = (128, 512, 2048)
DIL_RATES = (1, 4, 16)
DIL_GROUPS = 3
DIL_HEADS = 4
DIL_HD = 64
DIL_WIDTH = 768
DIL_OUT = 256
ALIBI_MAX = 8.0
D_FF = 2816
RMS_EPS = 1e-6

LANES = 128
Q_TILE = 128
TILES_PER_BODY = 3
RESIDUES_PER_BODY = 4
GLA_CHUNK = 128
GLA_ROWS_PROMPT = 2
GLA_ROWS_SAMPLE = 4
SAMPLE_PAD = 16
MASK_VALUE = -1e30
VMEM_LIMIT_BYTES = 48 * 1024 * 1024
IN_PROJ_VMEM_LIMIT_BYTES = 56 * 1024 * 1024
_LN2 = 0.6931471805599453

_OFF = {}
_o = 0
for _name, _w in (("gq", GLA_DK), ("gk", GLA_DK), ("gv", GLA_DV), ("gr", GLA_DV), ("glr", GLA_GATE_RANK),
                  ("dq", DIL_WIDTH), ("dk", DIL_WIDTH), ("dv", DIL_WIDTH), ("ga", D_MODEL), ("gb", D_MODEL)):
    _OFF[_name] = (_o, _o + _w)
    _o += _w
GLA_COLS = 2 * GLA_DK + 2 * GLA_DV + LANES


def _params(*sem):
    return pltpu.CompilerParams(dimension_semantics=sem, vmem_limit_bytes=VMEM_LIMIT_BYTES)


def _dot(a, b):
    return jnp.dot(a, b, preferred_element_type=F32)


def _dot_nt(a, b):
    return lax.dot_general(a, b, (((1,), (1,)), ((), ())), preferred_element_type=F32)


def _dot_tn(a, b):
    return lax.dot_general(a, b, (((0,), (0,)), ((), ())), preferred_element_type=F32)


def _rms(x, g):
    return x * lax.rsqrt(jnp.mean(x * x, axis=-1, keepdims=True) + RMS_EPS) * g


IN_PROJ_COL_CHUNK = 1024


def _in_proj_kernel(x_ref, g_ref, *refs):
    n_out = len(refs) // 2
    h = _rms(x_ref[...], g_ref[...]).astype(BF16)
    for w_ref, o_ref in zip(refs[:n_out], refs[n_out:]):
        n = w_ref.shape[1]
        for c0 in range(0, n, IN_PROJ_COL_CHUNK):
            cs = slice(c0, min(c0 + IN_PROJ_COL_CHUNK, n))
            o_ref[:, cs] = _dot(h, w_ref[:, cs]).astype(o_ref.dtype)


def _in_proj(x, g, weights, out_dtypes, tm):
    m, k = x.shape
    resident = lambda a: pl.BlockSpec(a.shape, lambda i: (0, 0), pipeline_mode=pl.Buffered(1))
    return pl.pallas_call(
        _in_proj_kernel,
        grid=(m // tm,),
        in_specs=[pl.BlockSpec((tm, k), lambda i: (i, 0)), resident(g)] + [resident(w) for w in weights],
        out_specs=[pl.BlockSpec((tm, w.shape[1]), lambda i: (i, 0)) for w in weights],
        out_shape=[jax.ShapeDtypeStruct((m, w.shape[1]), dt) for w, dt in zip(weights, out_dtypes)],
        compiler_params=pltpu.CompilerParams(dimension_semantics=("parallel",),
                                             vmem_limit_bytes=IN_PROJ_VMEM_LIMIT_BYTES),
        name="in_proj",
    )(x, g, *weights)


def _split3(x):
    x1 = x.astype(BF16)
    r1 = x - x1.astype(F32)
    x2 = r1.astype(BF16)
    x3 = (r1 - x2.astype(F32)).astype(BF16)
    return x1, x2, x3


def _gla_kernel(q_ref, k_ref, v_ref, r_ref, lr_ref, w2_ref, gb_ref, ng_ref, s0_ref,
                o_ref, sout_ref, s_ref, *, chunk, n_valid):
    c = pl.program_id(1)

    @pl.when(c == 0)
    def _():
        s_ref[...] = s0_ref[...]

    for r in range(q_ref.shape[0]):
        _gla_chunk(r, q_ref, k_ref, v_ref, r_ref, lr_ref, w2_ref, gb_ref, ng_ref, o_ref, s_ref, chunk, n_valid)

    @pl.when(c == pl.num_programs(1) - 1)
    def _():
        sout_ref[...] = s_ref[...]


def _gla_chunk(r, q_ref, k_ref, v_ref, r_ref, lr_ref, w2_ref, gb_ref, ng_ref, o_ref, s_ref, chunk, n_valid):
    gate = _dot(lr_ref[r], w2_ref[...]) + gb_ref[...]
    log_a = (jnp.minimum(gate, 0.0) - jnp.log(1.0 + jnp.exp(-jnp.abs(gate)))) * (1.0 / GLA_TAU)
    row = lax.broadcasted_iota(jnp.int32, (chunk, chunk), 0)
    col = lax.broadcasted_iota(jnp.int32, (chunk, chunk), 1)
    causal = row >= col
    if n_valid < chunk:
        tok = lax.broadcasted_iota(jnp.int32, log_a.shape, 0)
        log_a = jnp.where(tok < n_valid, log_a, 0.0)
    parts = _split3(log_a)
    tril = jnp.where(causal, 1.0, 0.0).astype(BF16)
    b = sum(_dot(tril, p) for p in parts)

    for h in range(GLA_HEADS):
        ks = slice(h * GLA_HDK, (h + 1) * GLA_HDK)
        vs = slice(h * GLA_HDV, (h + 1) * GLA_HDV)
        bh = b[:, ks]
        qh = q_ref[r, :, ks].astype(F32) * (GLA_HDK ** -0.5)
        kh = k_ref[r, :, ks].astype(F32)
        vh = v_ref[r, :, vs]
        if n_valid < chunk:
            tokv = lax.broadcasted_iota(jnp.int32, vh.shape, 0)
            vh = jnp.where(tokv < n_valid, vh, jnp.zeros_like(vh))
        qt = (qh * jnp.exp(bh)).astype(BF16)
        kt = (kh * jnp.exp(-bh)).astype(BF16)
        kd = (kh * jnp.exp(bh[chunk - 1:chunk, :] - bh)).astype(BF16)
        s_old = s_ref[r, h]
        scores = jnp.where(causal, _dot_nt(qt, kt), 0.0).astype(BF16)
        o = _dot(qt, s_old.astype(BF16)) + _dot(scores, vh)
        dec = jnp.exp(jnp.broadcast_to(bh[chunk - 1:chunk, :], (GLA_HDK, GLA_HDK)).T)
        s_ref[r, h] = s_old * jnp.concatenate([dec, dec], axis=1) + _dot_tn(kd, vh)
        on = _rms(o, ng_ref[...])
        gr = r_ref[r, :, vs].astype(F32)
        o_ref[r, :, vs] = (on * (gr * jax.nn.sigmoid(gr))).astype(o_ref.dtype)


def _gla(z, w2, gate_b, norm_g, s0, chunk, n_valid, rows):
    bsz, t, _ = z.shape
    kern = functools.partial(_gla_kernel, chunk=chunk, n_valid=n_valid)
    state = pl.BlockSpec((rows, GLA_HEADS, GLA_HDK, GLA_HDV), lambda b, c: (b, 0, 0, 0))
    return pl.pallas_call(
        kern,
        grid=(bsz // rows, t // chunk),
        in_specs=[pl.BlockSpec((rows, chunk, GLA_DK), lambda b, c: (b, c, 0)),
                  pl.BlockSpec((rows, chunk, GLA_DK), lambda b, c: (b, c, 1)),
                  pl.BlockSpec((rows, chunk, GLA_DV), lambda b, c: (b, c, 1)),
                  pl.BlockSpec((rows, chunk, GLA_DV), lambda b, c: (b, c, 2)),
                  pl.BlockSpec((rows, chunk, LANES), lambda b, c: (b, c, (2 * GLA_DK + 2 * GLA_DV) // LANES)),
                  pl.BlockSpec((LANES, GLA_DK), lambda b, c: (0, 0)),
                  pl.BlockSpec((1, GLA_DK), lambda b, c: (0, 0)),
                  pl.BlockSpec((1, GLA_HDV), lambda b, c: (0, 0)),
                  state],
        out_specs=[pl.BlockSpec((rows, chunk, GLA_DV), lambda b, c: (b, c, 0)), state],
        out_shape=[jax.ShapeDtypeStruct((bsz, t, GLA_DV), BF16),
                   jax.ShapeDtypeStruct((bsz, GLA_HEADS, GLA_HDK, GLA_HDV), F32)],
        scratch_shapes=[pltpu.VMEM((rows, GLA_HEADS, GLA_HDK, GLA_HDV), F32)],
        compiler_params=_params("parallel", "arbitrary"),
        name="gla",
    )(z, z, z, z, z, w2, gate_b, norm_g, s0)


def _alibi_slope(g, head):
    n = DIL_GROUPS * DIL_HEADS
    return jnp.exp((-ALIBI_MAX * _LN2 / n) * (head + (g * DIL_HEADS + 1.0)))


def _pair_slopes(g, hp, rows):
    head = 2.0 * hp.astype(F32) + jnp.where(rows >= Q_TILE, 1.0, 0.0)
    return _alibi_slope(g, head)


def _dil_prompt_kernel(q0_ref, q1_ref, q2_ref, k0_ref, k1_ref, k2_ref, v0_ref, v1_ref, v2_ref,
                       o_ref, qf_ref, og_ref, ld_ref, bf_ref, br_ref, *, seq):
    hp = pl.program_id(1)
    q_refs = (q0_ref, q1_ref, q2_ref)
    k_refs = (k0_ref, k1_ref, k2_ref)
    v_refs = (v0_ref, v1_ref, v2_ref)

    for g in range(DIL_GROUPS):
        qf_ref[g] = q_refs[g][...].astype(F32) * (DIL_HD ** -0.5)

    for g in range(DIL_GROUPS):
        rate = float(DIL_RATES[g])
        rows = lax.broadcasted_iota(jnp.int32, (2 * Q_TILE, Q_TILE), 0)
        cols = lax.broadcasted_iota(jnp.int32, (2 * Q_TILE, Q_TILE), 1)
        dist = (rows & (Q_TILE - 1)) - cols
        slope = _pair_slopes(g, hp, rows)
        bf_ref[g] = jnp.where(dist >= 0, -slope * rate * dist.astype(F32), MASK_VALUE)
        if g < 2:
            rows = lax.broadcasted_iota(jnp.int32, (2 * Q_TILE, 2 * Q_TILE), 0)
            cols = lax.broadcasted_iota(jnp.int32, (2 * Q_TILE, 2 * Q_TILE), 1)
            dist = (rows & (Q_TILE - 1)) + Q_TILE - cols
            slope = _pair_slopes(g, hp, rows)
            ok = jnp.abs(dist - Q_TILE // 2) <= Q_TILE // 2
            br_ref[g] = jnp.where(ok, -slope * rate * dist.astype(F32), MASK_VALUE)

    lane = lax.broadcasted_iota(jnp.int32, (Q_TILE, LANES), 1)
    lo = lane < DIL_HD

    def rows_of(start, size, rate):
        return pl.ds(start, size) if rate == 1 else pl.ds(start, size, stride=rate)

    def tile(g, q_row, k_row, n_keys, bias):
        rate = DIL_RATES[g]
        q2 = qf_ref[g, rows_of(q_row, Q_TILE, rate), :]
        k2 = k_refs[g][rows_of(k_row, n_keys, rate), :].astype(BF16)
        v2 = v_refs[g][rows_of(k_row, n_keys, rate), :]
        lo_k = lax.broadcasted_iota(jnp.int32, (n_keys, LANES), 1) < DIL_HD
        va = jnp.where(lo_k, v2, 1.0).astype(BF16)
        vb = jnp.where(lo_k, 1.0, v2).astype(BF16)
        qs = jnp.concatenate([jnp.where(lo, q2, 0.0), jnp.where(lo, 0.0, q2)], axis=0).astype(BF16)
        s = _dot_nt(qs, k2) + bias
        m = jnp.max(s, axis=-1, keepdims=True)
        p = jnp.exp(s - m).astype(BF16)
        ra = _dot(p[:Q_TILE], va)
        rb = _dot(p[Q_TILE:], vb)
        o2 = jnp.where(lo, ra, rb)
        l2 = pltpu.roll(jnp.where(lo, rb, ra), DIL_HD, 1)
        m2 = jnp.where(lo, m[:Q_TILE], m[Q_TILE:])
        og_ref[g, rows_of(q_row, Q_TILE, rate), :] = o2 / l2
        ld_ref[g, rows_of(q_row, Q_TILE, rate), :] = m2 + jnp.log(l2)

    for g in range(DIL_GROUPS):
        rate = DIL_RATES[g]
        n_tiles = seq // rate // Q_TILE
        span = rate * Q_TILE
        group = min(rate, RESIDUES_PER_BODY)
        n_groups = rate // group

        def first_tiles(i, carry, g=g, group=group):
            for u in range(group):
                rho = i * group + u
                tile(g, rho, rho, Q_TILE, bf_ref[g])
            return carry

        def consecutive_tiles(i, carry, g=g, span=span):
            for u in range(TILES_PER_BODY):
                n = 1 + i * TILES_PER_BODY + u
                tile(g, n * span, (n - 1) * span, 2 * Q_TILE, br_ref[g])
            return carry

        def later_tiles(i, carry, g=g, group=group, n_groups=n_groups, span=span):
            n = 1 + i // n_groups
            for u in range(group):
                rho = (i % n_groups) * group + u
                tile(g, rho + n * span, rho + (n - 1) * span, 2 * Q_TILE, br_ref[g])
            return carry

        if n_groups == 1:
            first_tiles(0, 0)
        else:
            lax.fori_loop(0, n_groups, first_tiles, 0)
        if n_tiles > 1:
            if rate == 1:
                assert (n_tiles - 1) % TILES_PER_BODY == 0
                lax.fori_loop(0, (n_tiles - 1) // TILES_PER_BODY, consecutive_tiles, 0)
            else:
                lax.fori_loop(0, (n_tiles - 1) * n_groups, later_tiles, 0)

    def combine(i, carry):
        rs = pl.ds(pl.multiple_of(i * Q_TILE, Q_TILE), Q_TILE)
        ld = [ld_ref[g, rs, :] for g in range(DIL_GROUPS)]
        top = jnp.maximum(jnp.maximum(ld[0], ld[1]), ld[2])
        w = [jnp.exp(x - top) for x in ld]
        num = sum(w[g] * og_ref[g, rs, :] for g in range(DIL_GROUPS))
        o_ref[rs, :] = (num / (w[0] + w[1] + w[2])).astype(o_ref.dtype)
        return carry

    lax.fori_loop(0, seq // Q_TILE, combine, 0)


def _dil_prompt(dq, kv):
    bsz, seq, _ = dq.shape
    pairs = DIL_HEADS * DIL_HD // LANES
    nblk = DIL_WIDTH // LANES

    def spec(col0):
        return pl.BlockSpec((None, seq, LANES), lambda b, hp, col0=col0: (b, 0, col0 + hp))

    in_specs = ([spec(g * pairs) for g in range(DIL_GROUPS)]
                + [spec(g * pairs) for g in range(DIL_GROUPS)]
                + [spec(nblk + g * pairs) for g in range(DIL_GROUPS)])
    return pl.pallas_call(
        functools.partial(_dil_prompt_kernel, seq=seq),
        grid=(bsz, pairs),
        in_specs=in_specs,
        out_specs=pl.BlockSpec((None, seq, LANES), lambda b, hp: (b, 0, hp)),
        out_shape=jax.ShapeDtypeStruct((bsz, seq, DIL_OUT), BF16),
        scratch_shapes=[pltpu.VMEM((DIL_GROUPS, seq, LANES), F32),
                        pltpu.VMEM((DIL_GROUPS, seq, LANES), F32),
                        pltpu.VMEM((DIL_GROUPS, seq, LANES), F32),
                        pltpu.VMEM((DIL_GROUPS, 2 * Q_TILE, Q_TILE), F32),
                        pltpu.VMEM((2, 2 * Q_TILE, 2 * Q_TILE), F32)],
        compiler_params=_params("parallel", "arbitrary"),
        name="dil_prompt",
    )(dq, dq, dq, kv, kv, kv, kv, kv, kv)


def _dil_sample_kernel(q_ref, kvn_ref, kb0_ref, vb0_ref, kb1_ref, vb1_ref, kb2_ref, vb2_ref,
                       o_ref, nk0_ref, nv0_ref, nk1_ref, nv1_ref, nk2_ref, nv2_ref, *, n_new):
    kb_refs = (kb0_ref, kb1_ref, kb2_ref)
    vb_refs = (vb0_ref, vb1_ref, vb2_ref)
    nk_refs = (nk0_ref, nk1_ref, nk2_ref)
    nv_refs = (nv0_ref, nv1_ref, nv2_ref)
    pad = q_ref.shape[1]
    nrow = DIL_HEADS * pad
    width = DIL_HEADS * DIL_HD

    lane_head = jnp.right_shift(lax.broadcasted_iota(jnp.int32, (pad, width), 1), DIL_HD.bit_length() - 1)

    def bias_for(g, dist):
        rate = DIL_RATES[g]
        rows = lax.broadcasted_iota(jnp.int32, dist.shape, 0)
        head = sum(jnp.where(rows >= h * pad, 1.0, 0.0) for h in range(1, DIL_HEADS))
        ok = (dist >= 0) & (dist <= DIL_WINDOWS[g]) & ((dist & (rate - 1)) == 0)
        return jnp.where(ok, -_alibi_slope(g, head) * dist.astype(F32), MASK_VALUE)

    sel_r = lax.broadcasted_iota(jnp.int32, (pad, LANES), 0)
    sel_c = lax.broadcasted_iota(jnp.int32, (pad, LANES), 1)
    place = jnp.where((sel_r < n_new) & (sel_c == sel_r + (LANES - n_new)), 1.0, 0.0).astype(BF16)
    tail_lane = lax.broadcasted_iota(jnp.int32, (width, LANES), 1) >= LANES - n_new

    def shifted(buf_t, new_rows):
        length = buf_t.shape[1]
        rolled = pltpu.roll(buf_t, length - n_new, 1)
        new_t = sum(_dot_tn(part, place) for part in _split3(new_rows))
        last = jnp.where(tail_lane, new_t, rolled[:, length - LANES:])
        return rolled, last

    scores, values, transposed = [], [], []
    for g in range(DIL_GROUPS):
        length = kb_refs[g].shape[2]
        cs = slice(g * width, (g + 1) * width)
        vcs = slice(DIL_WIDTH + g * width, DIL_WIDTH + (g + 1) * width)
        qg = q_ref[0, :, cs].astype(F32) * (DIL_HD ** -0.5)
        qs = jnp.concatenate([jnp.where(lane_head == h, qg, 0.0) for h in range(DIL_HEADS)], axis=0).astype(BF16)
        kb, vb = kb_refs[g][0], vb_refs[g][0]
        kn, vn = kvn_ref[0, :, cs], kvn_ref[0, :, vcs]
        rows = lax.broadcasted_iota(jnp.int32, (nrow, length), 0)
        cols = lax.broadcasted_iota(jnp.int32, (nrow, length), 1)
        dist = length + (rows & (pad - 1)) - cols
        scores.append(_dot(qs, kb.astype(BF16)) + bias_for(g, dist))
        values.append(vb.astype(BF16))
        transposed.append(True)
        rows = lax.broadcasted_iota(jnp.int32, (nrow, pad), 0)
        cols = lax.broadcasted_iota(jnp.int32, (nrow, pad), 1)
        dist = jnp.where(cols < n_new, (rows & (pad - 1)) - cols, -1)
        scores.append(_dot_nt(qs, kn.astype(BF16)) + bias_for(g, dist))
        values.append(vn.astype(BF16))
        transposed.append(False)
        for buf, new, out_ref in ((kb, kn, nk_refs[g]), (vb, vn, nv_refs[g])):
            rolled, last = shifted(buf, new)
            if length > LANES:
                out_ref[0, :, 0:length - LANES] = rolled[:, 0:length - LANES]
            out_ref[0, :, length - LANES:length] = last

    top = functools.reduce(jnp.maximum, [jnp.max(s, axis=-1, keepdims=True) for s in scores])
    probs = [jnp.exp(s - top) for s in scores]
    den = sum(jnp.sum(p, axis=-1, keepdims=True) for p in probs)
    acc = sum((_dot_nt if t else _dot)(p.astype(BF16), v)
              for p, v, t in zip(probs, values, transposed)) / den
    out = sum(jnp.where(lane_head == h, acc[h * pad:(h + 1) * pad], 0.0) for h in range(DIL_HEADS))
    o_ref[0] = out.astype(o_ref.dtype)


def _dil_sample(dq, kvn, bufs, n_new):
    bsz, pad, _ = dq.shape
    full = lambda a: pl.BlockSpec((1,) + a.shape[1:], lambda b: (b, 0, 0))
    out_shape = ([jax.ShapeDtypeStruct((bsz, pad, DIL_OUT), BF16)]
                 + [jax.ShapeDtypeStruct(a.shape, a.dtype) for a in bufs])
    return pl.pallas_call(
        functools.partial(_dil_sample_kernel, n_new=n_new),
        grid=(bsz,),
        in_specs=[full(dq), full(kvn)] + [full(a) for a in bufs],
        out_specs=[pl.BlockSpec((1, pad, DIL_OUT), lambda b: (b, 0, 0))] + [full(a) for a in bufs],
        out_shape=out_shape,
        compiler_params=_params("parallel"),
        name="dil_sample",
    )(dq, kvn, *bufs)


def _merge_out_kernel(x_ref, oa_ref, ob_ref, ga_ref, gb_ref, pa_ref, pb_ref, wo_ref, o_ref):
    pa = _dot(oa_ref[...], pa_ref[...])
    pb = _dot(ob_ref[...], pb_ref[...])
    merged = jax.nn.sigmoid(ga_ref[...].astype(F32)) * pa + jax.nn.sigmoid(gb_ref[...].astype(F32)) * pb
    o_ref[...] = x_ref[...] + _dot(merged.astype(BF16), wo_ref[...])


def _merge_out(x, oa, ob, gates, pa, pb, wo, tm):
    m = x.shape[0]
    row = lambda w: pl.BlockSpec((tm, w), lambda i: (i, 0))
    whole = lambda a: pl.BlockSpec(a.shape, lambda i: (0, 0))
    return pl.pallas_call(
        _merge_out_kernel,
        grid=(m // tm,),
        in_specs=[row(D_MODEL), row(GLA_DV), row(DIL_OUT), row(D_MODEL),
                  pl.BlockSpec((tm, D_MODEL), lambda i: (i, 1)),
                  whole(pa), whole(pb), whole(wo)],
        out_specs=row(D_MODEL),
        out_shape=jax.ShapeDtypeStruct((m, D_MODEL), F32),
        compiler_params=_params("parallel"),
        name="merge_out",
    )(x, oa, ob, gates, gates, pa, pb, wo)


FFN_COL_CHUNK = D_FF // 2


def _ffn_kernel(x_ref, n2_ref, nf_ref, wg_ref, wu_ref, wd_ref, y_ref):
    x = x_ref[...]
    h = _rms(x, n2_ref[...]).astype(BF16)
    acc = x
    for c0 in range(0, D_FF, FFN_COL_CHUNK):
        cs = slice(c0, c0 + FFN_COL_CHUNK)
        gate = _dot(h, wg_ref[:, cs])
        up = _dot(h, wu_ref[:, cs])
        act = (gate * jax.nn.sigmoid(gate) * up).astype(BF16)
        acc = acc + _dot(act, wd_ref[cs, :])
    y_ref[...] = _rms(acc, nf_ref[...])


def _ffn(x, n2, nf, wg, wu, wd, tm):
    m = x.shape[0]
    resident = lambda a: pl.BlockSpec(a.shape, lambda i: (0, 0), pipeline_mode=pl.Buffered(1))
    return pl.pallas_call(
        _ffn_kernel,
        grid=(m // tm,),
        in_specs=[pl.BlockSpec((tm, D_MODEL), lambda i: (i, 0)),
                  resident(n2), resident(nf), resident(wg), resident(wu), resident(wd)],
        out_specs=pl.BlockSpec((tm, D_MODEL), lambda i: (i, 0)),
        out_shape=jax.ShapeDtypeStruct((m, D_MODEL), F32),
        compiler_params=_params("parallel"),
        name="ffn",
    )(x, n2, nf, wg, wu, wd)


def _prep_weights(w_in, gla_gate_w2, proj_a, proj_b, w_out, w_ffn_gate, w_ffn_up, w_ffn_down):
    w = w_in[0]
    cols = lambda name: w[:, _OFF[name][0]:_OFF[name][1]]
    pad = jnp.zeros((D_MODEL, LANES - GLA_GATE_RANK), F32)
    w_gla = jnp.concatenate([cols("gq"), cols("gk"), cols("gv"), cols("gr"), cols("glr"), pad], axis=1)
    w2 = jnp.concatenate([gla_gate_w2[0], jnp.zeros((LANES - GLA_GATE_RANK, GLA_DK), F32)], axis=0)
    return dict(
        w_gla=w_gla.astype(BF16), w_dq=cols("dq").astype(BF16),
        w_kv=jnp.concatenate([cols("dk"), cols("dv")], axis=1).astype(BF16),
        w_gates=jnp.concatenate([cols("ga"), cols("gb")], axis=1).astype(BF16),
        w2=w2.astype(BF16), pa=proj_a[0].astype(BF16), pb=proj_b[0].astype(BF16), wo=w_out[0].astype(BF16),
        wg=w_ffn_gate[0].astype(BF16), wu=w_ffn_up[0].astype(BF16), wd=w_ffn_down[0].astype(BF16))


def _project(x2d, norm1_g, wts, tm):
    return _in_proj(x2d, norm1_g, [wts["w_gla"], wts["w_dq"], wts["w_kv"], wts["w_gates"]],
                    [BF16, BF16, F32, BF16], tm)


def _tail(x2d, o_a, o_b, z_gates, wts, norm2_g, norm_f_g, tm):
    x1 = _merge_out(x2d, o_a, o_b, z_gates, wts["pa"], wts["pb"], wts["wo"], tm)
    return _ffn(x1, norm2_g, norm_f_g.reshape(1, D_MODEL), wts["wg"], wts["wu"], wts["wd"], tm)


def kernel(x_prompt, x_sample, state_gla, state_win0_k, state_win0_v, state_win1_k, state_win1_v,
           state_win2_k, state_win2_v, norm1_g, w_in, gla_gate_w2, gla_gate_b, gla_norm_g,
           proj_a, proj_b, w_out, norm2_g, w_ffn_gate, w_ffn_up, w_ffn_down, norm_f_g):
    wts = _prep_weights(w_in, gla_gate_w2, proj_a, proj_b, w_out, w_ffn_gate, w_ffn_up, w_ffn_down)
    bp, seq, _ = x_prompt.shape
    bs, n_new, _ = x_sample.shape
    width = DIL_HEADS * DIL_HD

    xp = x_prompt.reshape(bp * seq, D_MODEL)
    z_gla, z_dq, z_kv, z_gates = _project(xp, norm1_g, wts, 512)
    s0 = jnp.zeros((bp, GLA_HEADS, GLA_HDK, GLA_HDV), F32)
    o_a, gla_p = _gla(z_gla.reshape(bp, seq, GLA_COLS), wts["w2"], gla_gate_b, gla_norm_g, s0,
                      GLA_CHUNK, GLA_CHUNK, GLA_ROWS_PROMPT)
    kv_p = z_kv.reshape(bp, seq, 2 * DIL_WIDTH)
    o_b = _dil_prompt(z_dq.reshape(bp, seq, DIL_WIDTH), kv_p)
    y_prompt = _tail(xp, o_a.reshape(bp * seq, GLA_DV), o_b.reshape(bp * seq, DIL_OUT), z_gates, wts,
                     norm2_g, norm_f_g, 512).reshape(bp, seq, D_MODEL)
    win_p = []
    for g in range(DIL_GROUPS):
        keep = min(DIL_WINDOWS[g], seq)
        for off in (0, DIL_WIDTH):
            c0 = off + g * width
            win_p.append(kv_p[:, seq - keep:, c0:c0 + width].reshape(1, bp, keep, DIL_HEADS, DIL_HD))

    xs = jnp.pad(x_sample, ((0, 0), (0, SAMPLE_PAD - n_new), (0, 0))).reshape(bs * SAMPLE_PAD, D_MODEL)
    zs_gla, zs_dq, zs_kv, zs_gates = _project(xs, norm1_g, wts, bs * SAMPLE_PAD)
    os_a, gla_s = _gla(zs_gla.reshape(bs, SAMPLE_PAD, GLA_COLS), wts["w2"], gla_gate_b, gla_norm_g,
                       state_gla[0], SAMPLE_PAD, n_new, GLA_ROWS_SAMPLE)
    bufs = [jnp.transpose(a[0], (0, 2, 3, 1)).reshape(bs, width, a.shape[2]) for a in
            (state_win0_k, state_win0_v, state_win1_k, state_win1_v, state_win2_k, state_win2_v)]
    outs = _dil_sample(zs_dq.reshape(bs, SAMPLE_PAD, DIL_WIDTH), zs_kv.reshape(bs, SAMPLE_PAD, 2 * DIL_WIDTH),
                       bufs, n_new)
    os_b, win_s = outs[0], outs[1:]
    ys = _tail(xs, os_a.reshape(bs * SAMPLE_PAD, GLA_DV), os_b.reshape(bs * SAMPLE_PAD, DIL_OUT), zs_gates, wts,
               norm2_g, norm_f_g, bs * SAMPLE_PAD)
    y_sample = ys.reshape(bs, SAMPLE_PAD, D_MODEL)[:, :n_new]
    win_s = [jnp.transpose(a.reshape(bs, DIL_HEADS, DIL_HD, a.shape[2]), (0, 3, 1, 2))[None] for a in win_s]

    return (y_prompt, y_sample, gla_p[None], *win_p, gla_s[None], *win_s)
```

```python
import functools

import jax
import jax.numpy as jnp
from jax import lax
from jax.experimental import pallas as pl
from jax.experimental.pallas import tpu as pltpu

F32 = jnp.float32
BF16 = jnp.bfloat16

D_MODEL = 1024
GLA_HEADS = 4
GLA_DK = 512
GLA_DV = 1024
GLA_HDK = 128
GLA_HDV = 256
GLA_GATE_RANK = 16
GLA_TAU = 16.0
DIL_WINDOWS = (128, 512, 2048)
DIL_RATES = (1, 4, 16)
DIL_GROUPS = 3
DIL_HEADS = 4
DIL_HD = 64
DIL_WIDTH = 768
DIL_OUT = 256
ALIBI_MAX = 8.0
D_FF = 2816
RMS_EPS = 1e-6

LANES = 128
Q_TILE = 128
TILES_PER_BODY = 5
RESIDUES_PER_BODY = 8
TILE_SLOTS = max(TILES_PER_BODY, RESIDUES_PER_BODY)
GLA_CHUNK = 128
GLA_ROWS_PROMPT = 2
GLA_ROWS_SAMPLE = 4
SAMPLE_PAD = 16
MASK_VALUE = -1e30
VMEM_LIMIT_BYTES = 48 * 1024 * 1024
IN_PROJ_VMEM_LIMIT_BYTES = 56 * 1024 * 1024
_LN2 = 0.6931471805599453

_OFF = {}
_o = 0
for _name, _w in (("gq", GLA_DK), ("gk", GLA_DK), ("gv", GLA_DV), ("gr", GLA_DV), ("glr", GLA_GATE_RANK),
                  ("dq", DIL_WIDTH), ("dk", DIL_WIDTH), ("dv", DIL_WIDTH), ("ga", D_MODEL), ("gb", D_MODEL)):
    _OFF[_name] = (_o, _o + _w)
    _o += _w
GLA_COLS = 2 * GLA_DK + 2 * GLA_DV + LANES


def _params(*sem):
    return pltpu.CompilerParams(dimension_semantics=sem, vmem_limit_bytes=VMEM_LIMIT_BYTES)


def _dot(a, b):
    return jnp.dot(a, b, preferred_element_type=F32)


def _dot_nt(a, b):
    return lax.dot_general(a, b, (((1,), (1,)), ((), ())), preferred_element_type=F32)


def _dot_tn(a, b):
    return lax.dot_general(a, b, (((0,), (0,)), ((), ())), preferred_element_type=F32)


def _rms(x, g):
    return x * lax.rsqrt(jnp.mean(x * x, axis=-1, keepdims=True) + RMS_EPS) * g


IN_PROJ_COL_CHUNK = 1024


def _in_proj_kernel(x_ref, g_ref, *refs):
    n_out = len(refs) // 2
    h = _rms(x_ref[...], g_ref[...]).astype(BF16)
    for w_ref, o_ref in zip(refs[:n_out], refs[n_out:]):
        n = w_ref.shape[1]
        for c0 in range(0, n, IN_PROJ_COL_CHUNK):
            cs = slice(c0, min(c0 + IN_PROJ_COL_CHUNK, n))
            o_ref[:, cs] = _dot(h, w_ref[:, cs]).astype(o_ref.dtype)


def _in_proj(x, g, weights, out_dtypes, tm):
    m, k = x.shape
    resident = lambda a: pl.BlockSpec(a.shape, lambda i: (0, 0), pipeline_mode=pl.Buffered(1))
    return pl.pallas_call(
        _in_proj_kernel,
        grid=(m // tm,),
        in_specs=[pl.BlockSpec((tm, k), lambda i: (i, 0)), resident(g)] + [resident(w) for w in weights],
        out_specs=[pl.BlockSpec((tm, w.shape[1]), lambda i: (i, 0)) for w in weights],
        out_shape=[jax.ShapeDtypeStruct((m, w.shape[1]), dt) for w, dt in zip(weights, out_dtypes)],
        compiler_params=pltpu.CompilerParams(dimension_semantics=("parallel",),
                                             vmem_limit_bytes=IN_PROJ_VMEM_LIMIT_BYTES),
        name="in_proj",
    )(x, g, *weights)


def _split3(x):
    x1 = x.astype(BF16)
    r1 = x - x1.astype(F32)
    x2 = r1.astype(BF16)
    x3 = (r1 - x2.astype(F32)).astype(BF16)
    return x1, x2, x3


def _gla_kernel(q_ref, k_ref, v_ref, r_ref, lr_ref, w2_ref, gb_ref, ng_ref, s0_ref,
                o_ref, sout_ref, s_ref, *, chunk, n_valid):
    c = pl.program_id(1)

    @pl.when(c == 0)
    def _():
        s_ref[...] = s0_ref[...]

    for r in range(q_ref.shape[0]):
        _gla_chunk(r, q_ref, k_ref, v_ref, r_ref, lr_ref, w2_ref, gb_ref, ng_ref, o_ref, s_ref, chunk, n_valid)

    @pl.when(c == pl.num_programs(1) - 1)
    def _():
        sout_ref[...] = s_ref[...]


def _gla_chunk(r, q_ref, k_ref, v_ref, r_ref, lr_ref, w2_ref, gb_ref, ng_ref, o_ref, s_ref, chunk, n_valid):
    gate = _dot(lr_ref[r], w2_ref[...]) + gb_ref[...]
    log_a = (jnp.minimum(gate, 0.0) - jnp.log(1.0 + jnp.exp(-jnp.abs(gate)))) * (1.0 / GLA_TAU)
    row = lax.broadcasted_iota(jnp.int32, (chunk, chunk), 0)
    col = lax.broadcasted_iota(jnp.int32, (chunk, chunk), 1)
    causal = row >= col
    if n_valid < chunk:
        tok = lax.broadcasted_iota(jnp.int32, log_a.shape, 0)
        log_a = jnp.where(tok < n_valid, log_a, 0.0)
    parts = _split3(log_a)
    tril = jnp.where(causal, 1.0, 0.0).astype(BF16)
    b = sum(_dot(tril, p) for p in parts)

    for h in range(GLA_HEADS):
        ks = slice(h * GLA_HDK, (h + 1) * GLA_HDK)
        vs = slice(h * GLA_HDV, (h + 1) * GLA_HDV)
        bh = b[:, ks]
        qh = q_ref[r, :, ks].astype(F32) * (GLA_HDK ** -0.5)
        kh = k_ref[r, :, ks].astype(F32)
        vh = v_ref[r, :, vs]
        if n_valid < chunk:
            tokv = lax.broadcasted_iota(jnp.int32, vh.shape, 0)
            vh = jnp.where(tokv < n_valid, vh, jnp.zeros_like(vh))
        qt = (qh * jnp.exp(bh)).astype(BF16)
        kt = (kh * jnp.exp(-bh)).astype(BF16)
        kd = (kh * jnp.exp(bh[chunk - 1:chunk, :] - bh)).astype(BF16)
        s_old = s_ref[r, h]
        scores = jnp.where(causal, _dot_nt(qt, kt), 0.0).astype(BF16)
        o = _dot(qt, s_old.astype(BF16)) + _dot(scores, vh)
        dec = jnp.exp(jnp.broadcast_to(bh[chunk - 1:chunk, :], (GLA_HDK, GLA_HDK)).T)
        s_ref[r, h] = s_old * jnp.concatenate([dec, dec], axis=1) + _dot_tn(kd, vh)
        on = _rms(o, ng_ref[...])
        gr = r_ref[r, :, vs].astype(F32)
        o_ref[r, :, vs] = (on * (gr * jax.nn.sigmoid(gr))).astype(o_ref.dtype)


def _gla(z, w2, gate_b, norm_g, s0, chunk, n_valid, rows):
    bsz, t, _ = z.shape
    kern = functools.partial(_gla_kernel, chunk=chunk, n_valid=n_valid)
    state = pl.BlockSpec((rows, GLA_HEADS, GLA_HDK, GLA_HDV), lambda b, c: (b, 0, 0, 0))
    return pl.pallas_call(
        kern,
        grid=(bsz // rows, t // chunk),
        in_specs=[pl.BlockSpec((rows, chunk, GLA_DK), lambda b, c: (b, c, 0)),
                  pl.BlockSpec((rows, chunk, GLA_DK), lambda b, c: (b, c, 1)),
                  pl.BlockSpec((rows, chunk, GLA_DV), lambda b, c: (b, c, 1)),
                  pl.BlockSpec((rows, chunk, GLA_DV), lambda b, c: (b, c, 2)),
                  pl.BlockSpec((rows, chunk, LANES), lambda b, c: (b, c, (2 * GLA_DK + 2 * GLA_DV) // LANES)),
                  pl.BlockSpec((LANES, GLA_DK), lambda b, c: (0, 0)),
                  pl.BlockSpec((1, GLA_DK), lambda b, c: (0, 0)),
                  pl.BlockSpec((1, GLA_HDV), lambda b, c: (0, 0)),
                  state],
        out_specs=[pl.BlockSpec((rows, chunk, GLA_DV), lambda b, c: (b, c, 0)), state],
        out_shape=[jax.ShapeDtypeStruct((bsz, t, GLA_DV), BF16),
                   jax.ShapeDtypeStruct((bsz, GLA_HEADS, GLA_HDK, GLA_HDV), F32)],
        scratch_shapes=[pltpu.VMEM((rows, GLA_HEADS, GLA_HDK, GLA_HDV), F32)],
        compiler_params=_params("parallel", "arbitrary"),
        name="gla",
    )(z, z, z, z, z, w2, gate_b, norm_g, s0)


def _alibi_slope(g, head):
    n = DIL_GROUPS * DIL_HEADS
    return jnp.exp((-ALIBI_MAX * _LN2 / n) * (head + (g * DIL_HEADS + 1.0)))


def _pair_slopes(g, hp, rows):
    head = 2.0 * hp.astype(F32) + jnp.where(rows >= Q_TILE, 1.0, 0.0)
    return _alibi_slope(g, head)


def _dil_prompt_kernel(q0_ref, q1_ref, q2_ref, k0_ref, k1_ref, k2_ref, v0_ref, v1_ref, v2_ref,
                       o_ref, qf_ref, og_ref, ld_ref, bf_ref, br_ref, p_ref, m_ref, *, seq):
    hp = pl.program_id(1)
    q_refs = (q0_ref, q1_ref, q2_ref)
    k_refs = (k0_ref, k1_ref, k2_ref)
    v_refs = (v0_ref, v1_ref, v2_ref)

    for g in range(DIL_GROUPS):
        qf_ref[g] = q_refs[g][...].astype(F32) * (DIL_HD ** -0.5)

    for g in range(DIL_GROUPS):
        rate = float(DIL_RATES[g])
        rows = lax.broadcasted_iota(jnp.int32, (2 * Q_TILE, Q_TILE), 0)
        cols = lax.broadcasted_iota(jnp.int32, (2 * Q_TILE, Q_TILE), 1)
        dist = (rows & (Q_TILE - 1)) - cols
        slope = _pair_slopes(g, hp, rows)
        bf_ref[g] = jnp.where(dist >= 0, -slope * rate * dist.astype(F32), MASK_VALUE)
        if g < 2:
            rows = lax.broadcasted_iota(jnp.int32, (2 * Q_TILE, 2 * Q_TILE), 0)
            cols = lax.broadcasted_iota(jnp.int32, (2 * Q_TILE, 2 * Q_TILE), 1)
            dist = (rows & (Q_TILE - 1)) + Q_TILE - cols
            slope = _pair_slopes(g, hp, rows)
            ok = jnp.abs(dist - Q_TILE // 2) <= Q_TILE // 2
            br_ref[g] = jnp.where(ok, -slope * rate * dist.astype(F32), MASK_VALUE)

    lane = lax.broadcasted_iota(jnp.int32, (Q_TILE, LANES), 1)
    lo = lane < DIL_HD

    def rows_of(start, size, rate):
        return pl.ds(start, size) if rate == 1 else pl.ds(start, size, stride=rate)

    def scores_stage(g, q_row, k_row, n_keys, bias, slot):
        rate = DIL_RATES[g]
        q2 = qf_ref[g, rows_of(q_row, Q_TILE, rate), :]
        k2 = k_refs[g][rows_of(k_row, n_keys, rate), :].astype(BF16)
        qs = jnp.concatenate([jnp.where(lo, q2, 0.0), jnp.where(lo, 0.0, q2)], axis=0).astype(BF16)
        s = _dot_nt(qs, k2) + bias()
        m = jnp.max(s, axis=-1, keepdims=True)
        p_ref[slot, :, 0:n_keys] = jnp.exp(s - m).astype(BF16)
        m_ref[slot] = jnp.where(lo, m[:Q_TILE], m[Q_TILE:])

    def values_stage(g, q_row, k_row, n_keys, slot):
        rate = DIL_RATES[g]
        v2 = v_refs[g][rows_of(k_row, n_keys, rate), :].astype(BF16)
        vo = jnp.concatenate([v2, jnp.ones((n_keys, LANES), BF16)], axis=1)
        r = _dot(p_ref[slot, :, 0:n_keys], vo)
        o2 = jnp.where(lo, r[:Q_TILE, :LANES], r[Q_TILE:, :LANES])
        l2 = jnp.where(lo, r[:Q_TILE, LANES:], r[Q_TILE:, LANES:])
        og_ref[g, rows_of(q_row, Q_TILE, rate), :] = o2 / l2
        ld_ref[g, rows_of(q_row, Q_TILE, rate), :] = m_ref[slot] + jnp.log(l2)

    pending = []
    emitted = [0]

    def run_tile_sets(n_sets, n_per_set, tile_of):
        base = emitted[0]

        def slot(i, u):
            return ((i + base) % 2) * TILE_SLOTS + u

        def scores(i):
            for u in range(n_per_set):
                g, q_row, k_row, n_keys, bias = tile_of(i, u)
                scores_stage(g, q_row, k_row, n_keys, bias, slot(i, u))

        def values(i):
            for u in range(n_per_set):
                g, q_row, k_row, n_keys, _ = tile_of(i, u)
                values_stage(g, q_row, k_row, n_keys, slot(i, u))

        if pending:
            pending.pop()()
        scores(0)
        if n_sets > 1:
            def body(i, carry):
                values(i - 1)
                scores(i)
                return carry
            lax.fori_loop(1, n_sets, body, 0)
        pending.append(lambda: values(n_sets - 1))
        emitted[0] += n_sets

    for g in range(DIL_GROUPS):
        rate = DIL_RATES[g]
        n_tiles = seq // rate // Q_TILE
        span = rate * Q_TILE
        group = min(rate, RESIDUES_PER_BODY)
        n_groups = rate // group

        def first_tile(i, u, g=g, group=group):
            rho = i * group + u
            return g, rho, rho, Q_TILE, lambda: bf_ref[g]

        def consecutive_tile(i, u, g=g, span=span):
            n = 1 + i * TILES_PER_BODY + u
            return g, n * span, (n - 1) * span, 2 * Q_TILE, lambda: br_ref[g]

        def later_tile(i, u, g=g, group=group, n_groups=n_groups, span=span):
            n = 1 + i // n_groups
            rho = (i % n_groups) * group + u
            return g, rho + n * span, rho + (n - 1) * span, 2 * Q_TILE, lambda: br_ref[g]

        run_tile_sets(n_groups, group, first_tile)
        if n_tiles > 1:
            if rate == 1:
                assert (n_tiles - 1) % TILES_PER_BODY == 0
                run_tile_sets((n_tiles - 1) // TILES_PER_BODY, TILES_PER_BODY, consecutive_tile)
            else:
                run_tile_sets((n_tiles - 1) * n_groups, group, later_tile)
    pending.pop()()

    def combine(i, carry):
        rs = pl.ds(pl.multiple_of(i * Q_TILE, Q_TILE), Q_TILE)
        ld = [ld_ref[g, rs, :] for g in range(DIL_GROUPS)]
        top = jnp.maximum(jnp.maximum(ld[0], ld[1]), ld[2])
        w = [jnp.exp(x - top) for x in ld]
        num = sum(w[g] * og_ref[g, rs, :] for g in range(DIL_GROUPS))
        o_ref[rs, :] = (num / (w[0] + w[1] + w[2])).astype(o_ref.dtype)
        return carry

    lax.fori_loop(0, seq // Q_TILE, combine, 0)


def _dil_prompt(dq, kv):
    bsz, seq, _ = dq.shape
    pairs = DIL_HEADS * DIL_HD // LANES
    nblk = DIL_WIDTH // LANES

    def spec(col0):
        return pl.BlockSpec((None, seq, LANES), lambda b, hp, col0=col0: (b, 0, col0 + hp))

    in_specs = ([spec(g * pairs) for g in range(DIL_GROUPS)]
                + [spec(g * pairs) for g in range(DIL_GROUPS)]
                + [spec(nblk + g * pairs) for g in range(DIL_GROUPS)])
    return pl.pallas_call(
        functools.partial(_dil_prompt_kernel, seq=seq),
        grid=(bsz, pairs),
        in_specs=in_specs,
        out_specs=pl.BlockSpec((None, seq, LANES), lambda b, hp: (b, 0, hp)),
        out_shape=jax.ShapeDtypeStruct((bsz, seq, DIL_OUT), BF16),
        scratch_shapes=[pltpu.VMEM((DIL_GROUPS, seq, LANES), F32),
                        pltpu.VMEM((DIL_GROUPS, seq, LANES), F32),
                        pltpu.VMEM((DIL_GROUPS, seq, LANES), F32),
                        pltpu.VMEM((DIL_GROUPS, 2 * Q_TILE, Q_TILE), F32),
                        pltpu.VMEM((2, 2 * Q_TILE, 2 * Q_TILE), F32),
                        pltpu.VMEM((2 * TILE_SLOTS, 2 * Q_TILE, 2 * Q_TILE), BF16),
                        pltpu.VMEM((2 * TILE_SLOTS, Q_TILE, LANES), F32)],
        compiler_params=_params("parallel", "arbitrary"),
        name="dil_prompt",
    )(dq, dq, dq, kv, kv, kv, kv, kv, kv)


def _dil_sample_kernel(q_ref, kvn_ref, kb0_ref, vb0_ref, kb1_ref, vb1_ref, kb2_ref, vb2_ref,
                       o_ref, nk0_ref, nv0_ref, nk1_ref, nv1_ref, nk2_ref, nv2_ref, *, n_new):
    kb_refs = (kb0_ref, kb1_ref, kb2_ref)
    vb_refs = (vb0_ref, vb1_ref, vb2_ref)
    nk_refs = (nk0_ref, nk1_ref, nk2_ref)
    nv_refs = (nv0_ref, nv1_ref, nv2_ref)
    pad = q_ref.shape[1]
    nrow = DIL_HEADS * pad
    width = DIL_HEADS * DIL_HD

    lane_head = jnp.right_shift(lax.broadcasted_iota(jnp.int32, (pad, width), 1), DIL_HD.bit_length() - 1)

    def bias_for(g, dist):
        rate = DIL_RATES[g]
        rows = lax.broadcasted_iota(jnp.int32, dist.shape, 0)
        head = sum(jnp.where(rows >= h * pad, 1.0, 0.0) for h in range(1, DIL_HEADS))
        ok = (dist >= 0) & (dist <= DIL_WINDOWS[g]) & ((dist & (rate - 1)) == 0)
        return jnp.where(ok, -_alibi_slope(g, head) * dist.astype(F32), MASK_VALUE)

    sel_r = lax.broadcasted_iota(jnp.int32, (pad, LANES), 0)
    sel_c = lax.broadcasted_iota(jnp.int32, (pad, LANES), 1)
    place = jnp.where((sel_r < n_new) & (sel_c == sel_r + (LANES - n_new)), 1.0, 0.0).astype(BF16)
    tail_lane = lax.broadcasted_iota(jnp.int32, (width, LANES), 1) >= LANES - n_new

    def shifted(buf_t, new_rows):
        length = buf_t.shape[1]
        rolled = pltpu.roll(buf_t, length - n_new, 1)
        new_t = sum(_dot_tn(part, place) for part in _split3(new_rows))
        last = jnp.where(tail_lane, new_t, rolled[:, length - LANES:])
        return rolled, last

    scores, values, transposed = [], [], []
    for g in range(DIL_GROUPS):
        length = kb_refs[g].shape[2]
        cs = slice(g * width, (g + 1) * width)
        vcs = slice(DIL_WIDTH + g * width, DIL_WIDTH + (g + 1) * width)
        qg = q_ref[0, :, cs].astype(F32) * (DIL_HD ** -0.5)
        qs = jnp.concatenate([jnp.where(lane_head == h, qg, 0.0) for h in range(DIL_HEADS)], axis=0).astype(BF16)
        kb, vb = kb_refs[g][0], vb_refs[g][0]
        kn, vn = kvn_ref[0, :, cs], kvn_ref[0, :, vcs]
        rows = lax.broadcasted_iota(jnp.int32, (nrow, length), 0)
        cols = lax.broadcasted_iota(jnp.int32, (nrow, length), 1)
        dist = length + (rows & (pad - 1)) - cols
        scores.append(_dot(qs, kb.astype(BF16)) + bias_for(g, dist))
        values.append(vb.astype(BF16))
        transposed.append(True)
        rows = lax.broadcasted_iota(jnp.int32, (nrow, pad), 0)
        cols = lax.broadcasted_iota(jnp.int32, (nrow, pad), 1)
        dist = jnp.where(cols < n_new, (rows & (pad - 1)) - cols, -1)
        scores.append(_dot_nt(qs, kn.astype(BF16)) + bias_for(g, dist))
        values.append(vn.astype(BF16))
        transposed.append(False)
        for buf, new, out_ref in ((kb, kn, nk_refs[g]), (vb, vn, nv_refs[g])):
            rolled, last = shifted(buf, new)
            if length > LANES:
                out_ref[0, :, 0:length - LANES] = rolled[:, 0:length - LANES]
            out_ref[0, :, length - LANES:length] = last

    top = functools.reduce(jnp.maximum, [jnp.max(s, axis=-1, keepdims=True) for s in scores])
    probs = [jnp.exp(s - top) for s in scores]
    den = sum(jnp.sum(p, axis=-1, keepdims=True) for p in probs)
    acc = sum((_dot_nt if t else _dot)(p.astype(BF16), v)
              for p, v, t in zip(probs, values, transposed)) / den
    out = sum(jnp.where(lane_head == h, acc[h * pad:(h + 1) * pad], 0.0) for h in range(DIL_HEADS))
    o_ref[0] = out.astype(o_ref.dtype)


def _dil_sample(dq, kvn, bufs, n_new):
    bsz, pad, _ = dq.shape
    full = lambda a: pl.BlockSpec((1,) + a.shape[1:], lambda b: (b, 0, 0))
    out_shape = ([jax.ShapeDtypeStruct((bsz, pad, DIL_OUT), BF16)]
                 + [jax.ShapeDtypeStruct(a.shape, a.dtype) for a in bufs])
    return pl.pallas_call(
        functools.partial(_dil_sample_kernel, n_new=n_new),
        grid=(bsz,),
        in_specs=[full(dq), full(kvn)] + [full(a) for a in bufs],
        out_specs=[pl.BlockSpec((1, pad, DIL_OUT), lambda b: (b, 0, 0))] + [full(a) for a in bufs],
        out_shape=out_shape,
        compiler_params=_params("parallel"),
        name="dil_sample",
    )(dq, kvn, *bufs)


def _merge_out_kernel(x_ref, oa_ref, ob_ref, ga_ref, gb_ref, pa_ref, pb_ref, wo_ref, o_ref):
    pa = _dot(oa_ref[...], pa_ref[...])
    pb = _dot(ob_ref[...], pb_ref[...])
    merged = jax.nn.sigmoid(ga_ref[...].astype(F32)) * pa + jax.nn.sigmoid(gb_ref[...].astype(F32)) * pb
    o_ref[...] = x_ref[...] + _dot(merged.astype(BF16), wo_ref[...])


def _merge_out(x, oa, ob, gates, pa, pb, wo, tm):
    m = x.shape[0]
    row = lambda w: pl.BlockSpec((tm, w), lambda i: (i, 0))
    whole = lambda a: pl.BlockSpec(a.shape, lambda i: (0, 0))
    return pl.pallas_call(
        _merge_out_kernel,
        grid=(m // tm,),
        in_specs=[row(D_MODEL), row(GLA_DV), row(DIL_OUT), row(D_MODEL),
                  pl.BlockSpec((tm, D_MODEL), lambda i: (i, 1)),
                  whole(pa), whole(pb), whole(wo)],
        out_specs=row(D_MODEL),
        out_shape=jax.ShapeDtypeStruct((m, D_MODEL), F32),
        compiler_params=_params("parallel"),
        name="merge_out",
    )(x, oa, ob, gates, gates, pa, pb, wo)


FFN_COL_CHUNK = D_FF // 2


def _ffn_kernel(x_ref, n2_ref, nf_ref, wg_ref, wu_ref, wd_ref, y_ref):
    x = x_ref[...]
    h = _rms(x, n2_ref[...]).astype(BF16)
    acc = x
    for c0 in range(0, D_FF, FFN_COL_CHUNK):
        cs = slice(c0, c0 + FFN_COL_CHUNK)
        gate = _dot(h, wg_ref[:, cs])
        up = _dot(h, wu_ref[:, cs])
        act = (gate * jax.nn.sigmoid(gate) * up).astype(BF16)
        acc = acc + _dot(act, wd_ref[cs, :])
    y_ref[...] = _rms(acc, nf_ref[...])


def _ffn(x, n2, nf, wg, wu, wd, tm):
    m = x.shape[0]
    resident = lambda a: pl.BlockSpec(a.shape, lambda i: (0, 0), pipeline_mode=pl.Buffered(1))
    return pl.pallas_call(
        _ffn_kernel,
        grid=(m // tm,),
        in_specs=[pl.BlockSpec((tm, D_MODEL), lambda i: (i, 0)),
                  resident(n2), resident(nf), resident(wg), resident(wu), resident(wd)],
        out_specs=pl.BlockSpec((tm, D_MODEL), lambda i: (i, 0)),
        out_shape=jax.ShapeDtypeStruct((m, D_MODEL), F32),
        compiler_params=_params("parallel"),
        name="ffn",
    )(x, n2, nf, wg, wu, wd)


def _prep_weights(w_in, gla_gate_w2, proj_a, proj_b, w_out, w_ffn_gate, w_ffn_up, w_ffn_down):
    w = w_in[0]
    cols = lambda name: w[:, _OFF[name][0]:_OFF[name][1]]
    pad = jnp.zeros((D_MODEL, LANES - GLA_GATE_RANK), F32)
    w_gla = jnp.concatenate([cols("gq"), cols("gk"), cols("gv"), cols("gr"), cols("glr"), pad], axis=1)
    w2 = jnp.concatenate([gla_gate_w2[0], jnp.zeros((LANES - GLA_GATE_RANK, GLA_DK), F32)], axis=0)
    return dict(
        w_gla=w_gla.astype(BF16), w_dq=cols("dq").astype(BF16),
        w_kv=jnp.concatenate([cols("dk"), cols("dv")], axis=1).astype(BF16),
        w_gates=jnp.concatenate([cols("ga"), cols("gb")], axis=1).astype(BF16),
        w2=w2.astype(BF16), pa=proj_a[0].astype(BF16), pb=proj_b[0].astype(BF16), wo=w_out[0].astype(BF16),
        wg=w_ffn_gate[0].astype(BF16), wu=w_ffn_up[0].astype(BF16), wd=w_ffn_down[0].astype(BF16))


def _project(x2d, norm1_g, wts, tm):
    return _in_proj(x2d, norm1_g, [wts["w_gla"], wts["w_dq"], wts["w_kv"], wts["w_gates"]],
                    [BF16, BF16, F32, BF16], tm)


def _tail(x2d, o_a, o_b, z_gates, wts, norm2_g, norm_f_g, tm):
    x1 = _merge_out(x2d, o_a, o_b, z_gates, wts["pa"], wts["pb"], wts["wo"], tm)
    return _ffn(x1, norm2_g, norm_f_g.reshape(1, D_MODEL), wts["wg"], wts["wu"], wts["wd"], tm)


def kernel(x_prompt, x_sample, state_gla, state_win0_k, state_win0_v, state_win1_k, state_win1_v,
           state_win2_k, state_win2_v, norm1_g, w_in, gla_gate_w2, gla_gate_b, gla_norm_g,
           proj_a, proj_b, w_out, norm2_g, w_ffn_gate, w_ffn_up, w_ffn_down, norm_f_g):
    wts = _prep_weights(w_in, gla_gate_w2, proj_a, proj_b, w_out, w_ffn_gate, w_ffn_up, w_ffn_down)
    bp, seq, _ = x_prompt.shape
    bs, n_new, _ = x_sample.shape
    width = DIL_HEADS * DIL_HD

    xp = x_prompt.reshape(bp * seq, D_MODEL)
    z_gla, z_dq, z_kv, z_gates = _project(xp, norm1_g, wts, 512)
    s0 = jnp.zeros((bp, GLA_HEADS, GLA_HDK, GLA_HDV), F32)
    o_a, gla_p = _gla(z_gla.reshape(bp, seq, GLA_COLS), wts["w2"], gla_gate_b, gla_norm_g, s0,
                      GLA_CHUNK, GLA_CHUNK, GLA_ROWS_PROMPT)
    kv_p = z_kv.reshape(bp, seq, 2 * DIL_WIDTH)
    o_b = _dil_prompt(z_dq.reshape(bp, seq, DIL_WIDTH), kv_p)
    y_prompt = _tail(xp, o_a.reshape(bp * seq, GLA_DV), o_b.reshape(bp * seq, DIL_OUT), z_gates, wts,
                     norm2_g, norm_f_g, 512).reshape(bp, seq, D_MODEL)
    win_p = []
    for g in range(DIL_GROUPS):
        keep = min(DIL_WINDOWS[g], seq)
        for off in (0, DIL_WIDTH):
            c0 = off + g * width
            win_p.append(kv_p[:, seq - keep:, c0:c0 + width].reshape(1, bp, keep, DIL_HEADS, DIL_HD))

    xs = jnp.pad(x_sample, ((0, 0), (0, SAMPLE_PAD - n_new), (0, 0))).reshape(bs * SAMPLE_PAD, D_MODEL)
    zs_gla, zs_dq, zs_kv, zs_gates = _project(xs, norm1_g, wts, bs * SAMPLE_PAD)
    os_a, gla_s = _gla(zs_gla.reshape(bs, SAMPLE_PAD, GLA_COLS), wts["w2"], gla_gate_b, gla_norm_g,
                       state_gla[0], SAMPLE_PAD, n_new, GLA_ROWS_SAMPLE)
    bufs = [jnp.transpose(a[0], (0, 2, 3, 1)).reshape(bs, width, a.shape[2]) for a in
            (state_win0_k, state_win0_v, state_win1_k, state_win1_v, state_win2_k, state_win2_v)]
    outs = _dil_sample(zs_dq.reshape(bs, SAMPLE_PAD, DIL_WIDTH), zs_kv.reshape(bs, SAMPLE_PAD, 2 * DIL_WIDTH),
                       bufs, n_new)
    os_b, win_s = outs[0], outs[1:]
    ys = _tail(xs, os_a.reshape(bs * SAMPLE_PAD, GLA_DV), os_b.reshape(bs * SAMPLE_PAD, DIL_OUT), zs_gates, wts,
               norm2_g, norm_f_g, bs * SAMPLE_PAD)
    y_sample = ys.reshape(bs, SAMPLE_PAD, D_MODEL)[:, :n_new]
    win_s = [jnp.transpose(a.reshape(bs, DIL_HEADS, DIL_HD, a.shape[2]), (0, 3, 1, 2))[None] for a in win_s]

    return (y_prompt, y_sample, gla_p[None], *win_p, gla_s[None], *win_s)
```

```python
import functools

import jax
import jax.numpy as jnp
from jax import lax
from jax.experimental import pallas as pl
from jax.experimental.pallas import tpu as pltpu

F32 = jnp.float32
BF16 = jnp.bfloat16

D_MODEL = 1024
GLA_HEADS = 4
GLA_DK = 512
GLA_DV = 1024
GLA_HDK = 128
GLA_HDV = 256
GLA_GATE_RANK = 16
GLA_TAU = 16.0
DIL_WINDOWS = (128, 512, 2048)
DIL_RATES = (1, 4, 16)
DIL_GROUPS = 3
DIL_HEADS = 4
DIL_HD = 64
DIL_WIDTH = 768
DIL_OUT = 256
ALIBI_MAX = 8.0
D_FF = 2816
RMS_EPS = 1e-6

LANES = 128
SUBLANES = 8
Q_TILE = 128
TILES_PER_BODY = 5
RESIDUES_PER_BODY = 8
TILE_SLOTS = max(TILES_PER_BODY, RESIDUES_PER_BODY)
GLA_CHUNK = 128
GLA_ROWS_PROMPT = 2
GLA_ROWS_SAMPLE = 4
SAMPLE_PAD = 16
MASK_VALUE = -1e30
VMEM_LIMIT_BYTES = 48 * 1024 * 1024
IN_PROJ_VMEM_LIMIT_BYTES = 56 * 1024 * 1024
_LN2 = 0.6931471805599453

_OFF = {}
_o = 0
for _name, _w in (("gq", GLA_DK), ("gk", GLA_DK), ("gv", GLA_DV), ("gr", GLA_DV), ("glr", GLA_GATE_RANK),
                  ("dq", DIL_WIDTH), ("dk", DIL_WIDTH), ("dv", DIL_WIDTH), ("ga", D_MODEL), ("gb", D_MODEL)):
    _OFF[_name] = (_o, _o + _w)
    _o += _w
GLA_COLS = 2 * GLA_DK + 2 * GLA_DV
GLA_SAFE_LOG_DECAY = 40.0


def _params(*sem):
    return pltpu.CompilerParams(dimension_semantics=sem, vmem_limit_bytes=VMEM_LIMIT_BYTES)


def _dot(a, b):
    return jnp.dot(a, b, preferred_element_type=F32)


def _dot_nt(a, b):
    return lax.dot_general(a, b, (((1,), (1,)), ((), ())), preferred_element_type=F32)


def _dot_tn(a, b):
    return lax.dot_general(a, b, (((0,), (0,)), ((), ())), preferred_element_type=F32)


def _rms(x, g):
    return x * lax.rsqrt(jnp.mean(x * x, axis=-1, keepdims=True) + RMS_EPS) * g


IN_PROJ_COL_CHUNK = 1024


def _split3(x):
    x1 = x.astype(BF16)
    r1 = x - x1.astype(F32)
    x2 = r1.astype(BF16)
    x3 = (r1 - x2.astype(F32)).astype(BF16)
    return x1, x2, x3


def _in_proj_kernel(x_ref, g_ref, wlr_ref, w2_ref, gb_ref, *refs, chunk, n_valid):
    n_out = (len(refs) - 1) // 2
    h = _rms(x_ref[...], g_ref[...]).astype(BF16)
    for w_ref, o_ref in zip(refs[:n_out], refs[n_out:2 * n_out]):
        n = w_ref.shape[1]
        for c0 in range(0, n, IN_PROJ_COL_CHUNK):
            cs = slice(c0, min(c0 + IN_PROJ_COL_CHUNK, n))
            o_ref[:, cs] = _dot(h, w_ref[:, cs]).astype(o_ref.dtype)

    b_ref = refs[2 * n_out]
    tm = x_ref.shape[0]
    low_rank = _dot(h, wlr_ref[...]).astype(BF16)
    gate = _dot(low_rank, w2_ref[...]) + gb_ref[...]
    log_a = (jnp.minimum(gate, 0.0) - jnp.log(1.0 + jnp.exp(-jnp.abs(gate)))) * (1.0 / GLA_TAU)
    if n_valid < chunk:
        tok = lax.broadcasted_iota(jnp.int32, log_a.shape, 0) & (chunk - 1)
        log_a = jnp.where(tok < n_valid, log_a, 0.0)
    span = min(tm, LANES)
    row = lax.broadcasted_iota(jnp.int32, (span, span), 0)
    col = lax.broadcasted_iota(jnp.int32, (span, span), 1)
    same_chunk = (row & -chunk) == (col & -chunk) if chunk < span else True
    tril = jnp.where((row >= col) & same_chunk, 1.0, 0.0).astype(BF16)
    for r0 in range(0, tm, span):
        parts = _split3(log_a[r0:r0 + span])[:2]
        b_ref[r0:r0 + span, :] = sum(_dot(tril, p) for p in parts)


def _in_proj(x, g, w_low_rank, w2, gate_b, weights, out_dtypes, tm, chunk, n_valid):
    m, k = x.shape
    assert chunk & (chunk - 1) == 0 and (chunk % LANES == 0 or LANES % chunk == 0) and tm % chunk == 0
    resident = lambda a: pl.BlockSpec(a.shape, lambda i: (0, 0), pipeline_mode=pl.Buffered(1))
    small = [w_low_rank, w2, gate_b]
    widths = [w.shape[1] for w in weights] + [GLA_DK]
    return pl.pallas_call(
        functools.partial(_in_proj_kernel, chunk=chunk, n_valid=n_valid),
        grid=(m // tm,),
        in_specs=([pl.BlockSpec((tm, k), lambda i: (i, 0)), resident(g)] + [resident(a) for a in small]
                  + [resident(w) for w in weights]),
        out_specs=[pl.BlockSpec((tm, n), lambda i: (i, 0)) for n in widths],
        out_shape=[jax.ShapeDtypeStruct((m, n), dt) for n, dt in zip(widths, list(out_dtypes) + [F32])],
        compiler_params=pltpu.CompilerParams(dimension_semantics=("parallel",),
                                             vmem_limit_bytes=IN_PROJ_VMEM_LIMIT_BYTES),
        name="in_proj",
    )(x, g, *small, *weights)


def _gla_kernel(q_ref, k_ref, v_ref, r_ref, b_ref, ng_ref, s0_ref,
                o_ref, sout_ref, s_ref, oi_ref, kf_ref, vf_ref, *, chunk, n_valid):
    c = pl.program_id(1)
    rows = q_ref.shape[0]

    @pl.when(c == 0)
    def _():
        s_ref[...] = s0_ref[...]

    for r in range(rows):
        _gla_chunk(r, q_ref, k_ref, v_ref, r_ref, b_ref, ng_ref, o_ref, s_ref, oi_ref, chunk, n_valid)

    for r in range(rows):
        @pl.when(jnp.min(b_ref[r, chunk - 1:chunk, :]) < -GLA_SAFE_LOG_DECAY)
        def _(r=r):
            _gla_chunk_exact_intra(r, q_ref, k_ref, v_ref, r_ref, b_ref, ng_ref, o_ref, oi_ref, kf_ref, vf_ref,
                                   chunk, n_valid)

    @pl.when(c == pl.num_programs(1) - 1)
    def _():
        sout_ref[...] = s_ref[...]


def _gla_epilogue(o, r, vs, r_ref, ng_ref, o_ref):
    gr = r_ref[r, :, vs].astype(F32)
    o_ref[r, :, vs] = (_rms(o, ng_ref[...]) * (gr * jax.nn.sigmoid(gr))).astype(o_ref.dtype)


def _gla_values(r, vs, v_ref, chunk, n_valid):
    vh = v_ref[r, :, vs]
    if n_valid < chunk:
        tok = lax.broadcasted_iota(jnp.int32, vh.shape, 0)
        vh = jnp.where(tok < n_valid, vh, jnp.zeros_like(vh))
    return vh


def _gla_chunk_exact_intra(r, q_ref, k_ref, v_ref, r_ref, b_ref, ng_ref, o_ref, oi_ref, kf_ref, vf_ref,
                           chunk, n_valid):
    tok = lax.broadcasted_iota(jnp.int32, (chunk, 1), 0)
    for h in range(GLA_HEADS):
        ks = slice(h * GLA_HDK, (h + 1) * GLA_HDK)
        vs = slice(h * GLA_HDV, (h + 1) * GLA_HDV)
        bh = b_ref[r, :, ks]
        qh = q_ref[r, :, ks].astype(F32) * (GLA_HDK ** -0.5)
        kf_ref[...] = k_ref[r, :, ks].astype(F32)
        vf_ref[...] = _gla_values(r, vs, v_ref, chunk, n_valid).astype(F32)

        def eight_keys(i, acc, bh=bh, qh=qh, ks=ks):
            rows8 = pl.ds(pl.multiple_of(i * SUBLANES, SUBLANES), SUBLANES)
            b8, k8, v8 = b_ref[r, rows8, ks], kf_ref[rows8, :], vf_ref[rows8, :]
            for j in range(SUBLANES):
                s = i * SUBLANES + j
                decay = jnp.exp(jnp.where(tok >= s, bh - b8[j:j + 1], MASK_VALUE))
                w = jnp.sum(qh * decay * k8[j:j + 1], axis=-1, keepdims=True)
                acc = acc + w * v8[j:j + 1]
            return acc

        intra = lax.fori_loop(0, chunk // SUBLANES, eight_keys, jnp.zeros((chunk, GLA_HDV), F32))
        _gla_epilogue(oi_ref[r, h] + intra, r, vs, r_ref, ng_ref, o_ref)


def _gla_chunk(r, q_ref, k_ref, v_ref, r_ref, b_ref, ng_ref, o_ref, s_ref, oi_ref, chunk, n_valid):
    row = lax.broadcasted_iota(jnp.int32, (chunk, chunk), 0)
    col = lax.broadcasted_iota(jnp.int32, (chunk, chunk), 1)
    causal = row >= col
    b = b_ref[r]

    for h in range(GLA_HEADS):
        ks = slice(h * GLA_HDK, (h + 1) * GLA_HDK)
        vs = slice(h * GLA_HDV, (h + 1) * GLA_HDV)
        bh = b[:, ks]
        qh = q_ref[r, :, ks].astype(F32) * (GLA_HDK ** -0.5)
        kh = k_ref[r, :, ks].astype(F32)
        vh = _gla_values(r, vs, v_ref, chunk, n_valid)
        qt = (qh * jnp.exp(bh)).astype(BF16)
        kt = (kh * jnp.exp(-bh)).astype(BF16)
        kd = (kh * jnp.exp(bh[chunk - 1:chunk, :] - bh)).astype(BF16)
        s_old = s_ref[r, h]
        scores = jnp.where(causal, _dot_nt(qt, kt), 0.0).astype(BF16)
        o_state = _dot(qt, s_old.astype(BF16))
        oi_ref[r, h] = o_state
        dec = jnp.exp(jnp.broadcast_to(bh[chunk - 1:chunk, :], (GLA_HDK, GLA_HDK)).T)
        s_ref[r, h] = s_old * jnp.concatenate([dec, dec], axis=1) + _dot_tn(kd, vh)
        _gla_epilogue(o_state + _dot(scores, vh), r, vs, r_ref, ng_ref, o_ref)


def _gla(z, b_cum, norm_g, s0, chunk, n_valid, rows):
    bsz, t, _ = z.shape
    kern = functools.partial(_gla_kernel, chunk=chunk, n_valid=n_valid)
    state = pl.BlockSpec((rows, GLA_HEADS, GLA_HDK, GLA_HDV), lambda b, c: (b, 0, 0, 0))
    return pl.pallas_call(
        kern,
        grid=(bsz // rows, t // chunk),
        in_specs=[pl.BlockSpec((rows, chunk, GLA_DK), lambda b, c: (b, c, 0)),
                  pl.BlockSpec((rows, chunk, GLA_DK), lambda b, c: (b, c, 1)),
                  pl.BlockSpec((rows, chunk, GLA_DV), lambda b, c: (b, c, 1)),
                  pl.BlockSpec((rows, chunk, GLA_DV), lambda b, c: (b, c, 2)),
                  pl.BlockSpec((rows, chunk, GLA_DK), lambda b, c: (b, c, 0)),
                  pl.BlockSpec((1, GLA_HDV), lambda b, c: (0, 0)),
                  state],
        out_specs=[pl.BlockSpec((rows, chunk, GLA_DV), lambda b, c: (b, c, 0)), state],
        out_shape=[jax.ShapeDtypeStruct((bsz, t, GLA_DV), BF16),
                   jax.ShapeDtypeStruct((bsz, GLA_HEADS, GLA_HDK, GLA_HDV), F32)],
        scratch_shapes=[pltpu.VMEM((rows, GLA_HEADS, GLA_HDK, GLA_HDV), F32),
                        pltpu.VMEM((rows, GLA_HEADS, chunk, GLA_HDV), F32),
                        pltpu.VMEM((chunk, GLA_HDK), F32),
                        pltpu.VMEM((chunk, GLA_HDV), F32)],
        compiler_params=_params("parallel", "arbitrary"),
        name="gla",
    )(z, z, z, z, b_cum, norm_g, s0)


def _alibi_slope(g, head):
    n = DIL_GROUPS * DIL_HEADS
    return jnp.exp((-ALIBI_MAX * _LN2 / n) * (head + (g * DIL_HEADS + 1.0)))


def _pair_slopes(g, hp, rows):
    head = 2.0 * hp.astype(F32) + jnp.where(rows >= Q_TILE, 1.0, 0.0)
    return _alibi_slope(g, head)


def _dil_prompt_kernel(q0_ref, q1_ref, q2_ref, k0_ref, k1_ref, k2_ref, v0_ref, v1_ref, v2_ref,
                       o_ref, qf_ref, og_ref, ld_ref, bf_ref, br_ref, p_ref, m_ref, *, seq):
    hp = pl.program_id(1)
    q_refs = (q0_ref, q1_ref, q2_ref)
    k_refs = (k0_ref, k1_ref, k2_ref)
    v_refs = (v0_ref, v1_ref, v2_ref)

    for g in range(DIL_GROUPS):
        qf_ref[g] = q_refs[g][...].astype(F32) * (DIL_HD ** -0.5)

    for g in range(DIL_GROUPS):
        rate = float(DIL_RATES[g])
        rows = lax.broadcasted_iota(jnp.int32, (2 * Q_TILE, Q_TILE), 0)
        cols = lax.broadcasted_iota(jnp.int32, (2 * Q_TILE, Q_TILE), 1)
        dist = (rows & (Q_TILE - 1)) - cols
        slope = _pair_slopes(g, hp, rows)
        bf_ref[g] = jnp.where(dist >= 0, -slope * rate * dist.astype(F32), MASK_VALUE)
        if g < 2:
            rows = lax.broadcasted_iota(jnp.int32, (2 * Q_TILE, 2 * Q_TILE), 0)
            cols = lax.broadcasted_iota(jnp.int32, (2 * Q_TILE, 2 * Q_TILE), 1)
            dist = (rows & (Q_TILE - 1)) + Q_TILE - cols
            slope = _pair_slopes(g, hp, rows)
            ok = jnp.abs(dist - Q_TILE // 2) <= Q_TILE // 2
            br_ref[g] = jnp.where(ok, -slope * rate * dist.astype(F32), MASK_VALUE)

    lane = lax.broadcasted_iota(jnp.int32, (Q_TILE, LANES), 1)
    lo = lane < DIL_HD

    def rows_of(start, size, rate):
        return pl.ds(start, size) if rate == 1 else pl.ds(start, size, stride=rate)

    def scores_stage(g, q_row, k_row, n_keys, bias, slot):
        rate = DIL_RATES[g]
        q2 = qf_ref[g, rows_of(q_row, Q_TILE, rate), :]
        k2 = k_refs[g][rows_of(k_row, n_keys, rate), :].astype(BF16)
        qs = jnp.concatenate([jnp.where(lo, q2, 0.0), jnp.where(lo, 0.0, q2)], axis=0).astype(BF16)
        s = _dot_nt(qs, k2) + bias()
        m = jnp.max(s, axis=-1, keepdims=True)
        p_ref[slot, :, 0:n_keys] = jnp.exp(s - m).astype(BF16)
        m_ref[slot] = jnp.where(lo, m[:Q_TILE], m[Q_TILE:])

    def values_stage(g, q_row, k_row, n_keys, slot):
        rate = DIL_RATES[g]
        v2 = v_refs[g][rows_of(k_row, n_keys, rate), :].astype(BF16)
        vo = jnp.concatenate([v2, jnp.ones((n_keys, LANES), BF16)], axis=1)
        r = _dot(p_ref[slot, :, 0:n_keys], vo)
        o2 = jnp.where(lo, r[:Q_TILE, :LANES], r[Q_TILE:, :LANES])
        l2 = jnp.where(lo, r[:Q_TILE, LANES:], r[Q_TILE:, LANES:])
        og_ref[g, rows_of(q_row, Q_TILE, rate), :] = o2 / l2
        ld_ref[g, rows_of(q_row, Q_TILE, rate), :] = m_ref[slot] + jnp.log(l2)

    pending = []
    emitted = [0]

    def run_tile_sets(n_sets, n_per_set, tile_of):
        base = emitted[0]

        def slot(i, u):
            return ((i + base) % 2) * TILE_SLOTS + u

        def scores(i):
            for u in range(n_per_set):
                g, q_row, k_row, n_keys, bias = tile_of(i, u)
                scores_stage(g, q_row, k_row, n_keys, bias, slot(i, u))

        def values(i):
            for u in range(n_per_set):
                g, q_row, k_row, n_keys, _ = tile_of(i, u)
                values_stage(g, q_row, k_row, n_keys, slot(i, u))

        if pending:
            pending.pop()()
        scores(0)
        if n_sets > 1:
            def body(i, carry):
                values(i - 1)
                scores(i)
                return carry
            lax.fori_loop(1, n_sets, body, 0)
        pending.append(lambda: values(n_sets - 1))
        emitted[0] += n_sets

    for g in range(DIL_GROUPS):
        rate = DIL_RATES[g]
        n_tiles = seq // rate // Q_TILE
        span = rate * Q_TILE
        group = min(rate, RESIDUES_PER_BODY)
        n_groups = rate // group

        def first_tile(i, u, g=g, group=group):
            rho = i * group + u
            return g, rho, rho, Q_TILE, lambda: bf_ref[g]

        def consecutive_tile(i, u, g=g, span=span):
            n = 1 + i * TILES_PER_BODY + u
            return g, n * span, (n - 1) * span, 2 * Q_TILE, lambda: br_ref[g]

        def later_tile(i, u, g=g, group=group, n_groups=n_groups, span=span):
            n = 1 + i // n_groups
            rho = (i % n_groups) * group + u
            return g, rho + n * span, rho + (n - 1) * span, 2 * Q_TILE, lambda: br_ref[g]

        run_tile_sets(n_groups, group, first_tile)
        if n_tiles > 1:
            if rate == 1:
                assert (n_tiles - 1) % TILES_PER_BODY == 0
                run_tile_sets((n_tiles - 1) // TILES_PER_BODY, TILES_PER_BODY, consecutive_tile)
            else:
                run_tile_sets((n_tiles - 1) * n_groups, group, later_tile)
    pending.pop()()

    def combine(i, carry):
        rs = pl.ds(pl.multiple_of(i * Q_TILE, Q_TILE), Q_TILE)
        ld = [ld_ref[g, rs, :] for g in range(DIL_GROUPS)]
        top = jnp.maximum(jnp.maximum(ld[0], ld[1]), ld[2])
        w = [jnp.exp(x - top) for x in ld]
        num = sum(w[g] * og_ref[g, rs, :] for g in range(DIL_GROUPS))
        o_ref[rs, :] = (num / (w[0] + w[1] + w[2])).astype(o_ref.dtype)
        return carry

    lax.fori_loop(0, seq // Q_TILE, combine, 0)


def _dil_prompt(dq, kv):
    bsz, seq, _ = dq.shape
    pairs = DIL_HEADS * DIL_HD // LANES
    nblk = DIL_WIDTH // LANES

    def spec(col0):
        return pl.BlockSpec((None, seq, LANES), lambda b, hp, col0=col0: (b, 0, col0 + hp))

    in_specs = ([spec(g * pairs) for g in range(DIL_GROUPS)]
                + [spec(g * pairs) for g in range(DIL_GROUPS)]
                + [spec(nblk + g * pairs) for g in range(DIL_GROUPS)])
    return pl.pallas_call(
        functools.partial(_dil_prompt_kernel, seq=seq),
        grid=(bsz, pairs),
        in_specs=in_specs,
        out_specs=pl.BlockSpec((None, seq, LANES), lambda b, hp: (b, 0, hp)),
        out_shape=jax.ShapeDtypeStruct((bsz, seq, DIL_OUT), BF16),
        scratch_shapes=[pltpu.VMEM((DIL_GROUPS, seq, LANES), F32),
                        pltpu.VMEM((DIL_GROUPS, seq, LANES), F32),
                        pltpu.VMEM((DIL_GROUPS, seq, LANES), F32),
                        pltpu.VMEM((DIL_GROUPS, 2 * Q_TILE, Q_TILE), F32),
                        pltpu.VMEM((2, 2 * Q_TILE, 2 * Q_TILE), F32),
                        pltpu.VMEM((2 * TILE_SLOTS, 2 * Q_TILE, 2 * Q_TILE), BF16),
                        pltpu.VMEM((2 * TILE_SLOTS, Q_TILE, LANES), F32)],
        compiler_params=_params("parallel", "arbitrary"),
        name="dil_prompt",
    )(dq, dq, dq, kv, kv, kv, kv, kv, kv)


def _dil_sample_kernel(q_ref, kvn_ref, kb0_ref, vb0_ref, kb1_ref, vb1_ref, kb2_ref, vb2_ref,
                       o_ref, nk0_ref, nv0_ref, nk1_ref, nv1_ref, nk2_ref, nv2_ref, *, n_new):
    kb_refs = (kb0_ref, kb1_ref, kb2_ref)
    vb_refs = (vb0_ref, vb1_ref, vb2_ref)
    nk_refs = (nk0_ref, nk1_ref, nk2_ref)
    nv_refs = (nv0_ref, nv1_ref, nv2_ref)
    pad = q_ref.shape[1]
    nrow = DIL_HEADS * pad
    width = DIL_HEADS * DIL_HD

    lane_head = jnp.right_shift(lax.broadcasted_iota(jnp.int32, (pad, width), 1), DIL_HD.bit_length() - 1)

    def bias_for(g, dist):
        rate = DIL_RATES[g]
        rows = lax.broadcasted_iota(jnp.int32, dist.shape, 0)
        head = sum(jnp.where(rows >= h * pad, 1.0, 0.0) for h in range(1, DIL_HEADS))
        ok = (dist >= 0) & (dist <= DIL_WINDOWS[g]) & ((dist & (rate - 1)) == 0)
        return jnp.where(ok, -_alibi_slope(g, head) * dist.astype(F32), MASK_VALUE)

    sel_r = lax.broadcasted_iota(jnp.int32, (pad, LANES), 0)
    sel_c = lax.broadcasted_iota(jnp.int32, (pad, LANES), 1)
    place = jnp.where((sel_r < n_new) & (sel_c == sel_r + (LANES - n_new)), 1.0, 0.0).astype(BF16)
    tail_lane = lax.broadcasted_iota(jnp.int32, (width, LANES), 1) >= LANES - n_new

    def shifted(buf_t, new_rows):
        length = buf_t.shape[1]
        rolled = pltpu.roll(buf_t, length - n_new, 1)
        new_t = sum(_dot_tn(part, place) for part in _split3(new_rows))
        last = jnp.where(tail_lane, new_t, rolled[:, length - LANES:])
        return rolled, last

    scores, values, transposed = [], [], []
    for g in range(DIL_GROUPS):
        length = kb_refs[g].shape[2]
        cs = slice(g * width, (g + 1) * width)
        vcs = slice(DIL_WIDTH + g * width, DIL_WIDTH + (g + 1) * width)
        qg = q_ref[0, :, cs].astype(F32) * (DIL_HD ** -0.5)
        qs = jnp.concatenate([jnp.where(lane_head == h, qg, 0.0) for h in range(DIL_HEADS)], axis=0).astype(BF16)
        kb, vb = kb_refs[g][0], vb_refs[g][0]
        kn, vn = kvn_ref[0, :, cs], kvn_ref[0, :, vcs]
        rows = lax.broadcasted_iota(jnp.int32, (nrow, length), 0)
        cols = lax.broadcasted_iota(jnp.int32, (nrow, length), 1)
        dist = length + (rows & (pad - 1)) - cols
        scores.append(_dot(qs, kb.astype(BF16)) + bias_for(g, dist))
        values.append(vb.astype(BF16))
        transposed.append(True)
        rows = lax.broadcasted_iota(jnp.int32, (nrow, pad), 0)
        cols = lax.broadcasted_iota(jnp.int32, (nrow, pad), 1)
        dist = jnp.where(cols < n_new, (rows & (pad - 1)) - cols, -1)
        scores.append(_dot_nt(qs, kn.astype(BF16)) + bias_for(g, dist))
        values.append(vn.astype(BF16))
        transposed.append(False)
        for buf, new, out_ref in ((kb, kn, nk_refs[g]), (vb, vn, nv_refs[g])):
            rolled, last = shifted(buf, new)
            if length > LANES:
                out_ref[0, :, 0:length - LANES] = rolled[:, 0:length - LANES]
            out_ref[0, :, length - LANES:length] = last

    top = functools.reduce(jnp.maximum, [jnp.max(s, axis=-1, keepdims=True) for s in scores])
    probs = [jnp.exp(s - top) for s in scores]
    den = sum(jnp.sum(p, axis=-1, keepdims=True) for p in probs)
    acc = sum((_dot_nt if t else _dot)(p.astype(BF16), v)
              for p, v, t in zip(probs, values, transposed)) / den
    out = sum(jnp.where(lane_head == h, acc[h * pad:(h + 1) * pad], 0.0) for h in range(DIL_HEADS))
    o_ref[0] = out.astype(o_ref.dtype)


def _dil_sample(dq, kvn, bufs, n_new):
    bsz, pad, _ = dq.shape
    full = lambda a: pl.BlockSpec((1,) + a.shape[1:], lambda b: (b, 0, 0))
    out_shape = ([jax.ShapeDtypeStruct((bsz, pad, DIL_OUT), BF16)]
                 + [jax.ShapeDtypeStruct(a.shape, a.dtype) for a in bufs])
    return pl.pallas_call(
        functools.partial(_dil_sample_kernel, n_new=n_new),
        grid=(bsz,),
        in_specs=[full(dq), full(kvn)] + [full(a) for a in bufs],
        out_specs=[pl.BlockSpec((1, pad, DIL_OUT), lambda b: (b, 0, 0))] + [full(a) for a in bufs],
        out_shape=out_shape,
        compiler_params=_params("parallel"),
        name="dil_sample",
    )(dq, kvn, *bufs)


def _merge_out_kernel(x_ref, oa_ref, ob_ref, ga_ref, gb_ref, pa_ref, pb_ref, wo_ref, o_ref):
    pa = _dot(oa_ref[...], pa_ref[...])
    pb = _dot(ob_ref[...], pb_ref[...])
    merged = jax.nn.sigmoid(ga_ref[...].astype(F32)) * pa + jax.nn.sigmoid(gb_ref[...].astype(F32)) * pb
    o_ref[...] = x_ref[...] + _dot(merged.astype(BF16), wo_ref[...])


def _merge_out(x, oa, ob, gates, pa, pb, wo, tm):
    m = x.shape[0]
    row = lambda w: pl.BlockSpec((tm, w), lambda i: (i, 0))
    whole = lambda a: pl.BlockSpec(a.shape, lambda i: (0, 0))
    return pl.pallas_call(
        _merge_out_kernel,
        grid=(m // tm,),
        in_specs=[row(D_MODEL), row(GLA_DV), row(DIL_OUT), row(D_MODEL),
                  pl.BlockSpec((tm, D_MODEL), lambda i: (i, 1)),
                  whole(pa), whole(pb), whole(wo)],
        out_specs=row(D_MODEL),
        out_shape=jax.ShapeDtypeStruct((m, D_MODEL), F32),
        compiler_params=_params("parallel"),
        name="merge_out",
    )(x, oa, ob, gates, gates, pa, pb, wo)


FFN_COL_CHUNK = D_FF // 2


def _ffn_kernel(x_ref, n2_ref, nf_ref, wg_ref, wu_ref, wd_ref, y_ref):
    x = x_ref[...]
    h = _rms(x, n2_ref[...]).astype(BF16)
    acc = x
    for c0 in range(0, D_FF, FFN_COL_CHUNK):
        cs = slice(c0, c0 + FFN_COL_CHUNK)
        gate = _dot(h, wg_ref[:, cs])
        up = _dot(h, wu_ref[:, cs])
        act = (gate * jax.nn.sigmoid(gate) * up).astype(BF16)
        acc = acc + _dot(act, wd_ref[cs, :])
    y_ref[...] = _rms(acc, nf_ref[...])


def _ffn(x, n2, nf, wg, wu, wd, tm):
    m = x.shape[0]
    resident = lambda a: pl.BlockSpec(a.shape, lambda i: (0, 0), pipeline_mode=pl.Buffered(1))
    return pl.pallas_call(
        _ffn_kernel,
        grid=(m // tm,),
        in_specs=[pl.BlockSpec((tm, D_MODEL), lambda i: (i, 0)),
                  resident(n2), resident(nf), resident(wg), resident(wu), resident(wd)],
        out_specs=pl.BlockSpec((tm, D_MODEL), lambda i: (i, 0)),
        out_shape=jax.ShapeDtypeStruct((m, D_MODEL), F32),
        compiler_params=_params("parallel"),
        name="ffn",
    )(x, n2, nf, wg, wu, wd)


def _prep_weights(w_in, gla_gate_w2, proj_a, proj_b, w_out, w_ffn_gate, w_ffn_up, w_ffn_down):
    w = w_in[0]
    cols = lambda name: w[:, _OFF[name][0]:_OFF[name][1]]
    w_lr = jnp.concatenate([cols("glr"), jnp.zeros((D_MODEL, LANES - GLA_GATE_RANK), F32)], axis=1)
    w2 = jnp.concatenate([gla_gate_w2[0], jnp.zeros((LANES - GLA_GATE_RANK, GLA_DK), F32)], axis=0)
    return dict(
        w_gla=w[:, :GLA_COLS].astype(BF16), w_lr=w_lr.astype(BF16), w_dq=cols("dq").astype(BF16),
        w_kv=jnp.concatenate([cols("dk"), cols("dv")], axis=1).astype(BF16),
        w_gates=jnp.concatenate([cols("ga"), cols("gb")], axis=1).astype(BF16),
        w2=w2.astype(BF16), pa=proj_a[0].astype(BF16), pb=proj_b[0].astype(BF16), wo=w_out[0].astype(BF16),
        wg=w_ffn_gate[0].astype(BF16), wu=w_ffn_up[0].astype(BF16), wd=w_ffn_down[0].astype(BF16))


def _project(x2d, norm1_g, gate_b, wts, tm, chunk, n_valid):
    return _in_proj(x2d, norm1_g, wts["w_lr"], wts["w2"], gate_b,
                    [wts["w_gla"], wts["w_dq"], wts["w_kv"], wts["w_gates"]], [BF16, BF16, F32, BF16],
                    tm, chunk, n_valid)


def _tail(x2d, o_a, o_b, z_gates, wts, norm2_g, norm_f_g, tm):
    x1 = _merge_out(x2d, o_a, o_b, z_gates, wts["pa"], wts["pb"], wts["wo"], tm)
    return _ffn(x1, norm2_g, norm_f_g.reshape(1, D_MODEL), wts["wg"], wts["wu"], wts["wd"], tm)


def kernel(x_prompt, x_sample, state_gla, state_win0_k, state_win0_v, state_win1_k, state_win1_v,
           state_win2_k, state_win2_v, norm1_g, w_in, gla_gate_w2, gla_gate_b, gla_norm_g,
           proj_a, proj_b, w_out, norm2_g, w_ffn_gate, w_ffn_up, w_ffn_down, norm_f_g):
    wts = _prep_weights(w_in, gla_gate_w2, proj_a, proj_b, w_out, w_ffn_gate, w_ffn_up, w_ffn_down)
    bp, seq, _ = x_prompt.shape
    bs, n_new, _ = x_sample.shape
    width = DIL_HEADS * DIL_HD

    xp = x_prompt.reshape(bp * seq, D_MODEL)
    z_gla, z_dq, z_kv, z_gates, b_cum = _project(xp, norm1_g, gla_gate_b, wts, 512, GLA_CHUNK, GLA_CHUNK)
    s0 = jnp.zeros((bp, GLA_HEADS, GLA_HDK, GLA_HDV), F32)
    o_a, gla_p = _gla(z_gla.reshape(bp, seq, GLA_COLS), b_cum.reshape(bp, seq, GLA_DK), gla_norm_g, s0,
                      GLA_CHUNK, GLA_CHUNK, GLA_ROWS_PROMPT)
    kv_p = z_kv.reshape(bp, seq, 2 * DIL_WIDTH)
    o_b = _dil_prompt(z_dq.reshape(bp, seq, DIL_WIDTH), kv_p)
    y_prompt = _tail(xp, o_a.reshape(bp * seq, GLA_DV), o_b.reshape(bp * seq, DIL_OUT), z_gates, wts,
                     norm2_g, norm_f_g, 512).reshape(bp, seq, D_MODEL)
    win_p = []
    for g in range(DIL_GROUPS):
        keep = min(DIL_WINDOWS[g], seq)
        for off in (0, DIL_WIDTH):
            c0 = off + g * width
            win_p.append(kv_p[:, seq - keep:, c0:c0 + width].reshape(1, bp, keep, DIL_HEADS, DIL_HD))

    xs = jnp.pad(x_sample, ((0, 0), (0, SAMPLE_PAD - n_new), (0, 0))).reshape(bs * SAMPLE_PAD, D_MODEL)
    zs_gla, zs_dq, zs_kv, zs_gates, bs_cum = _project(xs, norm1_g, gla_gate_b, wts, bs * SAMPLE_PAD,
                                                      SAMPLE_PAD, n_new)
    os_a, gla_s = _gla(zs_gla.reshape(bs, SAMPLE_PAD, GLA_COLS), bs_cum.reshape(bs, SAMPLE_PAD, GLA_DK),
                       gla_norm_g, state_gla[0], SAMPLE_PAD, n_new, GLA_ROWS_SAMPLE)
    bufs = [jnp.transpose(a[0], (0, 2, 3, 1)).reshape(bs, width, a.shape[2]) for a in
            (state_win0_k, state_win0_v, state_win1_k, state_win1_v, state_win2_k, state_win2_v)]
    outs = _dil_sample(zs_dq.reshape(bs, SAMPLE_PAD, DIL_WIDTH), zs_kv.reshape(bs, SAMPLE_PAD, 2 * DIL_WIDTH),
                       bufs, n_new)
    os_b, win_s = outs[0], outs[1:]
    ys = _tail(xs, os_a.reshape(bs * SAMPLE_PAD, GLA_DV), os_b.reshape(bs * SAMPLE_PAD, DIL_OUT), zs_gates, wts,
               norm2_g, norm_f_g, bs * SAMPLE_PAD)
    y_sample = ys.reshape(bs, SAMPLE_PAD, D_MODEL)[:, :n_new]
    win_s = [jnp.transpose(a.reshape(bs, DIL_HEADS, DIL_HD, a.shape[2]), (0, 3, 1, 2))[None] for a in win_s]

    return (y_prompt, y_sample, gla_p[None], *win_p, gla_s[None], *win_s)
```

```python
import functools

import jax
import jax.numpy as jnp
from jax import lax
from jax.experimental import pallas as pl
from jax.experimental.pallas import tpu as pltpu

F32 = jnp.float32
BF16 = jnp.bfloat16

D_MODEL = 1024
GLA_HEADS = 4
GLA_DK = 512
GLA_DV = 1024
GLA_HDK = 128
GLA_HDV = 256
GLA_GATE_RANK = 16
GLA_TAU = 16.0
DIL_WINDOWS = (128, 512, 2048)
DIL_RATES = (1, 4, 16)
DIL_GROUPS = 3
DIL_HEADS = 4
DIL_HD = 64
DIL_WIDTH = 768
DIL_OUT = 256
ALIBI_MAX = 8.0
D_FF = 2816
RMS_EPS = 1e-6

LANES = 128
SUBLANES = 8
Q_TILE = 128
TILES_PER_BODY = 5
RESIDUES_PER_BODY = 8
TILE_SLOTS = max(TILES_PER_BODY, RESIDUES_PER_BODY)
GLA_CHUNK = 128
GLA_ROWS_PROMPT = 2
GLA_ROWS_SAMPLE = 4
SAMPLE_PAD = 16
MASK_VALUE = -1e30
VMEM_LIMIT_BYTES = 48 * 1024 * 1024
IN_PROJ_VMEM_LIMIT_BYTES = 56 * 1024 * 1024
MERGE_FFN_VMEM_LIMIT_BYTES = 60 * 1024 * 1024
_LN2 = 0.6931471805599453

_OFF = {}
_o = 0
for _name, _w in (("gq", GLA_DK), ("gk", GLA_DK), ("gv", GLA_DV), ("gr", GLA_DV), ("glr", GLA_GATE_RANK),
                  ("dq", DIL_WIDTH), ("dk", DIL_WIDTH), ("dv", DIL_WIDTH), ("ga", D_MODEL), ("gb", D_MODEL)):
    _OFF[_name] = (_o, _o + _w)
    _o += _w
GLA_COLS = 2 * GLA_DK + 2 * GLA_DV
GLA_SAFE_LOG_DECAY = 40.0


def _params(*sem):
    return pltpu.CompilerParams(dimension_semantics=sem, vmem_limit_bytes=VMEM_LIMIT_BYTES)


def _dot(a, b):
    return jnp.dot(a, b, preferred_element_type=F32)


def _dot_nt(a, b):
    return lax.dot_general(a, b, (((1,), (1,)), ((), ())), preferred_element_type=F32)


def _dot_tn(a, b):
    return lax.dot_general(a, b, (((0,), (0,)), ((), ())), preferred_element_type=F32)


def _rms(x, g):
    return x * lax.rsqrt(jnp.mean(x * x, axis=-1, keepdims=True) + RMS_EPS) * g


IN_PROJ_COL_CHUNK = 1024


def _split3(x):
    x1 = x.astype(BF16)
    r1 = x - x1.astype(F32)
    x2 = r1.astype(BF16)
    x3 = (r1 - x2.astype(F32)).astype(BF16)
    return x1, x2, x3


def _in_proj_kernel(x_ref, g_ref, w_ref, w2_ref, gb_ref, *out_refs, widths, chunk, n_valid):
    h = _rms(x_ref[...], g_ref[...]).astype(BF16)
    col = 0
    for o_ref, n in zip(out_refs, widths):
        for c0 in range(0, n, IN_PROJ_COL_CHUNK):
            c1 = min(c0 + IN_PROJ_COL_CHUNK, n)
            o_ref[:, c0:c1] = _dot(h, w_ref[:, col + c0:col + c1]).astype(o_ref.dtype)
        col += n

    b_ref = out_refs[len(widths)]
    tm = x_ref.shape[0]
    low_rank = _dot(h, w_ref[:, col:col + LANES]).astype(BF16)
    gate = _dot(low_rank, w2_ref[...]) + gb_ref[...]
    log_a = (jnp.minimum(gate, 0.0) - jnp.log(1.0 + jnp.exp(-jnp.abs(gate)))) * (1.0 / GLA_TAU)
    if n_valid < chunk:
        tok = lax.broadcasted_iota(jnp.int32, log_a.shape, 0) & (chunk - 1)
        log_a = jnp.where(tok < n_valid, log_a, 0.0)
    span = min(tm, LANES)
    row = lax.broadcasted_iota(jnp.int32, (span, span), 0)
    col = lax.broadcasted_iota(jnp.int32, (span, span), 1)
    same_chunk = (row & -chunk) == (col & -chunk) if chunk < span else True
    tril = jnp.where((row >= col) & same_chunk, 1.0, 0.0).astype(BF16)
    for r0 in range(0, tm, span):
        parts = _split3(log_a[r0:r0 + span])[:2]
        b_ref[r0:r0 + span, :] = sum(_dot(tril, p) for p in parts)


def _in_proj(x, g, w_packed, w2, gate_b, widths, out_dtypes, tm, chunk, n_valid):
    m, k = x.shape
    assert chunk & (chunk - 1) == 0 and (chunk % LANES == 0 or LANES % chunk == 0) and tm % chunk == 0
    assert w_packed.shape[1] == sum(widths) + LANES and all(n % LANES == 0 for n in widths)
    resident = lambda a: pl.BlockSpec(a.shape, lambda i: (0, 0), pipeline_mode=pl.Buffered(1))
    out_widths = list(widths) + [GLA_DK]
    return pl.pallas_call(
        functools.partial(_in_proj_kernel, widths=tuple(widths), chunk=chunk, n_valid=n_valid),
        grid=(m // tm,),
        in_specs=[pl.BlockSpec((tm, k), lambda i: (i, 0))] + [resident(a) for a in (g, w_packed, w2, gate_b)],
        out_specs=[pl.BlockSpec((tm, n), lambda i: (i, 0)) for n in out_widths],
        out_shape=[jax.ShapeDtypeStruct((m, n), dt) for n, dt in zip(out_widths, list(out_dtypes) + [F32])],
        compiler_params=pltpu.CompilerParams(dimension_semantics=("parallel",),
                                             vmem_limit_bytes=IN_PROJ_VMEM_LIMIT_BYTES),
        name="in_proj",
    )(x, g, w_packed, w2, gate_b)


def _gla_kernel(q_ref, k_ref, v_ref, r_ref, b_ref, ng_ref, s0_ref,
                o_ref, sout_ref, s_ref, oi_ref, kf_ref, vf_ref, *, chunk, n_valid):
    c = pl.program_id(1)
    rows = q_ref.shape[0]

    @pl.when(c == 0)
    def _():
        s_ref[...] = s0_ref[...]

    for r in range(rows):
        _gla_chunk(r, q_ref, k_ref, v_ref, r_ref, b_ref, ng_ref, o_ref, s_ref, oi_ref, chunk, n_valid)

    for r in range(rows):
        @pl.when(jnp.min(b_ref[r, chunk - 1:chunk, :]) < -GLA_SAFE_LOG_DECAY)
        def _(r=r):
            _gla_chunk_exact_intra(r, q_ref, k_ref, v_ref, r_ref, b_ref, ng_ref, o_ref, oi_ref, kf_ref, vf_ref,
                                   chunk, n_valid)

    @pl.when(c == pl.num_programs(1) - 1)
    def _():
        sout_ref[...] = s_ref[...]


def _gla_epilogue(o, r, vs, r_ref, ng_ref, o_ref):
    gr = r_ref[r, :, vs].astype(F32)
    o_ref[r, :, vs] = (_rms(o, ng_ref[...]) * (gr * jax.nn.sigmoid(gr))).astype(o_ref.dtype)


def _gla_values(r, vs, v_ref, chunk, n_valid):
    vh = v_ref[r, :, vs]
    if n_valid < chunk:
        tok = lax.broadcasted_iota(jnp.int32, vh.shape, 0)
        vh = jnp.where(tok < n_valid, vh, jnp.zeros_like(vh))
    return vh


def _gla_chunk_exact_intra(r, q_ref, k_ref, v_ref, r_ref, b_ref, ng_ref, o_ref, oi_ref, kf_ref, vf_ref,
                           chunk, n_valid):
    tok = lax.broadcasted_iota(jnp.int32, (chunk, 1), 0)
    for h in range(GLA_HEADS):
        ks = slice(h * GLA_HDK, (h + 1) * GLA_HDK)
        vs = slice(h * GLA_HDV, (h + 1) * GLA_HDV)
        bh = b_ref[r, :, ks]
        qh = q_ref[r, :, ks].astype(F32) * (GLA_HDK ** -0.5)
        kf_ref[...] = k_ref[r, :, ks].astype(F32)
        vf_ref[...] = _gla_values(r, vs, v_ref, chunk, n_valid).astype(F32)

        def eight_keys(i, acc, bh=bh, qh=qh, ks=ks):
            rows8 = pl.ds(pl.multiple_of(i * SUBLANES, SUBLANES), SUBLANES)
            b8, k8, v8 = b_ref[r, rows8, ks], kf_ref[rows8, :], vf_ref[rows8, :]
            for j in range(SUBLANES):
                s = i * SUBLANES + j
                decay = jnp.exp(jnp.where(tok >= s, bh - b8[j:j + 1], MASK_VALUE))
                w = jnp.sum(qh * decay * k8[j:j + 1], axis=-1, keepdims=True)
                acc = acc + w * v8[j:j + 1]
            return acc

        intra = lax.fori_loop(0, chunk // SUBLANES, eight_keys, jnp.zeros((chunk, GLA_HDV), F32))
        _gla_epilogue(oi_ref[r, h] + intra, r, vs, r_ref, ng_ref, o_ref)


def _gla_chunk(r, q_ref, k_ref, v_ref, r_ref, b_ref, ng_ref, o_ref, s_ref, oi_ref, chunk, n_valid):
    row = lax.broadcasted_iota(jnp.int32, (chunk, chunk), 0)
    col = lax.broadcasted_iota(jnp.int32, (chunk, chunk), 1)
    causal = row >= col
    b = b_ref[r]

    for h in range(GLA_HEADS):
        ks = slice(h * GLA_HDK, (h + 1) * GLA_HDK)
        vs = slice(h * GLA_HDV, (h + 1) * GLA_HDV)
        bh = b[:, ks]
        qh = q_ref[r, :, ks].astype(F32) * (GLA_HDK ** -0.5)
        kh = k_ref[r, :, ks].astype(F32)
        vh = _gla_values(r, vs, v_ref, chunk, n_valid)
        qt = (qh * jnp.exp(bh)).astype(BF16)
        kt = (kh * jnp.exp(-bh)).astype(BF16)
        kd = (kh * jnp.exp(bh[chunk - 1:chunk, :] - bh)).astype(BF16)
        s_old = s_ref[r, h]
        scores = jnp.where(causal, _dot_nt(qt, kt), 0.0).astype(BF16)
        o_state = _dot(qt, s_old.astype(BF16))
        oi_ref[r, h] = o_state
        dec = jnp.exp(jnp.broadcast_to(bh[chunk - 1:chunk, :], (GLA_HDK, GLA_HDK)).T)
        s_ref[r, h] = s_old * jnp.concatenate([dec, dec], axis=1) + _dot_tn(kd, vh)
        _gla_epilogue(o_state + _dot(scores, vh), r, vs, r_ref, ng_ref, o_ref)


def _gla(z, b_cum, norm_g, s0, chunk, n_valid, rows):
    bsz, t, _ = z.shape
    kern = functools.partial(_gla_kernel, chunk=chunk, n_valid=n_valid)
    state = pl.BlockSpec((rows, GLA_HEADS, GLA_HDK, GLA_HDV), lambda b, c: (b, 0, 0, 0))
    return pl.pallas_call(
        kern,
        grid=(bsz // rows, t // chunk),
        in_specs=[pl.BlockSpec((rows, chunk, GLA_DK), lambda b, c: (b, c, 0)),
                  pl.BlockSpec((rows, chunk, GLA_DK), lambda b, c: (b, c, 1)),
                  pl.BlockSpec((rows, chunk, GLA_DV), lambda b, c: (b, c, 1)),
                  pl.BlockSpec((rows, chunk, GLA_DV), lambda b, c: (b, c, 2)),
                  pl.BlockSpec((rows, chunk, GLA_DK), lambda b, c: (b, c, 0)),
                  pl.BlockSpec((1, GLA_HDV), lambda b, c: (0, 0)),
                  state],
        out_specs=[pl.BlockSpec((rows, chunk, GLA_DV), lambda b, c: (b, c, 0)), state],
        out_shape=[jax.ShapeDtypeStruct((bsz, t, GLA_DV), BF16),
                   jax.ShapeDtypeStruct((bsz, GLA_HEADS, GLA_HDK, GLA_HDV), F32)],
        scratch_shapes=[pltpu.VMEM((rows, GLA_HEADS, GLA_HDK, GLA_HDV), F32),
                        pltpu.VMEM((rows, GLA_HEADS, chunk, GLA_HDV), F32),
                        pltpu.VMEM((chunk, GLA_HDK), F32),
                        pltpu.VMEM((chunk, GLA_HDV), F32)],
        compiler_params=_params("parallel", "arbitrary"),
        name="gla",
    )(z, z, z, z, b_cum, norm_g, s0)


def _alibi_slope(g, head):
    n = DIL_GROUPS * DIL_HEADS
    return jnp.exp((-ALIBI_MAX * _LN2 / n) * (head + (g * DIL_HEADS + 1.0)))


def _pair_slopes(g, hp, rows):
    head = 2.0 * hp.astype(F32) + jnp.where(rows >= Q_TILE, 1.0, 0.0)
    return _alibi_slope(g, head)


def _dil_prompt_kernel(q0_ref, q1_ref, q2_ref, k0_ref, k1_ref, k2_ref, v0_ref, v1_ref, v2_ref,
                       o_ref, qf_ref, og_ref, ld_ref, bf_ref, br_ref, p_ref, m_ref, *, seq):
    hp = pl.program_id(1)
    q_refs = (q0_ref, q1_ref, q2_ref)
    k_refs = (k0_ref, k1_ref, k2_ref)
    v_refs = (v0_ref, v1_ref, v2_ref)

    for g in range(DIL_GROUPS):
        qf_ref[g] = q_refs[g][...].astype(F32) * (DIL_HD ** -0.5)

    for g in range(DIL_GROUPS):
        rate = float(DIL_RATES[g])
        rows = lax.broadcasted_iota(jnp.int32, (2 * Q_TILE, Q_TILE), 0)
        cols = lax.broadcasted_iota(jnp.int32, (2 * Q_TILE, Q_TILE), 1)
        dist = (rows & (Q_TILE - 1)) - cols
        slope = _pair_slopes(g, hp, rows)
        bf_ref[g] = jnp.where(dist >= 0, -slope * rate * dist.astype(F32), MASK_VALUE)
        if g < 2:
            rows = lax.broadcasted_iota(jnp.int32, (2 * Q_TILE, 2 * Q_TILE), 0)
            cols = lax.broadcasted_iota(jnp.int32, (2 * Q_TILE, 2 * Q_TILE), 1)
            dist = (rows & (Q_TILE - 1)) + Q_TILE - cols
            slope = _pair_slopes(g, hp, rows)
            ok = jnp.abs(dist - Q_TILE // 2) <= Q_TILE // 2
            br_ref[g] = jnp.where(ok, -slope * rate * dist.astype(F32), MASK_VALUE)

    lane = lax.broadcasted_iota(jnp.int32, (Q_TILE, LANES), 1)
    lo = lane < DIL_HD

    def rows_of(start, size, rate):
        return pl.ds(start, size) if rate == 1 else pl.ds(start, size, stride=rate)

    def scores_stage(g, q_row, k_row, n_keys, bias, slot):
        rate = DIL_RATES[g]
        q2 = qf_ref[g, rows_of(q_row, Q_TILE, rate), :]
        k2 = k_refs[g][rows_of(k_row, n_keys, rate), :].astype(BF16)
        qs = jnp.concatenate([jnp.where(lo, q2, 0.0), jnp.where(lo, 0.0, q2)], axis=0).astype(BF16)
        s = _dot_nt(qs, k2) + bias()
        m = jnp.max(s, axis=-1, keepdims=True)
        p_ref[slot, :, 0:n_keys] = jnp.exp(s - m).astype(BF16)
        m_ref[slot] = jnp.where(lo, m[:Q_TILE], m[Q_TILE:])

    def values_stage(g, q_row, k_row, n_keys, slot):
        rate = DIL_RATES[g]
        v2 = v_refs[g][rows_of(k_row, n_keys, rate), :].astype(BF16)
        vo = jnp.concatenate([v2, jnp.ones((n_keys, LANES), BF16)], axis=1)
        r = _dot(p_ref[slot, :, 0:n_keys], vo)
        o2 = jnp.where(lo, r[:Q_TILE, :LANES], r[Q_TILE:, :LANES])
        l2 = jnp.where(lo, r[:Q_TILE, LANES:], r[Q_TILE:, LANES:])
        og_ref[g, rows_of(q_row, Q_TILE, rate), :] = o2 / l2
        ld_ref[g, rows_of(q_row, Q_TILE, rate), :] = m_ref[slot] + jnp.log(l2)

    pending = []
    emitted = [0]

    def run_tile_sets(n_sets, n_per_set, tile_of):
        base = emitted[0]

        def slot(i, u):
            return ((i + base) % 2) * TILE_SLOTS + u

        def scores(i):
            for u in range(n_per_set):
                g, q_row, k_row, n_keys, bias = tile_of(i, u)
                scores_stage(g, q_row, k_row, n_keys, bias, slot(i, u))

        def values(i):
            for u in range(n_per_set):
                g, q_row, k_row, n_keys, _ = tile_of(i, u)
                values_stage(g, q_row, k_row, n_keys, slot(i, u))

        if pending:
            pending.pop()()
        scores(0)
        if n_sets > 1:
            def body(i, carry):
                values(i - 1)
                scores(i)
                return carry
            lax.fori_loop(1, n_sets, body, 0)
        pending.append(lambda: values(n_sets - 1))
        emitted[0] += n_sets

    for g in range(DIL_GROUPS):
        rate = DIL_RATES[g]
        n_tiles = seq // rate // Q_TILE
        span = rate * Q_TILE
        group = min(rate, RESIDUES_PER_BODY)
        n_groups = rate // group

        def first_tile(i, u, g=g, group=group):
            rho = i * group + u
            return g, rho, rho, Q_TILE, lambda: bf_ref[g]

        def consecutive_tile(i, u, g=g, span=span):
            n = 1 + i * TILES_PER_BODY + u
            return g, n * span, (n - 1) * span, 2 * Q_TILE, lambda: br_ref[g]

        def later_tile(i, u, g=g, group=group, n_groups=n_groups, span=span):
            n = 1 + i // n_groups
            rho = (i % n_groups) * group + u
            return g, rho + n * span, rho + (n - 1) * span, 2 * Q_TILE, lambda: br_ref[g]

        run_tile_sets(n_groups, group, first_tile)
        if n_tiles > 1:
            if rate == 1:
                assert (n_tiles - 1) % TILES_PER_BODY == 0
                run_tile_sets((n_tiles - 1) // TILES_PER_BODY, TILES_PER_BODY, consecutive_tile)
            else:
                run_tile_sets((n_tiles - 1) * n_groups, group, later_tile)
    pending.pop()()

    def combine(i, carry):
        rs = pl.ds(pl.multiple_of(i * Q_TILE, Q_TILE), Q_TILE)
        ld = [ld_ref[g, rs, :] for g in range(DIL_GROUPS)]
        top = jnp.maximum(jnp.maximum(ld[0], ld[1]), ld[2])
        w = [jnp.exp(x - top) for x in ld]
        num = sum(w[g] * og_ref[g, rs, :] for g in range(DIL_GROUPS))
        o_ref[rs, :] = (num / (w[0] + w[1] + w[2])).astype(o_ref.dtype)
        return carry

    lax.fori_loop(0, seq // Q_TILE, combine, 0)


def _dil_prompt(dq, kv):
    bsz, seq, _ = dq.shape
    pairs = DIL_HEADS * DIL_HD // LANES
    nblk = DIL_WIDTH // LANES

    def spec(col0):
        return pl.BlockSpec((None, seq, LANES), lambda b, hp, col0=col0: (b, 0, col0 + hp))

    in_specs = ([spec(g * pairs) for g in range(DIL_GROUPS)]
                + [spec(g * pairs) for g in range(DIL_GROUPS)]
                + [spec(nblk + g * pairs) for g in range(DIL_GROUPS)])
    return pl.pallas_call(
        functools.partial(_dil_prompt_kernel, seq=seq),
        grid=(bsz, pairs),
        in_specs=in_specs,
        out_specs=pl.BlockSpec((None, seq, LANES), lambda b, hp: (b, 0, hp)),
        out_shape=jax.ShapeDtypeStruct((bsz, seq, DIL_OUT), BF16),
        scratch_shapes=[pltpu.VMEM((DIL_GROUPS, seq, LANES), F32),
                        pltpu.VMEM((DIL_GROUPS, seq, LANES), F32),
                        pltpu.VMEM((DIL_GROUPS, seq, LANES), F32),
                        pltpu.VMEM((DIL_GROUPS, 2 * Q_TILE, Q_TILE), F32),
                        pltpu.VMEM((2, 2 * Q_TILE, 2 * Q_TILE), F32),
                        pltpu.VMEM((2 * TILE_SLOTS, 2 * Q_TILE, 2 * Q_TILE), BF16),
                        pltpu.VMEM((2 * TILE_SLOTS, Q_TILE, LANES), F32)],
        compiler_params=_params("parallel", "arbitrary"),
        name="dil_prompt",
    )(dq, dq, dq, kv, kv, kv, kv, kv, kv)


def _dil_sample_kernel(q_ref, kvn_ref, kb0_ref, vb0_ref, kb1_ref, vb1_ref, kb2_ref, vb2_ref,
                       o_ref, nk0_ref, nv0_ref, nk1_ref, nv1_ref, nk2_ref, nv2_ref, *, n_new):
    kb_refs = (kb0_ref, kb1_ref, kb2_ref)
    vb_refs = (vb0_ref, vb1_ref, vb2_ref)
    nk_refs = (nk0_ref, nk1_ref, nk2_ref)
    nv_refs = (nv0_ref, nv1_ref, nv2_ref)
    pad = q_ref.shape[1]
    nrow = DIL_HEADS * pad
    width = DIL_HEADS * DIL_HD

    lane_head = jnp.right_shift(lax.broadcasted_iota(jnp.int32, (pad, width), 1), DIL_HD.bit_length() - 1)

    def bias_for(g, dist):
        rate = DIL_RATES[g]
        rows = lax.broadcasted_iota(jnp.int32, dist.shape, 0)
        head = sum(jnp.where(rows >= h * pad, 1.0, 0.0) for h in range(1, DIL_HEADS))
        ok = (dist >= 0) & (dist <= DIL_WINDOWS[g]) & ((dist & (rate - 1)) == 0)
        return jnp.where(ok, -_alibi_slope(g, head) * dist.astype(F32), MASK_VALUE)

    sel_r = lax.broadcasted_iota(jnp.int32, (pad, LANES), 0)
    sel_c = lax.broadcasted_iota(jnp.int32, (pad, LANES), 1)
    place = jnp.where((sel_r < n_new) & (sel_c == sel_r + (LANES - n_new)), 1.0, 0.0).astype(BF16)
    tail_lane = lax.broadcasted_iota(jnp.int32, (width, LANES), 1) >= LANES - n_new

    def shifted(buf_t, new_rows):
        length = buf_t.shape[1]
        rolled = pltpu.roll(buf_t, length - n_new, 1)
        new_t = sum(_dot_tn(part, place) for part in _split3(new_rows))
        last = jnp.where(tail_lane, new_t, rolled[:, length - LANES:])
        return rolled, last

    scores, values, transposed = [], [], []
    for g in range(DIL_GROUPS):
        length = kb_refs[g].shape[2]
        cs = slice(g * width, (g + 1) * width)
        vcs = slice(DIL_WIDTH + g * width, DIL_WIDTH + (g + 1) * width)
        qg = q_ref[0, :, cs].astype(F32) * (DIL_HD ** -0.5)
        qs = jnp.concatenate([jnp.where(lane_head == h, qg, 0.0) for h in range(DIL_HEADS)], axis=0).astype(BF16)
        kb, vb = kb_refs[g][0], vb_refs[g][0]
        kn, vn = kvn_ref[0, :, cs], kvn_ref[0, :, vcs]
        rows = lax.broadcasted_iota(jnp.int32, (nrow, length), 0)
        cols = lax.broadcasted_iota(jnp.int32, (nrow, length), 1)
        dist = length + (rows & (pad - 1)) - cols
        scores.append(_dot(qs, kb.astype(BF16)) + bias_for(g, dist))
        values.append(vb.astype(BF16))
        transposed.append(True)
        rows = lax.broadcasted_iota(jnp.int32, (nrow, pad), 0)
        cols = lax.broadcasted_iota(jnp.int32, (nrow, pad), 1)
        dist = jnp.where(cols < n_new, (rows & (pad - 1)) - cols, -1)
        scores.append(_dot_nt(qs, kn.astype(BF16)) + bias_for(g, dist))
        values.append(vn.astype(BF16))
        transposed.append(False)
        for buf, new, out_ref in ((kb, kn, nk_refs[g]), (vb, vn, nv_refs[g])):
            rolled, last = shifted(buf, new)
            if length > LANES:
                out_ref[0, :, 0:length - LANES] = rolled[:, 0:length - LANES]
            out_ref[0, :, length - LANES:length] = last

    top = functools.reduce(jnp.maximum, [jnp.max(s, axis=-1, keepdims=True) for s in scores])
    probs = [jnp.exp(s - top) for s in scores]
    den = sum(jnp.sum(p, axis=-1, keepdims=True) for p in probs)
    acc = sum((_dot_nt if t else _dot)(p.astype(BF16), v)
              for p, v, t in zip(probs, values, transposed)) / den
    out = sum(jnp.where(lane_head == h, acc[h * pad:(h + 1) * pad], 0.0) for h in range(DIL_HEADS))
    o_ref[0] = out.astype(o_ref.dtype)


def _dil_sample(dq, kvn, bufs, n_new):
    bsz, pad, _ = dq.shape
    full = lambda a: pl.BlockSpec((1,) + a.shape[1:], lambda b: (b, 0, 0))
    out_shape = ([jax.ShapeDtypeStruct((bsz, pad, DIL_OUT), BF16)]
                 + [jax.ShapeDtypeStruct(a.shape, a.dtype) for a in bufs])
    return pl.pallas_call(
        functools.partial(_dil_sample_kernel, n_new=n_new),
        grid=(bsz,),
        in_specs=[full(dq), full(kvn)] + [full(a) for a in bufs],
        out_specs=[pl.BlockSpec((1, pad, DIL_OUT), lambda b: (b, 0, 0))] + [full(a) for a in bufs],
        out_shape=out_shape,
        compiler_params=_params("parallel"),
        name="dil_sample",
    )(dq, kvn, *bufs)


FFN_COL_CHUNK = D_FF // 2


def _merge_ffn_kernel(x_ref, oa_ref, ob_ref, ga_ref, gb_ref, pa_ref, pb_ref, wo_ref, n2_ref, nf_ref,
                      wg_ref, wu_ref, wd_ref, y_ref):
    pa = _dot(oa_ref[...], pa_ref[...])
    pb = _dot(ob_ref[...], pb_ref[...])
    merged = jax.nn.sigmoid(ga_ref[...].astype(F32)) * pa + jax.nn.sigmoid(gb_ref[...].astype(F32)) * pb
    x1 = x_ref[...] + _dot(merged.astype(BF16), wo_ref[...])
    h = _rms(x1, n2_ref[...]).astype(BF16)
    acc = x1
    for c0 in range(0, D_FF, FFN_COL_CHUNK):
        cs = slice(c0, c0 + FFN_COL_CHUNK)
        gate = _dot(h, wg_ref[:, cs])
        up = _dot(h, wu_ref[:, cs])
        act = (gate * jax.nn.sigmoid(gate) * up).astype(BF16)
        acc = acc + _dot(act, wd_ref[cs, :])
    y_ref[...] = _rms(acc, nf_ref[...])


def _merge_ffn(x, oa, ob, gates, pa, pb, wo, n2, nf, wg, wu, wd, tm):
    m = x.shape[0]
    row = lambda w: pl.BlockSpec((tm, w), lambda i: (i, 0))
    resident = lambda a: pl.BlockSpec(a.shape, lambda i: (0, 0), pipeline_mode=pl.Buffered(1))
    return pl.pallas_call(
        _merge_ffn_kernel,
        grid=(m // tm,),
        in_specs=[row(D_MODEL), row(GLA_DV), row(DIL_OUT), row(D_MODEL),
                  pl.BlockSpec((tm, D_MODEL), lambda i: (i, 1))]
                 + [resident(a) for a in (pa, pb, wo, n2, nf, wg, wu, wd)],
        out_specs=row(D_MODEL),
        out_shape=jax.ShapeDtypeStruct((m, D_MODEL), F32),
        compiler_params=pltpu.CompilerParams(dimension_semantics=("parallel",),
                                             vmem_limit_bytes=MERGE_FFN_VMEM_LIMIT_BYTES),
        name="merge_ffn",
    )(x, oa, ob, gates, gates, pa, pb, wo, n2, nf, wg, wu, wd)


def _prep_weights(w_in, gla_gate_w2, proj_a, proj_b, w_out, w_ffn_gate, w_ffn_up, w_ffn_down):
    w = w_in[0]
    cols = lambda name: w[:, _OFF[name][0]:_OFF[name][1]]
    w_packed = jnp.concatenate(
        [w[:, :_OFF["gr"][1]], w[:, _OFF["dq"][0]:], cols("glr"),
         jnp.zeros((D_MODEL, LANES - GLA_GATE_RANK), F32)], axis=1).astype(BF16)
    w2 = jnp.concatenate([gla_gate_w2[0], jnp.zeros((LANES - GLA_GATE_RANK, GLA_DK), F32)], axis=0)
    return dict(
        w_packed=w_packed,
        w2=w2.astype(BF16), pa=proj_a[0].astype(BF16), pb=proj_b[0].astype(BF16), wo=w_out[0].astype(BF16),
        wg=w_ffn_gate[0].astype(BF16), wu=w_ffn_up[0].astype(BF16), wd=w_ffn_down[0].astype(BF16))


def _project(x2d, norm1_g, gate_b, wts, tm, chunk, n_valid):
    return _in_proj(x2d, norm1_g, wts["w_packed"], wts["w2"], gate_b,
                    [GLA_COLS, DIL_WIDTH, 2 * DIL_WIDTH, 2 * D_MODEL], [BF16, BF16, F32, BF16],
                    tm, chunk, n_valid)


def _tail(x2d, o_a, o_b, z_gates, wts, norm2_g, norm_f_g, tm):
    return _merge_ffn(x2d, o_a, o_b, z_gates, wts["pa"], wts["pb"], wts["wo"], norm2_g,
                      norm_f_g.reshape(1, D_MODEL), wts["wg"], wts["wu"], wts["wd"], tm)


def kernel(x_prompt, x_sample, state_gla, state_win0_k, state_win0_v, state_win1_k, state_win1_v,
           state_win2_k, state_win2_v, norm1_g, w_in, gla_gate_w2, gla_gate_b, gla_norm_g,
           proj_a, proj_b, w_out, norm2_g, w_ffn_gate, w_ffn_up, w_ffn_down, norm_f_g):
    wts = _prep_weights(w_in, gla_gate_w2, proj_a, proj_b, w_out, w_ffn_gate, w_ffn_up, w_ffn_down)
    bp, seq, _ = x_prompt.shape
    bs, n_new, _ = x_sample.shape
    width = DIL_HEADS * DIL_HD

    xp = x_prompt.reshape(bp * seq, D_MODEL)
    z_gla, z_dq, z_kv, z_gates, b_cum = _project(xp, norm1_g, gla_gate_b, wts, 512, GLA_CHUNK, GLA_CHUNK)
    s0 = jnp.zeros((bp, GLA_HEADS, GLA_HDK, GLA_HDV), F32)
    o_a, gla_p = _gla(z_gla.reshape(bp, seq, GLA_COLS), b_cum.reshape(bp, seq, GLA_DK), gla_norm_g, s0,
                      GLA_CHUNK, GLA_CHUNK, GLA_ROWS_PROMPT)
    kv_p = z_kv.reshape(bp, seq, 2 * DIL_WIDTH)
    o_b = _dil_prompt(z_dq.reshape(bp, seq, DIL_WIDTH), kv_p)
    y_prompt = _tail(xp, o_a.reshape(bp * seq, GLA_DV), o_b.reshape(bp * seq, DIL_OUT), z_gates, wts,
                     norm2_g, norm_f_g, 512).reshape(bp, seq, D_MODEL)
    win_p = []
    for g in range(DIL_GROUPS):
        keep = min(DIL_WINDOWS[g], seq)
        for off in (0, DIL_WIDTH):
            c0 = off + g * width
            win_p.append(kv_p[:, seq - keep:, c0:c0 + width].reshape(1, bp, keep, DIL_HEADS, DIL_HD))

    xs = jnp.pad(x_sample, ((0, 0), (0, SAMPLE_PAD - n_new), (0, 0))).reshape(bs * SAMPLE_PAD, D_MODEL)
    zs_gla, zs_dq, zs_kv, zs_gates, bs_cum = _project(xs, norm1_g, gla_gate_b, wts, bs * SAMPLE_PAD,
                                                      SAMPLE_PAD, n_new)
    os_a, gla_s = _gla(zs_gla.reshape(bs, SAMPLE_PAD, GLA_COLS), bs_cum.reshape(bs, SAMPLE_PAD, GLA_DK),
                       gla_norm_g, state_gla[0], SAMPLE_PAD, n_new, GLA_ROWS_SAMPLE)
    bufs = [jnp.transpose(a[0], (0, 2, 3, 1)).reshape(bs, width, a.shape[2]) for a in
            (state_win0_k, state_win0_v, state_win1_k, state_win1_v, state_win2_k, state_win2_v)]
    outs = _dil_sample(zs_dq.reshape(bs, SAMPLE_PAD, DIL_WIDTH), zs_kv.reshape(bs, SAMPLE_PAD, 2 * DIL_WIDTH),
                       bufs, n_new)
    os_b, win_s = outs[0], outs[1:]
    ys = _tail(xs, os_a.reshape(bs * SAMPLE_PAD, GLA_DV), os_b.reshape(bs * SAMPLE_PAD, DIL_OUT), zs_gates, wts,
               norm2_g, norm_f_g, bs * SAMPLE_PAD)
    y_sample = ys.reshape(bs, SAMPLE_PAD, D_MODEL)[:, :n_new]
    win_s = [jnp.transpose(a.reshape(bs, DIL_HEADS, DIL_HD, a.shape[2]), (0, 3, 1, 2))[None] for a in win_s]

    return (y_prompt, y_sample, gla_p[None], *win_p, gla_s[None], *win_s)
```

```python
import functools

import jax
import jax.numpy as jnp
from jax import lax
from jax.experimental import pallas as pl
from jax.experimental.pallas import tpu as pltpu

F32 = jnp.float32
BF16 = jnp.bfloat16

D_MODEL = 1024
GLA_HEADS = 4
GLA_DK = 512
GLA_DV = 1024
GLA_HDK = 128
GLA_HDV = 256
GLA_GATE_RANK = 16
GLA_TAU = 16.0
DIL_WINDOWS = (128, 512, 2048)
DIL_RATES = (1, 4, 16)
DIL_GROUPS = 3
DIL_HEADS = 4
DIL_HD = 64
DIL_WIDTH = 768
DIL_OUT = 256
ALIBI_MAX = 8.0
D_FF = 2816
RMS_EPS = 1e-6

LANES = 128
SUBLANES = 8
BF16_SUBLANES = 16
Q_TILE = 128
TILES_PER_BODY = 5
RESIDUES_PER_BODY = 8
TILE_SLOTS = max(TILES_PER_BODY, RESIDUES_PER_BODY)
GLA_CHUNK = 128
GLA_ROWS_PROMPT = 2
GLA_ROWS_SAMPLE = 4
SAMPLE_PAD = 16
MASK_VALUE = -1e30
VMEM_LIMIT_BYTES = 48 * 1024 * 1024
IN_PROJ_VMEM_LIMIT_BYTES = 56 * 1024 * 1024
MERGE_FFN_VMEM_LIMIT_BYTES = 60 * 1024 * 1024
_LN2 = 0.6931471805599453

_OFF = {}
_o = 0
for _name, _w in (("gq", GLA_DK), ("gk", GLA_DK), ("gv", GLA_DV), ("gr", GLA_DV), ("glr", GLA_GATE_RANK),
                  ("dq", DIL_WIDTH), ("dk", DIL_WIDTH), ("dv", DIL_WIDTH), ("ga", D_MODEL), ("gb", D_MODEL)):
    _OFF[_name] = (_o, _o + _w)
    _o += _w
GLA_COLS = 2 * GLA_DK + 2 * GLA_DV
GLA_SAFE_LOG_DECAY = 40.0


def _params(*sem):
    return pltpu.CompilerParams(dimension_semantics=sem, vmem_limit_bytes=VMEM_LIMIT_BYTES)


def _dot(a, b):
    return jnp.dot(a, b, preferred_element_type=F32)


def _dot_nt(a, b):
    return lax.dot_general(a, b, (((1,), (1,)), ((), ())), preferred_element_type=F32)


def _dot_tn(a, b):
    return lax.dot_general(a, b, (((0,), (0,)), ((), ())), preferred_element_type=F32)


def _rms(x, g):
    return x * lax.rsqrt(jnp.mean(x * x, axis=-1, keepdims=True) + RMS_EPS) * g


IN_PROJ_COL_CHUNK = 1024


def _split3(x):
    x1 = x.astype(BF16)
    r1 = x - x1.astype(F32)
    x2 = r1.astype(BF16)
    x3 = (r1 - x2.astype(F32)).astype(BF16)
    return x1, x2, x3


def _in_proj_kernel(x_ref, g_ref, wt_ref, w2_ref, gb_ref, *out_refs, groups, low_rank_col, chunk, n_valid):
    h = _rms(x_ref[...], g_ref[...]).astype(BF16)
    for o_ref, (col, n) in zip(out_refs, groups):
        for c0 in range(0, n, IN_PROJ_COL_CHUNK):
            c1 = min(c0 + IN_PROJ_COL_CHUNK, n)
            o_ref[:, c0:c1] = _dot_nt(h, wt_ref[col + c0:col + c1, :]).astype(o_ref.dtype)

    b_ref = out_refs[len(groups)]
    tm = x_ref.shape[0]
    low_rank = _dot_nt(h, wt_ref[low_rank_col:low_rank_col + LANES, :]).astype(BF16)
    gate = _dot(low_rank, w2_ref[...]) + gb_ref[...]
    log_a = (jnp.minimum(gate, 0.0) - jnp.log(1.0 + jnp.exp(-jnp.abs(gate)))) * (1.0 / GLA_TAU)
    if n_valid < chunk:
        tok = lax.broadcasted_iota(jnp.int32, log_a.shape, 0) & (chunk - 1)
        log_a = jnp.where(tok < n_valid, log_a, 0.0)
    span = min(tm, LANES)
    row = lax.broadcasted_iota(jnp.int32, (span, span), 0)
    col = lax.broadcasted_iota(jnp.int32, (span, span), 1)
    same_chunk = (row & -chunk) == (col & -chunk) if chunk < span else True
    tril = jnp.where((row >= col) & same_chunk, 1.0, 0.0).astype(BF16)
    for r0 in range(0, tm, span):
        parts = _split3(log_a[r0:r0 + span])[:2]
        b_ref[r0:r0 + span, :] = sum(_dot(tril, p) for p in parts)


def _in_proj(x, g, w_t, w2, gate_b, groups, low_rank_col, out_dtypes, tm, chunk, n_valid):
    m, k = x.shape
    assert chunk & (chunk - 1) == 0 and (chunk % LANES == 0 or LANES % chunk == 0) and tm % chunk == 0
    assert all(c % BF16_SUBLANES == 0 and n % LANES == 0 for c, n in groups) and low_rank_col % BF16_SUBLANES == 0
    resident = lambda a: pl.BlockSpec(a.shape, lambda i: (0, 0), pipeline_mode=pl.Buffered(1))
    out_widths = [n for _, n in groups] + [GLA_DK]
    return pl.pallas_call(
        functools.partial(_in_proj_kernel, groups=tuple(groups), low_rank_col=low_rank_col,
                          chunk=chunk, n_valid=n_valid),
        grid=(m // tm,),
        in_specs=[pl.BlockSpec((tm, k), lambda i: (i, 0))] + [resident(a) for a in (g, w_t, w2, gate_b)],
        out_specs=[pl.BlockSpec((tm, n), lambda i: (i, 0)) for n in out_widths],
        out_shape=[jax.ShapeDtypeStruct((m, n), dt) for n, dt in zip(out_widths, list(out_dtypes) + [F32])],
        compiler_params=pltpu.CompilerParams(dimension_semantics=("parallel",),
                                             vmem_limit_bytes=IN_PROJ_VMEM_LIMIT_BYTES),
        name="in_proj",
    )(x, g, w_t, w2, gate_b)


def _gla_kernel(q_ref, k_ref, v_ref, r_ref, b_ref, ng_ref, s0_ref,
                o_ref, sout_ref, s_ref, oi_ref, kf_ref, vf_ref, *, chunk, n_valid):
    c = pl.program_id(1)
    rows = q_ref.shape[0]

    @pl.when(c == 0)
    def _():
        s_ref[...] = s0_ref[...]

    for r in range(rows):
        _gla_chunk(r, q_ref, k_ref, v_ref, r_ref, b_ref, ng_ref, o_ref, s_ref, oi_ref, chunk, n_valid)

    for r in range(rows):
        @pl.when(jnp.min(b_ref[r, chunk - 1:chunk, :]) < -GLA_SAFE_LOG_DECAY)
        def _(r=r):
            _gla_chunk_exact_intra(r, q_ref, k_ref, v_ref, r_ref, b_ref, ng_ref, o_ref, oi_ref, kf_ref, vf_ref,
                                   chunk, n_valid)

    @pl.when(c == pl.num_programs(1) - 1)
    def _():
        sout_ref[...] = s_ref[...]


def _gla_epilogue(o, r, vs, r_ref, ng_ref, o_ref):
    gr = r_ref[r, :, vs].astype(F32)
    o_ref[r, :, vs] = (_rms(o, ng_ref[...]) * (gr * jax.nn.sigmoid(gr))).astype(o_ref.dtype)


def _gla_values(r, vs, v_ref, chunk, n_valid):
    vh = v_ref[r, :, vs]
    if n_valid < chunk:
        tok = lax.broadcasted_iota(jnp.int32, vh.shape, 0)
        vh = jnp.where(tok < n_valid, vh, jnp.zeros_like(vh))
    return vh


def _gla_chunk_exact_intra(r, q_ref, k_ref, v_ref, r_ref, b_ref, ng_ref, o_ref, oi_ref, kf_ref, vf_ref,
                           chunk, n_valid):
    tok = lax.broadcasted_iota(jnp.int32, (chunk, 1), 0)
    for h in range(GLA_HEADS):
        ks = slice(h * GLA_HDK, (h + 1) * GLA_HDK)
        vs = slice(h * GLA_HDV, (h + 1) * GLA_HDV)
        bh = b_ref[r, :, ks]
        qh = q_ref[r, :, ks].astype(F32) * (GLA_HDK ** -0.5)
        kf_ref[...] = k_ref[r, :, ks].astype(F32)
        vf_ref[...] = _gla_values(r, vs, v_ref, chunk, n_valid).astype(F32)

        def eight_keys(i, acc, bh=bh, qh=qh, ks=ks):
            rows8 = pl.ds(pl.multiple_of(i * SUBLANES, SUBLANES), SUBLANES)
            b8, k8, v8 = b_ref[r, rows8, ks], kf_ref[rows8, :], vf_ref[rows8, :]
            for j in range(SUBLANES):
                s = i * SUBLANES + j
                decay = jnp.exp(jnp.where(tok >= s, bh - b8[j:j + 1], MASK_VALUE))
                w = jnp.sum(qh * decay * k8[j:j + 1], axis=-1, keepdims=True)
                acc = acc + w * v8[j:j + 1]
            return acc

        intra = lax.fori_loop(0, chunk // SUBLANES, eight_keys, jnp.zeros((chunk, GLA_HDV), F32))
        _gla_epilogue(oi_ref[r, h] + intra, r, vs, r_ref, ng_ref, o_ref)


def _gla_chunk(r, q_ref, k_ref, v_ref, r_ref, b_ref, ng_ref, o_ref, s_ref, oi_ref, chunk, n_valid):
    row = lax.broadcasted_iota(jnp.int32, (chunk, chunk), 0)
    col = lax.broadcasted_iota(jnp.int32, (chunk, chunk), 1)
    causal = row >= col
    b = b_ref[r]

    for h in range(GLA_HEADS):
        ks = slice(h * GLA_HDK, (h + 1) * GLA_HDK)
        vs = slice(h * GLA_HDV, (h + 1) * GLA_HDV)
        bh = b[:, ks]
        qh = q_ref[r, :, ks].astype(F32) * (GLA_HDK ** -0.5)
        kh = k_ref[r, :, ks].astype(F32)
        vh = _gla_values(r, vs, v_ref, chunk, n_valid)
        qt = (qh * jnp.exp(bh)).astype(BF16)
        kt = (kh * jnp.exp(-bh)).astype(BF16)
        kd = (kh * jnp.exp(bh[chunk - 1:chunk, :] - bh)).astype(BF16)
        s_old = s_ref[r, h]
        scores = jnp.where(causal, _dot_nt(qt, kt), 0.0).astype(BF16)
        o_state = _dot(qt, s_old.astype(BF16))
        oi_ref[r, h] = o_state
        dec = jnp.exp(jnp.broadcast_to(bh[chunk - 1:chunk, :], (GLA_HDK, GLA_HDK)).T)
        s_ref[r, h] = s_old * jnp.concatenate([dec, dec], axis=1) + _dot_tn(kd, vh)
        _gla_epilogue(o_state + _dot(scores, vh), r, vs, r_ref, ng_ref, o_ref)


def _gla(z, b_cum, norm_g, s0, chunk, n_valid, rows):
    bsz, t, _ = z.shape
    kern = functools.partial(_gla_kernel, chunk=chunk, n_valid=n_valid)
    state = pl.BlockSpec((rows, GLA_HEADS, GLA_HDK, GLA_HDV), lambda b, c: (b, 0, 0, 0))
    return pl.pallas_call(
        kern,
        grid=(bsz // rows, t // chunk),
        in_specs=[pl.BlockSpec((rows, chunk, GLA_DK), lambda b, c: (b, c, 0)),
                  pl.BlockSpec((rows, chunk, GLA_DK), lambda b, c: (b, c, 1)),
                  pl.BlockSpec((rows, chunk, GLA_DV), lambda b, c: (b, c, 1)),
                  pl.BlockSpec((rows, chunk, GLA_DV), lambda b, c: (b, c, 2)),
                  pl.BlockSpec((rows, chunk, GLA_DK), lambda b, c: (b, c, 0)),
                  pl.BlockSpec((1, GLA_HDV), lambda b, c: (0, 0)),
                  state],
        out_specs=[pl.BlockSpec((rows, chunk, GLA_DV), lambda b, c: (b, c, 0)), state],
        out_shape=[jax.ShapeDtypeStruct((bsz, t, GLA_DV), BF16),
                   jax.ShapeDtypeStruct((bsz, GLA_HEADS, GLA_HDK, GLA_HDV), F32)],
        scratch_shapes=[pltpu.VMEM((rows, GLA_HEADS, GLA_HDK, GLA_HDV), F32),
                        pltpu.VMEM((rows, GLA_HEADS, chunk, GLA_HDV), F32),
                        pltpu.VMEM((chunk, GLA_HDK), F32),
                        pltpu.VMEM((chunk, GLA_HDV), F32)],
        compiler_params=_params("parallel", "arbitrary"),
        name="gla",
    )(z, z, z, z, b_cum, norm_g, s0)


def _alibi_slope(g, head):
    n = DIL_GROUPS * DIL_HEADS
    return jnp.exp((-ALIBI_MAX * _LN2 / n) * (head + (g * DIL_HEADS + 1.0)))


def _pair_slopes(g, hp, rows):
    head = 2.0 * hp.astype(F32) + jnp.where(rows >= Q_TILE, 1.0, 0.0)
    return _alibi_slope(g, head)


def _dil_prompt_kernel(q0_ref, q1_ref, q2_ref, k0_ref, k1_ref, k2_ref, v0_ref, v1_ref, v2_ref,
                       o_ref, qf_ref, og_ref, ld_ref, bf_ref, br_ref, p_ref, m_ref, *, seq):
    hp = pl.program_id(1)
    q_refs = (q0_ref, q1_ref, q2_ref)
    k_refs = (k0_ref, k1_ref, k2_ref)
    v_refs = (v0_ref, v1_ref, v2_ref)

    for g in range(DIL_GROUPS):
        qf_ref[g] = q_refs[g][...].astype(F32) * (DIL_HD ** -0.5)

    for g in range(DIL_GROUPS):
        rate = float(DIL_RATES[g])
        rows = lax.broadcasted_iota(jnp.int32, (2 * Q_TILE, Q_TILE), 0)
        cols = lax.broadcasted_iota(jnp.int32, (2 * Q_TILE, Q_TILE), 1)
        dist = (rows & (Q_TILE - 1)) - cols
        slope = _pair_slopes(g, hp, rows)
        bf_ref[g] = jnp.where(dist >= 0, -slope * rate * dist.astype(F32), MASK_VALUE)
        if g < 2:
            rows = lax.broadcasted_iota(jnp.int32, (2 * Q_TILE, 2 * Q_TILE), 0)
            cols = lax.broadcasted_iota(jnp.int32, (2 * Q_TILE, 2 * Q_TILE), 1)
            dist = (rows & (Q_TILE - 1)) + Q_TILE - cols
            slope = _pair_slopes(g, hp, rows)
            ok = jnp.abs(dist - Q_TILE // 2) <= Q_TILE // 2
            br_ref[g] = jnp.where(ok, -slope * rate * dist.astype(F32), MASK_VALUE)

    lane = lax.broadcasted_iota(jnp.int32, (Q_TILE, LANES), 1)
    lo = lane < DIL_HD

    def rows_of(start, size, rate):
        return pl.ds(start, size) if rate == 1 else pl.ds(start, size, stride=rate)

    def scores_stage(g, q_row, k_row, n_keys, bias, slot):
        rate = DIL_RATES[g]
        q2 = qf_ref[g, rows_of(q_row, Q_TILE, rate), :]
        k2 = k_refs[g][rows_of(k_row, n_keys, rate), :].astype(BF16)
        qs = jnp.concatenate([jnp.where(lo, q2, 0.0), jnp.where(lo, 0.0, q2)], axis=0).astype(BF16)
        s = _dot_nt(qs, k2) + bias()
        m = jnp.max(s, axis=-1, keepdims=True)
        p_ref[slot, :, 0:n_keys] = jnp.exp(s - m).astype(BF16)
        m_ref[slot] = jnp.where(lo, m[:Q_TILE], m[Q_TILE:])

    def values_stage(g, q_row, k_row, n_keys, slot):
        rate = DIL_RATES[g]
        v2 = v_refs[g][rows_of(k_row, n_keys, rate), :].astype(BF16)
        vo = jnp.concatenate([v2, jnp.ones((n_keys, LANES), BF16)], axis=1)
        r = _dot(p_ref[slot, :, 0:n_keys], vo)
        o2 = jnp.where(lo, r[:Q_TILE, :LANES], r[Q_TILE:, :LANES])
        l2 = jnp.where(lo, r[:Q_TILE, LANES:], r[Q_TILE:, LANES:])
        og_ref[g, rows_of(q_row, Q_TILE, rate), :] = o2 / l2
        ld_ref[g, rows_of(q_row, Q_TILE, rate), :] = m_ref[slot] + jnp.log(l2)

    pending = []
    emitted = [0]

    def run_tile_sets(n_sets, n_per_set, tile_of):
        base = emitted[0]

        def slot(i, u):
            return ((i + base) % 2) * TILE_SLOTS + u

        def scores(i):
            for u in range(n_per_set):
                g, q_row, k_row, n_keys, bias = tile_of(i, u)
                scores_stage(g, q_row, k_row, n_keys, bias, slot(i, u))

        def values(i):
            for u in range(n_per_set):
                g, q_row, k_row, n_keys, _ = tile_of(i, u)
                values_stage(g, q_row, k_row, n_keys, slot(i, u))

        if pending:
            pending.pop()()
        scores(0)
        if n_sets > 1:
            def body(i, carry):
                values(i - 1)
                scores(i)
                return carry
            lax.fori_loop(1, n_sets, body, 0)
        pending.append(lambda: values(n_sets - 1))
        emitted[0] += n_sets

    for g in range(DIL_GROUPS):
        rate = DIL_RATES[g]
        n_tiles = seq // rate // Q_TILE
        span = rate * Q_TILE
        group = min(rate, RESIDUES_PER_BODY)
        n_groups = rate // group

        def first_tile(i, u, g=g, group=group):
            rho = i * group + u
            return g, rho, rho, Q_TILE, lambda: bf_ref[g]

        def consecutive_tile(i, u, g=g, span=span):
            n = 1 + i * TILES_PER_BODY + u
            return g, n * span, (n - 1) * span, 2 * Q_TILE, lambda: br_ref[g]

        def later_tile(i, u, g=g, group=group, n_groups=n_groups, span=span):
            n = 1 + i // n_groups
            rho = (i % n_groups) * group + u
            return g, rho + n * span, rho + (n - 1) * span, 2 * Q_TILE, lambda: br_ref[g]

        run_tile_sets(n_groups, group, first_tile)
        if n_tiles > 1:
            if rate == 1:
                assert (n_tiles - 1) % TILES_PER_BODY == 0
                run_tile_sets((n_tiles - 1) // TILES_PER_BODY, TILES_PER_BODY, consecutive_tile)
            else:
                run_tile_sets((n_tiles - 1) * n_groups, group, later_tile)
    pending.pop()()

    def combine(i, carry):
        rs = pl.ds(pl.multiple_of(i * Q_TILE, Q_TILE), Q_TILE)
        ld = [ld_ref[g, rs, :] for g in range(DIL_GROUPS)]
        top = jnp.maximum(jnp.maximum(ld[0], ld[1]), ld[2])
        w = [jnp.exp(x - top) for x in ld]
        num = sum(w[g] * og_ref[g, rs, :] for g in range(DIL_GROUPS))
        o_ref[rs, :] = (num / (w[0] + w[1] + w[2])).astype(o_ref.dtype)
        return carry

    lax.fori_loop(0, seq // Q_TILE, combine, 0)


def _dil_prompt(dq, kv):
    bsz, seq, _ = dq.shape
    pairs = DIL_HEADS * DIL_HD // LANES
    nblk = DIL_WIDTH // LANES

    def spec(col0):
        return pl.BlockSpec((None, seq, LANES), lambda b, hp, col0=col0: (b, 0, col0 + hp))

    in_specs = ([spec(g * pairs) for g in range(DIL_GROUPS)]
                + [spec(g * pairs) for g in range(DIL_GROUPS)]
                + [spec(nblk + g * pairs) for g in range(DIL_GROUPS)])
    return pl.pallas_call(
        functools.partial(_dil_prompt_kernel, seq=seq),
        grid=(bsz, pairs),
        in_specs=in_specs,
        out_specs=pl.BlockSpec((None, seq, LANES), lambda b, hp: (b, 0, hp)),
        out_shape=jax.ShapeDtypeStruct((bsz, seq, DIL_OUT), BF16),
        scratch_shapes=[pltpu.VMEM((DIL_GROUPS, seq, LANES), F32),
                        pltpu.VMEM((DIL_GROUPS, seq, LANES), F32),
                        pltpu.VMEM((DIL_GROUPS, seq, LANES), F32),
                        pltpu.VMEM((DIL_GROUPS, 2 * Q_TILE, Q_TILE), F32),
                        pltpu.VMEM((2, 2 * Q_TILE, 2 * Q_TILE), F32),
                        pltpu.VMEM((2 * TILE_SLOTS, 2 * Q_TILE, 2 * Q_TILE), BF16),
                        pltpu.VMEM((2 * TILE_SLOTS, Q_TILE, LANES), F32)],
        compiler_params=_params("parallel", "arbitrary"),
        name="dil_prompt",
    )(dq, dq, dq, kv, kv, kv, kv, kv, kv)


def _dil_sample_kernel(q_ref, kvn_ref, kb0_ref, vb0_ref, kb1_ref, vb1_ref, kb2_ref, vb2_ref,
                       o_ref, nk0_ref, nv0_ref, nk1_ref, nv1_ref, nk2_ref, nv2_ref, *, n_new):
    kb_refs = (kb0_ref, kb1_ref, kb2_ref)
    vb_refs = (vb0_ref, vb1_ref, vb2_ref)
    nk_refs = (nk0_ref, nk1_ref, nk2_ref)
    nv_refs = (nv0_ref, nv1_ref, nv2_ref)
    pad = q_ref.shape[1]
    nrow = DIL_HEADS * pad
    width = DIL_HEADS * DIL_HD

    lane_head = jnp.right_shift(lax.broadcasted_iota(jnp.int32, (pad, width), 1), DIL_HD.bit_length() - 1)

    def bias_for(g, dist):
        rate = DIL_RATES[g]
        rows = lax.broadcasted_iota(jnp.int32, dist.shape, 0)
        head = sum(jnp.where(rows >= h * pad, 1.0, 0.0) for h in range(1, DIL_HEADS))
        ok = (dist >= 0) & (dist <= DIL_WINDOWS[g]) & ((dist & (rate - 1)) == 0)
        return jnp.where(ok, -_alibi_slope(g, head) * dist.astype(F32), MASK_VALUE)

    sel_r = lax.broadcasted_iota(jnp.int32, (pad, LANES), 0)
    sel_c = lax.broadcasted_iota(jnp.int32, (pad, LANES), 1)
    place = jnp.where((sel_r < n_new) & (sel_c == sel_r + (LANES - n_new)), 1.0, 0.0).astype(BF16)
    tail_lane = lax.broadcasted_iota(jnp.int32, (width, LANES), 1) >= LANES - n_new

    def shifted(buf_t, new_rows):
        length = buf_t.shape[1]
        rolled = pltpu.roll(buf_t, length - n_new, 1)
        new_t = sum(_dot_tn(part, place) for part in _split3(new_rows))
        last = jnp.where(tail_lane, new_t, rolled[:, length - LANES:])
        return rolled, last

    scores, values, transposed = [], [], []
    for g in range(DIL_GROUPS):
        length = kb_refs[g].shape[2]
        cs = slice(g * width, (g + 1) * width)
        vcs = slice(DIL_WIDTH + g * width, DIL_WIDTH + (g + 1) * width)
        qg = q_ref[0, :, cs].astype(F32) * (DIL_HD ** -0.5)
        qs = jnp.concatenate([jnp.where(lane_head == h, qg, 0.0) for h in range(DIL_HEADS)], axis=0).astype(BF16)
        kb, vb = kb_refs[g][0], vb_refs[g][0]
        kn, vn = kvn_ref[0, :, cs], kvn_ref[0, :, vcs]
        rows = lax.broadcasted_iota(jnp.int32, (nrow, length), 0)
        cols = lax.broadcasted_iota(jnp.int32, (nrow, length), 1)
        dist = length + (rows & (pad - 1)) - cols
        scores.append(_dot(qs, kb.astype(BF16)) + bias_for(g, dist))
        values.append(vb.astype(BF16))
        transposed.append(True)
        rows = lax.broadcasted_iota(jnp.int32, (nrow, pad), 0)
        cols = lax.broadcasted_iota(jnp.int32, (nrow, pad), 1)
        dist = jnp.where(cols < n_new, (rows & (pad - 1)) - cols, -1)
        scores.append(_dot_nt(qs, kn.astype(BF16)) + bias_for(g, dist))
        values.append(vn.astype(BF16))
        transposed.append(False)
        for buf, new, out_ref in ((kb, kn, nk_refs[g]), (vb, vn, nv_refs[g])):
            rolled, last = shifted(buf, new)
            if length > LANES:
                out_ref[0, :, 0:length - LANES] = rolled[:, 0:length - LANES]
            out_ref[0, :, length - LANES:length] = last

    top = functools.reduce(jnp.maximum, [jnp.max(s, axis=-1, keepdims=True) for s in scores])
    probs = [jnp.exp(s - top) for s in scores]
    den = sum(jnp.sum(p, axis=-1, keepdims=True) for p in probs)
    acc = sum((_dot_nt if t else _dot)(p.astype(BF16), v)
              for p, v, t in zip(probs, values, transposed)) / den
    out = sum(jnp.where(lane_head == h, acc[h * pad:(h + 1) * pad], 0.0) for h in range(DIL_HEADS))
    o_ref[0] = out.astype(o_ref.dtype)


def _dil_sample(dq, kvn, bufs, n_new):
    bsz, pad, _ = dq.shape
    full = lambda a: pl.BlockSpec((1,) + a.shape[1:], lambda b: (b, 0, 0))
    out_shape = ([jax.ShapeDtypeStruct((bsz, pad, DIL_OUT), BF16)]
                 + [jax.ShapeDtypeStruct(a.shape, a.dtype) for a in bufs])
    return pl.pallas_call(
        functools.partial(_dil_sample_kernel, n_new=n_new),
        grid=(bsz,),
        in_specs=[full(dq), full(kvn)] + [full(a) for a in bufs],
        out_specs=[pl.BlockSpec((1, pad, DIL_OUT), lambda b: (b, 0, 0))] + [full(a) for a in bufs],
        out_shape=out_shape,
        compiler_params=_params("parallel"),
        name="dil_sample",
    )(dq, kvn, *bufs)


FFN_COL_CHUNK = D_FF // 2


def _merge_ffn_kernel(x_ref, oa_ref, ob_ref, ga_ref, gb_ref, pa_ref, pb_ref, wo_ref, n2_ref, nf_ref,
                      wg_ref, wu_ref, wd_ref, y_ref):
    pa = _dot(oa_ref[...], pa_ref[...])
    pb = _dot(ob_ref[...], pb_ref[...])
    merged = jax.nn.sigmoid(ga_ref[...].astype(F32)) * pa + jax.nn.sigmoid(gb_ref[...].astype(F32)) * pb
    x1 = x_ref[...] + _dot(merged.astype(BF16), wo_ref[...])
    h = _rms(x1, n2_ref[...]).astype(BF16)
    acc = x1
    for c0 in range(0, D_FF, FFN_COL_CHUNK):
        cs = slice(c0, c0 + FFN_COL_CHUNK)
        gate = _dot(h, wg_ref[:, cs])
        up = _dot(h, wu_ref[:, cs])
        act = (gate * jax.nn.sigmoid(gate) * up).astype(BF16)
        acc = acc + _dot(act, wd_ref[cs, :])
    y_ref[...] = _rms(acc, nf_ref[...])


def _merge_ffn(x, oa, ob, gates, pa, pb, wo, n2, nf, wg, wu, wd, tm):
    m = x.shape[0]
    row = lambda w: pl.BlockSpec((tm, w), lambda i: (i, 0))
    resident = lambda a: pl.BlockSpec(a.shape, lambda i: (0, 0), pipeline_mode=pl.Buffered(1))
    return pl.pallas_call(
        _merge_ffn_kernel,
        grid=(m // tm,),
        in_specs=[row(D_MODEL), row(GLA_DV), row(DIL_OUT), row(D_MODEL),
                  pl.BlockSpec((tm, D_MODEL), lambda i: (i, 1))]
                 + [resident(a) for a in (pa, pb, wo, n2, nf, wg, wu, wd)],
        out_specs=row(D_MODEL),
        out_shape=jax.ShapeDtypeStruct((m, D_MODEL), F32),
        compiler_params=pltpu.CompilerParams(dimension_semantics=("parallel",),
                                             vmem_limit_bytes=MERGE_FFN_VMEM_LIMIT_BYTES),
        name="merge_ffn",
    )(x, oa, ob, gates, gates, pa, pb, wo, n2, nf, wg, wu, wd)


def _prep_weights(w_in, gla_gate_w2, proj_a, proj_b, w_out, w_ffn_gate, w_ffn_up, w_ffn_down):
    w2 = jnp.concatenate([gla_gate_w2[0], jnp.zeros((LANES - GLA_GATE_RANK, GLA_DK), F32)], axis=0)
    return dict(
        w_in_t=jnp.transpose(w_in[0]).astype(BF16),
        w2=w2.astype(BF16), pa=proj_a[0].astype(BF16), pb=proj_b[0].astype(BF16), wo=w_out[0].astype(BF16),
        wg=w_ffn_gate[0].astype(BF16), wu=w_ffn_up[0].astype(BF16), wd=w_ffn_down[0].astype(BF16))


def _project(x2d, norm1_g, gate_b, wts, tm, chunk, n_valid):
    groups = [(0, GLA_COLS), (_OFF["dq"][0], DIL_WIDTH), (_OFF["dk"][0], 2 * DIL_WIDTH), (_OFF["ga"][0], 2 * D_MODEL)]
    return _in_proj(x2d, norm1_g, wts["w_in_t"], wts["w2"], gate_b, groups, _OFF["glr"][0],
                    [BF16, BF16, F32, BF16], tm, chunk, n_valid)


def _tail(x2d, o_a, o_b, z_gates, wts, norm2_g, norm_f_g, tm):
    return _merge_ffn(x2d, o_a, o_b, z_gates, wts["pa"], wts["pb"], wts["wo"], norm2_g,
                      norm_f_g.reshape(1, D_MODEL), wts["wg"], wts["wu"], wts["wd"], tm)


def kernel(x_prompt, x_sample, state_gla, state_win0_k, state_win0_v, state_win1_k, state_win1_v,
           state_win2_k, state_win2_v, norm1_g, w_in, gla_gate_w2, gla_gate_b, gla_norm_g,
           proj_a, proj_b, w_out, norm2_g, w_ffn_gate, w_ffn_up, w_ffn_down, norm_f_g):
    wts = _prep_weights(w_in, gla_gate_w2, proj_a, proj_b, w_out, w_ffn_gate, w_ffn_up, w_ffn_down)
    bp, seq, _ = x_prompt.shape
    bs, n_new, _ = x_sample.shape
    width = DIL_HEADS * DIL_HD

    xp = x_prompt.reshape(bp * seq, D_MODEL)
    z_gla, z_dq, z_kv, z_gates, b_cum = _project(xp, norm1_g, gla_gate_b, wts, 512, GLA_CHUNK, GLA_CHUNK)
    s0 = jnp.zeros((bp, GLA_HEADS, GLA_HDK, GLA_HDV), F32)
    o_a, gla_p = _gla(z_gla.reshape(bp, seq, GLA_COLS), b_cum.reshape(bp, seq, GLA_DK), gla_norm_g, s0,
                      GLA_CHUNK, GLA_CHUNK, GLA_ROWS_PROMPT)
    kv_p = z_kv.reshape(bp, seq, 2 * DIL_WIDTH)
    o_b = _dil_prompt(z_dq.reshape(bp, seq, DIL_WIDTH), kv_p)
    y_prompt = _tail(xp, o_a.reshape(bp * seq, GLA_DV), o_b.reshape(bp * seq, DIL_OUT), z_gates, wts,
                     norm2_g, norm_f_g, 512).reshape(bp, seq, D_MODEL)
    win_p = []
    for g in range(DIL_GROUPS):
        keep = min(DIL_WINDOWS[g], seq)
        for off in (0, DIL_WIDTH):
            c0 = off + g * width
            win_p.append(kv_p[:, seq - keep:, c0:c0 + width].reshape(1, bp, keep, DIL_HEADS, DIL_HD))

    xs = jnp.pad(x_sample, ((0, 0), (0, SAMPLE_PAD - n_new), (0, 0))).reshape(bs * SAMPLE_PAD, D_MODEL)
    zs_gla, zs_dq, zs_kv, zs_gates, bs_cum = _project(xs, norm1_g, gla_gate_b, wts, bs * SAMPLE_PAD,
                                                      SAMPLE_PAD, n_new)
    os_a, gla_s = _gla(zs_gla.reshape(bs, SAMPLE_PAD, GLA_COLS), bs_cum.reshape(bs, SAMPLE_PAD, GLA_DK),
                       gla_norm_g, state_gla[0], SAMPLE_PAD, n_new, GLA_ROWS_SAMPLE)
    bufs = [jnp.transpose(a[0], (0, 2, 3, 1)).reshape(bs, width, a.shape[2]) for a in
            (state_win0_k, state_win0_v, state_win1_k, state_win1_v, state_win2_k, state_win2_v)]
    outs = _dil_sample(zs_dq.reshape(bs, SAMPLE_PAD, DIL_WIDTH), zs_kv.reshape(bs, SAMPLE_PAD, 2 * DIL_WIDTH),
                       bufs, n_new)
    os_b, win_s = outs[0], outs[1:]
    ys = _tail(xs, os_a.reshape(bs * SAMPLE_PAD, GLA_DV), os_b.reshape(bs * SAMPLE_PAD, DIL_OUT), zs_gates, wts,
               norm2_g, norm_f_g, bs * SAMPLE_PAD)
    y_sample = ys.reshape(bs, SAMPLE_PAD, D_MODEL)[:, :n_new]
    win_s = [jnp.transpose(a.reshape(bs, DIL_HEADS, DIL_HD, a.shape[2]), (0, 3, 1, 2))[None] for a in win_s]

    return (y_prompt, y_sample, gla_p[None], *win_p, gla_s[None], *win_s)
```

```python
import functools

import jax
import jax.numpy as jnp
from jax import lax
from jax.experimental import pallas as pl
from jax.experimental.pallas import tpu as pltpu

F32 = jnp.float32
BF16 = jnp.bfloat16

D_MODEL = 1024
GLA_HEADS = 4
GLA_DK = 512
GLA_DV = 1024
GLA_HDK = 128
GLA_HDV = 256
GLA_GATE_RANK = 16
GLA_TAU = 16.0
DIL_WINDOWS = (128, 512, 2048)
DIL_RATES = (1, 4, 16)
DIL_GROUPS = 3
DIL_HEADS = 4
DIL_HD = 64
DIL_WIDTH = 768
DIL_OUT = 256
ALIBI_MAX = 8.0
D_FF = 2816
RMS_EPS = 1e-6

LANES = 128
SUBLANES = 8
BF16_SUBLANES = 16
Q_TILE = 128
TILES_PER_BODY = 5
RESIDUES_PER_BODY = 8
TILE_SLOTS = max(TILES_PER_BODY, RESIDUES_PER_BODY)
GLA_CHUNK = 128
GLA_ROWS_PROMPT = 2
GLA_ROWS_SAMPLE = 4
SAMPLE_PAD = 16
MASK_VALUE = -1e30
VMEM_LIMIT_BYTES = 48 * 1024 * 1024
IN_PROJ_VMEM_LIMIT_BYTES = 56 * 1024 * 1024
MERGE_FFN_VMEM_LIMIT_BYTES = 60 * 1024 * 1024
_LN2 = 0.6931471805599453

_OFF = {}
_o = 0
for _name, _w in (("gq", GLA_DK), ("gk", GLA_DK), ("gv", GLA_DV), ("gr", GLA_DV), ("glr", GLA_GATE_RANK),
                  ("dq", DIL_WIDTH), ("dk", DIL_WIDTH), ("dv", DIL_WIDTH), ("ga", D_MODEL), ("gb", D_MODEL)):
    _OFF[_name] = (_o, _o + _w)
    _o += _w
GLA_COLS = 2 * GLA_DK + 2 * GLA_DV
GLA_SAFE_LOG_DECAY = 40.0


def _params(*sem):
    return pltpu.CompilerParams(dimension_semantics=sem, vmem_limit_bytes=VMEM_LIMIT_BYTES)


def _dot(a, b):
    return jnp.dot(a, b, preferred_element_type=F32)


def _dot_nt(a, b):
    return lax.dot_general(a, b, (((1,), (1,)), ((), ())), preferred_element_type=F32)


def _dot_tn(a, b):
    return lax.dot_general(a, b, (((0,), (0,)), ((), ())), preferred_element_type=F32)


def _rms(x, g):
    return x * lax.rsqrt(jnp.mean(x * x, axis=-1, keepdims=True) + RMS_EPS) * g


IN_PROJ_COL_CHUNK = 1024


def _split3(x):
    x1 = x.astype(BF16)
    r1 = x - x1.astype(F32)
    x2 = r1.astype(BF16)
    x3 = (r1 - x2.astype(F32)).astype(BF16)
    return x1, x2, x3


def _in_proj_kernel(x_ref, g_ref, wt_ref, w2_ref, gb_ref, *out_refs, groups, low_rank_col, chunk, n_valid):
    h = _rms(x_ref[...], g_ref[...]).astype(BF16)
    for o_ref, (col, n) in zip(out_refs, groups):
        for c0 in range(0, n, IN_PROJ_COL_CHUNK):
            c1 = min(c0 + IN_PROJ_COL_CHUNK, n)
            o_ref[:, c0:c1] = _dot_nt(h, wt_ref[col + c0:col + c1, :]).astype(o_ref.dtype)

    b_ref = out_refs[len(groups)]
    tm = x_ref.shape[0]
    low_rank = _dot_nt(h, wt_ref[low_rank_col:low_rank_col + LANES, :]).astype(BF16)
    gate = _dot(low_rank, w2_ref[...]) + gb_ref[...]
    log_a = (jnp.minimum(gate, 0.0) - jnp.log(1.0 + jnp.exp(-jnp.abs(gate)))) * (1.0 / GLA_TAU)
    if n_valid < chunk:
        tok = lax.broadcasted_iota(jnp.int32, log_a.shape, 0) & (chunk - 1)
        log_a = jnp.where(tok < n_valid, log_a, 0.0)
    span = min(tm, LANES)
    row = lax.broadcasted_iota(jnp.int32, (span, span), 0)
    col = lax.broadcasted_iota(jnp.int32, (span, span), 1)
    same_chunk = (row & -chunk) == (col & -chunk) if chunk < span else True
    tril = jnp.where((row >= col) & same_chunk, 1.0, 0.0).astype(BF16)
    for r0 in range(0, tm, span):
        parts = _split3(log_a[r0:r0 + span])[:2]
        b_ref[r0:r0 + span, :] = sum(_dot(tril, p) for p in parts)


def _in_proj(x, g, w_t, w2, gate_b, groups, low_rank_col, out_dtypes, tm, chunk, n_valid):
    m, k = x.shape
    assert chunk & (chunk - 1) == 0 and (chunk % LANES == 0 or LANES % chunk == 0) and tm % chunk == 0
    assert all(c % BF16_SUBLANES == 0 and n % LANES == 0 for c, n in groups) and low_rank_col % BF16_SUBLANES == 0
    resident = lambda a: pl.BlockSpec(a.shape, lambda i: (0, 0), pipeline_mode=pl.Buffered(1))
    out_widths = [n for _, n in groups] + [GLA_DK]
    return pl.pallas_call(
        functools.partial(_in_proj_kernel, groups=tuple(groups), low_rank_col=low_rank_col,
                          chunk=chunk, n_valid=n_valid),
        grid=(m // tm,),
        in_specs=[pl.BlockSpec((tm, k), lambda i: (i, 0))] + [resident(a) for a in (g, w_t, w2, gate_b)],
        out_specs=[pl.BlockSpec((tm, n), lambda i: (i, 0)) for n in out_widths],
        out_shape=[jax.ShapeDtypeStruct((m, n), dt) for n, dt in zip(out_widths, list(out_dtypes) + [F32])],
        compiler_params=pltpu.CompilerParams(dimension_semantics=("parallel",),
                                             vmem_limit_bytes=IN_PROJ_VMEM_LIMIT_BYTES),
        name="in_proj",
    )(x, g, w_t, w2, gate_b)


def _gla_kernel(q_ref, k_ref, v_ref, r_ref, b_ref, ng_ref, s0_ref,
                o_ref, sout_ref, s_ref, oi_ref, kf_ref, vf_ref, *, chunk, n_valid):
    c = pl.program_id(1)
    rows = q_ref.shape[0]

    @pl.when(c == 0)
    def _():
        s_ref[...] = s0_ref[...]

    for r in range(rows):
        _gla_chunk(r, q_ref, k_ref, v_ref, r_ref, b_ref, ng_ref, o_ref, s_ref, oi_ref, chunk, n_valid)

    for r in range(rows):
        @pl.when(jnp.min(b_ref[r, chunk - 1:chunk, :]) < -GLA_SAFE_LOG_DECAY)
        def _(r=r):
            _gla_chunk_exact_intra(r, q_ref, k_ref, v_ref, r_ref, b_ref, ng_ref, o_ref, oi_ref, kf_ref, vf_ref,
                                   chunk, n_valid)

    @pl.when(c == pl.num_programs(1) - 1)
    def _():
        sout_ref[...] = s_ref[...]


def _gla_epilogue(o, r, vs, r_ref, ng_ref, o_ref):
    gr = r_ref[r, :, vs].astype(F32)
    o_ref[r, :, vs] = (_rms(o, ng_ref[...]) * (gr * jax.nn.sigmoid(gr))).astype(o_ref.dtype)


def _gla_values(r, vs, v_ref, chunk, n_valid):
    vh = v_ref[r, :, vs]
    if n_valid < chunk:
        tok = lax.broadcasted_iota(jnp.int32, vh.shape, 0)
        vh = jnp.where(tok < n_valid, vh, jnp.zeros_like(vh))
    return vh


def _gla_chunk_exact_intra(r, q_ref, k_ref, v_ref, r_ref, b_ref, ng_ref, o_ref, oi_ref, kf_ref, vf_ref,
                           chunk, n_valid):
    tok = lax.broadcasted_iota(jnp.int32, (chunk, 1), 0)
    for h in range(GLA_HEADS):
        ks = slice(h * GLA_HDK, (h + 1) * GLA_HDK)
        vs = slice(h * GLA_HDV, (h + 1) * GLA_HDV)
        bh = b_ref[r, :, ks]
        qh = q_ref[r, :, ks].astype(F32) * (GLA_HDK ** -0.5)
        kf_ref[...] = k_ref[r, :, ks].astype(F32)
        vf_ref[...] = _gla_values(r, vs, v_ref, chunk, n_valid).astype(F32)

        def eight_keys(i, acc, bh=bh, qh=qh, ks=ks):
            rows8 = pl.ds(pl.multiple_of(i * SUBLANES, SUBLANES), SUBLANES)
            b8, k8, v8 = b_ref[r, rows8, ks], kf_ref[rows8, :], vf_ref[rows8, :]
            for j in range(SUBLANES):
                s = i * SUBLANES + j
                decay = jnp.exp(jnp.where(tok >= s, bh - b8[j:j + 1], MASK_VALUE))
                w = jnp.sum(qh * decay * k8[j:j + 1], axis=-1, keepdims=True)
                acc = acc + w * v8[j:j + 1]
            return acc

        intra = lax.fori_loop(0, chunk // SUBLANES, eight_keys, jnp.zeros((chunk, GLA_HDV), F32))
        _gla_epilogue(oi_ref[r, h] + intra, r, vs, r_ref, ng_ref, o_ref)


def _gla_chunk(r, q_ref, k_ref, v_ref, r_ref, b_ref, ng_ref, o_ref, s_ref, oi_ref, chunk, n_valid):
    row = lax.broadcasted_iota(jnp.int32, (chunk, chunk), 0)
    col = lax.broadcasted_iota(jnp.int32, (chunk, chunk), 1)
    causal = row >= col
    b = b_ref[r]

    for h in range(GLA_HEADS):
        ks = slice(h * GLA_HDK, (h + 1) * GLA_HDK)
        vs = slice(h * GLA_HDV, (h + 1) * GLA_HDV)
        bh = b[:, ks]
        qh = q_ref[r, :, ks].astype(F32) * (GLA_HDK ** -0.5)
        kh = k_ref[r, :, ks].astype(F32)
        vh = _gla_values(r, vs, v_ref, chunk, n_valid)
        qt = (qh * jnp.exp(bh)).astype(BF16)
        kt = (kh * jnp.exp(-bh)).astype(BF16)
        kd = (kh * jnp.exp(bh[chunk - 1:chunk, :] - bh)).astype(BF16)
        s_old = s_ref[r, h]
        scores = jnp.where(causal, _dot_nt(qt, kt), 0.0).astype(BF16)
        o_state = _dot(qt, s_old.astype(BF16))
        oi_ref[r, h] = o_state
        dec = jnp.exp(jnp.broadcast_to(bh[chunk - 1:chunk, :], (GLA_HDK, GLA_HDK)).T)
        s_ref[r, h] = s_old * jnp.concatenate([dec, dec], axis=1) + _dot_tn(kd, vh)
        _gla_epilogue(o_state + _dot(scores, vh), r, vs, r_ref, ng_ref, o_ref)


def _gla(z, b_cum, norm_g, s0, chunk, n_valid, rows):
    bsz, t, _ = z.shape
    kern = functools.partial(_gla_kernel, chunk=chunk, n_valid=n_valid)
    state = pl.BlockSpec((rows, GLA_HEADS, GLA_HDK, GLA_HDV), lambda b, c: (b, 0, 0, 0))
    return pl.pallas_call(
        kern,
        grid=(bsz // rows, t // chunk),
        in_specs=[pl.BlockSpec((rows, chunk, GLA_DK), lambda b, c: (b, c, 0)),
                  pl.BlockSpec((rows, chunk, GLA_DK), lambda b, c: (b, c, 1)),
                  pl.BlockSpec((rows, chunk, GLA_DV), lambda b, c: (b, c, 1)),
                  pl.BlockSpec((rows, chunk, GLA_DV), lambda b, c: (b, c, 2)),
                  pl.BlockSpec((rows, chunk, GLA_DK), lambda b, c: (b, c, 0)),
                  pl.BlockSpec((1, GLA_HDV), lambda b, c: (0, 0)),
                  state],
        out_specs=[pl.BlockSpec((rows, chunk, GLA_DV), lambda b, c: (b, c, 0)), state],
        out_shape=[jax.ShapeDtypeStruct((bsz, t, GLA_DV), BF16),
                   jax.ShapeDtypeStruct((bsz, GLA_HEADS, GLA_HDK, GLA_HDV), F32)],
        scratch_shapes=[pltpu.VMEM((rows, GLA_HEADS, GLA_HDK, GLA_HDV), F32),
                        pltpu.VMEM((rows, GLA_HEADS, chunk, GLA_HDV), F32),
                        pltpu.VMEM((chunk, GLA_HDK), F32),
                        pltpu.VMEM((chunk, GLA_HDV), F32)],
        compiler_params=_params("parallel", "arbitrary"),
        name="gla",
    )(z, z, z, z, b_cum, norm_g, s0)


def _alibi_slope(g, head):
    n = DIL_GROUPS * DIL_HEADS
    return jnp.exp((-ALIBI_MAX * _LN2 / n) * (head + (g * DIL_HEADS + 1.0)))


def _pair_slopes(g, hp, rows):
    head = 2.0 * hp.astype(F32) + jnp.where(rows >= Q_TILE, 1.0, 0.0)
    return _alibi_slope(g, head)


def _dil_prompt_kernel(q0_ref, q1_ref, q2_ref, k0_ref, k1_ref, k2_ref, v0_ref, v1_ref, v2_ref,
                       o_ref, wk0_ref, wv0_ref, wk1_ref, wv1_ref, wk2_ref, wv2_ref,
                       qf_ref, og_ref, ld_ref, bf_ref, br_ref, p_ref, m_ref, *, seq):
    hp = pl.program_id(1)
    q_refs = (q0_ref, q1_ref, q2_ref)
    k_refs = (k0_ref, k1_ref, k2_ref)
    v_refs = (v0_ref, v1_ref, v2_ref)
    wk_refs = (wk0_ref, wk1_ref, wk2_ref)
    wv_refs = (wv0_ref, wv1_ref, wv2_ref)

    for g in range(DIL_GROUPS):
        qf_ref[g] = q_refs[g][...].astype(F32) * (DIL_HD ** -0.5)

    for g in range(DIL_GROUPS):
        rate = float(DIL_RATES[g])
        rows = lax.broadcasted_iota(jnp.int32, (2 * Q_TILE, Q_TILE), 0)
        cols = lax.broadcasted_iota(jnp.int32, (2 * Q_TILE, Q_TILE), 1)
        dist = (rows & (Q_TILE - 1)) - cols
        slope = _pair_slopes(g, hp, rows)
        bf_ref[g] = jnp.where(dist >= 0, -slope * rate * dist.astype(F32), MASK_VALUE)
        if g < 2:
            rows = lax.broadcasted_iota(jnp.int32, (2 * Q_TILE, 2 * Q_TILE), 0)
            cols = lax.broadcasted_iota(jnp.int32, (2 * Q_TILE, 2 * Q_TILE), 1)
            dist = (rows & (Q_TILE - 1)) + Q_TILE - cols
            slope = _pair_slopes(g, hp, rows)
            ok = jnp.abs(dist - Q_TILE // 2) <= Q_TILE // 2
            br_ref[g] = jnp.where(ok, -slope * rate * dist.astype(F32), MASK_VALUE)

    lane = lax.broadcasted_iota(jnp.int32, (Q_TILE, LANES), 1)
    lo = lane < DIL_HD

    def rows_of(start, size, rate):
        return pl.ds(start, size) if rate == 1 else pl.ds(start, size, stride=rate)

    def scores_stage(g, q_row, k_row, n_keys, bias, slot):
        rate = DIL_RATES[g]
        q2 = qf_ref[g, rows_of(q_row, Q_TILE, rate), :]
        k2 = k_refs[g][rows_of(k_row, n_keys, rate), :].astype(BF16)
        qs = jnp.concatenate([jnp.where(lo, q2, 0.0), jnp.where(lo, 0.0, q2)], axis=0).astype(BF16)
        s = _dot_nt(qs, k2) + bias()
        m = jnp.max(s, axis=-1, keepdims=True)
        p_ref[slot, :, 0:n_keys] = jnp.exp(s - m).astype(BF16)
        m_ref[slot] = jnp.where(lo, m[:Q_TILE], m[Q_TILE:])

    def values_stage(g, q_row, k_row, n_keys, slot):
        rate = DIL_RATES[g]
        v2 = v_refs[g][rows_of(k_row, n_keys, rate), :].astype(BF16)
        vo = jnp.concatenate([v2, jnp.ones((n_keys, LANES), BF16)], axis=1)
        r = _dot(p_ref[slot, :, 0:n_keys], vo)
        o2 = jnp.where(lo, r[:Q_TILE, :LANES], r[Q_TILE:, :LANES])
        l2 = jnp.where(lo, r[:Q_TILE, LANES:], r[Q_TILE:, LANES:])
        og_ref[g, rows_of(q_row, Q_TILE, rate), :] = o2 / l2
        ld_ref[g, rows_of(q_row, Q_TILE, rate), :] = m_ref[slot] + jnp.log(l2)

    pending = []
    emitted = [0]

    def run_tile_sets(n_sets, n_per_set, tile_of):
        base = emitted[0]

        def slot(i, u):
            return ((i + base) % 2) * TILE_SLOTS + u

        def scores(i):
            for u in range(n_per_set):
                g, q_row, k_row, n_keys, bias = tile_of(i, u)
                scores_stage(g, q_row, k_row, n_keys, bias, slot(i, u))

        def values(i):
            for u in range(n_per_set):
                g, q_row, k_row, n_keys, _ = tile_of(i, u)
                values_stage(g, q_row, k_row, n_keys, slot(i, u))

        if pending:
            pending.pop()()
        scores(0)
        if n_sets > 1:
            def body(i, carry):
                values(i - 1)
                scores(i)
                return carry
            lax.fori_loop(1, n_sets, body, 0)
        pending.append(lambda: values(n_sets - 1))
        emitted[0] += n_sets

    for g in range(DIL_GROUPS):
        rate = DIL_RATES[g]
        n_tiles = seq // rate // Q_TILE
        span = rate * Q_TILE
        group = min(rate, RESIDUES_PER_BODY)
        n_groups = rate // group

        def first_tile(i, u, g=g, group=group):
            rho = i * group + u
            return g, rho, rho, Q_TILE, lambda: bf_ref[g]

        def consecutive_tile(i, u, g=g, span=span):
            n = 1 + i * TILES_PER_BODY + u
            return g, n * span, (n - 1) * span, 2 * Q_TILE, lambda: br_ref[g]

        def later_tile(i, u, g=g, group=group, n_groups=n_groups, span=span):
            n = 1 + i // n_groups
            rho = (i % n_groups) * group + u
            return g, rho + n * span, rho + (n - 1) * span, 2 * Q_TILE, lambda: br_ref[g]

        run_tile_sets(n_groups, group, first_tile)
        if n_tiles > 1:
            if rate == 1:
                assert (n_tiles - 1) % TILES_PER_BODY == 0
                run_tile_sets((n_tiles - 1) // TILES_PER_BODY, TILES_PER_BODY, consecutive_tile)
            else:
                run_tile_sets((n_tiles - 1) * n_groups, group, later_tile)
    pending.pop()()

    def combine(i, carry):
        rs = pl.ds(pl.multiple_of(i * Q_TILE, Q_TILE), Q_TILE)
        ld = [ld_ref[g, rs, :] for g in range(DIL_GROUPS)]
        top = jnp.maximum(jnp.maximum(ld[0], ld[1]), ld[2])
        w = [jnp.exp(x - top) for x in ld]
        num = sum(w[g] * og_ref[g, rs, :] for g in range(DIL_GROUPS))
        o_ref[rs, :] = (num / (w[0] + w[1] + w[2])).astype(o_ref.dtype)
        return carry

    lax.fori_loop(0, seq // Q_TILE, combine, 0)

    for g in range(DIL_GROUPS):
        keep = wk_refs[g].shape[1]
        for src, dst in ((k_refs[g], wk_refs[g]), (v_refs[g], wv_refs[g])):
            for off in range(0, keep, LANES):
                dst[:, off:off + LANES] = src[seq - keep + off:seq - keep + off + LANES, :].T


def _dil_prompt(dq, kv):
    bsz, seq, _ = dq.shape
    pairs = DIL_HEADS * DIL_HD // LANES
    nblk = DIL_WIDTH // LANES
    keeps = [min(w, seq) for w in DIL_WINDOWS]

    def spec(col0):
        return pl.BlockSpec((None, seq, LANES), lambda b, hp, col0=col0: (b, 0, col0 + hp))

    in_specs = ([spec(g * pairs) for g in range(DIL_GROUPS)]
                + [spec(g * pairs) for g in range(DIL_GROUPS)]
                + [spec(nblk + g * pairs) for g in range(DIL_GROUPS)])
    return pl.pallas_call(
        functools.partial(_dil_prompt_kernel, seq=seq),
        grid=(bsz, pairs),
        in_specs=in_specs,
        out_specs=[pl.BlockSpec((None, seq, LANES), lambda b, hp: (b, 0, hp))]
                  + [pl.BlockSpec((None, LANES, w), lambda b, hp: (b, hp, 0)) for w in keeps for _ in range(2)],
        out_shape=[jax.ShapeDtypeStruct((bsz, seq, DIL_OUT), BF16)]
                  + [jax.ShapeDtypeStruct((bsz, DIL_HEADS * DIL_HD, w), F32) for w in keeps for _ in range(2)],
        scratch_shapes=[pltpu.VMEM((DIL_GROUPS, seq, LANES), F32),
                        pltpu.VMEM((DIL_GROUPS, seq, LANES), F32),
                        pltpu.VMEM((DIL_GROUPS, seq, LANES), F32),
                        pltpu.VMEM((DIL_GROUPS, 2 * Q_TILE, Q_TILE), F32),
                        pltpu.VMEM((2, 2 * Q_TILE, 2 * Q_TILE), F32),
                        pltpu.VMEM((2 * TILE_SLOTS, 2 * Q_TILE, 2 * Q_TILE), BF16),
                        pltpu.VMEM((2 * TILE_SLOTS, Q_TILE, LANES), F32)],
        compiler_params=_params("parallel", "arbitrary"),
        name="dil_prompt",
    )(dq, dq, dq, kv, kv, kv, kv, kv, kv)


def _dil_sample_kernel(q_ref, kvn_ref, kb0_ref, vb0_ref, kb1_ref, vb1_ref, kb2_ref, vb2_ref,
                       o_ref, nk0_ref, nv0_ref, nk1_ref, nv1_ref, nk2_ref, nv2_ref, *, n_new):
    kb_refs = (kb0_ref, kb1_ref, kb2_ref)
    vb_refs = (vb0_ref, vb1_ref, vb2_ref)
    nk_refs = (nk0_ref, nk1_ref, nk2_ref)
    nv_refs = (nv0_ref, nv1_ref, nv2_ref)
    pad = q_ref.shape[1]
    nrow = DIL_HEADS * pad
    width = DIL_HEADS * DIL_HD

    lane_head = jnp.right_shift(lax.broadcasted_iota(jnp.int32, (pad, width), 1), DIL_HD.bit_length() - 1)

    def bias_for(g, dist):
        rate = DIL_RATES[g]
        rows = lax.broadcasted_iota(jnp.int32, dist.shape, 0)
        head = sum(jnp.where(rows >= h * pad, 1.0, 0.0) for h in range(1, DIL_HEADS))
        ok = (dist >= 0) & (dist <= DIL_WINDOWS[g]) & ((dist & (rate - 1)) == 0)
        return jnp.where(ok, -_alibi_slope(g, head) * dist.astype(F32), MASK_VALUE)

    sel_r = lax.broadcasted_iota(jnp.int32, (pad, LANES), 0)
    sel_c = lax.broadcasted_iota(jnp.int32, (pad, LANES), 1)
    place = jnp.where((sel_r < n_new) & (sel_c == sel_r + (LANES - n_new)), 1.0, 0.0).astype(BF16)
    tail_lane = lax.broadcasted_iota(jnp.int32, (width, LANES), 1) >= LANES - n_new

    def shifted(buf_t, new_rows):
        length = buf_t.shape[1]
        rolled = pltpu.roll(buf_t, length - n_new, 1)
        new_t = sum(_dot_tn(part, place) for part in _split3(new_rows))
        last = jnp.where(tail_lane, new_t, rolled[:, length - LANES:])
        return rolled, last

    scores, values, transposed = [], [], []
    for g in range(DIL_GROUPS):
        length = kb_refs[g].shape[2]
        cs = slice(g * width, (g + 1) * width)
        vcs = slice(DIL_WIDTH + g * width, DIL_WIDTH + (g + 1) * width)
        qg = q_ref[0, :, cs].astype(F32) * (DIL_HD ** -0.5)
        qs = jnp.concatenate([jnp.where(lane_head == h, qg, 0.0) for h in range(DIL_HEADS)], axis=0).astype(BF16)
        kb, vb = kb_refs[g][0], vb_refs[g][0]
        kn, vn = kvn_ref[0, :, cs], kvn_ref[0, :, vcs]
        rows = lax.broadcasted_iota(jnp.int32, (nrow, length), 0)
        cols = lax.broadcasted_iota(jnp.int32, (nrow, length), 1)
        dist = length + (rows & (pad - 1)) - cols
        scores.append(_dot(qs, kb.astype(BF16)) + bias_for(g, dist))
        values.append(vb.astype(BF16))
        transposed.append(True)
        rows = lax.broadcasted_iota(jnp.int32, (nrow, pad), 0)
        cols = lax.broadcasted_iota(jnp.int32, (nrow, pad), 1)
        dist = jnp.where(cols < n_new, (rows & (pad - 1)) - cols, -1)
        scores.append(_dot_nt(qs, kn.astype(BF16)) + bias_for(g, dist))
        values.append(vn.astype(BF16))
        transposed.append(False)
        for buf, new, out_ref in ((kb, kn, nk_refs[g]), (vb, vn, nv_refs[g])):
            rolled, last = shifted(buf, new)
            if length > LANES:
                out_ref[0, :, 0:length - LANES] = rolled[:, 0:length - LANES]
            out_ref[0, :, length - LANES:length] = last

    top = functools.reduce(jnp.maximum, [jnp.max(s, axis=-1, keepdims=True) for s in scores])
    probs = [jnp.exp(s - top) for s in scores]
    den = sum(jnp.sum(p, axis=-1, keepdims=True) for p in probs)
    acc = sum((_dot_nt if t else _dot)(p.astype(BF16), v)
              for p, v, t in zip(probs, values, transposed)) / den
    out = sum(jnp.where(lane_head == h, acc[h * pad:(h + 1) * pad], 0.0) for h in range(DIL_HEADS))
    o_ref[0] = out.astype(o_ref.dtype)


def _dil_sample(dq, kvn, bufs, n_new):
    bsz, pad, _ = dq.shape
    full = lambda a: pl.BlockSpec((1,) + a.shape[1:], lambda b: (b, 0, 0))
    out_shape = ([jax.ShapeDtypeStruct((bsz, pad, DIL_OUT), BF16)]
                 + [jax.ShapeDtypeStruct(a.shape, a.dtype) for a in bufs])
    return pl.pallas_call(
        functools.partial(_dil_sample_kernel, n_new=n_new),
        grid=(bsz,),
        in_specs=[full(dq), full(kvn)] + [full(a) for a in bufs],
        out_specs=[pl.BlockSpec((1, pad, DIL_OUT), lambda b: (b, 0, 0))] + [full(a) for a in bufs],
        out_shape=out_shape,
        compiler_params=_params("parallel"),
        name="dil_sample",
    )(dq, kvn, *bufs)


FFN_COL_CHUNK = D_FF // 2


def _merge_ffn_kernel(x_ref, oa_ref, ob_ref, ga_ref, gb_ref, pa_ref, pb_ref, wo_ref, n2_ref, nf_ref,
                      wg_ref, wu_ref, wd_ref, y_ref):
    pa = _dot(oa_ref[...], pa_ref[...])
    pb = _dot(ob_ref[...], pb_ref[...])
    merged = jax.nn.sigmoid(ga_ref[...].astype(F32)) * pa + jax.nn.sigmoid(gb_ref[...].astype(F32)) * pb
    x1 = x_ref[...] + _dot(merged.astype(BF16), wo_ref[...])
    h = _rms(x1, n2_ref[...]).astype(BF16)
    acc = x1
    for c0 in range(0, D_FF, FFN_COL_CHUNK):
        cs = slice(c0, c0 + FFN_COL_CHUNK)
        gate = _dot(h, wg_ref[:, cs])
        up = _dot(h, wu_ref[:, cs])
        act = (gate * jax.nn.sigmoid(gate) * up).astype(BF16)
        acc = acc + _dot(act, wd_ref[cs, :])
    y_ref[...] = _rms(acc, nf_ref[...])


def _merge_ffn(x, oa, ob, gates, pa, pb, wo, n2, nf, wg, wu, wd, tm):
    m = x.shape[0]
    row = lambda w: pl.BlockSpec((tm, w), lambda i: (i, 0))
    resident = lambda a: pl.BlockSpec(a.shape, lambda i: (0, 0), pipeline_mode=pl.Buffered(1))
    return pl.pallas_call(
        _merge_ffn_kernel,
        grid=(m // tm,),
        in_specs=[row(D_MODEL), row(GLA_DV), row(DIL_OUT), row(D_MODEL),
                  pl.BlockSpec((tm, D_MODEL), lambda i: (i, 1))]
                 + [resident(a) for a in (pa, pb, wo, n2, nf, wg, wu, wd)],
        out_specs=row(D_MODEL),
        out_shape=jax.ShapeDtypeStruct((m, D_MODEL), F32),
        compiler_params=pltpu.CompilerParams(dimension_semantics=("parallel",),
                                             vmem_limit_bytes=MERGE_FFN_VMEM_LIMIT_BYTES),
        name="merge_ffn",
    )(x, oa, ob, gates, gates, pa, pb, wo, n2, nf, wg, wu, wd)


def _prep_weights(w_in, gla_gate_w2, proj_a, proj_b, w_out, w_ffn_gate, w_ffn_up, w_ffn_down):
    w2 = jnp.concatenate([gla_gate_w2[0], jnp.zeros((LANES - GLA_GATE_RANK, GLA_DK), F32)], axis=0)
    return dict(
        w_in_t=jnp.transpose(w_in[0]).astype(BF16),
        w2=w2.astype(BF16), pa=proj_a[0].astype(BF16), pb=proj_b[0].astype(BF16), wo=w_out[0].astype(BF16),
        wg=w_ffn_gate[0].astype(BF16), wu=w_ffn_up[0].astype(BF16), wd=w_ffn_down[0].astype(BF16))


def _project(x2d, norm1_g, gate_b, wts, tm, chunk, n_valid):
    groups = [(0, GLA_COLS), (_OFF["dq"][0], DIL_WIDTH), (_OFF["dk"][0], 2 * DIL_WIDTH), (_OFF["ga"][0], 2 * D_MODEL)]
    return _in_proj(x2d, norm1_g, wts["w_in_t"], wts["w2"], gate_b, groups, _OFF["glr"][0],
                    [BF16, BF16, F32, BF16], tm, chunk, n_valid)


def _tail(x2d, o_a, o_b, z_gates, wts, norm2_g, norm_f_g, tm):
    return _merge_ffn(x2d, o_a, o_b, z_gates, wts["pa"], wts["pb"], wts["wo"], norm2_g,
                      norm_f_g.reshape(1, D_MODEL), wts["wg"], wts["wu"], wts["wd"], tm)


def kernel(x_prompt, x_sample, state_gla, state_win0_k, state_win0_v, state_win1_k, state_win1_v,
           state_win2_k, state_win2_v, norm1_g, w_in, gla_gate_w2, gla_gate_b, gla_norm_g,
           proj_a, proj_b, w_out, norm2_g, w_ffn_gate, w_ffn_up, w_ffn_down, norm_f_g):
    wts = _prep_weights(w_in, gla_gate_w2, proj_a, proj_b, w_out, w_ffn_gate, w_ffn_up, w_ffn_down)
    bp, seq, _ = x_prompt.shape
    bs, n_new, _ = x_sample.shape
    width = DIL_HEADS * DIL_HD

    xp = x_prompt.reshape(bp * seq, D_MODEL)
    z_gla, z_dq, z_kv, z_gates, b_cum = _project(xp, norm1_g, gla_gate_b, wts, 512, GLA_CHUNK, GLA_CHUNK)
    s0 = jnp.zeros((bp, GLA_HEADS, GLA_HDK, GLA_HDV), F32)
    o_a, gla_p = _gla(z_gla.reshape(bp, seq, GLA_COLS), b_cum.reshape(bp, seq, GLA_DK), gla_norm_g, s0,
                      GLA_CHUNK, GLA_CHUNK, GLA_ROWS_PROMPT)
    o_b, *win_p = _dil_prompt(z_dq.reshape(bp, seq, DIL_WIDTH), z_kv.reshape(bp, seq, 2 * DIL_WIDTH))
    y_prompt = _tail(xp, o_a.reshape(bp * seq, GLA_DV), o_b.reshape(bp * seq, DIL_OUT), z_gates, wts,
                     norm2_g, norm_f_g, 512).reshape(bp, seq, D_MODEL)
    win_p = [jnp.transpose(a.reshape(bp, DIL_HEADS, DIL_HD, a.shape[2]), (0, 3, 1, 2))[None] for a in win_p]

    xs = jnp.pad(x_sample, ((0, 0), (0, SAMPLE_PAD - n_new), (0, 0))).reshape(bs * SAMPLE_PAD, D_MODEL)
    zs_gla, zs_dq, zs_kv, zs_gates, bs_cum = _project(xs, norm1_g, gla_gate_b, wts, bs * SAMPLE_PAD,
                                                      SAMPLE_PAD, n_new)
    os_a, gla_s = _gla(zs_gla.reshape(bs, SAMPLE_PAD, GLA_COLS), bs_cum.reshape(bs, SAMPLE_PAD, GLA_DK),
                       gla_norm_g, state_gla[0], SAMPLE_PAD, n_new, GLA_ROWS_SAMPLE)
    bufs = [jnp.transpose(a[0], (0, 2, 3, 1)).reshape(bs, width, a.shape[2]) for a in
            (state_win0_k, state_win0_v, state_win1_k, state_win1_v, state_win2_k, state_win2_v)]
    outs = _dil_sample(zs_dq.reshape(bs, SAMPLE_PAD, DIL_WIDTH), zs_kv.reshape(bs, SAMPLE_PAD, 2 * DIL_WIDTH),
                       bufs, n_new)
    os_b, win_s = outs[0], outs[1:]
    ys = _tail(xs, os_a.reshape(bs * SAMPLE_PAD, GLA_DV), os_b.reshape(bs * SAMPLE_PAD, DIL_OUT), zs_gates, wts,
               norm2_g, norm_f_g, bs * SAMPLE_PAD)
    y_sample = ys.reshape(bs, SAMPLE_PAD, D_MODEL)[:, :n_new]
    win_s = [jnp.transpose(a.reshape(bs, DIL_HEADS, DIL_HD, a.shape[2]), (0, 3, 1, 2))[None] for a in win_s]

    return (y_prompt, y_sample, gla_p[None], *win_p, gla_s[None], *win_s)
```

```python
import functools

import jax
import jax.numpy as jnp
from jax import lax
from jax.experimental import pallas as pl
from jax.experimental.pallas import tpu as pltpu

F32 = jnp.float32
BF16 = jnp.bfloat16

D_MODEL = 1024
GLA_HEADS = 4
GLA_DK = 512
GLA_DV = 1024
GLA_HDK = 128
GLA_HDV = 256
GLA_GATE_RANK = 16
GLA_TAU = 16.0
DIL_WINDOWS = (128, 512, 2048)
DIL_RATES = (1, 4, 16)
DIL_GROUPS = 3
DIL_HEADS = 4
DIL_HD = 64
DIL_WIDTH = 768
DIL_OUT = 256
ALIBI_MAX = 8.0
D_FF = 2816
RMS_EPS = 1e-6

LANES = 128
SUBLANES = 8
BF16_SUBLANES = 16
Q_TILE = 128
TILES_PER_BODY = 5
RESIDUES_PER_BODY = 8
TILE_SLOTS = max(TILES_PER_BODY, RESIDUES_PER_BODY)
GLA_CHUNK = 128
GLA_ROWS_PROMPT = 4
GLA_ROWS_SAMPLE = 4
SAMPLE_PAD = 16
MASK_VALUE = -1e30
VMEM_LIMIT_BYTES = 48 * 1024 * 1024
IN_PROJ_VMEM_LIMIT_BYTES = 56 * 1024 * 1024
MERGE_FFN_VMEM_LIMIT_BYTES = 60 * 1024 * 1024
GLA_VMEM_LIMIT_BYTES = 56 * 1024 * 1024
_LN2 = 0.6931471805599453

_OFF = {}
_o = 0
for _name, _w in (("gq", GLA_DK), ("gk", GLA_DK), ("gv", GLA_DV), ("gr", GLA_DV), ("glr", GLA_GATE_RANK),
                  ("dq", DIL_WIDTH), ("dk", DIL_WIDTH), ("dv", DIL_WIDTH), ("ga", D_MODEL), ("gb", D_MODEL)):
    _OFF[_name] = (_o, _o + _w)
    _o += _w
GLA_COLS = 2 * GLA_DK + 2 * GLA_DV
GLA_SAFE_LOG_DECAY = 40.0


def _params(*sem):
    return pltpu.CompilerParams(dimension_semantics=sem, vmem_limit_bytes=VMEM_LIMIT_BYTES)


def _dot(a, b):
    return jnp.dot(a, b, preferred_element_type=F32)


def _dot_nt(a, b):
    return lax.dot_general(a, b, (((1,), (1,)), ((), ())), preferred_element_type=F32)


def _dot_tn(a, b):
    return lax.dot_general(a, b, (((0,), (0,)), ((), ())), preferred_element_type=F32)


def _rms(x, g):
    return x * lax.rsqrt(jnp.mean(x * x, axis=-1, keepdims=True) + RMS_EPS) * g


IN_PROJ_COL_CHUNK = 1024


def _split3(x):
    x1 = x.astype(BF16)
    r1 = x - x1.astype(F32)
    x2 = r1.astype(BF16)
    x3 = (r1 - x2.astype(F32)).astype(BF16)
    return x1, x2, x3


def _in_proj_kernel(x_ref, g_ref, wt_ref, w2_ref, gb_ref, *out_refs, groups, low_rank_col, chunk, n_valid):
    h = _rms(x_ref[...], g_ref[...]).astype(BF16)
    b_ref = out_refs[len(groups)]
    tm = x_ref.shape[0]

    for o_ref, (c_first, n) in zip(out_refs, groups):
        for c0 in range(0, n, IN_PROJ_COL_CHUNK):
            c1 = min(c0 + IN_PROJ_COL_CHUNK, n)
            o_ref[:, c0:c1] = _dot_nt(h, wt_ref[c_first + c0:c_first + c1, :]).astype(o_ref.dtype)

    low_rank = _dot_nt(h, wt_ref[low_rank_col:low_rank_col + LANES, :]).astype(BF16)
    gate = _dot(low_rank, w2_ref[...]) + gb_ref[...]
    log_a = (jnp.minimum(gate, 0.0) - jnp.log(1.0 + jnp.exp(-jnp.abs(gate)))) * (1.0 / GLA_TAU)
    if n_valid < chunk:
        tok = lax.broadcasted_iota(jnp.int32, log_a.shape, 0) & (chunk - 1)
        log_a = jnp.where(tok < n_valid, log_a, 0.0)
    span = min(tm, LANES)
    row = lax.broadcasted_iota(jnp.int32, (span, span), 0)
    col = lax.broadcasted_iota(jnp.int32, (span, span), 1)
    same_chunk = (row & -chunk) == (col & -chunk) if chunk < span else True
    tril = jnp.where((row >= col) & same_chunk, 1.0, 0.0).astype(BF16)
    for r0 in range(0, tm, span):
        parts = _split3(log_a[r0:r0 + span])[:2]
        b_ref[r0:r0 + span, :] = sum(_dot(tril, p) for p in parts)


def _in_proj(x, g, w_t, w2, gate_b, groups, low_rank_col, out_dtypes, tm, chunk, n_valid):
    m, k = x.shape
    assert chunk & (chunk - 1) == 0 and (chunk % LANES == 0 or LANES % chunk == 0) and tm % chunk == 0
    assert all(c % BF16_SUBLANES == 0 and n % LANES == 0 for c, n in groups) and low_rank_col % BF16_SUBLANES == 0
    resident = lambda a: pl.BlockSpec(a.shape, lambda i: (0, 0), pipeline_mode=pl.Buffered(1))
    out_widths = [n for _, n in groups] + [GLA_DK]
    return pl.pallas_call(
        functools.partial(_in_proj_kernel, groups=tuple(groups), low_rank_col=low_rank_col,
                          chunk=chunk, n_valid=n_valid),
        grid=(m // tm,),
        in_specs=[pl.BlockSpec((tm, k), lambda i: (i, 0))] + [resident(a) for a in (g, w_t, w2, gate_b)],
        out_specs=[pl.BlockSpec((tm, n), lambda i: (i, 0)) for n in out_widths],
        out_shape=[jax.ShapeDtypeStruct((m, n), dt) for n, dt in zip(out_widths, list(out_dtypes) + [F32])],
        compiler_params=pltpu.CompilerParams(dimension_semantics=("parallel",),
                                             vmem_limit_bytes=IN_PROJ_VMEM_LIMIT_BYTES),
        name="in_proj",
    )(x, g, w_t, w2, gate_b)


N_SAMPLE_IN = 2 + 2 * DIL_GROUPS
N_SAMPLE_OUT = 1 + 2 * DIL_GROUPS


def _gla_kernel(*refs, chunk, n_valid, n_new_sample):
    q_ref, k_ref, v_ref, r_ref, b_ref, ng_ref, s0_ref = refs[:7]
    refs = refs[7:]
    sample_in = ()
    if n_new_sample:
        sample_in, refs = refs[:N_SAMPLE_IN], refs[N_SAMPLE_IN:]
    o_ref, sout_ref = refs[:2]
    refs = refs[2:]
    sample_out = ()
    if n_new_sample:
        sample_out, refs = refs[:N_SAMPLE_OUT], refs[N_SAMPLE_OUT:]
    s_ref, oi_ref, kf_ref, vf_ref = refs

    c = pl.program_id(1)
    rows = q_ref.shape[0]

    @pl.when(c == 0)
    def _():
        s_ref[...] = s0_ref[...]

    for r in range(rows):
        _gla_chunk(r, q_ref, k_ref, v_ref, r_ref, b_ref, ng_ref, o_ref, s_ref, oi_ref, chunk, n_valid)
    if n_new_sample:
        _dil_sample_kernel(*sample_in, *sample_out, n_new=n_new_sample)

    for r in range(rows):
        @pl.when(jnp.min(b_ref[r, chunk - 1:chunk, :]) < -GLA_SAFE_LOG_DECAY)
        def _(r=r):
            _gla_chunk_exact_intra(r, q_ref, k_ref, v_ref, r_ref, b_ref, ng_ref, o_ref, oi_ref, kf_ref, vf_ref,
                                   chunk, n_valid)

    @pl.when(c == pl.num_programs(1) - 1)
    def _():
        sout_ref[...] = s_ref[...]


def _gla_epilogue(o, r, vs, r_ref, ng_ref, o_ref):
    gr = r_ref[r, :, vs].astype(F32)
    o_ref[r, :, vs] = (_rms(o, ng_ref[...]) * (gr * jax.nn.sigmoid(gr))).astype(o_ref.dtype)


def _gla_values(r, vs, v_ref, chunk, n_valid):
    vh = v_ref[r, :, vs]
    if n_valid < chunk:
        tok = lax.broadcasted_iota(jnp.int32, vh.shape, 0)
        vh = jnp.where(tok < n_valid, vh, jnp.zeros_like(vh))
    return vh


def _gla_chunk_exact_intra(r, q_ref, k_ref, v_ref, r_ref, b_ref, ng_ref, o_ref, oi_ref, kf_ref, vf_ref,
                           chunk, n_valid):
    tok = lax.broadcasted_iota(jnp.int32, (chunk, 1), 0)
    for h in range(GLA_HEADS):
        ks = slice(h * GLA_HDK, (h + 1) * GLA_HDK)
        vs = slice(h * GLA_HDV, (h + 1) * GLA_HDV)
        bh = b_ref[r, :, ks]
        qh = q_ref[r, :, ks].astype(F32) * (GLA_HDK ** -0.5)
        kf_ref[...] = k_ref[r, :, ks].astype(F32)
        vf_ref[...] = _gla_values(r, vs, v_ref, chunk, n_valid).astype(F32)

        def eight_keys(i, acc, bh=bh, qh=qh, ks=ks):
            rows8 = pl.ds(pl.multiple_of(i * SUBLANES, SUBLANES), SUBLANES)
            b8, k8, v8 = b_ref[r, rows8, ks], kf_ref[rows8, :], vf_ref[rows8, :]
            for j in range(SUBLANES):
                s = i * SUBLANES + j
                decay = jnp.exp(jnp.where(tok >= s, bh - b8[j:j + 1], MASK_VALUE))
                w = jnp.sum(qh * decay * k8[j:j + 1], axis=-1, keepdims=True)
                acc = acc + w * v8[j:j + 1]
            return acc

        intra = lax.fori_loop(0, chunk // SUBLANES, eight_keys, jnp.zeros((chunk, GLA_HDV), F32))
        _gla_epilogue(oi_ref[r, h] + intra, r, vs, r_ref, ng_ref, o_ref)


def _gla_chunk(r, q_ref, k_ref, v_ref, r_ref, b_ref, ng_ref, o_ref, s_ref, oi_ref, chunk, n_valid):
    row = lax.broadcasted_iota(jnp.int32, (chunk, chunk), 0)
    col = lax.broadcasted_iota(jnp.int32, (chunk, chunk), 1)
    causal = row >= col
    b = b_ref[r]

    for h in range(GLA_HEADS):
        ks = slice(h * GLA_HDK, (h + 1) * GLA_HDK)
        vs = slice(h * GLA_HDV, (h + 1) * GLA_HDV)
        bh = b[:, ks]
        qh = q_ref[r, :, ks].astype(F32) * (GLA_HDK ** -0.5)
        kh = k_ref[r, :, ks].astype(F32)
        vh = _gla_values(r, vs, v_ref, chunk, n_valid)
        qt = (qh * jnp.exp(bh)).astype(BF16)
        kt = (kh * jnp.exp(-bh)).astype(BF16)
        kd = (kh * jnp.exp(bh[chunk - 1:chunk, :] - bh)).astype(BF16)
        s_old = s_ref[r, h]
        scores = jnp.where(causal, _dot_nt(qt, kt), 0.0).astype(BF16)
        o_state = _dot(qt, s_old.astype(BF16))
        oi_ref[r, h] = o_state
        dec = jnp.exp(jnp.broadcast_to(bh[chunk - 1:chunk, :], (GLA_HDK, GLA_HDK)).T)
        s_ref[r, h] = s_old * jnp.concatenate([dec, dec], axis=1) + _dot_tn(kd, vh)
        _gla_epilogue(o_state + _dot(scores, vh), r, vs, r_ref, ng_ref, o_ref)


def _gla(z, b_cum, norm_g, s0, chunk, n_valid, rows, sample=None):
    bsz, t, _ = z.shape
    n_chunks = t // chunk
    state = pl.BlockSpec((rows, GLA_HEADS, GLA_HDK, GLA_HDV), lambda b, c: (b, 0, 0, 0))
    in_specs = [pl.BlockSpec((rows, chunk, GLA_DK), lambda b, c: (b, c, 0)),
                pl.BlockSpec((rows, chunk, GLA_DK), lambda b, c: (b, c, 1)),
                pl.BlockSpec((rows, chunk, GLA_DV), lambda b, c: (b, c, 1)),
                pl.BlockSpec((rows, chunk, GLA_DV), lambda b, c: (b, c, 2)),
                pl.BlockSpec((rows, chunk, GLA_DK), lambda b, c: (b, c, 0)),
                pl.BlockSpec((1, GLA_HDV), lambda b, c: (0, 0)),
                state]
    out_specs = [pl.BlockSpec((rows, chunk, GLA_DV), lambda b, c: (b, c, 0)), state]
    out_shape = [jax.ShapeDtypeStruct((bsz, t, GLA_DV), BF16),
                 jax.ShapeDtypeStruct((bsz, GLA_HEADS, GLA_HDK, GLA_HDV), F32)]
    operands = [z, z, z, z, b_cum, norm_g, s0]
    n_new = 0
    if sample is not None:
        dq, kvn, bufs, n_new = sample
        assert dq.shape[0] == (bsz // rows) * n_chunks
        per_step = lambda a: pl.BlockSpec((1,) + a.shape[1:], lambda b, c: (b * n_chunks + c, 0, 0))
        in_specs += [per_step(a) for a in (dq, kvn, *bufs)]
        operands += [dq, kvn, *bufs]
        o_s = jax.ShapeDtypeStruct((dq.shape[0], dq.shape[1], DIL_OUT), BF16)
        out_specs += [per_step(o_s)] + [per_step(a) for a in bufs]
        out_shape += [o_s] + [jax.ShapeDtypeStruct(a.shape, a.dtype) for a in bufs]
    return pl.pallas_call(
        functools.partial(_gla_kernel, chunk=chunk, n_valid=n_valid, n_new_sample=n_new),
        grid=(bsz // rows, n_chunks),
        in_specs=in_specs,
        out_specs=out_specs,
        out_shape=out_shape,
        scratch_shapes=[pltpu.VMEM((rows, GLA_HEADS, GLA_HDK, GLA_HDV), F32),
                        pltpu.VMEM((rows, GLA_HEADS, chunk, GLA_HDV), F32),
                        pltpu.VMEM((chunk, GLA_HDK), F32),
                        pltpu.VMEM((chunk, GLA_HDV), F32)],
        compiler_params=pltpu.CompilerParams(dimension_semantics=("parallel", "arbitrary"),
                                             vmem_limit_bytes=GLA_VMEM_LIMIT_BYTES),
        name="gla",
    )(*operands)


def _alibi_slope(g, head):
    n = DIL_GROUPS * DIL_HEADS
    return jnp.exp((-ALIBI_MAX * _LN2 / n) * (head + (g * DIL_HEADS + 1.0)))


def _pair_slopes(g, hp, rows):
    head = 2.0 * hp.astype(F32) + jnp.where(rows >= Q_TILE, 1.0, 0.0)
    return _alibi_slope(g, head)


def _dil_prompt_kernel(q0_ref, q1_ref, q2_ref, k0_ref, k1_ref, k2_ref, v0_ref, v1_ref, v2_ref,
                       o_ref, wk0_ref, wv0_ref, wk1_ref, wv1_ref, wk2_ref, wv2_ref,
                       qf_ref, og_ref, ld_ref, bf_ref, br_ref, p_ref, m_ref, *, seq):
    hp = pl.program_id(1)
    q_refs = (q0_ref, q1_ref, q2_ref)
    k_refs = (k0_ref, k1_ref, k2_ref)
    v_refs = (v0_ref, v1_ref, v2_ref)
    wk_refs = (wk0_ref, wk1_ref, wk2_ref)
    wv_refs = (wv0_ref, wv1_ref, wv2_ref)

    for g in range(DIL_GROUPS):
        qf_ref[g] = q_refs[g][...].astype(F32) * (DIL_HD ** -0.5)

    for g in range(DIL_GROUPS):
        rate = float(DIL_RATES[g])
        rows = lax.broadcasted_iota(jnp.int32, (2 * Q_TILE, Q_TILE), 0)
        cols = lax.broadcasted_iota(jnp.int32, (2 * Q_TILE, Q_TILE), 1)
        dist = (rows & (Q_TILE - 1)) - cols
        slope = _pair_slopes(g, hp, rows)
        bf_ref[g] = jnp.where(dist >= 0, -slope * rate * dist.astype(F32), MASK_VALUE)
        if g < 2:
            rows = lax.broadcasted_iota(jnp.int32, (2 * Q_TILE, 2 * Q_TILE), 0)
            cols = lax.broadcasted_iota(jnp.int32, (2 * Q_TILE, 2 * Q_TILE), 1)
            dist = (rows & (Q_TILE - 1)) + Q_TILE - cols
            slope = _pair_slopes(g, hp, rows)
            ok = jnp.abs(dist - Q_TILE // 2) <= Q_TILE // 2
            br_ref[g] = jnp.where(ok, -slope * rate * dist.astype(F32), MASK_VALUE)

    lane = lax.broadcasted_iota(jnp.int32, (Q_TILE, LANES), 1)
    lo = lane < DIL_HD

    def rows_of(start, size, rate):
        return pl.ds(start, size) if rate == 1 else pl.ds(start, size, stride=rate)

    def scores_stage(g, q_row, k_row, n_keys, bias, slot):
        rate = DIL_RATES[g]
        q2 = qf_ref[g, rows_of(q_row, Q_TILE, rate), :]
        k2 = k_refs[g][rows_of(k_row, n_keys, rate), :].astype(BF16)
        qs = jnp.concatenate([jnp.where(lo, q2, 0.0), jnp.where(lo, 0.0, q2)], axis=0).astype(BF16)
        s = _dot_nt(qs, k2) + bias()
        m = jnp.max(s, axis=-1, keepdims=True)
        p_ref[slot, :, 0:n_keys] = jnp.exp(s - m).astype(BF16)
        m_ref[slot] = jnp.where(lo, m[:Q_TILE], m[Q_TILE:])

    def values_stage(g, q_row, k_row, n_keys, slot):
        rate = DIL_RATES[g]
        v2 = v_refs[g][rows_of(k_row, n_keys, rate), :].astype(BF16)
        vo = jnp.concatenate([v2, jnp.ones((n_keys, LANES), BF16)], axis=1)
        r = _dot(p_ref[slot, :, 0:n_keys], vo)
        o2 = jnp.where(lo, r[:Q_TILE, :LANES], r[Q_TILE:, :LANES])
        l2 = jnp.where(lo, r[:Q_TILE, LANES:], r[Q_TILE:, LANES:])
        og_ref[g, rows_of(q_row, Q_TILE, rate), :] = o2 / l2
        ld_ref[g, rows_of(q_row, Q_TILE, rate), :] = m_ref[slot] + jnp.log(l2)

    pending = []
    emitted = [0]

    def run_tile_sets(n_sets, n_per_set, tile_of):
        base = emitted[0]

        def slot(i, u):
            return ((i + base) % 2) * TILE_SLOTS + u

        def scores(i):
            for u in range(n_per_set):
                g, q_row, k_row, n_keys, bias = tile_of(i, u)
                scores_stage(g, q_row, k_row, n_keys, bias, slot(i, u))

        def values(i):
            for u in range(n_per_set):
                g, q_row, k_row, n_keys, _ = tile_of(i, u)
                values_stage(g, q_row, k_row, n_keys, slot(i, u))

        if pending:
            pending.pop()()
        scores(0)
        if n_sets > 1:
            def body(i, carry):
                values(i - 1)
                scores(i)
                return carry
            lax.fori_loop(1, n_sets, body, 0)
        pending.append(lambda: values(n_sets - 1))
        emitted[0] += n_sets

    for g in range(DIL_GROUPS):
        rate = DIL_RATES[g]
        n_tiles = seq // rate // Q_TILE
        span = rate * Q_TILE
        group = min(rate, RESIDUES_PER_BODY)
        n_groups = rate // group

        def first_tile(i, u, g=g, group=group):
            rho = i * group + u
            return g, rho, rho, Q_TILE, lambda: bf_ref[g]

        def consecutive_tile(i, u, g=g, span=span):
            n = 1 + i * TILES_PER_BODY + u
            return g, n * span, (n - 1) * span, 2 * Q_TILE, lambda: br_ref[g]

        def later_tile(i, u, g=g, group=group, n_groups=n_groups, span=span):
            n = 1 + i // n_groups
            rho = (i % n_groups) * group + u
            return g, rho + n * span, rho + (n - 1) * span, 2 * Q_TILE, lambda: br_ref[g]

        run_tile_sets(n_groups, group, first_tile)
        if n_tiles > 1:
            if rate == 1:
                assert (n_tiles - 1) % TILES_PER_BODY == 0
                run_tile_sets((n_tiles - 1) // TILES_PER_BODY, TILES_PER_BODY, consecutive_tile)
            else:
                run_tile_sets((n_tiles - 1) * n_groups, group, later_tile)
    pending.pop()()

    def combine(i, carry):
        rs = pl.ds(pl.multiple_of(i * Q_TILE, Q_TILE), Q_TILE)
        ld = [ld_ref[g, rs, :] for g in range(DIL_GROUPS)]
        top = jnp.maximum(jnp.maximum(ld[0], ld[1]), ld[2])
        w = [jnp.exp(x - top) for x in ld]
        num = sum(w[g] * og_ref[g, rs, :] for g in range(DIL_GROUPS))
        o_ref[rs, :] = (num / (w[0] + w[1] + w[2])).astype(o_ref.dtype)
        return carry

    lax.fori_loop(0, seq // Q_TILE, combine, 0)

    for g in range(DIL_GROUPS):
        keep = wk_refs[g].shape[1]
        for src, dst in ((k_refs[g], wk_refs[g]), (v_refs[g], wv_refs[g])):
            for off in range(0, keep, LANES):
                dst[:, off:off + LANES] = src[seq - keep + off:seq - keep + off + LANES, :].T


def _dil_prompt(dq, kv):
    bsz, seq, _ = dq.shape
    pairs = DIL_HEADS * DIL_HD // LANES
    nblk = DIL_WIDTH // LANES
    keeps = [min(w, seq) for w in DIL_WINDOWS]

    def spec(col0):
        return pl.BlockSpec((None, seq, LANES), lambda b, hp, col0=col0: (b, 0, col0 + hp))

    in_specs = ([spec(g * pairs) for g in range(DIL_GROUPS)]
                + [spec(g * pairs) for g in range(DIL_GROUPS)]
                + [spec(nblk + g * pairs) for g in range(DIL_GROUPS)])
    return pl.pallas_call(
        functools.partial(_dil_prompt_kernel, seq=seq),
        grid=(bsz, pairs),
        in_specs=in_specs,
        out_specs=[pl.BlockSpec((None, seq, LANES), lambda b, hp: (b, 0, hp))]
                  + [pl.BlockSpec((None, LANES, w), lambda b, hp: (b, hp, 0)) for w in keeps for _ in range(2)],
        out_shape=[jax.ShapeDtypeStruct((bsz, seq, DIL_OUT), BF16)]
                  + [jax.ShapeDtypeStruct((bsz, DIL_HEADS * DIL_HD, w), F32) for w in keeps for _ in range(2)],
        scratch_shapes=[pltpu.VMEM((DIL_GROUPS, seq, LANES), F32),
                        pltpu.VMEM((DIL_GROUPS, seq, LANES), F32),
                        pltpu.VMEM((DIL_GROUPS, seq, LANES), F32),
                        pltpu.VMEM((DIL_GROUPS, 2 * Q_TILE, Q_TILE), F32),
                        pltpu.VMEM((2, 2 * Q_TILE, 2 * Q_TILE), F32),
                        pltpu.VMEM((2 * TILE_SLOTS, 2 * Q_TILE, 2 * Q_TILE), BF16),
                        pltpu.VMEM((2 * TILE_SLOTS, Q_TILE, LANES), F32)],
        compiler_params=_params("parallel", "arbitrary"),
        name="dil_prompt",
    )(dq, dq, dq, kv, kv, kv, kv, kv, kv)


def _dil_sample_kernel(q_ref, kvn_ref, kb0_ref, vb0_ref, kb1_ref, vb1_ref, kb2_ref, vb2_ref,
                       o_ref, nk0_ref, nv0_ref, nk1_ref, nv1_ref, nk2_ref, nv2_ref, *, n_new):
    kb_refs = (kb0_ref, kb1_ref, kb2_ref)
    vb_refs = (vb0_ref, vb1_ref, vb2_ref)
    nk_refs = (nk0_ref, nk1_ref, nk2_ref)
    nv_refs = (nv0_ref, nv1_ref, nv2_ref)
    pad = q_ref.shape[1]
    nrow = DIL_HEADS * pad
    width = DIL_HEADS * DIL_HD

    lane_head = jnp.right_shift(lax.broadcasted_iota(jnp.int32, (pad, width), 1), DIL_HD.bit_length() - 1)

    def bias_for(g, dist):
        rate = DIL_RATES[g]
        rows = lax.broadcasted_iota(jnp.int32, dist.shape, 0)
        head = sum(jnp.where(rows >= h * pad, 1.0, 0.0) for h in range(1, DIL_HEADS))
        ok = (dist >= 0) & (dist <= DIL_WINDOWS[g]) & ((dist & (rate - 1)) == 0)
        return jnp.where(ok, -_alibi_slope(g, head) * dist.astype(F32), MASK_VALUE)

    sel_r = lax.broadcasted_iota(jnp.int32, (pad, LANES), 0)
    sel_c = lax.broadcasted_iota(jnp.int32, (pad, LANES), 1)
    place = jnp.where((sel_r < n_new) & (sel_c == sel_r + (LANES - n_new)), 1.0, 0.0).astype(BF16)
    tail_lane = lax.broadcasted_iota(jnp.int32, (width, LANES), 1) >= LANES - n_new

    def shifted(buf_t, new_rows):
        length = buf_t.shape[1]
        rolled = pltpu.roll(buf_t, length - n_new, 1)
        new_t = sum(_dot_tn(part, place) for part in _split3(new_rows))
        last = jnp.where(tail_lane, new_t, rolled[:, length - LANES:])
        return rolled, last

    scores, values, transposed = [], [], []
    for g in range(DIL_GROUPS):
        length = kb_refs[g].shape[2]
        cs = slice(g * width, (g + 1) * width)
        vcs = slice(DIL_WIDTH + g * width, DIL_WIDTH + (g + 1) * width)
        qg = q_ref[0, :, cs].astype(F32) * (DIL_HD ** -0.5)
        qs = jnp.concatenate([jnp.where(lane_head == h, qg, 0.0) for h in range(DIL_HEADS)], axis=0).astype(BF16)
        kb, vb = kb_refs[g][0], vb_refs[g][0]
        kn, vn = kvn_ref[0, :, cs], kvn_ref[0, :, vcs]
        rows = lax.broadcasted_iota(jnp.int32, (nrow, length), 0)
        cols = lax.broadcasted_iota(jnp.int32, (nrow, length), 1)
        dist = length + (rows & (pad - 1)) - cols
        scores.append(_dot(qs, kb.astype(BF16)) + bias_for(g, dist))
        values.append(vb.astype(BF16))
        transposed.append(True)
        rows = lax.broadcasted_iota(jnp.int32, (nrow, pad), 0)
        cols = lax.broadcasted_iota(jnp.int32, (nrow, pad), 1)
        dist = jnp.where(cols < n_new, (rows & (pad - 1)) - cols, -1)
        scores.append(_dot_nt(qs, kn.astype(BF16)) + bias_for(g, dist))
        values.append(vn.astype(BF16))
        transposed.append(False)
        for buf, new, out_ref in ((kb, kn, nk_refs[g]), (vb, vn, nv_refs[g])):
            rolled, last = shifted(buf, new)
            if length > LANES:
                out_ref[0, :, 0:length - LANES] = rolled[:, 0:length - LANES]
            out_ref[0, :, length - LANES:length] = last

    top = functools.reduce(jnp.maximum, [jnp.max(s, axis=-1, keepdims=True) for s in scores])
    probs = [jnp.exp(s - top) for s in scores]
    den = sum(jnp.sum(p, axis=-1, keepdims=True) for p in probs)
    acc = sum((_dot_nt if t else _dot)(p.astype(BF16), v)
              for p, v, t in zip(probs, values, transposed)) / den
    out = sum(jnp.where(lane_head == h, acc[h * pad:(h + 1) * pad], 0.0) for h in range(DIL_HEADS))
    o_ref[0] = out.astype(o_ref.dtype)


FFN_COL_CHUNK = D_FF // 2


def _merge_ffn_kernel(x_ref, oa_ref, ob_ref, ga_ref, gb_ref, pa_ref, pb_ref, wo_ref, n2_ref, nf_ref,
                      wg_ref, wu_ref, wd_ref, y_ref):
    pa = _dot(oa_ref[...], pa_ref[...])
    pb = _dot(ob_ref[...], pb_ref[...])
    merged = jax.nn.sigmoid(ga_ref[...].astype(F32)) * pa + jax.nn.sigmoid(gb_ref[...].astype(F32)) * pb
    x1 = x_ref[...] + _dot(merged.astype(BF16), wo_ref[...])
    h = _rms(x1, n2_ref[...]).astype(BF16)
    acc = x1
    for c0 in range(0, D_FF, FFN_COL_CHUNK):
        cs = slice(c0, c0 + FFN_COL_CHUNK)
        gate = _dot(h, wg_ref[:, cs])
        up = _dot(h, wu_ref[:, cs])
        act = (gate * jax.nn.sigmoid(gate) * up).astype(BF16)
        acc = acc + _dot(act, wd_ref[cs, :])
    y_ref[...] = _rms(acc, nf_ref[...])


def _merge_ffn(x, oa, ob, gates, pa, pb, wo, n2, nf, wg, wu, wd, tm):
    m = x.shape[0]
    row = lambda w: pl.BlockSpec((tm, w), lambda i: (i, 0))
    resident = lambda a: pl.BlockSpec(a.shape, lambda i: (0, 0), pipeline_mode=pl.Buffered(1))
    return pl.pallas_call(
        _merge_ffn_kernel,
        grid=(m // tm,),
        in_specs=[row(D_MODEL), row(GLA_DV), row(DIL_OUT), row(D_MODEL),
                  pl.BlockSpec((tm, D_MODEL), lambda i: (i, 1))]
                 + [resident(a) for a in (pa, pb, wo, n2, nf, wg, wu, wd)],
        out_specs=row(D_MODEL),
        out_shape=jax.ShapeDtypeStruct((m, D_MODEL), F32),
        compiler_params=pltpu.CompilerParams(dimension_semantics=("parallel",),
                                             vmem_limit_bytes=MERGE_FFN_VMEM_LIMIT_BYTES),
        name="merge_ffn",
    )(x, oa, ob, gates, gates, pa, pb, wo, n2, nf, wg, wu, wd)


def _prep_weights(w_in, gla_gate_w2, proj_a, proj_b, w_out, w_ffn_gate, w_ffn_up, w_ffn_down):
    w2 = jnp.concatenate([gla_gate_w2[0], jnp.zeros((LANES - GLA_GATE_RANK, GLA_DK), F32)], axis=0)
    return dict(
        w_in_t=jnp.transpose(w_in[0]).astype(BF16),
        w2=w2.astype(BF16), pa=proj_a[0].astype(BF16), pb=proj_b[0].astype(BF16), wo=w_out[0].astype(BF16),
        wg=w_ffn_gate[0].astype(BF16), wu=w_ffn_up[0].astype(BF16), wd=w_ffn_down[0].astype(BF16))


def _project(x2d, norm1_g, gate_b, wts, tm, chunk, n_valid):
    groups = [(0, GLA_COLS), (_OFF["dq"][0], DIL_WIDTH), (_OFF["dk"][0], 2 * DIL_WIDTH), (_OFF["ga"][0], 2 * D_MODEL)]
    return _in_proj(x2d, norm1_g, wts["w_in_t"], wts["w2"], gate_b, groups, _OFF["glr"][0],
                    [BF16, BF16, F32, BF16], tm, chunk, n_valid)


def _tail(x2d, o_a, o_b, z_gates, wts, norm2_g, norm_f_g, tm):
    return _merge_ffn(x2d, o_a, o_b, z_gates, wts["pa"], wts["pb"], wts["wo"], norm2_g,
                      norm_f_g.reshape(1, D_MODEL), wts["wg"], wts["wu"], wts["wd"], tm)


def kernel(x_prompt, x_sample, state_gla, state_win0_k, state_win0_v, state_win1_k, state_win1_v,
           state_win2_k, state_win2_v, norm1_g, w_in, gla_gate_w2, gla_gate_b, gla_norm_g,
           proj_a, proj_b, w_out, norm2_g, w_ffn_gate, w_ffn_up, w_ffn_down, norm_f_g):
    wts = _prep_weights(w_in, gla_gate_w2, proj_a, proj_b, w_out, w_ffn_gate, w_ffn_up, w_ffn_down)
    bp, seq, _ = x_prompt.shape
    bs, n_new, _ = x_sample.shape
    width = DIL_HEADS * DIL_HD

    xp = x_prompt.reshape(bp * seq, D_MODEL)
    z_gla, z_dq, z_kv, z_gates, b_cum = _project(xp, norm1_g, gla_gate_b, wts, 512, GLA_CHUNK, GLA_CHUNK)
    xs = jnp.pad(x_sample, ((0, 0), (0, SAMPLE_PAD - n_new), (0, 0))).reshape(bs * SAMPLE_PAD, D_MODEL)
    zs_gla, zs_dq, zs_kv, zs_gates, bs_cum = _project(xs, norm1_g, gla_gate_b, wts, bs * SAMPLE_PAD,
                                                      SAMPLE_PAD, n_new)

    bufs = [jnp.transpose(a[0], (0, 2, 3, 1)).reshape(bs, width, a.shape[2]) for a in
            (state_win0_k, state_win0_v, state_win1_k, state_win1_v, state_win2_k, state_win2_v)]
    sample = (zs_dq.reshape(bs, SAMPLE_PAD, DIL_WIDTH), zs_kv.reshape(bs, SAMPLE_PAD, 2 * DIL_WIDTH), bufs, n_new)
    s0 = jnp.zeros((bp, GLA_HEADS, GLA_HDK, GLA_HDV), F32)
    o_a, gla_p, os_b, *win_s = _gla(z_gla.reshape(bp, seq, GLA_COLS), b_cum.reshape(bp, seq, GLA_DK), gla_norm_g,
                                    s0, GLA_CHUNK, GLA_CHUNK, GLA_ROWS_PROMPT, sample)

    o_b, *win_p = _dil_prompt(z_dq.reshape(bp, seq, DIL_WIDTH), z_kv.reshape(bp, seq, 2 * DIL_WIDTH))
    y_prompt = _tail(xp, o_a.reshape(bp * seq, GLA_DV), o_b.reshape(bp * seq, DIL_OUT), z_gates, wts,
                     norm2_g, norm_f_g, 512).reshape(bp, seq, D_MODEL)
    win_p = [jnp.transpose(a.reshape(bp, DIL_HEADS, DIL_HD, a.shape[2]), (0, 3, 1, 2))[None] for a in win_p]

    os_a, gla_s = _gla(zs_gla.reshape(bs, SAMPLE_PAD, GLA_COLS), bs_cum.reshape(bs, SAMPLE_PAD, GLA_DK),
                       gla_norm_g, state_gla[0], SAMPLE_PAD, n_new, GLA_ROWS_SAMPLE)
    ys = _tail(xs, os_a.reshape(bs * SAMPLE_PAD, GLA_DV), os_b.reshape(bs * SAMPLE_PAD, DIL_OUT), zs_gates, wts,
               norm2_g, norm_f_g, bs * SAMPLE_PAD)
    y_sample = ys.reshape(bs, SAMPLE_PAD, D_MODEL)[:, :n_new]
    win_s = [jnp.transpose(a.reshape(bs, DIL_HEADS, DIL_HD, a.shape[2]), (0, 3, 1, 2))[None] for a in win_s]

    return (y_prompt, y_sample, gla_p[None], *win_p, gla_s[None], *win_s)
```

```python
import functools

import jax
import jax.numpy as jnp
from jax import lax
from jax.experimental import pallas as pl
from jax.experimental.pallas import tpu as pltpu

F32 = jnp.float32
BF16 = jnp.bfloat16

D_MODEL = 1024
GLA_HEADS = 4
GLA_DK = 512
GLA_DV = 1024
GLA_HDK = 128
GLA_HDV = 256
GLA_GATE_RANK = 16
GLA_TAU = 16.0
DIL_WINDOWS = (128, 512, 2048)
DIL_RATES = (1, 4, 16)
DIL_GROUPS = 3
DIL_HEADS = 4
DIL_HD = 64
DIL_WIDTH = 768
DIL_OUT = 256
ALIBI_MAX = 8.0
D_FF = 2816
RMS_EPS = 1e-6

LANES = 128
SUBLANES = 8
BF16_SUBLANES = 16
Q_TILE = 128
TILES_PER_BODY = 15
RESIDUES_PER_BODY = 8
TILE_SLOTS = max(TILES_PER_BODY, RESIDUES_PER_BODY)
GLA_CHUNK = 128
GLA_ROWS_PROMPT = 4
GLA_ROWS_SAMPLE = 4
SAMPLE_PAD = 16
MASK_VALUE = -1e30
VMEM_LIMIT_BYTES = 48 * 1024 * 1024
IN_PROJ_VMEM_LIMIT_BYTES = 56 * 1024 * 1024
MERGE_FFN_VMEM_LIMIT_BYTES = 60 * 1024 * 1024
GLA_VMEM_LIMIT_BYTES = 56 * 1024 * 1024
_LN2 = 0.6931471805599453

_OFF = {}
_o = 0
for _name, _w in (("gq", GLA_DK), ("gk", GLA_DK), ("gv", GLA_DV), ("gr", GLA_DV), ("glr", GLA_GATE_RANK),
                  ("dq", DIL_WIDTH), ("dk", DIL_WIDTH), ("dv", DIL_WIDTH), ("ga", D_MODEL), ("gb", D_MODEL)):
    _OFF[_name] = (_o, _o + _w)
    _o += _w
GLA_COLS = 2 * GLA_DK + 2 * GLA_DV
GLA_SAFE_LOG_DECAY = 20.0
ROW_BLOCK = 512


def _params(*sem):
    return pltpu.CompilerParams(dimension_semantics=sem, vmem_limit_bytes=VMEM_LIMIT_BYTES)


def _dot(a, b):
    return jnp.dot(a, b, preferred_element_type=F32)


def _dot_nt(a, b):
    return lax.dot_general(a, b, (((1,), (1,)), ((), ())), preferred_element_type=F32)


def _dot_tn(a, b):
    return lax.dot_general(a, b, (((0,), (0,)), ((), ())), preferred_element_type=F32)


def _rms(x, g):
    return x * lax.rsqrt(jnp.mean(x * x, axis=-1, keepdims=True) + RMS_EPS) * g


IN_PROJ_COL_CHUNK = 1024


def _split3(x):
    x1 = x.astype(BF16)
    r1 = x - x1.astype(F32)
    x2 = r1.astype(BF16)
    x3 = (r1 - x2.astype(F32)).astype(BF16)
    return x1, x2, x3


def _in_proj_kernel(x_ref, g_ref, wt_ref, w2_ref, gb_ref, *out_refs, groups, low_rank_col, chunk, n_valid):
    h = _rms(x_ref[...], g_ref[...]).astype(BF16)
    b_ref = out_refs[len(groups)]
    tm = x_ref.shape[0]

    for o_ref, (c_first, n) in zip(out_refs, groups):
        for c0 in range(0, n, IN_PROJ_COL_CHUNK):
            c1 = min(c0 + IN_PROJ_COL_CHUNK, n)
            o_ref[:, c0:c1] = _dot_nt(h, wt_ref[c_first + c0:c_first + c1, :]).astype(o_ref.dtype)

    low_rank = _dot_nt(h, wt_ref[low_rank_col:low_rank_col + LANES, :]).astype(BF16)
    gate = _dot(low_rank, w2_ref[...]) + gb_ref[...]
    log_a = (jnp.minimum(gate, 0.0) - jnp.log(1.0 + jnp.exp(-jnp.abs(gate)))) * (1.0 / GLA_TAU)
    if n_valid < chunk:
        tok = lax.broadcasted_iota(jnp.int32, log_a.shape, 0) & (chunk - 1)
        log_a = jnp.where(tok < n_valid, log_a, 0.0)
    span = min(tm, LANES)
    row = lax.broadcasted_iota(jnp.int32, (span, span), 0)
    col = lax.broadcasted_iota(jnp.int32, (span, span), 1)
    same_chunk = (row & -chunk) == (col & -chunk) if chunk < span else True
    tril = jnp.where((row >= col) & same_chunk, 1.0, 0.0).astype(BF16)
    for r0 in range(0, tm, span):
        parts = _split3(log_a[r0:r0 + span])[:2]
        b_ref[r0:r0 + span, :] = sum(_dot(tril, p) for p in parts)


def _in_proj(x, g, w_t, w2, gate_b, groups, low_rank_col, out_dtypes, tm, chunk, n_valid):
    m, k = x.shape
    assert chunk & (chunk - 1) == 0 and (chunk % LANES == 0 or LANES % chunk == 0) and tm % chunk == 0
    assert all(c % BF16_SUBLANES == 0 and n % LANES == 0 for c, n in groups) and low_rank_col % BF16_SUBLANES == 0
    resident = lambda a: pl.BlockSpec(a.shape, lambda i: (0, 0), pipeline_mode=pl.Buffered(1))
    out_widths = [n for _, n in groups] + [GLA_DK]
    return pl.pallas_call(
        functools.partial(_in_proj_kernel, groups=tuple(groups), low_rank_col=low_rank_col,
                          chunk=chunk, n_valid=n_valid),
        grid=(m // tm,),
        in_specs=[pl.BlockSpec((tm, k), lambda i: (i, 0))] + [resident(a) for a in (g, w_t, w2, gate_b)],
        out_specs=[pl.BlockSpec((tm, n), lambda i: (i, 0)) for n in out_widths],
        out_shape=[jax.ShapeDtypeStruct((m, n), dt) for n, dt in zip(out_widths, list(out_dtypes) + [F32])],
        compiler_params=pltpu.CompilerParams(dimension_semantics=("parallel",),
                                             vmem_limit_bytes=IN_PROJ_VMEM_LIMIT_BYTES),
        name="in_proj",
    )(x, g, w_t, w2, gate_b)


N_SAMPLE_IN = 2 + 2 * DIL_GROUPS
N_SAMPLE_OUT = 1 + 2 * DIL_GROUPS


def _gla_kernel(*refs, chunk, n_valid, n_new_sample):
    q_ref, k_ref, v_ref, r_ref, b_ref, ng_ref, s0_ref = refs[:7]
    refs = refs[7:]
    sample_in = ()
    if n_new_sample:
        sample_in, refs = refs[:N_SAMPLE_IN], refs[N_SAMPLE_IN:]
    o_ref, sout_ref = refs[:2]
    refs = refs[2:]
    sample_out = ()
    if n_new_sample:
        sample_out, refs = refs[:N_SAMPLE_OUT], refs[N_SAMPLE_OUT:]
    s_ref, oi_ref, kf_ref, vf_ref = refs

    c = pl.program_id(1)
    rows = q_ref.shape[0]

    @pl.when(c == 0)
    def _():
        s_ref[...] = s0_ref[...]

    for r in range(rows):
        _gla_chunk(r, q_ref, k_ref, v_ref, r_ref, b_ref, ng_ref, o_ref, s_ref, oi_ref, chunk, n_valid)
    if n_new_sample:
        _dil_sample_kernel(*sample_in, *sample_out, n_new=n_new_sample)

    for r in range(rows):
        @pl.when(jnp.min(b_ref[r, chunk - 1:chunk, :]) < -GLA_SAFE_LOG_DECAY)
        def _(r=r):
            _gla_chunk_exact_intra(r, q_ref, k_ref, v_ref, r_ref, b_ref, ng_ref, o_ref, oi_ref, kf_ref, vf_ref,
                                   chunk, n_valid)

    @pl.when(c == pl.num_programs(1) - 1)
    def _():
        sout_ref[...] = s_ref[...]


def _gla_epilogue(o, r, vs, r_ref, ng_ref, o_ref):
    gr = r_ref[r, :, vs].astype(F32)
    o_ref[r, :, vs] = (_rms(o, ng_ref[...]) * (gr * jax.nn.sigmoid(gr))).astype(o_ref.dtype)


def _gla_values(r, vs, v_ref, chunk, n_valid):
    vh = v_ref[r, :, vs]
    if n_valid < chunk:
        tok = lax.broadcasted_iota(jnp.int32, vh.shape, 0)
        vh = jnp.where(tok < n_valid, vh, jnp.zeros_like(vh))
    return vh


def _gla_chunk_exact_intra(r, q_ref, k_ref, v_ref, r_ref, b_ref, ng_ref, o_ref, oi_ref, kf_ref, vf_ref,
                           chunk, n_valid):
    tok = lax.broadcasted_iota(jnp.int32, (chunk, 1), 0)
    for h in range(GLA_HEADS):
        ks = slice(h * GLA_HDK, (h + 1) * GLA_HDK)
        vs = slice(h * GLA_HDV, (h + 1) * GLA_HDV)
        bh = b_ref[r, :, ks]
        qh = q_ref[r, :, ks].astype(F32) * (GLA_HDK ** -0.5)
        kf_ref[...] = k_ref[r, :, ks].astype(F32)
        vf_ref[...] = _gla_values(r, vs, v_ref, chunk, n_valid).astype(F32)

        def eight_keys(i, acc, bh=bh, qh=qh, ks=ks):
            rows8 = pl.ds(pl.multiple_of(i * SUBLANES, SUBLANES), SUBLANES)
            b8, k8, v8 = b_ref[r, rows8, ks], kf_ref[rows8, :], vf_ref[rows8, :]
            for j in range(SUBLANES):
                s = i * SUBLANES + j
                decay = jnp.exp(jnp.where(tok >= s, bh - b8[j:j + 1], MASK_VALUE))
                w = jnp.sum(qh * decay * k8[j:j + 1], axis=-1, keepdims=True)
                acc = acc + w * v8[j:j + 1]
            return acc

        intra = lax.fori_loop(0, chunk // SUBLANES, eight_keys, jnp.zeros((chunk, GLA_HDV), F32))
        _gla_epilogue(oi_ref[r, h] + intra, r, vs, r_ref, ng_ref, o_ref)


def _gla_chunk(r, q_ref, k_ref, v_ref, r_ref, b_ref, ng_ref, o_ref, s_ref, oi_ref, chunk, n_valid):
    row = lax.broadcasted_iota(jnp.int32, (chunk, chunk), 0)
    col = lax.broadcasted_iota(jnp.int32, (chunk, chunk), 1)
    causal = row >= col
    b = b_ref[r]

    for h in range(GLA_HEADS):
        ks = slice(h * GLA_HDK, (h + 1) * GLA_HDK)
        vs = slice(h * GLA_HDV, (h + 1) * GLA_HDV)
        bh = b[:, ks]
        qh = q_ref[r, :, ks].astype(F32) * (GLA_HDK ** -0.5)
        kh = k_ref[r, :, ks].astype(F32)
        vh = _gla_values(r, vs, v_ref, chunk, n_valid)
        qt = (qh * jnp.exp(bh)).astype(BF16)
        kt = (kh * jnp.exp(-bh)).astype(BF16)
        kd = (kh * jnp.exp(bh[chunk - 1:chunk, :] - bh)).astype(BF16)
        s_old = s_ref[r, h]
        scores = jnp.where(causal, _dot_nt(qt, kt), 0.0).astype(BF16)
        o_state = _dot(qt, s_old.astype(BF16))
        oi_ref[r, h] = o_state
        dec = jnp.exp(jnp.broadcast_to(bh[chunk - 1:chunk, :], (GLA_HDK, GLA_HDK)).T)
        s_ref[r, h] = s_old * jnp.concatenate([dec, dec], axis=1) + _dot_tn(kd, vh)
        _gla_epilogue(o_state + _dot(scores, vh), r, vs, r_ref, ng_ref, o_ref)


def _gla(z, b_cum, norm_g, s0, chunk, n_valid, rows, sample=None):
    bsz, t, _ = z.shape
    n_chunks = t // chunk
    state = pl.BlockSpec((rows, GLA_HEADS, GLA_HDK, GLA_HDV), lambda b, c: (b, 0, 0, 0))
    in_specs = [pl.BlockSpec((rows, chunk, GLA_DK), lambda b, c: (b, c, 0)),
                pl.BlockSpec((rows, chunk, GLA_DK), lambda b, c: (b, c, 1)),
                pl.BlockSpec((rows, chunk, GLA_DV), lambda b, c: (b, c, 1)),
                pl.BlockSpec((rows, chunk, GLA_DV), lambda b, c: (b, c, 2)),
                pl.BlockSpec((rows, chunk, GLA_DK), lambda b, c: (b, c, 0)),
                pl.BlockSpec((1, GLA_HDV), lambda b, c: (0, 0)),
                state]
    out_specs = [pl.BlockSpec((rows, chunk, GLA_DV), lambda b, c: (b, c, 0)), state]
    out_shape = [jax.ShapeDtypeStruct((bsz, t, GLA_DV), BF16),
                 jax.ShapeDtypeStruct((bsz, GLA_HEADS, GLA_HDK, GLA_HDV), F32)]
    operands = [z, z, z, z, b_cum, norm_g, s0]
    n_new = 0
    if sample is not None:
        dq, kvn, bufs, n_new = sample
        assert dq.shape[0] == (bsz // rows) * n_chunks
        per_step = lambda a: pl.BlockSpec((1,) + a.shape[1:], lambda b, c: (b * n_chunks + c, 0, 0))
        in_specs += [per_step(a) for a in (dq, kvn, *bufs)]
        operands += [dq, kvn, *bufs]
        o_s = jax.ShapeDtypeStruct((dq.shape[0], dq.shape[1], DIL_OUT), BF16)
        out_specs += [per_step(o_s)] + [per_step(a) for a in bufs]
        out_shape += [o_s] + [jax.ShapeDtypeStruct(a.shape, a.dtype) for a in bufs]
    return pl.pallas_call(
        functools.partial(_gla_kernel, chunk=chunk, n_valid=n_valid, n_new_sample=n_new),
        grid=(bsz // rows, n_chunks),
        in_specs=in_specs,
        out_specs=out_specs,
        out_shape=out_shape,
        scratch_shapes=[pltpu.VMEM((rows, GLA_HEADS, GLA_HDK, GLA_HDV), F32),
                        pltpu.VMEM((rows, GLA_HEADS, chunk, GLA_HDV), F32),
                        pltpu.VMEM((chunk, GLA_HDK), F32),
                        pltpu.VMEM((chunk, GLA_HDV), F32)],
        compiler_params=pltpu.CompilerParams(dimension_semantics=("parallel", "arbitrary"),
                                             vmem_limit_bytes=GLA_VMEM_LIMIT_BYTES),
        name="gla",
    )(*operands)


def _alibi_slope(g, head):
    n = DIL_GROUPS * DIL_HEADS
    return jnp.exp((-ALIBI_MAX * _LN2 / n) * (head + (g * DIL_HEADS + 1.0)))


def _pair_slopes(g, hp, rows):
    head = 2.0 * hp.astype(F32) + jnp.where(rows >= Q_TILE, 1.0, 0.0)
    return _alibi_slope(g, head)


def _dil_prompt_kernel(q0_ref, q1_ref, q2_ref, k0_ref, k1_ref, k2_ref, v0_ref, v1_ref, v2_ref,
                       o_ref, wk0_ref, wv0_ref, wk1_ref, wv1_ref, wk2_ref, wv2_ref,
                       qf_ref, og_ref, ld_ref, bf_ref, br_ref, p_ref, m_ref, *, seq):
    hp = pl.program_id(1)
    q_refs = (q0_ref, q1_ref, q2_ref)
    k_refs = (k0_ref, k1_ref, k2_ref)
    v_refs = (v0_ref, v1_ref, v2_ref)
    wk_refs = (wk0_ref, wk1_ref, wk2_ref)
    wv_refs = (wv0_ref, wv1_ref, wv2_ref)

    for g in range(DIL_GROUPS):
        qf_ref[g] = q_refs[g][...].astype(F32) * (DIL_HD ** -0.5)

    for g in range(DIL_GROUPS):
        rate = float(DIL_RATES[g])
        rows = lax.broadcasted_iota(jnp.int32, (2 * Q_TILE, Q_TILE), 0)
        cols = lax.broadcasted_iota(jnp.int32, (2 * Q_TILE, Q_TILE), 1)
        dist = (rows & (Q_TILE - 1)) - cols
        slope = _pair_slopes(g, hp, rows)
        bf_ref[g] = jnp.where(dist >= 0, -slope * rate * dist.astype(F32), MASK_VALUE)
        if g < 2:
            rows = lax.broadcasted_iota(jnp.int32, (2 * Q_TILE, 2 * Q_TILE), 0)
            cols = lax.broadcasted_iota(jnp.int32, (2 * Q_TILE, 2 * Q_TILE), 1)
            dist = (rows & (Q_TILE - 1)) + Q_TILE - cols
            slope = _pair_slopes(g, hp, rows)
            ok = jnp.abs(dist - Q_TILE // 2) <= Q_TILE // 2
            br_ref[g] = jnp.where(ok, -slope * rate * dist.astype(F32), MASK_VALUE)

    lane = lax.broadcasted_iota(jnp.int32, (Q_TILE, LANES), 1)
    lo = lane < DIL_HD

    def rows_of(start, size, rate):
        return pl.ds(start, size) if rate == 1 else pl.ds(start, size, stride=rate)

    def scores_stage(g, q_row, k_row, n_keys, bias, slot):
        rate = DIL_RATES[g]
        q2 = qf_ref[g, rows_of(q_row, Q_TILE, rate), :]
        k2 = k_refs[g][rows_of(k_row, n_keys, rate), :].astype(BF16)
        qs = jnp.concatenate([jnp.where(lo, q2, 0.0), jnp.where(lo, 0.0, q2)], axis=0).astype(BF16)
        s = _dot_nt(qs, k2) + bias()
        m = jnp.max(s, axis=-1, keepdims=True)
        p_ref[slot, :, 0:n_keys] = jnp.exp(s - m).astype(BF16)
        m_ref[slot] = jnp.where(lo, m[:Q_TILE], m[Q_TILE:])

    def values_stage(g, q_row, k_row, n_keys, slot):
        rate = DIL_RATES[g]
        v2 = v_refs[g][rows_of(k_row, n_keys, rate), :].astype(BF16)
        vo = jnp.concatenate([v2, jnp.ones((n_keys, LANES), BF16)], axis=1)
        r = _dot(p_ref[slot, :, 0:n_keys], vo)
        o2 = jnp.where(lo, r[:Q_TILE, :LANES], r[Q_TILE:, :LANES])
        l2 = jnp.where(lo, r[:Q_TILE, LANES:], r[Q_TILE:, LANES:])
        og_ref[g, rows_of(q_row, Q_TILE, rate), :] = o2 / l2
        ld_ref[g, rows_of(q_row, Q_TILE, rate), :] = m_ref[slot] + jnp.log(l2)

    pending = []
    emitted = [0]

    def run_tile_sets(n_sets, n_per_set, tile_of):
        base = emitted[0]

        def slot(i, u):
            return ((i + base) % 2) * TILE_SLOTS + u

        def scores(i):
            for u in range(n_per_set):
                g, q_row, k_row, n_keys, bias = tile_of(i, u)
                scores_stage(g, q_row, k_row, n_keys, bias, slot(i, u))

        def values(i):
            for u in range(n_per_set):
                g, q_row, k_row, n_keys, _ = tile_of(i, u)
                values_stage(g, q_row, k_row, n_keys, slot(i, u))

        if pending:
            pending.pop()()
        scores(0)
        if n_sets > 1:
            def body(i, carry):
                values(i - 1)
                scores(i)
                return carry
            lax.fori_loop(1, n_sets, body, 0)
        pending.append(lambda: values(n_sets - 1))
        emitted[0] += n_sets

    for g in range(DIL_GROUPS):
        rate = DIL_RATES[g]
        n_tiles = seq // rate // Q_TILE
        span = rate * Q_TILE
        group = min(rate, RESIDUES_PER_BODY)
        n_groups = rate // group

        def first_tile(i, u, g=g, group=group):
            rho = i * group + u
            return g, rho, rho, Q_TILE, lambda: bf_ref[g]

        def consecutive_tile(i, u, g=g, span=span):
            n = 1 + i * TILES_PER_BODY + u
            return g, n * span, (n - 1) * span, 2 * Q_TILE, lambda: br_ref[g]

        def later_tile(i, u, g=g, group=group, n_groups=n_groups, span=span):
            n = 1 + i // n_groups
            rho = (i % n_groups) * group + u
            return g, rho + n * span, rho + (n - 1) * span, 2 * Q_TILE, lambda: br_ref[g]

        run_tile_sets(n_groups, group, first_tile)
        if n_tiles > 1:
            if rate == 1:
                assert (n_tiles - 1) % TILES_PER_BODY == 0
                run_tile_sets((n_tiles - 1) // TILES_PER_BODY, TILES_PER_BODY, consecutive_tile)
            else:
                run_tile_sets((n_tiles - 1) * n_groups, group, later_tile)
    pending.pop()()

    def combine(i, carry):
        rs = pl.ds(pl.multiple_of(i * Q_TILE, Q_TILE), Q_TILE)
        ld = [ld_ref[g, rs, :] for g in range(DIL_GROUPS)]
        top = jnp.maximum(jnp.maximum(ld[0], ld[1]), ld[2])
        w = [jnp.exp(x - top) for x in ld]
        num = sum(w[g] * og_ref[g, rs, :] for g in range(DIL_GROUPS))
        o_ref[rs, :] = (num / (w[0] + w[1] + w[2])).astype(o_ref.dtype)
        return carry

    lax.fori_loop(0, seq // Q_TILE, combine, 0)

    for g in range(DIL_GROUPS):
        keep = wk_refs[g].shape[1]
        for src, dst in ((k_refs[g], wk_refs[g]), (v_refs[g], wv_refs[g])):
            for off in range(0, keep, LANES):
                dst[:, off:off + LANES] = src[seq - keep + off:seq - keep + off + LANES, :].T


def _dil_prompt(dq, kv):
    bsz, seq, _ = dq.shape
    pairs = DIL_HEADS * DIL_HD // LANES
    nblk = DIL_WIDTH // LANES
    keeps = [min(w, seq) for w in DIL_WINDOWS]

    def spec(col0):
        return pl.BlockSpec((None, seq, LANES), lambda b, hp, col0=col0: (b, 0, col0 + hp))

    in_specs = ([spec(g * pairs) for g in range(DIL_GROUPS)]
                + [spec(g * pairs) for g in range(DIL_GROUPS)]
                + [spec(nblk + g * pairs) for g in range(DIL_GROUPS)])
    return pl.pallas_call(
        functools.partial(_dil_prompt_kernel, seq=seq),
        grid=(bsz, pairs),
        in_specs=in_specs,
        out_specs=[pl.BlockSpec((None, seq, LANES), lambda b, hp: (b, 0, hp))]
                  + [pl.BlockSpec((None, LANES, w), lambda b, hp: (b, hp, 0)) for w in keeps for _ in range(2)],
        out_shape=[jax.ShapeDtypeStruct((bsz, seq, DIL_OUT), BF16)]
                  + [jax.ShapeDtypeStruct((bsz, DIL_HEADS * DIL_HD, w), F32) for w in keeps for _ in range(2)],
        scratch_shapes=[pltpu.VMEM((DIL_GROUPS, seq, LANES), F32),
                        pltpu.VMEM((DIL_GROUPS, seq, LANES), F32),
                        pltpu.VMEM((DIL_GROUPS, seq, LANES), F32),
                        pltpu.VMEM((DIL_GROUPS, 2 * Q_TILE, Q_TILE), F32),
                        pltpu.VMEM((2, 2 * Q_TILE, 2 * Q_TILE), F32),
                        pltpu.VMEM((2 * TILE_SLOTS, 2 * Q_TILE, 2 * Q_TILE), BF16),
                        pltpu.VMEM((2 * TILE_SLOTS, Q_TILE, LANES), F32)],
        compiler_params=_params("parallel", "arbitrary"),
        name="dil_prompt",
    )(dq, dq, dq, kv, kv, kv, kv, kv, kv)


def _dil_sample_kernel(q_ref, kvn_ref, kb0_ref, vb0_ref, kb1_ref, vb1_ref, kb2_ref, vb2_ref,
                       o_ref, nk0_ref, nv0_ref, nk1_ref, nv1_ref, nk2_ref, nv2_ref, *, n_new):
    kb_refs = (kb0_ref, kb1_ref, kb2_ref)
    vb_refs = (vb0_ref, vb1_ref, vb2_ref)
    nk_refs = (nk0_ref, nk1_ref, nk2_ref)
    nv_refs = (nv0_ref, nv1_ref, nv2_ref)
    pad = q_ref.shape[1]
    nrow = DIL_HEADS * pad
    width = DIL_HEADS * DIL_HD

    lane_head = jnp.right_shift(lax.broadcasted_iota(jnp.int32, (pad, width), 1), DIL_HD.bit_length() - 1)

    def bias_for(g, dist):
        rate = DIL_RATES[g]
        rows = lax.broadcasted_iota(jnp.int32, dist.shape, 0)
        head = sum(jnp.where(rows >= h * pad, 1.0, 0.0) for h in range(1, DIL_HEADS))
        ok = (dist >= 0) & (dist <= DIL_WINDOWS[g]) & ((dist & (rate - 1)) == 0)
        return jnp.where(ok, -_alibi_slope(g, head) * dist.astype(F32), MASK_VALUE)

    sel_r = lax.broadcasted_iota(jnp.int32, (pad, LANES), 0)
    sel_c = lax.broadcasted_iota(jnp.int32, (pad, LANES), 1)
    place = jnp.where((sel_r < n_new) & (sel_c == sel_r + (LANES - n_new)), 1.0, 0.0).astype(BF16)
    tail_lane = lax.broadcasted_iota(jnp.int32, (width, LANES), 1) >= LANES - n_new

    def shifted(buf_t, new_rows):
        length = buf_t.shape[1]
        rolled = pltpu.roll(buf_t, length - n_new, 1)
        new_t = sum(_dot_tn(part, place) for part in _split3(new_rows))
        last = jnp.where(tail_lane, new_t, rolled[:, length - LANES:])
        return rolled, last

    scores, values, transposed = [], [], []
    for g in range(DIL_GROUPS):
        length = kb_refs[g].shape[2]
        cs = slice(g * width, (g + 1) * width)
        vcs = slice(DIL_WIDTH + g * width, DIL_WIDTH + (g + 1) * width)
        qg = q_ref[0, :, cs].astype(F32) * (DIL_HD ** -0.5)
        qs = jnp.concatenate([jnp.where(lane_head == h, qg, 0.0) for h in range(DIL_HEADS)], axis=0).astype(BF16)
        kb, vb = kb_refs[g][0], vb_refs[g][0]
        kn, vn = kvn_ref[0, :, cs], kvn_ref[0, :, vcs]
        rows = lax.broadcasted_iota(jnp.int32, (nrow, length), 0)
        cols = lax.broadcasted_iota(jnp.int32, (nrow, length), 1)
        dist = length + (rows & (pad - 1)) - cols
        scores.append(_dot(qs, kb.astype(BF16)) + bias_for(g, dist))
        values.append(vb.astype(BF16))
        transposed.append(True)
        rows = lax.broadcasted_iota(jnp.int32, (nrow, pad), 0)
        cols = lax.broadcasted_iota(jnp.int32, (nrow, pad), 1)
        dist = jnp.where(cols < n_new, (rows & (pad - 1)) - cols, -1)
        scores.append(_dot_nt(qs, kn.astype(BF16)) + bias_for(g, dist))
        values.append(vn.astype(BF16))
        transposed.append(False)
        for buf, new, out_ref in ((kb, kn, nk_refs[g]), (vb, vn, nv_refs[g])):
            rolled, last = shifted(buf, new)
            if length > LANES:
                out_ref[0, :, 0:length - LANES] = rolled[:, 0:length - LANES]
            out_ref[0, :, length - LANES:length] = last

    top = functools.reduce(jnp.maximum, [jnp.max(s, axis=-1, keepdims=True) for s in scores])
    probs = [jnp.exp(s - top) for s in scores]
    den = sum(jnp.sum(p, axis=-1, keepdims=True) for p in probs)
    acc = sum((_dot_nt if t else _dot)(p.astype(BF16), v)
              for p, v, t in zip(probs, values, transposed)) / den
    out = sum(jnp.where(lane_head == h, acc[h * pad:(h + 1) * pad], 0.0) for h in range(DIL_HEADS))
    o_ref[0] = out.astype(o_ref.dtype)


FFN_COL_CHUNK = D_FF // 2


def _merge_ffn_kernel(x_ref, oa_ref, ob_ref, ga_ref, gb_ref, pa_ref, pb_ref, wo_ref, n2_ref, nf_ref,
                      wg_ref, wu_ref, wd_ref, y_ref):
    pa = _dot(oa_ref[...], pa_ref[...])
    pb = _dot(ob_ref[...], pb_ref[...])
    merged = jax.nn.sigmoid(ga_ref[...].astype(F32)) * pa + jax.nn.sigmoid(gb_ref[...].astype(F32)) * pb
    x1 = x_ref[...] + _dot(merged.astype(BF16), wo_ref[...])
    h = _rms(x1, n2_ref[...]).astype(BF16)
    acc = x1
    for c0 in range(0, D_FF, FFN_COL_CHUNK):
        cs = slice(c0, c0 + FFN_COL_CHUNK)
        gate = _dot(h, wg_ref[:, cs])
        up = _dot(h, wu_ref[:, cs])
        act = (gate * jax.nn.sigmoid(gate) * up).astype(BF16)
        acc = acc + _dot(act, wd_ref[cs, :])
    y_ref[...] = _rms(acc, nf_ref[...])


def _merge_ffn(x, oa, ob, gates, pa, pb, wo, n2, nf, wg, wu, wd, tm):
    m = x.shape[0]
    row = lambda w: pl.BlockSpec((tm, w), lambda i: (i, 0))
    resident = lambda a: pl.BlockSpec(a.shape, lambda i: (0, 0), pipeline_mode=pl.Buffered(1))
    return pl.pallas_call(
        _merge_ffn_kernel,
        grid=(m // tm,),
        in_specs=[row(D_MODEL), row(GLA_DV), row(DIL_OUT), row(D_MODEL),
                  pl.BlockSpec((tm, D_MODEL), lambda i: (i, 1))]
                 + [resident(a) for a in (pa, pb, wo, n2, nf, wg, wu, wd)],
        out_specs=row(D_MODEL),
        out_shape=jax.ShapeDtypeStruct((m, D_MODEL), F32),
        compiler_params=pltpu.CompilerParams(dimension_semantics=("parallel",),
                                             vmem_limit_bytes=MERGE_FFN_VMEM_LIMIT_BYTES),
        name="merge_ffn",
    )(x, oa, ob, gates, gates, pa, pb, wo, n2, nf, wg, wu, wd)


def _prep_weights(w_in, gla_gate_w2, proj_a, proj_b, w_out, w_ffn_gate, w_ffn_up, w_ffn_down):
    w2 = jnp.concatenate([gla_gate_w2[0], jnp.zeros((LANES - GLA_GATE_RANK, GLA_DK), F32)], axis=0)
    return dict(
        w_in_t=jnp.transpose(w_in[0]).astype(BF16),
        w2=w2.astype(BF16), pa=proj_a[0].astype(BF16), pb=proj_b[0].astype(BF16), wo=w_out[0].astype(BF16),
        wg=w_ffn_gate[0].astype(BF16), wu=w_ffn_up[0].astype(BF16), wd=w_ffn_down[0].astype(BF16))


def _project(x2d, norm1_g, gate_b, wts, tm, chunk, n_valid):
    groups = [(0, GLA_COLS), (_OFF["dq"][0], DIL_WIDTH), (_OFF["dk"][0], 2 * DIL_WIDTH), (_OFF["ga"][0], 2 * D_MODEL)]
    return _in_proj(x2d, norm1_g, wts["w_in_t"], wts["w2"], gate_b, groups, _OFF["glr"][0],
                    [BF16, BF16, F32, BF16], tm, chunk, n_valid)


def _tail(x2d, o_a, o_b, z_gates, wts, norm2_g, norm_f_g, tm):
    return _merge_ffn(x2d, o_a, o_b, z_gates, wts["pa"], wts["pb"], wts["wo"], norm2_g,
                      norm_f_g.reshape(1, D_MODEL), wts["wg"], wts["wu"], wts["wd"], tm)


def kernel(x_prompt, x_sample, state_gla, state_win0_k, state_win0_v, state_win1_k, state_win1_v,
           state_win2_k, state_win2_v, norm1_g, w_in, gla_gate_w2, gla_gate_b, gla_norm_g,
           proj_a, proj_b, w_out, norm2_g, w_ffn_gate, w_ffn_up, w_ffn_down, norm_f_g):
    wts = _prep_weights(w_in, gla_gate_w2, proj_a, proj_b, w_out, w_ffn_gate, w_ffn_up, w_ffn_down)
    bp, seq, _ = x_prompt.shape
    bs, n_new, _ = x_sample.shape
    width = DIL_HEADS * DIL_HD

    xp = x_prompt.reshape(bp * seq, D_MODEL)
    z_gla, z_dq, z_kv, z_gates, b_cum = _project(xp, norm1_g, gla_gate_b, wts, ROW_BLOCK, GLA_CHUNK, GLA_CHUNK)
    xs = jnp.pad(x_sample, ((0, 0), (0, SAMPLE_PAD - n_new), (0, 0))).reshape(bs * SAMPLE_PAD, D_MODEL)
    zs_gla, zs_dq, zs_kv, zs_gates, bs_cum = _project(xs, norm1_g, gla_gate_b, wts, bs * SAMPLE_PAD,
                                                      SAMPLE_PAD, n_new)

    bufs = [jnp.transpose(a[0], (0, 2, 3, 1)).reshape(bs, width, a.shape[2]) for a in
            (state_win0_k, state_win0_v, state_win1_k, state_win1_v, state_win2_k, state_win2_v)]
    sample = (zs_dq.reshape(bs, SAMPLE_PAD, DIL_WIDTH), zs_kv.reshape(bs, SAMPLE_PAD, 2 * DIL_WIDTH), bufs, n_new)
    s0 = jnp.zeros((bp, GLA_HEADS, GLA_HDK, GLA_HDV), F32)
    o_a, gla_p, os_b, *win_s = _gla(z_gla.reshape(bp, seq, GLA_COLS), b_cum.reshape(bp, seq, GLA_DK), gla_norm_g,
                                    s0, GLA_CHUNK, GLA_CHUNK, GLA_ROWS_PROMPT, sample)

    o_b, *win_p = _dil_prompt(z_dq.reshape(bp, seq, DIL_WIDTH), z_kv.reshape(bp, seq, 2 * DIL_WIDTH))
    y_prompt = _tail(xp, o_a.reshape(bp * seq, GLA_DV), o_b.reshape(bp * seq, DIL_OUT), z_gates, wts,
                     norm2_g, norm_f_g, ROW_BLOCK).reshape(bp, seq, D_MODEL)
    win_p = [jnp.transpose(a.reshape(bp, DIL_HEADS, DIL_HD, a.shape[2]), (0, 3, 1, 2))[None] for a in win_p]

    os_a, gla_s = _gla(zs_gla.reshape(bs, SAMPLE_PAD, GLA_COLS), bs_cum.reshape(bs, SAMPLE_PAD, GLA_DK),
                       gla_norm_g, state_gla[0], SAMPLE_PAD, n_new, GLA_ROWS_SAMPLE)
    ys = _tail(xs, os_a.reshape(bs * SAMPLE_PAD, GLA_DV), os_b.reshape(bs * SAMPLE_PAD, DIL_OUT), zs_gates, wts,
               norm2_g, norm_f_g, bs * SAMPLE_PAD)
    y_sample = ys.reshape(bs, SAMPLE_PAD, D_MODEL)[:, :n_new]
    win_s = [jnp.transpose(a.reshape(bs, DIL_HEADS, DIL_HD, a.shape[2]), (0, 3, 1, 2))[None] for a in win_s]

    return (y_prompt, y_sample, gla_p[None], *win_p, gla_s[None], *win_s)
```

```python
import functools

import jax
import jax.numpy as jnp
from jax import lax
from jax.experimental import pallas as pl
from jax.experimental.pallas import tpu as pltpu

F32 = jnp.float32
BF16 = jnp.bfloat16

D_MODEL = 1024
GLA_HEADS = 4
GLA_DK = 512
GLA_DV = 1024
GLA_HDK = 128
GLA_HDV = 256
GLA_GATE_RANK = 16
GLA_TAU = 16.0
DIL_WINDOWS = (128, 512, 2048)
DIL_RATES = (1, 4, 16)
DIL_GROUPS = 3
DIL_HEADS = 4
DIL_HD = 64
DIL_WIDTH = 768
DIL_OUT = 256
ALIBI_MAX = 8.0
D_FF = 2816
RMS_EPS = 1e-6

LANES = 128
SUBLANES = 8
BF16_SUBLANES = 16
Q_TILE = 128
TILES_PER_BODY = 15
RESIDUES_PER_BODY = 8
TILE_SLOTS = max(TILES_PER_BODY, RESIDUES_PER_BODY)
GLA_CHUNK = 128
GLA_ROWS_PROMPT = 4
GLA_ROWS_SAMPLE = 4
SAMPLE_PAD = 16
MASK_VALUE = -1e30
VMEM_LIMIT_BYTES = 48 * 1024 * 1024
IN_PROJ_VMEM_LIMIT_BYTES = 56 * 1024 * 1024
MERGE_FFN_VMEM_LIMIT_BYTES = 60 * 1024 * 1024
GLA_VMEM_LIMIT_BYTES = 56 * 1024 * 1024
_LN2 = 0.6931471805599453

_OFF = {}
_o = 0
for _name, _w in (("gq", GLA_DK), ("gk", GLA_DK), ("gv", GLA_DV), ("gr", GLA_DV), ("glr", GLA_GATE_RANK),
                  ("dq", DIL_WIDTH), ("dk", DIL_WIDTH), ("dv", DIL_WIDTH), ("ga", D_MODEL), ("gb", D_MODEL)):
    _OFF[_name] = (_o, _o + _w)
    _o += _w
GLA_COLS = 2 * GLA_DK + 2 * GLA_DV
GLA_SAFE_LOG_DECAY = 20.0
ROW_BLOCK = 512


def _params(*sem):
    return pltpu.CompilerParams(dimension_semantics=sem, vmem_limit_bytes=VMEM_LIMIT_BYTES)


def _dot(a, b):
    return jnp.dot(a, b, preferred_element_type=F32)


def _dot_nt(a, b):
    return lax.dot_general(a, b, (((1,), (1,)), ((), ())), preferred_element_type=F32)


def _dot_tn(a, b):
    return lax.dot_general(a, b, (((0,), (0,)), ((), ())), preferred_element_type=F32)


def _rms(x, g):
    return x * lax.rsqrt(jnp.mean(x * x, axis=-1, keepdims=True) + RMS_EPS) * g


IN_PROJ_COL_CHUNK = 1024


def _split3(x):
    x1 = x.astype(BF16)
    r1 = x - x1.astype(F32)
    x2 = r1.astype(BF16)
    x3 = (r1 - x2.astype(F32)).astype(BF16)
    return x1, x2, x3


def _in_proj_kernel(x_ref, g_ref, wt_ref, w2_ref, gb_ref, *refs, groups, low_rank_col, chunk, n_valid, n_casts):
    cast_in, out_refs = refs[:n_casts], refs[n_casts:]
    for src, dst in zip(cast_in, out_refs[len(out_refs) - n_casts:]):
        dst[...] = src[...].astype(dst.dtype)

    h = _rms(x_ref[...], g_ref[...]).astype(BF16)
    b_ref = out_refs[len(groups)]
    tm = x_ref.shape[0]

    for o_ref, (c_first, n) in zip(out_refs, groups):
        for c0 in range(0, n, IN_PROJ_COL_CHUNK):
            c1 = min(c0 + IN_PROJ_COL_CHUNK, n)
            o_ref[:, c0:c1] = _dot_nt(h, wt_ref[c_first + c0:c_first + c1, :]).astype(o_ref.dtype)

    low_rank = _dot_nt(h, wt_ref[low_rank_col:low_rank_col + LANES, :]).astype(BF16)
    gate = _dot(low_rank, w2_ref[...]) + gb_ref[...]
    log_a = (jnp.minimum(gate, 0.0) - jnp.log(1.0 + jnp.exp(-jnp.abs(gate)))) * (1.0 / GLA_TAU)
    if n_valid < chunk:
        tok = lax.broadcasted_iota(jnp.int32, log_a.shape, 0) & (chunk - 1)
        log_a = jnp.where(tok < n_valid, log_a, 0.0)
    span = min(tm, LANES)
    row = lax.broadcasted_iota(jnp.int32, (span, span), 0)
    col = lax.broadcasted_iota(jnp.int32, (span, span), 1)
    same_chunk = (row & -chunk) == (col & -chunk) if chunk < span else True
    tril = jnp.where((row >= col) & same_chunk, 1.0, 0.0).astype(BF16)
    for r0 in range(0, tm, span):
        parts = _split3(log_a[r0:r0 + span])[:2]
        b_ref[r0:r0 + span, :] = sum(_dot(tril, p) for p in parts)


def _cast_blocks(a, steps):
    count = max(c for c in range(1, steps + 1) if a.shape[0] % (c * BF16_SUBLANES) == 0)
    return a.shape[0] // count, count


def _in_proj(x, g, w_t, w2, gate_b, groups, low_rank_col, out_dtypes, tm, chunk, n_valid, casts=()):
    m, k = x.shape
    steps = m // tm
    assert chunk & (chunk - 1) == 0 and (chunk % LANES == 0 or LANES % chunk == 0) and tm % chunk == 0
    assert all(c % BF16_SUBLANES == 0 and n % LANES == 0 for c, n in groups) and low_rank_col % BF16_SUBLANES == 0
    resident = lambda a: pl.BlockSpec(a.shape, lambda i: (0, 0), pipeline_mode=pl.Buffered(1))
    out_widths = [n for _, n in groups] + [GLA_DK]

    def cast_spec(a):
        rows, count = _cast_blocks(a, steps)
        return pl.BlockSpec((rows, a.shape[1]), lambda i, count=count: (jnp.minimum(i, count - 1), 0))

    return pl.pallas_call(
        functools.partial(_in_proj_kernel, groups=tuple(groups), low_rank_col=low_rank_col,
                          chunk=chunk, n_valid=n_valid, n_casts=len(casts)),
        grid=(steps,),
        in_specs=([pl.BlockSpec((tm, k), lambda i: (i, 0))] + [resident(a) for a in (g, w_t, w2, gate_b)]
                  + [cast_spec(a) for a in casts]),
        out_specs=[pl.BlockSpec((tm, n), lambda i: (i, 0)) for n in out_widths] + [cast_spec(a) for a in casts],
        out_shape=([jax.ShapeDtypeStruct((m, n), dt) for n, dt in zip(out_widths, list(out_dtypes) + [F32])]
                   + [jax.ShapeDtypeStruct(a.shape, BF16) for a in casts]),
        compiler_params=pltpu.CompilerParams(dimension_semantics=("arbitrary",),
                                             vmem_limit_bytes=IN_PROJ_VMEM_LIMIT_BYTES),
        name="in_proj",
    )(x, g, w_t, w2, gate_b, *casts)


N_SAMPLE_IN = 2 + 2 * DIL_GROUPS
N_SAMPLE_OUT = 1 + 2 * DIL_GROUPS


def _gla_kernel(*refs, chunk, n_valid, n_new_sample):
    q_ref, k_ref, v_ref, r_ref, b_ref, ng_ref, s0_ref = refs[:7]
    refs = refs[7:]
    sample_in = ()
    if n_new_sample:
        sample_in, refs = refs[:N_SAMPLE_IN], refs[N_SAMPLE_IN:]
    o_ref, sout_ref = refs[:2]
    refs = refs[2:]
    sample_out = ()
    if n_new_sample:
        sample_out, refs = refs[:N_SAMPLE_OUT], refs[N_SAMPLE_OUT:]
    s_ref, oi_ref, kf_ref, vf_ref = refs

    c = pl.program_id(1)
    rows = q_ref.shape[0]

    @pl.when(c == 0)
    def _():
        s_ref[...] = s0_ref[...]

    for r in range(rows):
        _gla_chunk(r, q_ref, k_ref, v_ref, r_ref, b_ref, ng_ref, o_ref, s_ref, oi_ref, chunk, n_valid)
    if n_new_sample:
        _dil_sample_kernel(*sample_in, *sample_out, n_new=n_new_sample)

    for r in range(rows):
        @pl.when(jnp.min(b_ref[r, chunk - 1:chunk, :]) < -GLA_SAFE_LOG_DECAY)
        def _(r=r):
            _gla_chunk_exact_intra(r, q_ref, k_ref, v_ref, r_ref, b_ref, ng_ref, o_ref, oi_ref, kf_ref, vf_ref,
                                   chunk, n_valid)

    @pl.when(c == pl.num_programs(1) - 1)
    def _():
        sout_ref[...] = s_ref[...]


def _gla_epilogue(o, r, vs, r_ref, ng_ref, o_ref):
    gr = r_ref[r, :, vs].astype(F32)
    o_ref[r, :, vs] = (_rms(o, ng_ref[...]) * (gr * jax.nn.sigmoid(gr))).astype(o_ref.dtype)


def _gla_values(r, vs, v_ref, chunk, n_valid):
    vh = v_ref[r, :, vs]
    if n_valid < chunk:
        tok = lax.broadcasted_iota(jnp.int32, vh.shape, 0)
        vh = jnp.where(tok < n_valid, vh, jnp.zeros_like(vh))
    return vh


def _gla_chunk_exact_intra(r, q_ref, k_ref, v_ref, r_ref, b_ref, ng_ref, o_ref, oi_ref, kf_ref, vf_ref,
                           chunk, n_valid):
    tok = lax.broadcasted_iota(jnp.int32, (chunk, 1), 0)
    for h in range(GLA_HEADS):
        ks = slice(h * GLA_HDK, (h + 1) * GLA_HDK)
        vs = slice(h * GLA_HDV, (h + 1) * GLA_HDV)
        bh = b_ref[r, :, ks]
        qh = q_ref[r, :, ks].astype(F32) * (GLA_HDK ** -0.5)
        kf_ref[...] = k_ref[r, :, ks].astype(F32)
        vf_ref[...] = _gla_values(r, vs, v_ref, chunk, n_valid).astype(F32)

        def eight_keys(i, acc, bh=bh, qh=qh, ks=ks):
            rows8 = pl.ds(pl.multiple_of(i * SUBLANES, SUBLANES), SUBLANES)
            b8, k8, v8 = b_ref[r, rows8, ks], kf_ref[rows8, :], vf_ref[rows8, :]
            for j in range(SUBLANES):
                s = i * SUBLANES + j
                decay = jnp.exp(jnp.where(tok >= s, bh - b8[j:j + 1], MASK_VALUE))
                w = jnp.sum(qh * decay * k8[j:j + 1], axis=-1, keepdims=True)
                acc = acc + w * v8[j:j + 1]
            return acc

        intra = lax.fori_loop(0, chunk // SUBLANES, eight_keys, jnp.zeros((chunk, GLA_HDV), F32))
        _gla_epilogue(oi_ref[r, h] + intra, r, vs, r_ref, ng_ref, o_ref)


def _gla_chunk(r, q_ref, k_ref, v_ref, r_ref, b_ref, ng_ref, o_ref, s_ref, oi_ref, chunk, n_valid):
    row = lax.broadcasted_iota(jnp.int32, (chunk, chunk), 0)
    col = lax.broadcasted_iota(jnp.int32, (chunk, chunk), 1)
    causal = row >= col
    b = b_ref[r]

    for h in range(GLA_HEADS):
        ks = slice(h * GLA_HDK, (h + 1) * GLA_HDK)
        vs = slice(h * GLA_HDV, (h + 1) * GLA_HDV)
        bh = b[:, ks]
        qh = q_ref[r, :, ks].astype(F32) * (GLA_HDK ** -0.5)
        kh = k_ref[r, :, ks].astype(F32)
        vh = _gla_values(r, vs, v_ref, chunk, n_valid)
        qt = (qh * jnp.exp(bh)).astype(BF16)
        kt = (kh * jnp.exp(-bh)).astype(BF16)
        kd = (kh * jnp.exp(bh[chunk - 1:chunk, :] - bh)).astype(BF16)
        s_old = s_ref[r, h]
        scores = jnp.where(causal, _dot_nt(qt, kt), 0.0).astype(BF16)
        o_state = _dot(qt, s_old.astype(BF16))
        oi_ref[r, h] = o_state
        dec = jnp.exp(jnp.broadcast_to(bh[chunk - 1:chunk, :], (GLA_HDK, GLA_HDK)).T)
        s_ref[r, h] = s_old * jnp.concatenate([dec, dec], axis=1) + _dot_tn(kd, vh)
        _gla_epilogue(o_state + _dot(scores, vh), r, vs, r_ref, ng_ref, o_ref)


def _gla(z, b_cum, norm_g, s0, chunk, n_valid, rows, sample=None):
    bsz, t, _ = z.shape
    n_chunks = t // chunk
    state = pl.BlockSpec((rows, GLA_HEADS, GLA_HDK, GLA_HDV), lambda b, c: (b, 0, 0, 0))
    in_specs = [pl.BlockSpec((rows, chunk, GLA_DK), lambda b, c: (b, c, 0)),
                pl.BlockSpec((rows, chunk, GLA_DK), lambda b, c: (b, c, 1)),
                pl.BlockSpec((rows, chunk, GLA_DV), lambda b, c: (b, c, 1)),
                pl.BlockSpec((rows, chunk, GLA_DV), lambda b, c: (b, c, 2)),
                pl.BlockSpec((rows, chunk, GLA_DK), lambda b, c: (b, c, 0)),
                pl.BlockSpec((1, GLA_HDV), lambda b, c: (0, 0)),
                state]
    out_specs = [pl.BlockSpec((rows, chunk, GLA_DV), lambda b, c: (b, c, 0)), state]
    out_shape = [jax.ShapeDtypeStruct((bsz, t, GLA_DV), BF16),
                 jax.ShapeDtypeStruct((bsz, GLA_HEADS, GLA_HDK, GLA_HDV), F32)]
    operands = [z, z, z, z, b_cum, norm_g, s0]
    n_new = 0
    if sample is not None:
        dq, kvn, bufs, n_new = sample
        assert dq.shape[0] == (bsz // rows) * n_chunks
        per_step = lambda a: pl.BlockSpec((1,) + a.shape[1:], lambda b, c: (b * n_chunks + c, 0, 0))
        in_specs += [per_step(a) for a in (dq, kvn, *bufs)]
        operands += [dq, kvn, *bufs]
        o_s = jax.ShapeDtypeStruct((dq.shape[0], dq.shape[1], DIL_OUT), BF16)
        out_specs += [per_step(o_s)] + [per_step(a) for a in bufs]
        out_shape += [o_s] + [jax.ShapeDtypeStruct(a.shape, a.dtype) for a in bufs]
    return pl.pallas_call(
        functools.partial(_gla_kernel, chunk=chunk, n_valid=n_valid, n_new_sample=n_new),
        grid=(bsz // rows, n_chunks),
        in_specs=in_specs,
        out_specs=out_specs,
        out_shape=out_shape,
        scratch_shapes=[pltpu.VMEM((rows, GLA_HEADS, GLA_HDK, GLA_HDV), F32),
                        pltpu.VMEM((rows, GLA_HEADS, chunk, GLA_HDV), F32),
                        pltpu.VMEM((chunk, GLA_HDK), F32),
                        pltpu.VMEM((chunk, GLA_HDV), F32)],
        compiler_params=pltpu.CompilerParams(dimension_semantics=("parallel", "arbitrary"),
                                             vmem_limit_bytes=GLA_VMEM_LIMIT_BYTES),
        name="gla",
    )(*operands)


def _alibi_slope(g, head):
    n = DIL_GROUPS * DIL_HEADS
    return jnp.exp((-ALIBI_MAX * _LN2 / n) * (head + (g * DIL_HEADS + 1.0)))


def _pair_slopes(g, hp, rows):
    head = 2.0 * hp.astype(F32) + jnp.where(rows >= Q_TILE, 1.0, 0.0)
    return _alibi_slope(g, head)


def _dil_prompt_kernel(q0_ref, q1_ref, q2_ref, k0_ref, k1_ref, k2_ref, v0_ref, v1_ref, v2_ref,
                       o_ref, wk0_ref, wv0_ref, wk1_ref, wv1_ref, wk2_ref, wv2_ref,
                       qf_ref, og_ref, ld_ref, bf_ref, br_ref, p_ref, m_ref, *, seq):
    hp = pl.program_id(1)
    q_refs = (q0_ref, q1_ref, q2_ref)
    k_refs = (k0_ref, k1_ref, k2_ref)
    v_refs = (v0_ref, v1_ref, v2_ref)
    wk_refs = (wk0_ref, wk1_ref, wk2_ref)
    wv_refs = (wv0_ref, wv1_ref, wv2_ref)

    for g in range(DIL_GROUPS):
        qf_ref[g] = q_refs[g][...].astype(F32) * (DIL_HD ** -0.5)

    for g in range(DIL_GROUPS):
        rate = float(DIL_RATES[g])
        rows = lax.broadcasted_iota(jnp.int32, (2 * Q_TILE, Q_TILE), 0)
        cols = lax.broadcasted_iota(jnp.int32, (2 * Q_TILE, Q_TILE), 1)
        dist = (rows & (Q_TILE - 1)) - cols
        slope = _pair_slopes(g, hp, rows)
        bf_ref[g] = jnp.where(dist >= 0, -slope * rate * dist.astype(F32), MASK_VALUE)
        if g < 2:
            rows = lax.broadcasted_iota(jnp.int32, (2 * Q_TILE, 2 * Q_TILE), 0)
            cols = lax.broadcasted_iota(jnp.int32, (2 * Q_TILE, 2 * Q_TILE), 1)
            dist = (rows & (Q_TILE - 1)) + Q_TILE - cols
            slope = _pair_slopes(g, hp, rows)
            ok = jnp.abs(dist - Q_TILE // 2) <= Q_TILE // 2
            br_ref[g] = jnp.where(ok, -slope * rate * dist.astype(F32), MASK_VALUE)

    lane = lax.broadcasted_iota(jnp.int32, (Q_TILE, LANES), 1)
    lo = lane < DIL_HD

    def rows_of(start, size, rate):
        return pl.ds(start, size) if rate == 1 else pl.ds(start, size, stride=rate)

    def scores_stage(g, q_row, k_row, n_keys, bias, slot):
        rate = DIL_RATES[g]
        q2 = qf_ref[g, rows_of(q_row, Q_TILE, rate), :]
        k2 = k_refs[g][rows_of(k_row, n_keys, rate), :].astype(BF16)
        qs = jnp.concatenate([jnp.where(lo, q2, 0.0), jnp.where(lo, 0.0, q2)], axis=0).astype(BF16)
        s = _dot_nt(qs, k2) + bias()
        m = jnp.max(s, axis=-1, keepdims=True)
        p_ref[slot, :, 0:n_keys] = jnp.exp(s - m).astype(BF16)
        m_ref[slot] = jnp.where(lo, m[:Q_TILE], m[Q_TILE:])

    def values_stage(g, q_row, k_row, n_keys, slot):
        rate = DIL_RATES[g]
        v2 = v_refs[g][rows_of(k_row, n_keys, rate), :].astype(BF16)
        vo = jnp.concatenate([v2, jnp.ones((n_keys, LANES), BF16)], axis=1)
        r = _dot(p_ref[slot, :, 0:n_keys], vo)
        o2 = jnp.where(lo, r[:Q_TILE, :LANES], r[Q_TILE:, :LANES])
        l2 = jnp.where(lo, r[:Q_TILE, LANES:], r[Q_TILE:, LANES:])
        og_ref[g, rows_of(q_row, Q_TILE, rate), :] = o2 / l2
        ld_ref[g, rows_of(q_row, Q_TILE, rate), :] = m_ref[slot] + jnp.log(l2)

    pending = []
    emitted = [0]

    def run_tile_sets(n_sets, n_per_set, tile_of):
        base = emitted[0]

        def slot(i, u):
            return ((i + base) % 2) * TILE_SLOTS + u

        def scores(i):
            for u in range(n_per_set):
                g, q_row, k_row, n_keys, bias = tile_of(i, u)
                scores_stage(g, q_row, k_row, n_keys, bias, slot(i, u))

        def values(i):
            for u in range(n_per_set):
                g, q_row, k_row, n_keys, _ = tile_of(i, u)
                values_stage(g, q_row, k_row, n_keys, slot(i, u))

        if pending:
            pending.pop()()
        scores(0)
        if n_sets > 1:
            def body(i, carry):
                values(i - 1)
                scores(i)
                return carry
            lax.fori_loop(1, n_sets, body, 0)
        pending.append(lambda: values(n_sets - 1))
        emitted[0] += n_sets

    for g in range(DIL_GROUPS):
        rate = DIL_RATES[g]
        n_tiles = seq // rate // Q_TILE
        span = rate * Q_TILE
        group = min(rate, RESIDUES_PER_BODY)
        n_groups = rate // group

        def first_tile(i, u, g=g, group=group):
            rho = i * group + u
            return g, rho, rho, Q_TILE, lambda: bf_ref[g]

        def consecutive_tile(i, u, g=g, span=span):
            n = 1 + i * TILES_PER_BODY + u
            return g, n * span, (n - 1) * span, 2 * Q_TILE, lambda: br_ref[g]

        def later_tile(i, u, g=g, group=group, n_groups=n_groups, span=span):
            n = 1 + i // n_groups
            rho = (i % n_groups) * group + u
            return g, rho + n * span, rho + (n - 1) * span, 2 * Q_TILE, lambda: br_ref[g]

        run_tile_sets(n_groups, group, first_tile)
        if n_tiles > 1:
            if rate == 1:
                assert (n_tiles - 1) % TILES_PER_BODY == 0
                run_tile_sets((n_tiles - 1) // TILES_PER_BODY, TILES_PER_BODY, consecutive_tile)
            else:
                run_tile_sets((n_tiles - 1) * n_groups, group, later_tile)
    pending.pop()()

    def combine(i, carry):
        rs = pl.ds(pl.multiple_of(i * Q_TILE, Q_TILE), Q_TILE)
        ld = [ld_ref[g, rs, :] for g in range(DIL_GROUPS)]
        top = jnp.maximum(jnp.maximum(ld[0], ld[1]), ld[2])
        w = [jnp.exp(x - top) for x in ld]
        num = sum(w[g] * og_ref[g, rs, :] for g in range(DIL_GROUPS))
        o_ref[rs, :] = (num / (w[0] + w[1] + w[2])).astype(o_ref.dtype)
        return carry

    lax.fori_loop(0, seq // Q_TILE, combine, 0)

    for g in range(DIL_GROUPS):
        keep = wk_refs[g].shape[1]
        for src, dst in ((k_refs[g], wk_refs[g]), (v_refs[g], wv_refs[g])):
            for off in range(0, keep, LANES):
                dst[:, off:off + LANES] = src[seq - keep + off:seq - keep + off + LANES, :].T


def _dil_prompt(dq, kv):
    bsz, seq, _ = dq.shape
    pairs = DIL_HEADS * DIL_HD // LANES
    nblk = DIL_WIDTH // LANES
    keeps = [min(w, seq) for w in DIL_WINDOWS]

    def spec(col0):
        return pl.BlockSpec((None, seq, LANES), lambda b, hp, col0=col0: (b, 0, col0 + hp))

    in_specs = ([spec(g * pairs) for g in range(DIL_GROUPS)]
                + [spec(g * pairs) for g in range(DIL_GROUPS)]
                + [spec(nblk + g * pairs) for g in range(DIL_GROUPS)])
    return pl.pallas_call(
        functools.partial(_dil_prompt_kernel, seq=seq),
        grid=(bsz, pairs),
        in_specs=in_specs,
        out_specs=[pl.BlockSpec((None, seq, LANES), lambda b, hp: (b, 0, hp))]
                  + [pl.BlockSpec((None, LANES, w), lambda b, hp: (b, hp, 0)) for w in keeps for _ in range(2)],
        out_shape=[jax.ShapeDtypeStruct((bsz, seq, DIL_OUT), BF16)]
                  + [jax.ShapeDtypeStruct((bsz, DIL_HEADS * DIL_HD, w), F32) for w in keeps for _ in range(2)],
        scratch_shapes=[pltpu.VMEM((DIL_GROUPS, seq, LANES), F32),
                        pltpu.VMEM((DIL_GROUPS, seq, LANES), F32),
                        pltpu.VMEM((DIL_GROUPS, seq, LANES), F32),
                        pltpu.VMEM((DIL_GROUPS, 2 * Q_TILE, Q_TILE), F32),
                        pltpu.VMEM((2, 2 * Q_TILE, 2 * Q_TILE), F32),
                        pltpu.VMEM((2 * TILE_SLOTS, 2 * Q_TILE, 2 * Q_TILE), BF16),
                        pltpu.VMEM((2 * TILE_SLOTS, Q_TILE, LANES), F32)],
        compiler_params=_params("parallel", "arbitrary"),
        name="dil_prompt",
    )(dq, dq, dq, kv, kv, kv, kv, kv, kv)


def _dil_sample_kernel(q_ref, kvn_ref, kb0_ref, vb0_ref, kb1_ref, vb1_ref, kb2_ref, vb2_ref,
                       o_ref, nk0_ref, nv0_ref, nk1_ref, nv1_ref, nk2_ref, nv2_ref, *, n_new):
    kb_refs = (kb0_ref, kb1_ref, kb2_ref)
    vb_refs = (vb0_ref, vb1_ref, vb2_ref)
    nk_refs = (nk0_ref, nk1_ref, nk2_ref)
    nv_refs = (nv0_ref, nv1_ref, nv2_ref)
    pad = q_ref.shape[1]
    nrow = DIL_HEADS * pad
    width = DIL_HEADS * DIL_HD

    lane_head = jnp.right_shift(lax.broadcasted_iota(jnp.int32, (pad, width), 1), DIL_HD.bit_length() - 1)

    def bias_for(g, dist):
        rate = DIL_RATES[g]
        rows = lax.broadcasted_iota(jnp.int32, dist.shape, 0)
        head = sum(jnp.where(rows >= h * pad, 1.0, 0.0) for h in range(1, DIL_HEADS))
        ok = (dist >= 0) & (dist <= DIL_WINDOWS[g]) & ((dist & (rate - 1)) == 0)
        return jnp.where(ok, -_alibi_slope(g, head) * dist.astype(F32), MASK_VALUE)

    sel_r = lax.broadcasted_iota(jnp.int32, (pad, LANES), 0)
    sel_c = lax.broadcasted_iota(jnp.int32, (pad, LANES), 1)
    place = jnp.where((sel_r < n_new) & (sel_c == sel_r + (LANES - n_new)), 1.0, 0.0).astype(BF16)
    tail_lane = lax.broadcasted_iota(jnp.int32, (width, LANES), 1) >= LANES - n_new

    def shifted(buf_t, new_rows):
        length = buf_t.shape[1]
        rolled = pltpu.roll(buf_t, length - n_new, 1)
        new_t = sum(_dot_tn(part, place) for part in _split3(new_rows))
        last = jnp.where(tail_lane, new_t, rolled[:, length - LANES:])
        return rolled, last

    scores, values, transposed = [], [], []
    for g in range(DIL_GROUPS):
        length = kb_refs[g].shape[2]
        cs = slice(g * width, (g + 1) * width)
        vcs = slice(DIL_WIDTH + g * width, DIL_WIDTH + (g + 1) * width)
        qg = q_ref[0, :, cs].astype(F32) * (DIL_HD ** -0.5)
        qs = jnp.concatenate([jnp.where(lane_head == h, qg, 0.0) for h in range(DIL_HEADS)], axis=0).astype(BF16)
        kb, vb = kb_refs[g][0], vb_refs[g][0]
        kn, vn = kvn_ref[0, :, cs], kvn_ref[0, :, vcs]
        rows = lax.broadcasted_iota(jnp.int32, (nrow, length), 0)
        cols = lax.broadcasted_iota(jnp.int32, (nrow, length), 1)
        dist = length + (rows & (pad - 1)) - cols
        scores.append(_dot(qs, kb.astype(BF16)) + bias_for(g, dist))
        values.append(vb.astype(BF16))
        transposed.append(True)
        rows = lax.broadcasted_iota(jnp.int32, (nrow, pad), 0)
        cols = lax.broadcasted_iota(jnp.int32, (nrow, pad), 1)
        dist = jnp.where(cols < n_new, (rows & (pad - 1)) - cols, -1)
        scores.append(_dot_nt(qs, kn.astype(BF16)) + bias_for(g, dist))
        values.append(vn.astype(BF16))
        transposed.append(False)
        for buf, new, out_ref in ((kb, kn, nk_refs[g]), (vb, vn, nv_refs[g])):
            rolled, last = shifted(buf, new)
            if length > LANES:
                out_ref[0, :, 0:length - LANES] = rolled[:, 0:length - LANES]
            out_ref[0, :, length - LANES:length] = last

    top = functools.reduce(jnp.maximum, [jnp.max(s, axis=-1, keepdims=True) for s in scores])
    probs = [jnp.exp(s - top) for s in scores]
    den = sum(jnp.sum(p, axis=-1, keepdims=True) for p in probs)
    acc = sum((_dot_nt if t else _dot)(p.astype(BF16), v)
              for p, v, t in zip(probs, values, transposed)) / den
    out = sum(jnp.where(lane_head == h, acc[h * pad:(h + 1) * pad], 0.0) for h in range(DIL_HEADS))
    o_ref[0] = out.astype(o_ref.dtype)


FFN_COL_CHUNK = D_FF // 2


def _merge_ffn_kernel(x_ref, oa_ref, ob_ref, ga_ref, gb_ref, pa_ref, pb_ref, wo_ref, n2_ref, nf_ref,
                      wg_ref, wu_ref, wd_ref, y_ref):
    pa = _dot(oa_ref[...], pa_ref[...])
    pb = _dot(ob_ref[...], pb_ref[...])
    merged = jax.nn.sigmoid(ga_ref[...].astype(F32)) * pa + jax.nn.sigmoid(gb_ref[...].astype(F32)) * pb
    x1 = x_ref[...] + _dot(merged.astype(BF16), wo_ref[...])
    h = _rms(x1, n2_ref[...]).astype(BF16)
    acc = x1
    for c0 in range(0, D_FF, FFN_COL_CHUNK):
        cs = slice(c0, c0 + FFN_COL_CHUNK)
        gate = _dot(h, wg_ref[:, cs])
        up = _dot(h, wu_ref[:, cs])
        act = (gate * jax.nn.sigmoid(gate) * up).astype(BF16)
        acc = acc + _dot(act, wd_ref[cs, :])
    y_ref[...] = _rms(acc, nf_ref[...])


def _merge_ffn(x, oa, ob, gates, pa, pb, wo, n2, nf, wg, wu, wd, tm):
    m = x.shape[0]
    row = lambda w: pl.BlockSpec((tm, w), lambda i: (i, 0))
    resident = lambda a: pl.BlockSpec(a.shape, lambda i: (0, 0), pipeline_mode=pl.Buffered(1))
    return pl.pallas_call(
        _merge_ffn_kernel,
        grid=(m // tm,),
        in_specs=[row(D_MODEL), row(GLA_DV), row(DIL_OUT), row(D_MODEL),
                  pl.BlockSpec((tm, D_MODEL), lambda i: (i, 1))]
                 + [resident(a) for a in (pa, pb, wo, n2, nf, wg, wu, wd)],
        out_specs=row(D_MODEL),
        out_shape=jax.ShapeDtypeStruct((m, D_MODEL), F32),
        compiler_params=pltpu.CompilerParams(dimension_semantics=("parallel",),
                                             vmem_limit_bytes=MERGE_FFN_VMEM_LIMIT_BYTES),
        name="merge_ffn",
    )(x, oa, ob, gates, gates, pa, pb, wo, n2, nf, wg, wu, wd)


def _prep_weights(w_in, gla_gate_w2):
    w2 = jnp.concatenate([gla_gate_w2[0], jnp.zeros((LANES - GLA_GATE_RANK, GLA_DK), F32)], axis=0)
    return dict(w_in_t=jnp.transpose(w_in[0]).astype(BF16), w2=w2.astype(BF16))


def _project(x2d, norm1_g, gate_b, wts, tm, chunk, n_valid, casts=()):
    groups = [(0, GLA_COLS), (_OFF["dq"][0], DIL_WIDTH), (_OFF["dk"][0], 2 * DIL_WIDTH), (_OFF["ga"][0], 2 * D_MODEL)]
    return _in_proj(x2d, norm1_g, wts["w_in_t"], wts["w2"], gate_b, groups, _OFF["glr"][0],
                    [BF16, BF16, F32, BF16], tm, chunk, n_valid, casts)


def _tail(x2d, o_a, o_b, z_gates, wts, norm2_g, norm_f_g, tm):
    return _merge_ffn(x2d, o_a, o_b, z_gates, wts["pa"], wts["pb"], wts["wo"], norm2_g,
                      norm_f_g.reshape(1, D_MODEL), wts["wg"], wts["wu"], wts["wd"], tm)


def kernel(x_prompt, x_sample, state_gla, state_win0_k, state_win0_v, state_win1_k, state_win1_v,
           state_win2_k, state_win2_v, norm1_g, w_in, gla_gate_w2, gla_gate_b, gla_norm_g,
           proj_a, proj_b, w_out, norm2_g, w_ffn_gate, w_ffn_up, w_ffn_down, norm_f_g):
    wts = _prep_weights(w_in, gla_gate_w2)
    bp, seq, _ = x_prompt.shape
    bs, n_new, _ = x_sample.shape
    width = DIL_HEADS * DIL_HD

    xp = x_prompt.reshape(bp * seq, D_MODEL)
    later = dict(pa=proj_a[0], pb=proj_b[0], wo=w_out[0], wg=w_ffn_gate[0], wu=w_ffn_up[0], wd=w_ffn_down[0])
    z_gla, z_dq, z_kv, z_gates, b_cum, *cast = _project(xp, norm1_g, gla_gate_b, wts, ROW_BLOCK, GLA_CHUNK,
                                                        GLA_CHUNK, tuple(later.values()))
    wts.update(zip(later.keys(), cast))
    xs = jnp.pad(x_sample, ((0, 0), (0, SAMPLE_PAD - n_new), (0, 0))).reshape(bs * SAMPLE_PAD, D_MODEL)
    zs_gla, zs_dq, zs_kv, zs_gates, bs_cum = _project(xs, norm1_g, gla_gate_b, wts, bs * SAMPLE_PAD,
                                                      SAMPLE_PAD, n_new)

    bufs = [jnp.transpose(a[0], (0, 2, 3, 1)).reshape(bs, width, a.shape[2]) for a in
            (state_win0_k, state_win0_v, state_win1_k, state_win1_v, state_win2_k, state_win2_v)]
    sample = (zs_dq.reshape(bs, SAMPLE_PAD, DIL_WIDTH), zs_kv.reshape(bs, SAMPLE_PAD, 2 * DIL_WIDTH), bufs, n_new)
    s0 = jnp.zeros((bp, GLA_HEADS, GLA_HDK, GLA_HDV), F32)
    o_a, gla_p, os_b, *win_s = _gla(z_gla.reshape(bp, seq, GLA_COLS), b_cum.reshape(bp, seq, GLA_DK), gla_norm_g,
                                    s0, GLA_CHUNK, GLA_CHUNK, GLA_ROWS_PROMPT, sample)

    o_b, *win_p = _dil_prompt(z_dq.reshape(bp, seq, DIL_WIDTH), z_kv.reshape(bp, seq, 2 * DIL_WIDTH))
    y_prompt = _tail(xp, o_a.reshape(bp * seq, GLA_DV), o_b.reshape(bp * seq, DIL_OUT), z_gates, wts,
                     norm2_g, norm_f_g, ROW_BLOCK).reshape(bp, seq, D_MODEL)
    win_p = [jnp.transpose(a.reshape(bp, DIL_HEADS, DIL_HD, a.shape[2]), (0, 3, 1, 2))[None] for a in win_p]

    os_a, gla_s = _gla(zs_gla.reshape(bs, SAMPLE_PAD, GLA_COLS), bs_cum.reshape(bs, SAMPLE_PAD, GLA_DK),
                       gla_norm_g, state_gla[0], SAMPLE_PAD, n_new, GLA_ROWS_SAMPLE)
    ys = _tail(xs, os_a.reshape(bs * SAMPLE_PAD, GLA_DV), os_b.reshape(bs * SAMPLE_PAD, DIL_OUT), zs_gates, wts,
               norm2_g, norm_f_g, bs * SAMPLE_PAD)
    y_sample = ys.reshape(bs, SAMPLE_PAD, D_MODEL)[:, :n_new]
    win_s = [jnp.transpose(a.reshape(bs, DIL_HEADS, DIL_HD, a.shape[2]), (0, 3, 1, 2))[None] for a in win_s]

    return (y_prompt, y_sample, gla_p[None], *win_p, gla_s[None], *win_s)
```

```python
import functools

import jax
import jax.numpy as jnp
from jax import lax
from jax.experimental import pallas as pl
from jax.experimental.pallas import tpu as pltpu

F32 = jnp.float32
BF16 = jnp.bfloat16

D_MODEL = 1024
GLA_HEADS = 4
GLA_DK = 512
GLA_DV = 1024
GLA_HDK = 128
GLA_HDV = 256
GLA_GATE_RANK = 16
GLA_TAU = 16.0
DIL_WINDOWS = (128, 512, 2048)
DIL_RATES = (1, 4, 16)
DIL_GROUPS = 3
DIL_HEADS = 4
DIL_HD = 64
DIL_WIDTH = 768
DIL_OUT = 256
ALIBI_MAX = 8.0
D_FF = 2816
RMS_EPS = 1e-6

LANES = 128
SUBLANES = 8
BF16_SUBLANES = 16
Q_TILE = 128
TILES_PER_BODY = 15
RESIDUES_PER_BODY = 8
TILE_SLOTS = max(TILES_PER_BODY, RESIDUES_PER_BODY)
GLA_CHUNK = 128
GLA_ROWS_PROMPT = 4
GLA_ROWS_SAMPLE = 4
SAMPLE_PAD = 8
MASK_VALUE = -1e30
VMEM_LIMIT_BYTES = 48 * 1024 * 1024
IN_PROJ_VMEM_LIMIT_BYTES = 56 * 1024 * 1024
MERGE_FFN_VMEM_LIMIT_BYTES = 60 * 1024 * 1024
GLA_VMEM_LIMIT_BYTES = 56 * 1024 * 1024
_LN2 = 0.6931471805599453

_OFF = {}
_o = 0
for _name, _w in (("gq", GLA_DK), ("gk", GLA_DK), ("gv", GLA_DV), ("gr", GLA_DV), ("glr", GLA_GATE_RANK),
                  ("dq", DIL_WIDTH), ("dk", DIL_WIDTH), ("dv", DIL_WIDTH), ("ga", D_MODEL), ("gb", D_MODEL)):
    _OFF[_name] = (_o, _o + _w)
    _o += _w
GLA_COLS = 2 * GLA_DK + 2 * GLA_DV
GLA_SAFE_LOG_DECAY = 20.0
ROW_BLOCK = 512


def _params(*sem):
    return pltpu.CompilerParams(dimension_semantics=sem, vmem_limit_bytes=VMEM_LIMIT_BYTES)


def _dot(a, b):
    return jnp.dot(a, b, preferred_element_type=F32)


def _dot_nt(a, b):
    return lax.dot_general(a, b, (((1,), (1,)), ((), ())), preferred_element_type=F32)


def _dot_tn(a, b):
    return lax.dot_general(a, b, (((0,), (0,)), ((), ())), preferred_element_type=F32)


def _rms(x, g):
    return x * lax.rsqrt(jnp.mean(x * x, axis=-1, keepdims=True) + RMS_EPS) * g


IN_PROJ_COL_CHUNK = 1024


def _split3(x):
    x1 = x.astype(BF16)
    r1 = x - x1.astype(F32)
    x2 = r1.astype(BF16)
    x3 = (r1 - x2.astype(F32)).astype(BF16)
    return x1, x2, x3


def _in_proj_kernel(x_ref, g_ref, wt_ref, w2_ref, gb_ref, *refs, groups, low_rank_col, chunk, n_valid, n_casts):
    cast_in, out_refs = refs[:n_casts], refs[n_casts:]
    for src, dst in zip(cast_in, out_refs[len(out_refs) - n_casts:]):
        dst[...] = src[...].astype(dst.dtype)

    h = _rms(x_ref[...], g_ref[...]).astype(BF16)
    b_ref = out_refs[len(groups)]
    tm = x_ref.shape[0]

    for o_ref, (c_first, n) in zip(out_refs, groups):
        for c0 in range(0, n, IN_PROJ_COL_CHUNK):
            c1 = min(c0 + IN_PROJ_COL_CHUNK, n)
            o_ref[:, c0:c1] = _dot_nt(h, wt_ref[c_first + c0:c_first + c1, :]).astype(o_ref.dtype)

    low_rank = _dot_nt(h, wt_ref[low_rank_col:low_rank_col + LANES, :]).astype(BF16)
    gate = _dot(low_rank, w2_ref[...]) + gb_ref[...]
    log_a = (jnp.minimum(gate, 0.0) - jnp.log(1.0 + jnp.exp(-jnp.abs(gate)))) * (1.0 / GLA_TAU)
    if n_valid < chunk:
        tok = lax.broadcasted_iota(jnp.int32, log_a.shape, 0) & (chunk - 1)
        log_a = jnp.where(tok < n_valid, log_a, 0.0)
    span = min(tm, LANES)
    row = lax.broadcasted_iota(jnp.int32, (span, span), 0)
    col = lax.broadcasted_iota(jnp.int32, (span, span), 1)
    same_chunk = (row & -chunk) == (col & -chunk) if chunk < span else True
    tril = jnp.where((row >= col) & same_chunk, 1.0, 0.0).astype(BF16)
    for r0 in range(0, tm, span):
        parts = _split3(log_a[r0:r0 + span])[:2]
        b_ref[r0:r0 + span, :] = sum(_dot(tril, p) for p in parts)


def _cast_blocks(a, steps):
    count = max(c for c in range(1, steps + 1) if a.shape[0] % (c * BF16_SUBLANES) == 0)
    return a.shape[0] // count, count


def _in_proj(x, g, w_t, w2, gate_b, groups, low_rank_col, out_dtypes, tm, chunk, n_valid, casts=()):
    m, k = x.shape
    steps = m // tm
    assert chunk & (chunk - 1) == 0 and (chunk % LANES == 0 or LANES % chunk == 0) and tm % chunk == 0
    assert all(c % BF16_SUBLANES == 0 and n % LANES == 0 for c, n in groups) and low_rank_col % BF16_SUBLANES == 0
    resident = lambda a: pl.BlockSpec(a.shape, lambda i: (0, 0), pipeline_mode=pl.Buffered(1))
    out_widths = [n for _, n in groups] + [GLA_DK]

    def cast_spec(a):
        rows, count = _cast_blocks(a, steps)
        return pl.BlockSpec((rows, a.shape[1]), lambda i, count=count: (jnp.minimum(i, count - 1), 0))

    return pl.pallas_call(
        functools.partial(_in_proj_kernel, groups=tuple(groups), low_rank_col=low_rank_col,
                          chunk=chunk, n_valid=n_valid, n_casts=len(casts)),
        grid=(steps,),
        in_specs=([pl.BlockSpec((tm, k), lambda i: (i, 0))] + [resident(a) for a in (g, w_t, w2, gate_b)]
                  + [cast_spec(a) for a in casts]),
        out_specs=[pl.BlockSpec((tm, n), lambda i: (i, 0)) for n in out_widths] + [cast_spec(a) for a in casts],
        out_shape=([jax.ShapeDtypeStruct((m, n), dt) for n, dt in zip(out_widths, list(out_dtypes) + [F32])]
                   + [jax.ShapeDtypeStruct(a.shape, BF16) for a in casts]),
        compiler_params=pltpu.CompilerParams(dimension_semantics=("arbitrary",),
                                             vmem_limit_bytes=IN_PROJ_VMEM_LIMIT_BYTES),
        name="in_proj",
    )(x, g, w_t, w2, gate_b, *casts)


N_SAMPLE_IN = 2 + 2 * DIL_GROUPS
N_SAMPLE_OUT = 1 + 2 * DIL_GROUPS


def _gla_kernel(*refs, chunk, n_valid, n_new_sample):
    q_ref, k_ref, v_ref, r_ref, b_ref, ng_ref, s0_ref = refs[:7]
    refs = refs[7:]
    sample_in = ()
    if n_new_sample:
        sample_in, refs = refs[:N_SAMPLE_IN], refs[N_SAMPLE_IN:]
    o_ref, sout_ref = refs[:2]
    refs = refs[2:]
    sample_out = ()
    if n_new_sample:
        sample_out, refs = refs[:N_SAMPLE_OUT], refs[N_SAMPLE_OUT:]
    s_ref, oi_ref, kf_ref, vf_ref = refs

    c = pl.program_id(1)
    rows = q_ref.shape[0]

    @pl.when(c == 0)
    def _():
        s_ref[...] = s0_ref[...]

    for r in range(rows):
        _gla_chunk(r, q_ref, k_ref, v_ref, r_ref, b_ref, ng_ref, o_ref, s_ref, oi_ref, chunk, n_valid)
    if n_new_sample:
        _dil_sample_kernel(*sample_in, *sample_out, n_new=n_new_sample)

    for r in range(rows):
        @pl.when(jnp.min(b_ref[r, chunk - 1:chunk, :]) < -GLA_SAFE_LOG_DECAY)
        def _(r=r):
            _gla_chunk_exact_intra(r, q_ref, k_ref, v_ref, r_ref, b_ref, ng_ref, o_ref, oi_ref, kf_ref, vf_ref,
                                   chunk, n_valid)

    @pl.when(c == pl.num_programs(1) - 1)
    def _():
        sout_ref[...] = s_ref[...]


def _gla_epilogue(o, r, vs, r_ref, ng_ref, o_ref):
    gr = r_ref[r, :, vs].astype(F32)
    o_ref[r, :, vs] = (_rms(o, ng_ref[...]) * (gr * jax.nn.sigmoid(gr))).astype(o_ref.dtype)


def _gla_values(r, vs, v_ref, chunk, n_valid):
    vh = v_ref[r, :, vs]
    if n_valid < chunk:
        tok = lax.broadcasted_iota(jnp.int32, vh.shape, 0)
        vh = jnp.where(tok < n_valid, vh, jnp.zeros_like(vh))
    return vh


def _gla_chunk_exact_intra(r, q_ref, k_ref, v_ref, r_ref, b_ref, ng_ref, o_ref, oi_ref, kf_ref, vf_ref,
                           chunk, n_valid):
    tok = lax.broadcasted_iota(jnp.int32, (chunk, 1), 0)
    for h in range(GLA_HEADS):
        ks = slice(h * GLA_HDK, (h + 1) * GLA_HDK)
        vs = slice(h * GLA_HDV, (h + 1) * GLA_HDV)
        bh = b_ref[r, :, ks]
        qh = q_ref[r, :, ks].astype(F32) * (GLA_HDK ** -0.5)
        kf_ref[...] = k_ref[r, :, ks].astype(F32)
        vf_ref[...] = _gla_values(r, vs, v_ref, chunk, n_valid).astype(F32)

        def eight_keys(i, acc, bh=bh, qh=qh, ks=ks):
            rows8 = pl.ds(pl.multiple_of(i * SUBLANES, SUBLANES), SUBLANES)
            b8, k8, v8 = b_ref[r, rows8, ks], kf_ref[rows8, :], vf_ref[rows8, :]
            for j in range(SUBLANES):
                s = i * SUBLANES + j
                decay = jnp.exp(jnp.where(tok >= s, bh - b8[j:j + 1], MASK_VALUE))
                w = jnp.sum(qh * decay * k8[j:j + 1], axis=-1, keepdims=True)
                acc = acc + w * v8[j:j + 1]
            return acc

        intra = lax.fori_loop(0, chunk // SUBLANES, eight_keys, jnp.zeros((chunk, GLA_HDV), F32))
        _gla_epilogue(oi_ref[r, h] + intra, r, vs, r_ref, ng_ref, o_ref)


def _gla_chunk(r, q_ref, k_ref, v_ref, r_ref, b_ref, ng_ref, o_ref, s_ref, oi_ref, chunk, n_valid):
    row = lax.broadcasted_iota(jnp.int32, (chunk, chunk), 0)
    col = lax.broadcasted_iota(jnp.int32, (chunk, chunk), 1)
    causal = row >= col
    b = b_ref[r]

    for h in range(GLA_HEADS):
        ks = slice(h * GLA_HDK, (h + 1) * GLA_HDK)
        vs = slice(h * GLA_HDV, (h + 1) * GLA_HDV)
        bh = b[:, ks]
        qh = q_ref[r, :, ks].astype(F32) * (GLA_HDK ** -0.5)
        kh = k_ref[r, :, ks].astype(F32)
        vh = _gla_values(r, vs, v_ref, chunk, n_valid)
        qt = (qh * jnp.exp(bh)).astype(BF16)
        kt = (kh * jnp.exp(-bh)).astype(BF16)
        kd = (kh * jnp.exp(bh[chunk - 1:chunk, :] - bh)).astype(BF16)
        s_old = s_ref[r, h]
        scores = jnp.where(causal, _dot_nt(qt, kt), 0.0).astype(BF16)
        o_state = _dot(qt, s_old.astype(BF16))
        oi_ref[r, h] = o_state
        dec = jnp.exp(jnp.broadcast_to(bh[chunk - 1:chunk, :], (GLA_HDK, GLA_HDK)).T)
        s_ref[r, h] = s_old * jnp.concatenate([dec, dec], axis=1) + _dot_tn(kd, vh)
        _gla_epilogue(o_state + _dot(scores, vh), r, vs, r_ref, ng_ref, o_ref)


def _gla(z, b_cum, norm_g, s0, chunk, n_valid, rows, sample=None):
    bsz, t, _ = z.shape
    n_chunks = t // chunk
    state = pl.BlockSpec((rows, GLA_HEADS, GLA_HDK, GLA_HDV), lambda b, c: (b, 0, 0, 0))
    in_specs = [pl.BlockSpec((rows, chunk, GLA_DK), lambda b, c: (b, c, 0)),
                pl.BlockSpec((rows, chunk, GLA_DK), lambda b, c: (b, c, 1)),
                pl.BlockSpec((rows, chunk, GLA_DV), lambda b, c: (b, c, 1)),
                pl.BlockSpec((rows, chunk, GLA_DV), lambda b, c: (b, c, 2)),
                pl.BlockSpec((rows, chunk, GLA_DK), lambda b, c: (b, c, 0)),
                pl.BlockSpec((1, GLA_HDV), lambda b, c: (0, 0)),
                state]
    out_specs = [pl.BlockSpec((rows, chunk, GLA_DV), lambda b, c: (b, c, 0)), state]
    out_shape = [jax.ShapeDtypeStruct((bsz, t, GLA_DV), BF16),
                 jax.ShapeDtypeStruct((bsz, GLA_HEADS, GLA_HDK, GLA_HDV), F32)]
    operands = [z, z, z, z, b_cum, norm_g, s0]
    n_new = 0
    if sample is not None:
        dq, kvn, bufs, n_new = sample
        assert dq.shape[0] == (bsz // rows) * n_chunks
        per_step = lambda a: pl.BlockSpec((1,) + a.shape[1:], lambda b, c: (b * n_chunks + c, 0, 0))
        in_specs += [per_step(a) for a in (dq, kvn, *bufs)]
        operands += [dq, kvn, *bufs]
        o_s = jax.ShapeDtypeStruct((dq.shape[0], dq.shape[1], DIL_OUT), BF16)
        out_specs += [per_step(o_s)] + [per_step(a) for a in bufs]
        out_shape += [o_s] + [jax.ShapeDtypeStruct(a.shape, a.dtype) for a in bufs]
    return pl.pallas_call(
        functools.partial(_gla_kernel, chunk=chunk, n_valid=n_valid, n_new_sample=n_new),
        grid=(bsz // rows, n_chunks),
        in_specs=in_specs,
        out_specs=out_specs,
        out_shape=out_shape,
        scratch_shapes=[pltpu.VMEM((rows, GLA_HEADS, GLA_HDK, GLA_HDV), F32),
                        pltpu.VMEM((rows, GLA_HEADS, chunk, GLA_HDV), F32),
                        pltpu.VMEM((chunk, GLA_HDK), F32),
                        pltpu.VMEM((chunk, GLA_HDV), F32)],
        compiler_params=pltpu.CompilerParams(dimension_semantics=("parallel", "arbitrary"),
                                             vmem_limit_bytes=GLA_VMEM_LIMIT_BYTES),
        name="gla",
    )(*operands)


def _alibi_slope(g, head):
    n = DIL_GROUPS * DIL_HEADS
    return jnp.exp((-ALIBI_MAX * _LN2 / n) * (head + (g * DIL_HEADS + 1.0)))


def _pair_slopes(g, hp, rows):
    head = 2.0 * hp.astype(F32) + jnp.where(rows >= Q_TILE, 1.0, 0.0)
    return _alibi_slope(g, head)


def _dil_prompt_kernel(q0_ref, q1_ref, q2_ref, k0_ref, k1_ref, k2_ref, v0_ref, v1_ref, v2_ref,
                       o_ref, wk0_ref, wv0_ref, wk1_ref, wv1_ref, wk2_ref, wv2_ref,
                       qf_ref, og_ref, ld_ref, bf_ref, br_ref, p_ref, m_ref, *, seq):
    hp = pl.program_id(1)
    q_refs = (q0_ref, q1_ref, q2_ref)
    k_refs = (k0_ref, k1_ref, k2_ref)
    v_refs = (v0_ref, v1_ref, v2_ref)
    wk_refs = (wk0_ref, wk1_ref, wk2_ref)
    wv_refs = (wv0_ref, wv1_ref, wv2_ref)

    for g in range(DIL_GROUPS):
        qf_ref[g] = q_refs[g][...].astype(F32) * (DIL_HD ** -0.5)

    for g in range(DIL_GROUPS):
        rate = float(DIL_RATES[g])
        rows = lax.broadcasted_iota(jnp.int32, (2 * Q_TILE, Q_TILE), 0)
        cols = lax.broadcasted_iota(jnp.int32, (2 * Q_TILE, Q_TILE), 1)
        dist = (rows & (Q_TILE - 1)) - cols
        slope = _pair_slopes(g, hp, rows)
        bf_ref[g] = jnp.where(dist >= 0, -slope * rate * dist.astype(F32), MASK_VALUE)
        if g < 2:
            rows = lax.broadcasted_iota(jnp.int32, (2 * Q_TILE, 2 * Q_TILE), 0)
            cols = lax.broadcasted_iota(jnp.int32, (2 * Q_TILE, 2 * Q_TILE), 1)
            dist = (rows & (Q_TILE - 1)) + Q_TILE - cols
            slope = _pair_slopes(g, hp, rows)
            ok = jnp.abs(dist - Q_TILE // 2) <= Q_TILE // 2
            br_ref[g] = jnp.where(ok, -slope * rate * dist.astype(F32), MASK_VALUE)

    lane = lax.broadcasted_iota(jnp.int32, (Q_TILE, LANES), 1)
    lo = lane < DIL_HD

    def rows_of(start, size, rate):
        return pl.ds(start, size) if rate == 1 else pl.ds(start, size, stride=rate)

    def scores_stage(g, q_row, k_row, n_keys, bias, slot):
        rate = DIL_RATES[g]
        q2 = qf_ref[g, rows_of(q_row, Q_TILE, rate), :]
        k2 = k_refs[g][rows_of(k_row, n_keys, rate), :].astype(BF16)
        qs = jnp.concatenate([jnp.where(lo, q2, 0.0), jnp.where(lo, 0.0, q2)], axis=0).astype(BF16)
        s = _dot_nt(qs, k2) + bias()
        m = jnp.max(s, axis=-1, keepdims=True)
        p_ref[slot, :, 0:n_keys] = jnp.exp(s - m).astype(BF16)
        m_ref[slot] = jnp.where(lo, m[:Q_TILE], m[Q_TILE:])

    def values_stage(g, q_row, k_row, n_keys, slot):
        rate = DIL_RATES[g]
        v2 = v_refs[g][rows_of(k_row, n_keys, rate), :].astype(BF16)
        vo = jnp.concatenate([v2, jnp.ones((n_keys, LANES), BF16)], axis=1)
        r = _dot(p_ref[slot, :, 0:n_keys], vo)
        o2 = jnp.where(lo, r[:Q_TILE, :LANES], r[Q_TILE:, :LANES])
        l2 = jnp.where(lo, r[:Q_TILE, LANES:], r[Q_TILE:, LANES:])
        og_ref[g, rows_of(q_row, Q_TILE, rate), :] = o2 / l2
        ld_ref[g, rows_of(q_row, Q_TILE, rate), :] = m_ref[slot] + jnp.log(l2)

    pending = []
    emitted = [0]

    def run_tile_sets(n_sets, n_per_set, tile_of):
        base = emitted[0]

        def slot(i, u):
            return ((i + base) % 2) * TILE_SLOTS + u

        def scores(i):
            for u in range(n_per_set):
                g, q_row, k_row, n_keys, bias = tile_of(i, u)
                scores_stage(g, q_row, k_row, n_keys, bias, slot(i, u))

        def values(i):
            for u in range(n_per_set):
                g, q_row, k_row, n_keys, _ = tile_of(i, u)
                values_stage(g, q_row, k_row, n_keys, slot(i, u))

        if pending:
            pending.pop()()
        scores(0)
        if n_sets > 1:
            def body(i, carry):
                values(i - 1)
                scores(i)
                return carry
            lax.fori_loop(1, n_sets, body, 0)
        pending.append(lambda: values(n_sets - 1))
        emitted[0] += n_sets

    for g in range(DIL_GROUPS):
        rate = DIL_RATES[g]
        n_tiles = seq // rate // Q_TILE
        span = rate * Q_TILE
        group = min(rate, RESIDUES_PER_BODY)
        n_groups = rate // group

        def first_tile(i, u, g=g, group=group):
            rho = i * group + u
            return g, rho, rho, Q_TILE, lambda: bf_ref[g]

        def consecutive_tile(i, u, g=g, span=span):
            n = 1 + i * TILES_PER_BODY + u
            return g, n * span, (n - 1) * span, 2 * Q_TILE, lambda: br_ref[g]

        def later_tile(i, u, g=g, group=group, n_groups=n_groups, span=span):
            n = 1 + i // n_groups
            rho = (i % n_groups) * group + u
            return g, rho + n * span, rho + (n - 1) * span, 2 * Q_TILE, lambda: br_ref[g]

        run_tile_sets(n_groups, group, first_tile)
        if n_tiles > 1:
            if rate == 1:
                assert (n_tiles - 1) % TILES_PER_BODY == 0
                run_tile_sets((n_tiles - 1) // TILES_PER_BODY, TILES_PER_BODY, consecutive_tile)
            else:
                run_tile_sets((n_tiles - 1) * n_groups, group, later_tile)
    pending.pop()()

    def combine(i, carry):
        rs = pl.ds(pl.multiple_of(i * Q_TILE, Q_TILE), Q_TILE)
        ld = [ld_ref[g, rs, :] for g in range(DIL_GROUPS)]
        top = jnp.maximum(jnp.maximum(ld[0], ld[1]), ld[2])
        w = [jnp.exp(x - top) for x in ld]
        num = sum(w[g] * og_ref[g, rs, :] for g in range(DIL_GROUPS))
        o_ref[rs, :] = (num / (w[0] + w[1] + w[2])).astype(o_ref.dtype)
        return carry

    lax.fori_loop(0, seq // Q_TILE, combine, 0)

    for g in range(DIL_GROUPS):
        keep = wk_refs[g].shape[1]
        for src, dst in ((k_refs[g], wk_refs[g]), (v_refs[g], wv_refs[g])):
            for off in range(0, keep, LANES):
                dst[:, off:off + LANES] = src[seq - keep + off:seq - keep + off + LANES, :].T


def _dil_prompt(dq, kv):
    bsz, seq, _ = dq.shape
    pairs = DIL_HEADS * DIL_HD // LANES
    nblk = DIL_WIDTH // LANES
    keeps = [min(w, seq) for w in DIL_WINDOWS]

    def spec(col0):
        return pl.BlockSpec((None, seq, LANES), lambda b, hp, col0=col0: (b, 0, col0 + hp))

    in_specs = ([spec(g * pairs) for g in range(DIL_GROUPS)]
                + [spec(g * pairs) for g in range(DIL_GROUPS)]
                + [spec(nblk + g * pairs) for g in range(DIL_GROUPS)])
    return pl.pallas_call(
        functools.partial(_dil_prompt_kernel, seq=seq),
        grid=(bsz, pairs),
        in_specs=in_specs,
        out_specs=[pl.BlockSpec((None, seq, LANES), lambda b, hp: (b, 0, hp))]
                  + [pl.BlockSpec((None, LANES, w), lambda b, hp: (b, hp, 0)) for w in keeps for _ in range(2)],
        out_shape=[jax.ShapeDtypeStruct((bsz, seq, DIL_OUT), BF16)]
                  + [jax.ShapeDtypeStruct((bsz, DIL_HEADS * DIL_HD, w), F32) for w in keeps for _ in range(2)],
        scratch_shapes=[pltpu.VMEM((DIL_GROUPS, seq, LANES), F32),
                        pltpu.VMEM((DIL_GROUPS, seq, LANES), F32),
                        pltpu.VMEM((DIL_GROUPS, seq, LANES), F32),
                        pltpu.VMEM((DIL_GROUPS, 2 * Q_TILE, Q_TILE), F32),
                        pltpu.VMEM((2, 2 * Q_TILE, 2 * Q_TILE), F32),
                        pltpu.VMEM((2 * TILE_SLOTS, 2 * Q_TILE, 2 * Q_TILE), BF16),
                        pltpu.VMEM((2 * TILE_SLOTS, Q_TILE, LANES), F32)],
        compiler_params=_params("parallel", "arbitrary"),
        name="dil_prompt",
    )(dq, dq, dq, kv, kv, kv, kv, kv, kv)


def _dil_sample_kernel(q_ref, kvn_ref, kb0_ref, vb0_ref, kb1_ref, vb1_ref, kb2_ref, vb2_ref,
                       o_ref, nk0_ref, nv0_ref, nk1_ref, nv1_ref, nk2_ref, nv2_ref, *, n_new):
    kb_refs = (kb0_ref, kb1_ref, kb2_ref)
    vb_refs = (vb0_ref, vb1_ref, vb2_ref)
    nk_refs = (nk0_ref, nk1_ref, nk2_ref)
    nv_refs = (nv0_ref, nv1_ref, nv2_ref)
    pad = q_ref.shape[1]
    nrow = DIL_HEADS * pad
    width = DIL_HEADS * DIL_HD

    lane_head = jnp.right_shift(lax.broadcasted_iota(jnp.int32, (pad, width), 1), DIL_HD.bit_length() - 1)

    def bias_for(g, dist):
        rate = DIL_RATES[g]
        rows = lax.broadcasted_iota(jnp.int32, dist.shape, 0)
        head = sum(jnp.where(rows >= h * pad, 1.0, 0.0) for h in range(1, DIL_HEADS))
        ok = (dist >= 0) & (dist <= DIL_WINDOWS[g]) & ((dist & (rate - 1)) == 0)
        return jnp.where(ok, -_alibi_slope(g, head) * dist.astype(F32), MASK_VALUE)

    sel_r = lax.broadcasted_iota(jnp.int32, (pad, LANES), 0)
    sel_c = lax.broadcasted_iota(jnp.int32, (pad, LANES), 1)
    place = jnp.where((sel_r < n_new) & (sel_c == sel_r + (LANES - n_new)), 1.0, 0.0).astype(BF16)
    tail_lane = lax.broadcasted_iota(jnp.int32, (width, LANES), 1) >= LANES - n_new

    def shifted(buf_t, new_rows):
        length = buf_t.shape[1]
        rolled = pltpu.roll(buf_t, length - n_new, 1)
        new_t = sum(_dot_tn(part, place) for part in _split3(new_rows))
        last = jnp.where(tail_lane, new_t, rolled[:, length - LANES:])
        return rolled, last

    scores, values, transposed = [], [], []
    for g in range(DIL_GROUPS):
        length = kb_refs[g].shape[2]
        cs = slice(g * width, (g + 1) * width)
        vcs = slice(DIL_WIDTH + g * width, DIL_WIDTH + (g + 1) * width)
        qg = q_ref[0, :, cs].astype(F32) * (DIL_HD ** -0.5)
        qs = jnp.concatenate([jnp.where(lane_head == h, qg, 0.0) for h in range(DIL_HEADS)], axis=0).astype(BF16)
        kb, vb = kb_refs[g][0], vb_refs[g][0]
        kn, vn = kvn_ref[0, :, cs], kvn_ref[0, :, vcs]
        rows = lax.broadcasted_iota(jnp.int32, (nrow, length), 0)
        cols = lax.broadcasted_iota(jnp.int32, (nrow, length), 1)
        dist = length + (rows & (pad - 1)) - cols
        scores.append(_dot(qs, kb.astype(BF16)) + bias_for(g, dist))
        values.append(vb.astype(BF16))
        transposed.append(True)
        rows = lax.broadcasted_iota(jnp.int32, (nrow, pad), 0)
        cols = lax.broadcasted_iota(jnp.int32, (nrow, pad), 1)
        dist = jnp.where(cols < n_new, (rows & (pad - 1)) - cols, -1)
        scores.append(_dot_nt(qs, kn.astype(BF16)) + bias_for(g, dist))
        values.append(vn.astype(BF16))
        transposed.append(False)
        for buf, new, out_ref in ((kb, kn, nk_refs[g]), (vb, vn, nv_refs[g])):
            rolled, last = shifted(buf, new)
            if length > LANES:
                out_ref[0, :, 0:length - LANES] = rolled[:, 0:length - LANES]
            out_ref[0, :, length - LANES:length] = last

    top = functools.reduce(jnp.maximum, [jnp.max(s, axis=-1, keepdims=True) for s in scores])
    probs = [jnp.exp(s - top) for s in scores]
    den = sum(jnp.sum(p, axis=-1, keepdims=True) for p in probs)
    acc = sum((_dot_nt if t else _dot)(p.astype(BF16), v)
              for p, v, t in zip(probs, values, transposed)) / den
    out = sum(jnp.where(lane_head == h, acc[h * pad:(h + 1) * pad], 0.0) for h in range(DIL_HEADS))
    o_ref[0] = out.astype(o_ref.dtype)


FFN_COL_CHUNK = D_FF // 2


def _merge_ffn_kernel(x_ref, oa_ref, ob_ref, ga_ref, gb_ref, pa_ref, pb_ref, wo_ref, n2_ref, nf_ref,
                      wg_ref, wu_ref, wd_ref, y_ref):
    pa = _dot(oa_ref[...], pa_ref[...])
    pb = _dot(ob_ref[...], pb_ref[...])
    merged = jax.nn.sigmoid(ga_ref[...].astype(F32)) * pa + jax.nn.sigmoid(gb_ref[...].astype(F32)) * pb
    x1 = x_ref[...] + _dot(merged.astype(BF16), wo_ref[...])
    h = _rms(x1, n2_ref[...]).astype(BF16)
    acc = x1
    for c0 in range(0, D_FF, FFN_COL_CHUNK):
        cs = slice(c0, c0 + FFN_COL_CHUNK)
        gate = _dot(h, wg_ref[:, cs])
        up = _dot(h, wu_ref[:, cs])
        act = (gate * jax.nn.sigmoid(gate) * up).astype(BF16)
        acc = acc + _dot(act, wd_ref[cs, :])
    y_ref[...] = _rms(acc, nf_ref[...])


def _merge_ffn(x, oa, ob, gates, pa, pb, wo, n2, nf, wg, wu, wd, tm):
    m = x.shape[0]
    row = lambda w: pl.BlockSpec((tm, w), lambda i: (i, 0))
    resident = lambda a: pl.BlockSpec(a.shape, lambda i: (0, 0), pipeline_mode=pl.Buffered(1))
    return pl.pallas_call(
        _merge_ffn_kernel,
        grid=(m // tm,),
        in_specs=[row(D_MODEL), row(GLA_DV), row(DIL_OUT), row(D_MODEL),
                  pl.BlockSpec((tm, D_MODEL), lambda i: (i, 1))]
                 + [resident(a) for a in (pa, pb, wo, n2, nf, wg, wu, wd)],
        out_specs=row(D_MODEL),
        out_shape=jax.ShapeDtypeStruct((m, D_MODEL), F32),
        compiler_params=pltpu.CompilerParams(dimension_semantics=("parallel",),
                                             vmem_limit_bytes=MERGE_FFN_VMEM_LIMIT_BYTES),
        name="merge_ffn",
    )(x, oa, ob, gates, gates, pa, pb, wo, n2, nf, wg, wu, wd)


def _prep_weights(w_in, gla_gate_w2):
    w2 = jnp.concatenate([gla_gate_w2[0], jnp.zeros((LANES - GLA_GATE_RANK, GLA_DK), F32)], axis=0)
    return dict(w_in_t=jnp.transpose(w_in[0]).astype(BF16), w2=w2.astype(BF16))


def _project(x2d, norm1_g, gate_b, wts, tm, chunk, n_valid, casts=()):
    groups = [(0, GLA_COLS), (_OFF["dq"][0], DIL_WIDTH), (_OFF["dk"][0], 2 * DIL_WIDTH), (_OFF["ga"][0], 2 * D_MODEL)]
    return _in_proj(x2d, norm1_g, wts["w_in_t"], wts["w2"], gate_b, groups, _OFF["glr"][0],
                    [BF16, BF16, F32, BF16], tm, chunk, n_valid, casts)


def _tail(x2d, o_a, o_b, z_gates, wts, norm2_g, norm_f_g, tm):
    return _merge_ffn(x2d, o_a, o_b, z_gates, wts["pa"], wts["pb"], wts["wo"], norm2_g,
                      norm_f_g.reshape(1, D_MODEL), wts["wg"], wts["wu"], wts["wd"], tm)


def kernel(x_prompt, x_sample, state_gla, state_win0_k, state_win0_v, state_win1_k, state_win1_v,
           state_win2_k, state_win2_v, norm1_g, w_in, gla_gate_w2, gla_gate_b, gla_norm_g,
           proj_a, proj_b, w_out, norm2_g, w_ffn_gate, w_ffn_up, w_ffn_down, norm_f_g):
    wts = _prep_weights(w_in, gla_gate_w2)
    bp, seq, _ = x_prompt.shape
    bs, n_new, _ = x_sample.shape
    width = DIL_HEADS * DIL_HD

    xp = x_prompt.reshape(bp * seq, D_MODEL)
    later = dict(pa=proj_a[0], pb=proj_b[0], wo=w_out[0], wg=w_ffn_gate[0], wu=w_ffn_up[0], wd=w_ffn_down[0])
    z_gla, z_dq, z_kv, z_gates, b_cum, *cast = _project(xp, norm1_g, gla_gate_b, wts, ROW_BLOCK, GLA_CHUNK,
                                                        GLA_CHUNK, tuple(later.values()))
    wts.update(zip(later.keys(), cast))
    xs = jnp.pad(x_sample, ((0, 0), (0, SAMPLE_PAD - n_new), (0, 0))).reshape(bs * SAMPLE_PAD, D_MODEL)
    zs_gla, zs_dq, zs_kv, zs_gates, bs_cum = _project(xs, norm1_g, gla_gate_b, wts, bs * SAMPLE_PAD,
                                                      SAMPLE_PAD, n_new)

    bufs = [jnp.transpose(a[0], (0, 2, 3, 1)).reshape(bs, width, a.shape[2]) for a in
            (state_win0_k, state_win0_v, state_win1_k, state_win1_v, state_win2_k, state_win2_v)]
    sample = (zs_dq.reshape(bs, SAMPLE_PAD, DIL_WIDTH), zs_kv.reshape(bs, SAMPLE_PAD, 2 * DIL_WIDTH), bufs, n_new)
    s0 = jnp.zeros((bp, GLA_HEADS, GLA_HDK, GLA_HDV), F32)
    o_a, gla_p, os_b, *win_s = _gla(z_gla.reshape(bp, seq, GLA_COLS), b_cum.reshape(bp, seq, GLA_DK), gla_norm_g,
                                    s0, GLA_CHUNK, GLA_CHUNK, GLA_ROWS_PROMPT, sample)

    o_b, *win_p = _dil_prompt(z_dq.reshape(bp, seq, DIL_WIDTH), z_kv.reshape(bp, seq, 2 * DIL_WIDTH))
    y_prompt = _tail(xp, o_a.reshape(bp * seq, GLA_DV), o_b.reshape(bp * seq, DIL_OUT), z_gates, wts,
                     norm2_g, norm_f_g, ROW_BLOCK).reshape(bp, seq, D_MODEL)
    win_p = [jnp.transpose(a.reshape(bp, DIL_HEADS, DIL_HD, a.shape[2]), (0, 3, 1, 2))[None] for a in win_p]

    os_a, gla_s = _gla(zs_gla.reshape(bs, SAMPLE_PAD, GLA_COLS), bs_cum.reshape(bs, SAMPLE_PAD, GLA_DK),
                       gla_norm_g, state_gla[0], SAMPLE_PAD, n_new, GLA_ROWS_SAMPLE)
    ys = _tail(xs, os_a.reshape(bs * SAMPLE_PAD, GLA_DV), os_b.reshape(bs * SAMPLE_PAD, DIL_OUT), zs_gates, wts,
               norm2_g, norm_f_g, bs * SAMPLE_PAD)
    y_sample = ys.reshape(bs, SAMPLE_PAD, D_MODEL)[:, :n_new]
    win_s = [jnp.transpose(a.reshape(bs, DIL_HEADS, DIL_HD, a.shape[2]), (0, 3, 1, 2))[None] for a in win_s]

    return (y_prompt, y_sample, gla_p[None], *win_p, gla_s[None], *win_s)
```

```python
import functools

import jax
import jax.numpy as jnp
from jax import lax
from jax.experimental import pallas as pl
from jax.experimental.pallas import tpu as pltpu

F32 = jnp.float32
BF16 = jnp.bfloat16

D_MODEL = 1024
GLA_HEADS = 4
GLA_DK = 512
GLA_DV = 1024
GLA_HDK = 128
GLA_HDV = 256
GLA_GATE_RANK = 16
GLA_TAU = 16.0
DIL_WINDOWS = (128, 512, 2048)
DIL_RATES = (1, 4, 16)
DIL_GROUPS = 3
DIL_HEADS = 4
DIL_HD = 64
DIL_WIDTH = 768
DIL_OUT = 256
ALIBI_MAX = 8.0
D_FF = 2816
RMS_EPS = 1e-6

LANES = 128
SUBLANES = 8
BF16_SUBLANES = 16
Q_TILE = 128
TILES_PER_BODY = 15
RESIDUES_PER_BODY = 8
TILE_SLOTS = max(TILES_PER_BODY, RESIDUES_PER_BODY)
GLA_CHUNK = 128
GLA_ROWS_PROMPT = 4
GLA_ROWS_SAMPLE = 4
SAMPLE_PAD = 4
MASK_VALUE = -1e30
VMEM_LIMIT_BYTES = 48 * 1024 * 1024
IN_PROJ_VMEM_LIMIT_BYTES = 56 * 1024 * 1024
MERGE_FFN_VMEM_LIMIT_BYTES = 60 * 1024 * 1024
GLA_VMEM_LIMIT_BYTES = 56 * 1024 * 1024
_LN2 = 0.6931471805599453

_OFF = {}
_o = 0
for _name, _w in (("gq", GLA_DK), ("gk", GLA_DK), ("gv", GLA_DV), ("gr", GLA_DV), ("glr", GLA_GATE_RANK),
                  ("dq", DIL_WIDTH), ("dk", DIL_WIDTH), ("dv", DIL_WIDTH), ("ga", D_MODEL), ("gb", D_MODEL)):
    _OFF[_name] = (_o, _o + _w)
    _o += _w
GLA_COLS = 2 * GLA_DK + 2 * GLA_DV
GLA_SAFE_LOG_DECAY = 20.0
ROW_BLOCK = 512


def _params(*sem):
    return pltpu.CompilerParams(dimension_semantics=sem, vmem_limit_bytes=VMEM_LIMIT_BYTES)


def _dot(a, b):
    return jnp.dot(a, b, preferred_element_type=F32)


def _dot_nt(a, b):
    return lax.dot_general(a, b, (((1,), (1,)), ((), ())), preferred_element_type=F32)


def _dot_tn(a, b):
    return lax.dot_general(a, b, (((0,), (0,)), ((), ())), preferred_element_type=F32)


def _rms(x, g):
    return x * lax.rsqrt(jnp.mean(x * x, axis=-1, keepdims=True) + RMS_EPS) * g


IN_PROJ_COL_CHUNK = 1024


def _split3(x):
    x1 = x.astype(BF16)
    r1 = x - x1.astype(F32)
    x2 = r1.astype(BF16)
    x3 = (r1 - x2.astype(F32)).astype(BF16)
    return x1, x2, x3


def _in_proj_kernel(x_ref, g_ref, wt_ref, w2_ref, gb_ref, *refs, groups, low_rank_col, chunk, n_valid, n_casts):
    cast_in, out_refs = refs[:n_casts], refs[n_casts:]
    for src, dst in zip(cast_in, out_refs[len(out_refs) - n_casts:]):
        dst[...] = src[...].astype(dst.dtype)

    h = _rms(x_ref[...], g_ref[...]).astype(BF16)
    b_ref = out_refs[len(groups)]
    tm = x_ref.shape[0]

    for o_ref, (c_first, n) in zip(out_refs, groups):
        for c0 in range(0, n, IN_PROJ_COL_CHUNK):
            c1 = min(c0 + IN_PROJ_COL_CHUNK, n)
            o_ref[:, c0:c1] = _dot_nt(h, wt_ref[c_first + c0:c_first + c1, :]).astype(o_ref.dtype)

    low_rank = _dot_nt(h, wt_ref[low_rank_col:low_rank_col + LANES, :]).astype(BF16)
    gate = _dot(low_rank, w2_ref[...]) + gb_ref[...]
    log_a = (jnp.minimum(gate, 0.0) - jnp.log(1.0 + jnp.exp(-jnp.abs(gate)))) * (1.0 / GLA_TAU)
    if n_valid < chunk:
        tok = lax.broadcasted_iota(jnp.int32, log_a.shape, 0) & (chunk - 1)
        log_a = jnp.where(tok < n_valid, log_a, 0.0)
    span = min(tm, LANES)
    row = lax.broadcasted_iota(jnp.int32, (span, span), 0)
    col = lax.broadcasted_iota(jnp.int32, (span, span), 1)
    same_chunk = (row & -chunk) == (col & -chunk) if chunk < span else True
    tril = jnp.where((row >= col) & same_chunk, 1.0, 0.0).astype(BF16)
    for r0 in range(0, tm, span):
        parts = _split3(log_a[r0:r0 + span])[:2]
        b_ref[r0:r0 + span, :] = sum(_dot(tril, p) for p in parts)


def _cast_blocks(a, steps):
    count = max(c for c in range(1, steps + 1) if a.shape[0] % (c * BF16_SUBLANES) == 0)
    return a.shape[0] // count, count


def _in_proj(x, g, w_t, w2, gate_b, groups, low_rank_col, out_dtypes, tm, chunk, n_valid, casts=()):
    m, k = x.shape
    steps = m // tm
    assert chunk & (chunk - 1) == 0 and (chunk % LANES == 0 or LANES % chunk == 0) and tm % chunk == 0
    assert all(c % BF16_SUBLANES == 0 and n % LANES == 0 for c, n in groups) and low_rank_col % BF16_SUBLANES == 0
    resident = lambda a: pl.BlockSpec(a.shape, lambda i: (0, 0), pipeline_mode=pl.Buffered(1))
    out_widths = [n for _, n in groups] + [GLA_DK]

    def cast_spec(a):
        rows, count = _cast_blocks(a, steps)
        return pl.BlockSpec((rows, a.shape[1]), lambda i, count=count: (jnp.minimum(i, count - 1), 0))

    return pl.pallas_call(
        functools.partial(_in_proj_kernel, groups=tuple(groups), low_rank_col=low_rank_col,
                          chunk=chunk, n_valid=n_valid, n_casts=len(casts)),
        grid=(steps,),
        in_specs=([pl.BlockSpec((tm, k), lambda i: (i, 0))] + [resident(a) for a in (g, w_t, w2, gate_b)]
                  + [cast_spec(a) for a in casts]),
        out_specs=[pl.BlockSpec((tm, n), lambda i: (i, 0)) for n in out_widths] + [cast_spec(a) for a in casts],
        out_shape=([jax.ShapeDtypeStruct((m, n), dt) for n, dt in zip(out_widths, list(out_dtypes) + [F32])]
                   + [jax.ShapeDtypeStruct(a.shape, BF16) for a in casts]),
        compiler_params=pltpu.CompilerParams(dimension_semantics=("arbitrary",),
                                             vmem_limit_bytes=IN_PROJ_VMEM_LIMIT_BYTES),
        name="in_proj",
    )(x, g, w_t, w2, gate_b, *casts)


N_SAMPLE_IN = 2 + 2 * DIL_GROUPS
N_SAMPLE_OUT = 1 + 2 * DIL_GROUPS


def _gla_kernel(*refs, chunk, n_valid, n_new_sample):
    q_ref, k_ref, v_ref, r_ref, b_ref, ng_ref, s0_ref = refs[:7]
    refs = refs[7:]
    sample_in = ()
    if n_new_sample:
        sample_in, refs = refs[:N_SAMPLE_IN], refs[N_SAMPLE_IN:]
    o_ref, sout_ref = refs[:2]
    refs = refs[2:]
    sample_out = ()
    if n_new_sample:
        sample_out, refs = refs[:N_SAMPLE_OUT], refs[N_SAMPLE_OUT:]
    s_ref, oi_ref, kf_ref, vf_ref = refs

    c = pl.program_id(1)
    rows = q_ref.shape[0]

    @pl.when(c == 0)
    def _():
        s_ref[...] = s0_ref[...]

    for r in range(rows):
        _gla_chunk(r, q_ref, k_ref, v_ref, r_ref, b_ref, ng_ref, o_ref, s_ref, oi_ref, chunk, n_valid)
    if n_new_sample:
        _dil_sample_kernel(*sample_in, *sample_out, n_new=n_new_sample)

    for r in range(rows):
        @pl.when(jnp.min(b_ref[r, chunk - 1:chunk, :]) < -GLA_SAFE_LOG_DECAY)
        def _(r=r):
            _gla_chunk_exact_intra(r, q_ref, k_ref, v_ref, r_ref, b_ref, ng_ref, o_ref, oi_ref, kf_ref, vf_ref,
                                   chunk, n_valid)

    @pl.when(c == pl.num_programs(1) - 1)
    def _():
        sout_ref[...] = s_ref[...]


def _gla_epilogue(o, r, vs, r_ref, ng_ref, o_ref):
    gr = r_ref[r, :, vs].astype(F32)
    o_ref[r, :, vs] = (_rms(o, ng_ref[...]) * (gr * jax.nn.sigmoid(gr))).astype(o_ref.dtype)


def _gla_values(r, vs, v_ref, chunk, n_valid):
    vh = v_ref[r, :, vs]
    if n_valid < chunk:
        tok = lax.broadcasted_iota(jnp.int32, vh.shape, 0)
        vh = jnp.where(tok < n_valid, vh, jnp.zeros_like(vh))
    return vh


def _gla_chunk_exact_intra(r, q_ref, k_ref, v_ref, r_ref, b_ref, ng_ref, o_ref, oi_ref, kf_ref, vf_ref,
                           chunk, n_valid):
    tok = lax.broadcasted_iota(jnp.int32, (chunk, 1), 0)
    for h in range(GLA_HEADS):
        ks = slice(h * GLA_HDK, (h + 1) * GLA_HDK)
        vs = slice(h * GLA_HDV, (h + 1) * GLA_HDV)
        bh = b_ref[r, :, ks]
        qh = q_ref[r, :, ks].astype(F32) * (GLA_HDK ** -0.5)
        kf_ref[...] = k_ref[r, :, ks].astype(F32)
        vf_ref[...] = _gla_values(r, vs, v_ref, chunk, n_valid).astype(F32)

        def eight_keys(i, acc, bh=bh, qh=qh, ks=ks):
            rows8 = pl.ds(pl.multiple_of(i * SUBLANES, SUBLANES), SUBLANES)
            b8, k8, v8 = b_ref[r, rows8, ks], kf_ref[rows8, :], vf_ref[rows8, :]
            for j in range(SUBLANES):
                s = i * SUBLANES + j
                decay = jnp.exp(jnp.where(tok >= s, bh - b8[j:j + 1], MASK_VALUE))
                w = jnp.sum(qh * decay * k8[j:j + 1], axis=-1, keepdims=True)
                acc = acc + w * v8[j:j + 1]
            return acc

        intra = lax.fori_loop(0, chunk // SUBLANES, eight_keys, jnp.zeros((chunk, GLA_HDV), F32))
        _gla_epilogue(oi_ref[r, h] + intra, r, vs, r_ref, ng_ref, o_ref)


def _gla_chunk(r, q_ref, k_ref, v_ref, r_ref, b_ref, ng_ref, o_ref, s_ref, oi_ref, chunk, n_valid):
    row = lax.broadcasted_iota(jnp.int32, (chunk, chunk), 0)
    col = lax.broadcasted_iota(jnp.int32, (chunk, chunk), 1)
    causal = row >= col
    b = b_ref[r]

    for h in range(GLA_HEADS):
        ks = slice(h * GLA_HDK, (h + 1) * GLA_HDK)
        vs = slice(h * GLA_HDV, (h + 1) * GLA_HDV)
        bh = b[:, ks]
        qh = q_ref[r, :, ks].astype(F32) * (GLA_HDK ** -0.5)
        kh = k_ref[r, :, ks].astype(F32)
        vh = _gla_values(r, vs, v_ref, chunk, n_valid)
        qt = (qh * jnp.exp(bh)).astype(BF16)
        kt = (kh * jnp.exp(-bh)).astype(BF16)
        kd = (kh * jnp.exp(bh[chunk - 1:chunk, :] - bh)).astype(BF16)
        s_old = s_ref[r, h]
        scores = jnp.where(causal, _dot_nt(qt, kt), 0.0).astype(BF16)
        o_state = _dot(qt, s_old.astype(BF16))
        oi_ref[r, h] = o_state
        dec = jnp.exp(jnp.broadcast_to(bh[chunk - 1:chunk, :], (GLA_HDK, GLA_HDK)).T)
        s_ref[r, h] = s_old * jnp.concatenate([dec, dec], axis=1) + _dot_tn(kd, vh)
        _gla_epilogue(o_state + _dot(scores, vh), r, vs, r_ref, ng_ref, o_ref)


def _gla(z, b_cum, norm_g, s0, chunk, n_valid, rows, sample=None):
    bsz, t, _ = z.shape
    n_chunks = t // chunk
    state = pl.BlockSpec((rows, GLA_HEADS, GLA_HDK, GLA_HDV), lambda b, c: (b, 0, 0, 0))
    in_specs = [pl.BlockSpec((rows, chunk, GLA_DK), lambda b, c: (b, c, 0)),
                pl.BlockSpec((rows, chunk, GLA_DK), lambda b, c: (b, c, 1)),
                pl.BlockSpec((rows, chunk, GLA_DV), lambda b, c: (b, c, 1)),
                pl.BlockSpec((rows, chunk, GLA_DV), lambda b, c: (b, c, 2)),
                pl.BlockSpec((rows, chunk, GLA_DK), lambda b, c: (b, c, 0)),
                pl.BlockSpec((1, GLA_HDV), lambda b, c: (0, 0)),
                state]
    out_specs = [pl.BlockSpec((rows, chunk, GLA_DV), lambda b, c: (b, c, 0)), state]
    out_shape = [jax.ShapeDtypeStruct((bsz, t, GLA_DV), BF16),
                 jax.ShapeDtypeStruct((bsz, GLA_HEADS, GLA_HDK, GLA_HDV), F32)]
    operands = [z, z, z, z, b_cum, norm_g, s0]
    n_new = 0
    if sample is not None:
        dq, kvn, bufs, n_new = sample
        assert dq.shape[0] == (bsz // rows) * n_chunks
        per_step = lambda a: pl.BlockSpec((1,) + a.shape[1:], lambda b, c: (b * n_chunks + c, 0, 0))
        in_specs += [per_step(a) for a in (dq, kvn, *bufs)]
        operands += [dq, kvn, *bufs]
        o_s = jax.ShapeDtypeStruct((dq.shape[0], dq.shape[1], DIL_OUT), BF16)
        out_specs += [per_step(o_s)] + [per_step(a) for a in bufs]
        out_shape += [o_s] + [jax.ShapeDtypeStruct(a.shape, a.dtype) for a in bufs]
    return pl.pallas_call(
        functools.partial(_gla_kernel, chunk=chunk, n_valid=n_valid, n_new_sample=n_new),
        grid=(bsz // rows, n_chunks),
        in_specs=in_specs,
        out_specs=out_specs,
        out_shape=out_shape,
        scratch_shapes=[pltpu.VMEM((rows, GLA_HEADS, GLA_HDK, GLA_HDV), F32),
                        pltpu.VMEM((rows, GLA_HEADS, chunk, GLA_HDV), F32),
                        pltpu.VMEM((chunk, GLA_HDK), F32),
                        pltpu.VMEM((chunk, GLA_HDV), F32)],
        compiler_params=pltpu.CompilerParams(dimension_semantics=("parallel", "arbitrary"),
                                             vmem_limit_bytes=GLA_VMEM_LIMIT_BYTES),
        name="gla",
    )(*operands)


def _alibi_slope(g, head):
    n = DIL_GROUPS * DIL_HEADS
    return jnp.exp((-ALIBI_MAX * _LN2 / n) * (head + (g * DIL_HEADS + 1.0)))


def _pair_slopes(g, hp, rows):
    head = 2.0 * hp.astype(F32) + jnp.where(rows >= Q_TILE, 1.0, 0.0)
    return _alibi_slope(g, head)


def _dil_prompt_kernel(q0_ref, q1_ref, q2_ref, k0_ref, k1_ref, k2_ref, v0_ref, v1_ref, v2_ref,
                       o_ref, wk0_ref, wv0_ref, wk1_ref, wv1_ref, wk2_ref, wv2_ref,
                       qf_ref, og_ref, ld_ref, bf_ref, br_ref, p_ref, m_ref, *, seq):
    hp = pl.program_id(1)
    q_refs = (q0_ref, q1_ref, q2_ref)
    k_refs = (k0_ref, k1_ref, k2_ref)
    v_refs = (v0_ref, v1_ref, v2_ref)
    wk_refs = (wk0_ref, wk1_ref, wk2_ref)
    wv_refs = (wv0_ref, wv1_ref, wv2_ref)

    for g in range(DIL_GROUPS):
        qf_ref[g] = q_refs[g][...].astype(F32) * (DIL_HD ** -0.5)

    for g in range(DIL_GROUPS):
        rate = float(DIL_RATES[g])
        rows = lax.broadcasted_iota(jnp.int32, (2 * Q_TILE, Q_TILE), 0)
        cols = lax.broadcasted_iota(jnp.int32, (2 * Q_TILE, Q_TILE), 1)
        dist = (rows & (Q_TILE - 1)) - cols
        slope = _pair_slopes(g, hp, rows)
        bf_ref[g] = jnp.where(dist >= 0, -slope * rate * dist.astype(F32), MASK_VALUE)
        if g < 2:
            rows = lax.broadcasted_iota(jnp.int32, (2 * Q_TILE, 2 * Q_TILE), 0)
            cols = lax.broadcasted_iota(jnp.int32, (2 * Q_TILE, 2 * Q_TILE), 1)
            dist = (rows & (Q_TILE - 1)) + Q_TILE - cols
            slope = _pair_slopes(g, hp, rows)
            ok = jnp.abs(dist - Q_TILE // 2) <= Q_TILE // 2
            br_ref[g] = jnp.where(ok, -slope * rate * dist.astype(F32), MASK_VALUE)

    lane = lax.broadcasted_iota(jnp.int32, (Q_TILE, LANES), 1)
    lo = lane < DIL_HD

    def rows_of(start, size, rate):
        return pl.ds(start, size) if rate == 1 else pl.ds(start, size, stride=rate)

    def scores_stage(g, q_row, k_row, n_keys, bias, slot):
        rate = DIL_RATES[g]
        q2 = qf_ref[g, rows_of(q_row, Q_TILE, rate), :]
        k2 = k_refs[g][rows_of(k_row, n_keys, rate), :].astype(BF16)
        qs = jnp.concatenate([jnp.where(lo, q2, 0.0), jnp.where(lo, 0.0, q2)], axis=0).astype(BF16)
        s = _dot_nt(qs, k2) + bias()
        m = jnp.max(s, axis=-1, keepdims=True)
        p_ref[slot, :, 0:n_keys] = jnp.exp(s - m).astype(BF16)
        m_ref[slot] = jnp.where(lo, m[:Q_TILE], m[Q_TILE:])

    def values_stage(g, q_row, k_row, n_keys, slot):
        rate = DIL_RATES[g]
        v2 = v_refs[g][rows_of(k_row, n_keys, rate), :].astype(BF16)
        vo = jnp.concatenate([v2, jnp.ones((n_keys, LANES), BF16)], axis=1)
        r = _dot(p_ref[slot, :, 0:n_keys], vo)
        o2 = jnp.where(lo, r[:Q_TILE, :LANES], r[Q_TILE:, :LANES])
        l2 = jnp.where(lo, r[:Q_TILE, LANES:], r[Q_TILE:, LANES:])
        og_ref[g, rows_of(q_row, Q_TILE, rate), :] = o2 / l2
        ld_ref[g, rows_of(q_row, Q_TILE, rate), :] = m_ref[slot] + jnp.log(l2)

    pending = []
    emitted = [0]

    def run_tile_sets(n_sets, n_per_set, tile_of):
        base = emitted[0]

        def slot(i, u):
            return ((i + base) % 2) * TILE_SLOTS + u

        def scores(i):
            for u in range(n_per_set):
                g, q_row, k_row, n_keys, bias = tile_of(i, u)
                scores_stage(g, q_row, k_row, n_keys, bias, slot(i, u))

        def values(i):
            for u in range(n_per_set):
                g, q_row, k_row, n_keys, _ = tile_of(i, u)
                values_stage(g, q_row, k_row, n_keys, slot(i, u))

        if pending:
            pending.pop()()
        scores(0)
        if n_sets > 1:
            def body(i, carry):
                values(i - 1)
                scores(i)
                return carry
            lax.fori_loop(1, n_sets, body, 0)
        pending.append(lambda: values(n_sets - 1))
        emitted[0] += n_sets

    for g in range(DIL_GROUPS):
        rate = DIL_RATES[g]
        n_tiles = seq // rate // Q_TILE
        span = rate * Q_TILE
        group = min(rate, RESIDUES_PER_BODY)
        n_groups = rate // group

        def first_tile(i, u, g=g, group=group):
            rho = i * group + u
            return g, rho, rho, Q_TILE, lambda: bf_ref[g]

        def consecutive_tile(i, u, g=g, span=span):
            n = 1 + i * TILES_PER_BODY + u
            return g, n * span, (n - 1) * span, 2 * Q_TILE, lambda: br_ref[g]

        def later_tile(i, u, g=g, group=group, n_groups=n_groups, span=span):
            n = 1 + i // n_groups
            rho = (i % n_groups) * group + u
            return g, rho + n * span, rho + (n - 1) * span, 2 * Q_TILE, lambda: br_ref[g]

        run_tile_sets(n_groups, group, first_tile)
        if n_tiles > 1:
            if rate == 1:
                assert (n_tiles - 1) % TILES_PER_BODY == 0
                run_tile_sets((n_tiles - 1) // TILES_PER_BODY, TILES_PER_BODY, consecutive_tile)
            else:
                run_tile_sets((n_tiles - 1) * n_groups, group, later_tile)
    pending.pop()()

    def combine(i, carry):
        rs = pl.ds(pl.multiple_of(i * Q_TILE, Q_TILE), Q_TILE)
        ld = [ld_ref[g, rs, :] for g in range(DIL_GROUPS)]
        top = jnp.maximum(jnp.maximum(ld[0], ld[1]), ld[2])
        w = [jnp.exp(x - top) for x in ld]
        num = sum(w[g] * og_ref[g, rs, :] for g in range(DIL_GROUPS))
        o_ref[rs, :] = (num / (w[0] + w[1] + w[2])).astype(o_ref.dtype)
        return carry

    lax.fori_loop(0, seq // Q_TILE, combine, 0)

    for g in range(DIL_GROUPS):
        keep = wk_refs[g].shape[1]
        for src, dst in ((k_refs[g], wk_refs[g]), (v_refs[g], wv_refs[g])):
            for off in range(0, keep, LANES):
                dst[:, off:off + LANES] = src[seq - keep + off:seq - keep + off + LANES, :].T


def _dil_prompt(dq, kv):
    bsz, seq, _ = dq.shape
    pairs = DIL_HEADS * DIL_HD // LANES
    nblk = DIL_WIDTH // LANES
    keeps = [min(w, seq) for w in DIL_WINDOWS]

    def spec(col0):
        return pl.BlockSpec((None, seq, LANES), lambda b, hp, col0=col0: (b, 0, col0 + hp))

    in_specs = ([spec(g * pairs) for g in range(DIL_GROUPS)]
                + [spec(g * pairs) for g in range(DIL_GROUPS)]
                + [spec(nblk + g * pairs) for g in range(DIL_GROUPS)])
    return pl.pallas_call(
        functools.partial(_dil_prompt_kernel, seq=seq),
        grid=(bsz, pairs),
        in_specs=in_specs,
        out_specs=[pl.BlockSpec((None, seq, LANES), lambda b, hp: (b, 0, hp))]
                  + [pl.BlockSpec((None, LANES, w), lambda b, hp: (b, hp, 0)) for w in keeps for _ in range(2)],
        out_shape=[jax.ShapeDtypeStruct((bsz, seq, DIL_OUT), BF16)]
                  + [jax.ShapeDtypeStruct((bsz, DIL_HEADS * DIL_HD, w), F32) for w in keeps for _ in range(2)],
        scratch_shapes=[pltpu.VMEM((DIL_GROUPS, seq, LANES), F32),
                        pltpu.VMEM((DIL_GROUPS, seq, LANES), F32),
                        pltpu.VMEM((DIL_GROUPS, seq, LANES), F32),
                        pltpu.VMEM((DIL_GROUPS, 2 * Q_TILE, Q_TILE), F32),
                        pltpu.VMEM((2, 2 * Q_TILE, 2 * Q_TILE), F32),
                        pltpu.VMEM((2 * TILE_SLOTS, 2 * Q_TILE, 2 * Q_TILE), BF16),
                        pltpu.VMEM((2 * TILE_SLOTS, Q_TILE, LANES), F32)],
        compiler_params=_params("parallel", "arbitrary"),
        name="dil_prompt",
    )(dq, dq, dq, kv, kv, kv, kv, kv, kv)


def _dil_sample_kernel(q_ref, kvn_ref, kb0_ref, vb0_ref, kb1_ref, vb1_ref, kb2_ref, vb2_ref,
                       o_ref, nk0_ref, nv0_ref, nk1_ref, nv1_ref, nk2_ref, nv2_ref, *, n_new):
    kb_refs = (kb0_ref, kb1_ref, kb2_ref)
    vb_refs = (vb0_ref, vb1_ref, vb2_ref)
    nk_refs = (nk0_ref, nk1_ref, nk2_ref)
    nv_refs = (nv0_ref, nv1_ref, nv2_ref)
    pad = q_ref.shape[1]
    nrow = DIL_HEADS * pad
    width = DIL_HEADS * DIL_HD

    lane_head = jnp.right_shift(lax.broadcasted_iota(jnp.int32, (pad, width), 1), DIL_HD.bit_length() - 1)

    def bias_for(g, dist):
        rate = DIL_RATES[g]
        rows = lax.broadcasted_iota(jnp.int32, dist.shape, 0)
        head = sum(jnp.where(rows >= h * pad, 1.0, 0.0) for h in range(1, DIL_HEADS))
        ok = (dist >= 0) & (dist <= DIL_WINDOWS[g]) & ((dist & (rate - 1)) == 0)
        return jnp.where(ok, -_alibi_slope(g, head) * dist.astype(F32), MASK_VALUE)

    sel_r = lax.broadcasted_iota(jnp.int32, (pad, LANES), 0)
    sel_c = lax.broadcasted_iota(jnp.int32, (pad, LANES), 1)
    place = jnp.where((sel_r < n_new) & (sel_c == sel_r + (LANES - n_new)), 1.0, 0.0).astype(BF16)
    tail_lane = lax.broadcasted_iota(jnp.int32, (width, LANES), 1) >= LANES - n_new

    def shifted(buf_t, new_rows):
        length = buf_t.shape[1]
        rolled = pltpu.roll(buf_t, length - n_new, 1)
        new_t = sum(_dot_tn(part, place) for part in _split3(new_rows))
        last = jnp.where(tail_lane, new_t, rolled[:, length - LANES:])
        return rolled, last

    scores, values, transposed = [], [], []
    for g in range(DIL_GROUPS):
        length = kb_refs[g].shape[2]
        cs = slice(g * width, (g + 1) * width)
        vcs = slice(DIL_WIDTH + g * width, DIL_WIDTH + (g + 1) * width)
        qg = q_ref[0, :, cs].astype(F32) * (DIL_HD ** -0.5)
        qs = jnp.concatenate([jnp.where(lane_head == h, qg, 0.0) for h in range(DIL_HEADS)], axis=0).astype(BF16)
        kb, vb = kb_refs[g][0], vb_refs[g][0]
        kn, vn = kvn_ref[0, :, cs], kvn_ref[0, :, vcs]
        rows = lax.broadcasted_iota(jnp.int32, (nrow, length), 0)
        cols = lax.broadcasted_iota(jnp.int32, (nrow, length), 1)
        dist = length + (rows & (pad - 1)) - cols
        scores.append(_dot(qs, kb.astype(BF16)) + bias_for(g, dist))
        values.append(vb.astype(BF16))
        transposed.append(True)
        rows = lax.broadcasted_iota(jnp.int32, (nrow, pad), 0)
        cols = lax.broadcasted_iota(jnp.int32, (nrow, pad), 1)
        dist = jnp.where(cols < n_new, (rows & (pad - 1)) - cols, -1)
        scores.append(_dot_nt(qs, kn.astype(BF16)) + bias_for(g, dist))
        values.append(vn.astype(BF16))
        transposed.append(False)
        for buf, new, out_ref in ((kb, kn, nk_refs[g]), (vb, vn, nv_refs[g])):
            rolled, last = shifted(buf, new)
            if length > LANES:
                out_ref[0, :, 0:length - LANES] = rolled[:, 0:length - LANES]
            out_ref[0, :, length - LANES:length] = last

    top = functools.reduce(jnp.maximum, [jnp.max(s, axis=-1, keepdims=True) for s in scores])
    probs = [jnp.exp(s - top) for s in scores]
    den = sum(jnp.sum(p, axis=-1, keepdims=True) for p in probs)
    acc = sum((_dot_nt if t else _dot)(p.astype(BF16), v)
              for p, v, t in zip(probs, values, transposed)) / den
    out = sum(jnp.where(lane_head == h, acc[h * pad:(h + 1) * pad], 0.0) for h in range(DIL_HEADS))
    o_ref[0] = out.astype(o_ref.dtype)


FFN_COL_CHUNK = D_FF // 2


def _merge_ffn_kernel(x_ref, oa_ref, ob_ref, ga_ref, gb_ref, pa_ref, pb_ref, wo_ref, n2_ref, nf_ref,
                      wg_ref, wu_ref, wd_ref, y_ref):
    pa = _dot(oa_ref[...], pa_ref[...])
    pb = _dot(ob_ref[...], pb_ref[...])
    merged = jax.nn.sigmoid(ga_ref[...].astype(F32)) * pa + jax.nn.sigmoid(gb_ref[...].astype(F32)) * pb
    x1 = x_ref[...] + _dot(merged.astype(BF16), wo_ref[...])
    h = _rms(x1, n2_ref[...]).astype(BF16)
    acc = x1
    for c0 in range(0, D_FF, FFN_COL_CHUNK):
        cs = slice(c0, c0 + FFN_COL_CHUNK)
        gate = _dot(h, wg_ref[:, cs])
        up = _dot(h, wu_ref[:, cs])
        act = (gate * jax.nn.sigmoid(gate) * up).astype(BF16)
        acc = acc + _dot(act, wd_ref[cs, :])
    y_ref[...] = _rms(acc, nf_ref[...])


def _merge_ffn(x, oa, ob, gates, pa, pb, wo, n2, nf, wg, wu, wd, tm):
    m = x.shape[0]
    row = lambda w: pl.BlockSpec((tm, w), lambda i: (i, 0))
    resident = lambda a: pl.BlockSpec(a.shape, lambda i: (0, 0), pipeline_mode=pl.Buffered(1))
    return pl.pallas_call(
        _merge_ffn_kernel,
        grid=(m // tm,),
        in_specs=[row(D_MODEL), row(GLA_DV), row(DIL_OUT), row(D_MODEL),
                  pl.BlockSpec((tm, D_MODEL), lambda i: (i, 1))]
                 + [resident(a) for a in (pa, pb, wo, n2, nf, wg, wu, wd)],
        out_specs=row(D_MODEL),
        out_shape=jax.ShapeDtypeStruct((m, D_MODEL), F32),
        compiler_params=pltpu.CompilerParams(dimension_semantics=("parallel",),
                                             vmem_limit_bytes=MERGE_FFN_VMEM_LIMIT_BYTES),
        name="merge_ffn",
    )(x, oa, ob, gates, gates, pa, pb, wo, n2, nf, wg, wu, wd)


def _prep_weights(w_in, gla_gate_w2):
    w2 = jnp.concatenate([gla_gate_w2[0], jnp.zeros((LANES - GLA_GATE_RANK, GLA_DK), F32)], axis=0)
    return dict(w_in_t=jnp.transpose(w_in[0]).astype(BF16), w2=w2.astype(BF16))


def _project(x2d, norm1_g, gate_b, wts, tm, chunk, n_valid, casts=()):
    groups = [(0, GLA_COLS), (_OFF["dq"][0], DIL_WIDTH), (_OFF["dk"][0], 2 * DIL_WIDTH), (_OFF["ga"][0], 2 * D_MODEL)]
    return _in_proj(x2d, norm1_g, wts["w_in_t"], wts["w2"], gate_b, groups, _OFF["glr"][0],
                    [BF16, BF16, F32, BF16], tm, chunk, n_valid, casts)


def _tail(x2d, o_a, o_b, z_gates, wts, norm2_g, norm_f_g, tm):
    return _merge_ffn(x2d, o_a, o_b, z_gates, wts["pa"], wts["pb"], wts["wo"], norm2_g,
                      norm_f_g.reshape(1, D_MODEL), wts["wg"], wts["wu"], wts["wd"], tm)


def kernel(x_prompt, x_sample, state_gla, state_win0_k, state_win0_v, state_win1_k, state_win1_v,
           state_win2_k, state_win2_v, norm1_g, w_in, gla_gate_w2, gla_gate_b, gla_norm_g,
           proj_a, proj_b, w_out, norm2_g, w_ffn_gate, w_ffn_up, w_ffn_down, norm_f_g):
    wts = _prep_weights(w_in, gla_gate_w2)
    bp, seq, _ = x_prompt.shape
    bs, n_new, _ = x_sample.shape
    width = DIL_HEADS * DIL_HD

    xp = x_prompt.reshape(bp * seq, D_MODEL)
    later = dict(pa=proj_a[0], pb=proj_b[0], wo=w_out[0], wg=w_ffn_gate[0], wu=w_ffn_up[0], wd=w_ffn_down[0])
    z_gla, z_dq, z_kv, z_gates, b_cum, *cast = _project(xp, norm1_g, gla_gate_b, wts, ROW_BLOCK, GLA_CHUNK,
                                                        GLA_CHUNK, tuple(later.values()))
    wts.update(zip(later.keys(), cast))
    xs = jnp.pad(x_sample, ((0, 0), (0, SAMPLE_PAD - n_new), (0, 0))).reshape(bs * SAMPLE_PAD, D_MODEL)
    zs_gla, zs_dq, zs_kv, zs_gates, bs_cum = _project(xs, norm1_g, gla_gate_b, wts, bs * SAMPLE_PAD,
                                                      SAMPLE_PAD, n_new)

    bufs = [jnp.transpose(a[0], (0, 2, 3, 1)).reshape(bs, width, a.shape[2]) for a in
            (state_win0_k, state_win0_v, state_win1_k, state_win1_v, state_win2_k, state_win2_v)]
    sample = (zs_dq.reshape(bs, SAMPLE_PAD, DIL_WIDTH), zs_kv.reshape(bs, SAMPLE_PAD, 2 * DIL_WIDTH), bufs, n_new)
    s0 = jnp.zeros((bp, GLA_HEADS, GLA_HDK, GLA_HDV), F32)
    o_a, gla_p, os_b, *win_s = _gla(z_gla.reshape(bp, seq, GLA_COLS), b_cum.reshape(bp, seq, GLA_DK), gla_norm_g,
                                    s0, GLA_CHUNK, GLA_CHUNK, GLA_ROWS_PROMPT, sample)

    o_b, *win_p = _dil_prompt(z_dq.reshape(bp, seq, DIL_WIDTH), z_kv.reshape(bp, seq, 2 * DIL_WIDTH))
    y_prompt = _tail(xp, o_a.reshape(bp * seq, GLA_DV), o_b.reshape(bp * seq, DIL_OUT), z_gates, wts,
                     norm2_g, norm_f_g, ROW_BLOCK).reshape(bp, seq, D_MODEL)
    win_p = [jnp.transpose(a.reshape(bp, DIL_HEADS, DIL_HD, a.shape[2]), (0, 3, 1, 2))[None] for a in win_p]

    os_a, gla_s = _gla(zs_gla.reshape(bs, SAMPLE_PAD, GLA_COLS), bs_cum.reshape(bs, SAMPLE_PAD, GLA_DK),
                       gla_norm_g, state_gla[0], SAMPLE_PAD, n_new, GLA_ROWS_SAMPLE)
    ys = _tail(xs, os_a.reshape(bs * SAMPLE_PAD, GLA_DV), os_b.reshape(bs * SAMPLE_PAD, DIL_OUT), zs_gates, wts,
               norm2_g, norm_f_g, bs * SAMPLE_PAD)
    y_sample = ys.reshape(bs, SAMPLE_PAD, D_MODEL)[:, :n_new]
    win_s = [jnp.transpose(a.reshape(bs, DIL_HEADS, DIL_HD, a.shape[2]), (0, 3, 1, 2))[None] for a in win_s]

    return (y_prompt, y_sample, gla_p[None], *win_p, gla_s[None], *win_s)
```

```python
import functools

import jax
import jax.numpy as jnp
from jax import lax
from jax.experimental import pallas as pl
from jax.experimental.pallas import tpu as pltpu

F32 = jnp.float32
BF16 = jnp.bfloat16

D_MODEL = 1024
GLA_HEADS = 4
GLA_DK = 512
GLA_DV = 1024
GLA_HDK = 128
GLA_HDV = 256
GLA_GATE_RANK = 16
GLA_TAU = 16.0
DIL_WINDOWS = (128, 512, 2048)
DIL_RATES = (1, 4, 16)
DIL_GROUPS = 3
DIL_HEADS = 4
DIL_HD = 64
DIL_WIDTH = 768
DIL_OUT = 256
ALIBI_MAX = 8.0
D_FF = 2816
RMS_EPS = 1e-6

LANES = 128
SUBLANES = 8
BF16_SUBLANES = 16
Q_TILE = 128
TILE_SLOTS = 2
GLA_CHUNK = 128
GLA_ROWS_PROMPT = 4
GLA_ROWS_SAMPLE = 4
SAMPLE_PAD = 8
MASK_VALUE = -1e30
VMEM_LIMIT_BYTES = 48 * 1024 * 1024
IN_PROJ_VMEM_LIMIT_BYTES = 56 * 1024 * 1024
MERGE_FFN_VMEM_LIMIT_BYTES = 60 * 1024 * 1024
GLA_VMEM_LIMIT_BYTES = 56 * 1024 * 1024
_LN2 = 0.6931471805599453

_OFF = {}
_o = 0
for _name, _w in (("gq", GLA_DK), ("gk", GLA_DK), ("gv", GLA_DV), ("gr", GLA_DV), ("glr", GLA_GATE_RANK),
                  ("dq", DIL_WIDTH), ("dk", DIL_WIDTH), ("dv", DIL_WIDTH), ("ga", D_MODEL), ("gb", D_MODEL)):
    _OFF[_name] = (_o, _o + _w)
    _o += _w
GLA_COLS = 2 * GLA_DK + 2 * GLA_DV
GLA_SAFE_LOG_DECAY = 20.0
ROW_BLOCK = 512


def _params(*sem):
    return pltpu.CompilerParams(dimension_semantics=sem, vmem_limit_bytes=VMEM_LIMIT_BYTES)


def _dot(a, b):
    return jnp.dot(a, b, preferred_element_type=F32)


def _dot_nt(a, b):
    return lax.dot_general(a, b, (((1,), (1,)), ((), ())), preferred_element_type=F32)


def _dot_tn(a, b):
    return lax.dot_general(a, b, (((0,), (0,)), ((), ())), preferred_element_type=F32)


def _rms(x, g):
    return x * lax.rsqrt(jnp.mean(x * x, axis=-1, keepdims=True) + RMS_EPS) * g


IN_PROJ_COL_CHUNK = 1024


def _split3(x):
    x1 = x.astype(BF16)
    r1 = x - x1.astype(F32)
    x2 = r1.astype(BF16)
    x3 = (r1 - x2.astype(F32)).astype(BF16)
    return x1, x2, x3


def _in_proj_kernel(x_ref, g_ref, wt_ref, w2_ref, gb_ref, *refs, groups, low_rank_col, chunk, n_valid, n_casts):
    cast_in, out_refs = refs[:n_casts], refs[n_casts:]
    for src, dst in zip(cast_in, out_refs[len(out_refs) - n_casts:]):
        dst[...] = src[...].astype(dst.dtype)

    h = _rms(x_ref[...], g_ref[...]).astype(BF16)
    b_ref = out_refs[len(groups)]
    tm = x_ref.shape[0]

    for o_ref, (c_first, n) in zip(out_refs, groups):
        for c0 in range(0, n, IN_PROJ_COL_CHUNK):
            c1 = min(c0 + IN_PROJ_COL_CHUNK, n)
            o_ref[:, c0:c1] = _dot_nt(h, wt_ref[c_first + c0:c_first + c1, :]).astype(o_ref.dtype)

    low_rank = _dot_nt(h, wt_ref[low_rank_col:low_rank_col + LANES, :]).astype(BF16)
    gate = _dot(low_rank, w2_ref[...]) + gb_ref[...]
    log_a = (jnp.minimum(gate, 0.0) - jnp.log(1.0 + jnp.exp(-jnp.abs(gate)))) * (1.0 / GLA_TAU)
    if n_valid < chunk:
        tok = lax.broadcasted_iota(jnp.int32, log_a.shape, 0) & (chunk - 1)
        log_a = jnp.where(tok < n_valid, log_a, 0.0)
    span = min(tm, LANES)
    row = lax.broadcasted_iota(jnp.int32, (span, span), 0)
    col = lax.broadcasted_iota(jnp.int32, (span, span), 1)
    same_chunk = (row & -chunk) == (col & -chunk) if chunk < span else True
    tril = jnp.where((row >= col) & same_chunk, 1.0, 0.0).astype(BF16)
    for r0 in range(0, tm, span):
        parts = _split3(log_a[r0:r0 + span])[:2]
        b_ref[r0:r0 + span, :] = sum(_dot(tril, p) for p in parts)


def _cast_blocks(a, steps):
    count = max(c for c in range(1, steps + 1) if a.shape[0] % (c * BF16_SUBLANES) == 0)
    return a.shape[0] // count, count


def _in_proj(x, g, w_t, w2, gate_b, groups, low_rank_col, out_dtypes, tm, chunk, n_valid, casts=()):
    m, k = x.shape
    steps = m // tm
    assert chunk & (chunk - 1) == 0 and (chunk % LANES == 0 or LANES % chunk == 0) and tm % chunk == 0
    assert all(c % BF16_SUBLANES == 0 and n % LANES == 0 for c, n in groups) and low_rank_col % BF16_SUBLANES == 0
    resident = lambda a: pl.BlockSpec(a.shape, lambda i: (0, 0), pipeline_mode=pl.Buffered(1))
    out_widths = [n for _, n in groups] + [GLA_DK]

    def cast_spec(a):
        rows, count = _cast_blocks(a, steps)
        return pl.BlockSpec((rows, a.shape[1]), lambda i, count=count: (jnp.minimum(i, count - 1), 0))

    return pl.pallas_call(
        functools.partial(_in_proj_kernel, groups=tuple(groups), low_rank_col=low_rank_col,
                          chunk=chunk, n_valid=n_valid, n_casts=len(casts)),
        grid=(steps,),
        in_specs=([pl.BlockSpec((tm, k), lambda i: (i, 0))] + [resident(a) for a in (g, w_t, w2, gate_b)]
                  + [cast_spec(a) for a in casts]),
        out_specs=[pl.BlockSpec((tm, n), lambda i: (i, 0)) for n in out_widths] + [cast_spec(a) for a in casts],
        out_shape=([jax.ShapeDtypeStruct((m, n), dt) for n, dt in zip(out_widths, list(out_dtypes) + [F32])]
                   + [jax.ShapeDtypeStruct(a.shape, BF16) for a in casts]),
        compiler_params=pltpu.CompilerParams(dimension_semantics=("arbitrary",),
                                             vmem_limit_bytes=IN_PROJ_VMEM_LIMIT_BYTES),
        name="in_proj",
    )(x, g, w_t, w2, gate_b, *casts)


N_SAMPLE_IN = 2 + 2 * DIL_GROUPS
N_SAMPLE_OUT = 1 + 2 * DIL_GROUPS


def _gla_kernel(*refs, chunk, n_valid, n_new_sample):
    q_ref, k_ref, v_ref, r_ref, b_ref, ng_ref, s0_ref = refs[:7]
    refs = refs[7:]
    sample_in = ()
    if n_new_sample:
        sample_in, refs = refs[:N_SAMPLE_IN], refs[N_SAMPLE_IN:]
    o_ref, sout_ref = refs[:2]
    refs = refs[2:]
    sample_out = ()
    if n_new_sample:
        sample_out, refs = refs[:N_SAMPLE_OUT], refs[N_SAMPLE_OUT:]
    s_ref, oi_ref, kf_ref, vf_ref = refs

    c = pl.program_id(1)
    rows = q_ref.shape[0]

    @pl.when(c == 0)
    def _():
        s_ref[...] = s0_ref[...]

    for r in range(rows):
        _gla_chunk(r, q_ref, k_ref, v_ref, r_ref, b_ref, ng_ref, o_ref, s_ref, oi_ref, chunk, n_valid)
    if n_new_sample:
        _dil_sample_kernel(*sample_in, *sample_out, n_new=n_new_sample)

    for r in range(rows):
        @pl.when(jnp.min(b_ref[r, chunk - 1:chunk, :]) < -GLA_SAFE_LOG_DECAY)
        def _(r=r):
            _gla_chunk_exact_intra(r, q_ref, k_ref, v_ref, r_ref, b_ref, ng_ref, o_ref, oi_ref, kf_ref, vf_ref,
                                   chunk, n_valid)

    @pl.when(c == pl.num_programs(1) - 1)
    def _():
        sout_ref[...] = s_ref[...]


def _gla_epilogue(o, r, vs, r_ref, ng_ref, o_ref):
    gr = r_ref[r, :, vs].astype(F32)
    o_ref[r, :, vs] = (_rms(o, ng_ref[...]) * (gr * jax.nn.sigmoid(gr))).astype(o_ref.dtype)


def _gla_values(r, vs, v_ref, chunk, n_valid):
    vh = v_ref[r, :, vs]
    if n_valid < chunk:
        tok = lax.broadcasted_iota(jnp.int32, vh.shape, 0)
        vh = jnp.where(tok < n_valid, vh, jnp.zeros_like(vh))
    return vh


def _gla_chunk_exact_intra(r, q_ref, k_ref, v_ref, r_ref, b_ref, ng_ref, o_ref, oi_ref, kf_ref, vf_ref,
                           chunk, n_valid):
    tok = lax.broadcasted_iota(jnp.int32, (chunk, 1), 0)
    for h in range(GLA_HEADS):
        ks = slice(h * GLA_HDK, (h + 1) * GLA_HDK)
        vs = slice(h * GLA_HDV, (h + 1) * GLA_HDV)
        bh = b_ref[r, :, ks]
        qh = q_ref[r, :, ks].astype(F32) * (GLA_HDK ** -0.5)
        kf_ref[...] = k_ref[r, :, ks].astype(F32)
        vf_ref[...] = _gla_values(r, vs, v_ref, chunk, n_valid).astype(F32)

        def eight_keys(i, acc, bh=bh, qh=qh, ks=ks):
            rows8 = pl.ds(pl.multiple_of(i * SUBLANES, SUBLANES), SUBLANES)
            b8, k8, v8 = b_ref[r, rows8, ks], kf_ref[rows8, :], vf_ref[rows8, :]
            for j in range(SUBLANES):
                s = i * SUBLANES + j
                decay = jnp.exp(jnp.where(tok >= s, bh - b8[j:j + 1], MASK_VALUE))
                w = jnp.sum(qh * decay * k8[j:j + 1], axis=-1, keepdims=True)
                acc = acc + w * v8[j:j + 1]
            return acc

        intra = lax.fori_loop(0, chunk // SUBLANES, eight_keys, jnp.zeros((chunk, GLA_HDV), F32))
        _gla_epilogue(oi_ref[r, h] + intra, r, vs, r_ref, ng_ref, o_ref)


def _gla_chunk(r, q_ref, k_ref, v_ref, r_ref, b_ref, ng_ref, o_ref, s_ref, oi_ref, chunk, n_valid):
    row = lax.broadcasted_iota(jnp.int32, (chunk, chunk), 0)
    col = lax.broadcasted_iota(jnp.int32, (chunk, chunk), 1)
    causal = row >= col
    b = b_ref[r]

    for h in range(GLA_HEADS):
        ks = slice(h * GLA_HDK, (h + 1) * GLA_HDK)
        vs = slice(h * GLA_HDV, (h + 1) * GLA_HDV)
        bh = b[:, ks]
        qh = q_ref[r, :, ks].astype(F32) * (GLA_HDK ** -0.5)
        kh = k_ref[r, :, ks].astype(F32)
        vh = _gla_values(r, vs, v_ref, chunk, n_valid)
        qt = (qh * jnp.exp(bh)).astype(BF16)
        kt = (kh * jnp.exp(-bh)).astype(BF16)
        kd = (kh * jnp.exp(bh[chunk - 1:chunk, :] - bh)).astype(BF16)
        s_old = s_ref[r, h]
        scores = jnp.where(causal, _dot_nt(qt, kt), 0.0).astype(BF16)
        o_state = _dot(qt, s_old.astype(BF16))
        oi_ref[r, h] = o_state
        dec = jnp.exp(jnp.broadcast_to(bh[chunk - 1:chunk, :], (GLA_HDK, GLA_HDK)).T)
        s_ref[r, h] = s_old * jnp.concatenate([dec, dec], axis=1) + _dot_tn(kd, vh)
        _gla_epilogue(o_state + _dot(scores, vh), r, vs, r_ref, ng_ref, o_ref)


def _gla(z, b_cum, norm_g, s0, chunk, n_valid, rows, sample=None):
    bsz, t, _ = z.shape
    n_chunks = t // chunk
    state = pl.BlockSpec((rows, GLA_HEADS, GLA_HDK, GLA_HDV), lambda b, c: (b, 0, 0, 0))
    in_specs = [pl.BlockSpec((rows, chunk, GLA_DK), lambda b, c: (b, c, 0)),
                pl.BlockSpec((rows, chunk, GLA_DK), lambda b, c: (b, c, 1)),
                pl.BlockSpec((rows, chunk, GLA_DV), lambda b, c: (b, c, 1)),
                pl.BlockSpec((rows, chunk, GLA_DV), lambda b, c: (b, c, 2)),
                pl.BlockSpec((rows, chunk, GLA_DK), lambda b, c: (b, c, 0)),
                pl.BlockSpec((1, GLA_HDV), lambda b, c: (0, 0)),
                state]
    out_specs = [pl.BlockSpec((rows, chunk, GLA_DV), lambda b, c: (b, c, 0)), state]
    out_shape = [jax.ShapeDtypeStruct((bsz, t, GLA_DV), BF16),
                 jax.ShapeDtypeStruct((bsz, GLA_HEADS, GLA_HDK, GLA_HDV), F32)]
    operands = [z, z, z, z, b_cum, norm_g, s0]
    n_new = 0
    if sample is not None:
        dq, kvn, bufs, n_new = sample
        assert dq.shape[0] == (bsz // rows) * n_chunks
        per_step = lambda a: pl.BlockSpec((1,) + a.shape[1:], lambda b, c: (b * n_chunks + c, 0, 0))
        in_specs += [per_step(a) for a in (dq, kvn, *bufs)]
        operands += [dq, kvn, *bufs]
        o_s = jax.ShapeDtypeStruct((dq.shape[0], dq.shape[1], DIL_OUT), BF16)
        out_specs += [per_step(o_s)] + [per_step(a) for a in bufs]
        out_shape += [o_s] + [jax.ShapeDtypeStruct(a.shape, a.dtype) for a in bufs]
    return pl.pallas_call(
        functools.partial(_gla_kernel, chunk=chunk, n_valid=n_valid, n_new_sample=n_new),
        grid=(bsz // rows, n_chunks),
        in_specs=in_specs,
        out_specs=out_specs,
        out_shape=out_shape,
        scratch_shapes=[pltpu.VMEM((rows, GLA_HEADS, GLA_HDK, GLA_HDV), F32),
                        pltpu.VMEM((rows, GLA_HEADS, chunk, GLA_HDV), F32),
                        pltpu.VMEM((chunk, GLA_HDK), F32),
                        pltpu.VMEM((chunk, GLA_HDV), F32)],
        compiler_params=pltpu.CompilerParams(dimension_semantics=("parallel", "arbitrary"),
                                             vmem_limit_bytes=GLA_VMEM_LIMIT_BYTES),
        name="gla",
    )(*operands)


def _alibi_slope(g, head):
    n = DIL_GROUPS * DIL_HEADS
    return jnp.exp((-ALIBI_MAX * _LN2 / n) * (head + (g * DIL_HEADS + 1.0)))


def _pair_slopes(g, hp, rows):
    head = 2.0 * hp.astype(F32) + jnp.where(rows >= Q_TILE, 1.0, 0.0)
    return _alibi_slope(g, head)


def _dil_prompt_kernel(q0_ref, q1_ref, q2_ref, k0_ref, k1_ref, k2_ref, v0_ref, v1_ref, v2_ref,
                       o_ref, wk0_ref, wv0_ref, wk1_ref, wv1_ref, wk2_ref, wv2_ref,
                       qf_ref, og_ref, ld_ref, bf_ref, br_ref, s_ref, p_ref, m_ref, *, seq):
    hp = pl.program_id(1)
    q_refs = (q0_ref, q1_ref, q2_ref)
    k_refs = (k0_ref, k1_ref, k2_ref)
    v_refs = (v0_ref, v1_ref, v2_ref)
    wk_refs = (wk0_ref, wk1_ref, wk2_ref)
    wv_refs = (wv0_ref, wv1_ref, wv2_ref)

    for g in range(DIL_GROUPS):
        qf_ref[g] = q_refs[g][...].astype(F32) * (DIL_HD ** -0.5)

    for g in range(DIL_GROUPS):
        rate = float(DIL_RATES[g])
        rows = lax.broadcasted_iota(jnp.int32, (2 * Q_TILE, Q_TILE), 0)
        cols = lax.broadcasted_iota(jnp.int32, (2 * Q_TILE, Q_TILE), 1)
        dist = (rows & (Q_TILE - 1)) - cols
        slope = _pair_slopes(g, hp, rows)
        bf_ref[g] = jnp.where(dist >= 0, -slope * rate * dist.astype(F32), MASK_VALUE)
        if g < 2:
            rows = lax.broadcasted_iota(jnp.int32, (2 * Q_TILE, 2 * Q_TILE), 0)
            cols = lax.broadcasted_iota(jnp.int32, (2 * Q_TILE, 2 * Q_TILE), 1)
            dist = (rows & (Q_TILE - 1)) + Q_TILE - cols
            slope = _pair_slopes(g, hp, rows)
            ok = jnp.abs(dist - Q_TILE // 2) <= Q_TILE // 2
            br_ref[g] = jnp.where(ok, -slope * rate * dist.astype(F32), MASK_VALUE)

    lane = lax.broadcasted_iota(jnp.int32, (Q_TILE, LANES), 1)
    lo = lane < DIL_HD

    def rows_of(start, size, rate):
        return pl.ds(start, size) if rate == 1 else pl.ds(start, size, stride=rate)

    def scores_stage(g, q_row, k_row, n_keys, slot):
        rate = DIL_RATES[g]
        q2 = qf_ref[g, rows_of(q_row, Q_TILE, rate), :]
        k2 = k_refs[g][rows_of(k_row, n_keys, rate), :].astype(BF16)
        qs = jnp.concatenate([jnp.where(lo, q2, 0.0), jnp.where(lo, 0.0, q2)], axis=0).astype(BF16)
        s_ref[slot, :, 0:n_keys] = _dot_nt(qs, k2)

    def softmax_stage(n_keys, bias, slot):
        s = s_ref[slot, :, 0:n_keys] + bias
        m = jnp.max(s, axis=-1, keepdims=True)
        p_ref[slot, :, 0:n_keys] = jnp.exp(s - m).astype(BF16)
        m_ref[slot] = jnp.where(lo, m[:Q_TILE], m[Q_TILE:])

    def values_stage(g, q_row, k_row, n_keys, slot):
        rate = DIL_RATES[g]
        v2 = v_refs[g][rows_of(k_row, n_keys, rate), :].astype(BF16)
        vo = jnp.concatenate([v2, jnp.ones((n_keys, LANES), BF16)], axis=1)
        r = _dot(p_ref[slot, :, 0:n_keys], vo)
        o2 = jnp.where(lo, r[:Q_TILE, :LANES], r[Q_TILE:, :LANES])
        l2 = jnp.where(lo, r[:Q_TILE, LANES:], r[Q_TILE:, LANES:])
        og_ref[g, rows_of(q_row, Q_TILE, rate), :] = o2 / l2
        ld_ref[g, rows_of(q_row, Q_TILE, rate), :] = m_ref[slot] + jnp.log(l2)

    tiles = []
    for g in range(DIL_GROUPS):
        rate = DIL_RATES[g]
        span = rate * Q_TILE
        for rho in range(rate):
            tiles.append((g, rho, rho, Q_TILE, ("first", g)))
        for n in range(1, seq // span):
            for rho in range(rate):
                tiles.append((g, rho + n * span, rho + (n - 1) * span, 2 * Q_TILE, ("later", g)))
    sets = [tiles[i:i + TILE_SLOTS] for i in range(0, len(tiles), TILE_SLOTS)]

    def slot_of(set_index, u):
        return (set_index % 2) * TILE_SLOTS + u

    for t in range(len(sets) + 2):
        if 0 <= t - 2 < len(sets):
            for u, (g, q_row, k_row, n_keys, _) in enumerate(sets[t - 2]):
                values_stage(g, q_row, k_row, n_keys, slot_of(t - 2, u))
        if 0 <= t - 1 < len(sets):
            for u, (g, _, _, n_keys, (kind, gb)) in enumerate(sets[t - 1]):
                softmax_stage(n_keys, bf_ref[gb] if kind == "first" else br_ref[gb], slot_of(t - 1, u))
        if t < len(sets):
            for u, (g, q_row, k_row, n_keys, _) in enumerate(sets[t]):
                scores_stage(g, q_row, k_row, n_keys, slot_of(t, u))

    def combine(i, carry):
        rs = pl.ds(pl.multiple_of(i * Q_TILE, Q_TILE), Q_TILE)
        ld = [ld_ref[g, rs, :] for g in range(DIL_GROUPS)]
        top = jnp.maximum(jnp.maximum(ld[0], ld[1]), ld[2])
        w = [jnp.exp(x - top) for x in ld]
        num = sum(w[g] * og_ref[g, rs, :] for g in range(DIL_GROUPS))
        o_ref[rs, :] = (num / (w[0] + w[1] + w[2])).astype(o_ref.dtype)
        return carry

    lax.fori_loop(0, seq // Q_TILE, combine, 0)

    for g in range(DIL_GROUPS):
        keep = wk_refs[g].shape[1]
        for src, dst in ((k_refs[g], wk_refs[g]), (v_refs[g], wv_refs[g])):
            for off in range(0, keep, LANES):
                dst[:, off:off + LANES] = src[seq - keep + off:seq - keep + off + LANES, :].T


def _dil_prompt(dq, kv):
    bsz, seq, _ = dq.shape
    pairs = DIL_HEADS * DIL_HD // LANES
    nblk = DIL_WIDTH // LANES
    keeps = [min(w, seq) for w in DIL_WINDOWS]

    def spec(col0):
        return pl.BlockSpec((None, seq, LANES), lambda b, hp, col0=col0: (b, 0, col0 + hp))

    in_specs = ([spec(g * pairs) for g in range(DIL_GROUPS)]
                + [spec(g * pairs) for g in range(DIL_GROUPS)]
                + [spec(nblk + g * pairs) for g in range(DIL_GROUPS)])
    return pl.pallas_call(
        functools.partial(_dil_prompt_kernel, seq=seq),
        grid=(bsz, pairs),
        in_specs=in_specs,
        out_specs=[pl.BlockSpec((None, seq, LANES), lambda b, hp: (b, 0, hp))]
                  + [pl.BlockSpec((None, LANES, w), lambda b, hp: (b, hp, 0)) for w in keeps for _ in range(2)],
        out_shape=[jax.ShapeDtypeStruct((bsz, seq, DIL_OUT), BF16)]
                  + [jax.ShapeDtypeStruct((bsz, DIL_HEADS * DIL_HD, w), F32) for w in keeps for _ in range(2)],
        scratch_shapes=[pltpu.VMEM((DIL_GROUPS, seq, LANES), F32),
                        pltpu.VMEM((DIL_GROUPS, seq, LANES), F32),
                        pltpu.VMEM((DIL_GROUPS, seq, LANES), F32),
                        pltpu.VMEM((DIL_GROUPS, 2 * Q_TILE, Q_TILE), F32),
                        pltpu.VMEM((2, 2 * Q_TILE, 2 * Q_TILE), F32),
                        pltpu.VMEM((2 * TILE_SLOTS, 2 * Q_TILE, 2 * Q_TILE), F32),
                        pltpu.VMEM((2 * TILE_SLOTS, 2 * Q_TILE, 2 * Q_TILE), BF16),
                        pltpu.VMEM((2 * TILE_SLOTS, Q_TILE, LANES), F32)],
        compiler_params=_params("parallel", "arbitrary"),
        name="dil_prompt",
    )(dq, dq, dq, kv, kv, kv, kv, kv, kv)


def _dil_sample_kernel(q_ref, kvn_ref, kb0_ref, vb0_ref, kb1_ref, vb1_ref, kb2_ref, vb2_ref,
                       o_ref, nk0_ref, nv0_ref, nk1_ref, nv1_ref, nk2_ref, nv2_ref, *, n_new):
    kb_refs = (kb0_ref, kb1_ref, kb2_ref)
    vb_refs = (vb0_ref, vb1_ref, vb2_ref)
    nk_refs = (nk0_ref, nk1_ref, nk2_ref)
    nv_refs = (nv0_ref, nv1_ref, nv2_ref)
    pad = q_ref.shape[1]
    nrow = DIL_HEADS * pad
    width = DIL_HEADS * DIL_HD

    lane_head = jnp.right_shift(lax.broadcasted_iota(jnp.int32, (pad, width), 1), DIL_HD.bit_length() - 1)

    def bias_for(g, dist):
        rate = DIL_RATES[g]
        rows = lax.broadcasted_iota(jnp.int32, dist.shape, 0)
        head = sum(jnp.where(rows >= h * pad, 1.0, 0.0) for h in range(1, DIL_HEADS))
        ok = (dist >= 0) & (dist <= DIL_WINDOWS[g]) & ((dist & (rate - 1)) == 0)
        return jnp.where(ok, -_alibi_slope(g, head) * dist.astype(F32), MASK_VALUE)

    sel_r = lax.broadcasted_iota(jnp.int32, (pad, LANES), 0)
    sel_c = lax.broadcasted_iota(jnp.int32, (pad, LANES), 1)
    place = jnp.where((sel_r < n_new) & (sel_c == sel_r + (LANES - n_new)), 1.0, 0.0).astype(BF16)
    tail_lane = lax.broadcasted_iota(jnp.int32, (width, LANES), 1) >= LANES - n_new

    def shifted(buf_t, new_rows):
        length = buf_t.shape[1]
        rolled = pltpu.roll(buf_t, length - n_new, 1)
        new_t = sum(_dot_tn(part, place) for part in _split3(new_rows))
        last = jnp.where(tail_lane, new_t, rolled[:, length - LANES:])
        return rolled, last

    scores, values, transposed = [], [], []
    for g in range(DIL_GROUPS):
        length = kb_refs[g].shape[2]
        cs = slice(g * width, (g + 1) * width)
        vcs = slice(DIL_WIDTH + g * width, DIL_WIDTH + (g + 1) * width)
        qg = q_ref[0, :, cs].astype(F32) * (DIL_HD ** -0.5)
        qs = jnp.concatenate([jnp.where(lane_head == h, qg, 0.0) for h in range(DIL_HEADS)], axis=0).astype(BF16)
        kb, vb = kb_refs[g][0], vb_refs[g][0]
        kn, vn = kvn_ref[0, :, cs], kvn_ref[0, :, vcs]
        rows = lax.broadcasted_iota(jnp.int32, (nrow, length), 0)
        cols = lax.broadcasted_iota(jnp.int32, (nrow, length), 1)
        dist = length + (rows & (pad - 1)) - cols
        scores.append(_dot(qs, kb.astype(BF16)) + bias_for(g, dist))
        values.append(vb.astype(BF16))
        transposed.append(True)
        rows = lax.broadcasted_iota(jnp.int32, (nrow, pad), 0)
        cols = lax.broadcasted_iota(jnp.int32, (nrow, pad), 1)
        dist = jnp.where(cols < n_new, (rows & (pad - 1)) - cols, -1)
        scores.append(_dot_nt(qs, kn.astype(BF16)) + bias_for(g, dist))
        values.append(vn.astype(BF16))
        transposed.append(False)
        for buf, new, out_ref in ((kb, kn, nk_refs[g]), (vb, vn, nv_refs[g])):
            rolled, last = shifted(buf, new)
            if length > LANES:
                out_ref[0, :, 0:length - LANES] = rolled[:, 0:length - LANES]
            out_ref[0, :, length - LANES:length] = last

    top = functools.reduce(jnp.maximum, [jnp.max(s, axis=-1, keepdims=True) for s in scores])
    probs = [jnp.exp(s - top) for s in scores]
    den = sum(jnp.sum(p, axis=-1, keepdims=True) for p in probs)
    acc = sum((_dot_nt if t else _dot)(p.astype(BF16), v)
              for p, v, t in zip(probs, values, transposed)) / den
    out = sum(jnp.where(lane_head == h, acc[h * pad:(h + 1) * pad], 0.0) for h in range(DIL_HEADS))
    o_ref[0] = out.astype(o_ref.dtype)


FFN_COL_CHUNK = D_FF // 2


def _merge_ffn_kernel(x_ref, oa_ref, ob_ref, ga_ref, gb_ref, pa_ref, pb_ref, wo_ref, n2_ref, nf_ref,
                      wg_ref, wu_ref, wd_ref, y_ref):
    pa = _dot(oa_ref[...], pa_ref[...])
    pb = _dot(ob_ref[...], pb_ref[...])
    merged = jax.nn.sigmoid(ga_ref[...].astype(F32)) * pa + jax.nn.sigmoid(gb_ref[...].astype(F32)) * pb
    x1 = x_ref[...] + _dot(merged.astype(BF16), wo_ref[...])
    h = _rms(x1, n2_ref[...]).astype(BF16)
    acc = x1
    for c0 in range(0, D_FF, FFN_COL_CHUNK):
        cs = slice(c0, c0 + FFN_COL_CHUNK)
        gate = _dot(h, wg_ref[:, cs])
        up = _dot(h, wu_ref[:, cs])
        act = (gate * jax.nn.sigmoid(gate) * up).astype(BF16)
        acc = acc + _dot(act, wd_ref[cs, :])
    y_ref[...] = _rms(acc, nf_ref[...])


def _merge_ffn(x, oa, ob, gates, pa, pb, wo, n2, nf, wg, wu, wd, tm):
    m = x.shape[0]
    row = lambda w: pl.BlockSpec((tm, w), lambda i: (i, 0))
    resident = lambda a: pl.BlockSpec(a.shape, lambda i: (0, 0), pipeline_mode=pl.Buffered(1))
    return pl.pallas_call(
        _merge_ffn_kernel,
        grid=(m // tm,),
        in_specs=[row(D_MODEL), row(GLA_DV), row(DIL_OUT), row(D_MODEL),
                  pl.BlockSpec((tm, D_MODEL), lambda i: (i, 1))]
                 + [resident(a) for a in (pa, pb, wo, n2, nf, wg, wu, wd)],
        out_specs=row(D_MODEL),
        out_shape=jax.ShapeDtypeStruct((m, D_MODEL), F32),
        compiler_params=pltpu.CompilerParams(dimension_semantics=("parallel",),
                                             vmem_limit_bytes=MERGE_FFN_VMEM_LIMIT_BYTES),
        name="merge_ffn",
    )(x, oa, ob, gates, gates, pa, pb, wo, n2, nf, wg, wu, wd)


def _prep_weights(w_in, gla_gate_w2):
    w2 = jnp.concatenate([gla_gate_w2[0], jnp.zeros((LANES - GLA_GATE_RANK, GLA_DK), F32)], axis=0)
    return dict(w_in_t=jnp.transpose(w_in[0]).astype(BF16), w2=w2.astype(BF16))


def _project(x2d, norm1_g, gate_b, wts, tm, chunk, n_valid, casts=()):
    groups = [(0, GLA_COLS), (_OFF["dq"][0], DIL_WIDTH), (_OFF["dk"][0], 2 * DIL_WIDTH), (_OFF["ga"][0], 2 * D_MODEL)]
    return _in_proj(x2d, norm1_g, wts["w_in_t"], wts["w2"], gate_b, groups, _OFF["glr"][0],
                    [BF16, BF16, F32, BF16], tm, chunk, n_valid, casts)


def _tail(x2d, o_a, o_b, z_gates, wts, norm2_g, norm_f_g, tm):
    return _merge_ffn(x2d, o_a, o_b, z_gates, wts["pa"], wts["pb"], wts["wo"], norm2_g,
                      norm_f_g.reshape(1, D_MODEL), wts["wg"], wts["wu"], wts["wd"], tm)


def kernel(x_prompt, x_sample, state_gla, state_win0_k, state_win0_v, state_win1_k, state_win1_v,
           state_win2_k, state_win2_v, norm1_g, w_in, gla_gate_w2, gla_gate_b, gla_norm_g,
           proj_a, proj_b, w_out, norm2_g, w_ffn_gate, w_ffn_up, w_ffn_down, norm_f_g):
    wts = _prep_weights(w_in, gla_gate_w2)
    bp, seq, _ = x_prompt.shape
    bs, n_new, _ = x_sample.shape
    width = DIL_HEADS * DIL_HD

    xp = x_prompt.reshape(bp * seq, D_MODEL)
    later = dict(pa=proj_a[0], pb=proj_b[0], wo=w_out[0], wg=w_ffn_gate[0], wu=w_ffn_up[0], wd=w_ffn_down[0])
    z_gla, z_dq, z_kv, z_gates, b_cum, *cast = _project(xp, norm1_g, gla_gate_b, wts, ROW_BLOCK, GLA_CHUNK,
                                                        GLA_CHUNK, tuple(later.values()))
    wts.update(zip(later.keys(), cast))
    xs = jnp.pad(x_sample, ((0, 0), (0, SAMPLE_PAD - n_new), (0, 0))).reshape(bs * SAMPLE_PAD, D_MODEL)
    zs_gla, zs_dq, zs_kv, zs_gates, bs_cum = _project(xs, norm1_g, gla_gate_b, wts, bs * SAMPLE_PAD,
                                                      SAMPLE_PAD, n_new)

    bufs = [jnp.transpose(a[0], (0, 2, 3, 1)).reshape(bs, width, a.shape[2]) for a in
            (state_win0_k, state_win0_v, state_win1_k, state_win1_v, state_win2_k, state_win2_v)]
    sample = (zs_dq.reshape(bs, SAMPLE_PAD, DIL_WIDTH), zs_kv.reshape(bs, SAMPLE_PAD, 2 * DIL_WIDTH), bufs, n_new)
    s0 = jnp.zeros((bp, GLA_HEADS, GLA_HDK, GLA_HDV), F32)
    o_a, gla_p, os_b, *win_s = _gla(z_gla.reshape(bp, seq, GLA_COLS), b_cum.reshape(bp, seq, GLA_DK), gla_norm_g,
                                    s0, GLA_CHUNK, GLA_CHUNK, GLA_ROWS_PROMPT, sample)

    o_b, *win_p = _dil_prompt(z_dq.reshape(bp, seq, DIL_WIDTH), z_kv.reshape(bp, seq, 2 * DIL_WIDTH))
    y_prompt = _tail(xp, o_a.reshape(bp * seq, GLA_DV), o_b.reshape(bp * seq, DIL_OUT), z_gates, wts,
                     norm2_g, norm_f_g, ROW_BLOCK).reshape(bp, seq, D_MODEL)
    win_p = [jnp.transpose(a.reshape(bp, DIL_HEADS, DIL_HD, a.shape[2]), (0, 3, 1, 2))[None] for a in win_p]

    os_a, gla_s = _gla(zs_gla.reshape(bs, SAMPLE_PAD, GLA_COLS), bs_cum.reshape(bs, SAMPLE_PAD, GLA_DK),
                       gla_norm_g, state_gla[0], SAMPLE_PAD, n_new, GLA_ROWS_SAMPLE)
    ys = _tail(xs, os_a.reshape(bs * SAMPLE_PAD, GLA_DV), os_b.reshape(bs * SAMPLE_PAD, DIL_OUT), zs_gates, wts,
               norm2_g, norm_f_g, bs * SAMPLE_PAD)
    y_sample = ys.reshape(bs, SAMPLE_PAD, D_MODEL)[:, :n_new]
    win_s = [jnp.transpose(a.reshape(bs, DIL_HEADS, DIL_HD, a.shape[2]), (0, 3, 1, 2))[None] for a in win_s]

    return (y_prompt, y_sample, gla_p[None], *win_p, gla_s[None], *win_s)
```

```python
import functools

import jax
import jax.numpy as jnp
from jax import lax
from jax.experimental import pallas as pl
from jax.experimental.pallas import tpu as pltpu

F32 = jnp.float32
BF16 = jnp.bfloat16

D_MODEL = 1024
GLA_HEADS = 4
GLA_DK = 512
GLA_DV = 1024
GLA_HDK = 128
GLA_HDV = 256
GLA_GATE_RANK = 16
GLA_TAU = 16.0
DIL_WINDOWS = (128, 512, 2048)
DIL_RATES = (1, 4, 16)
DIL_GROUPS = 3
DIL_HEADS = 4
DIL_HD = 64
DIL_WIDTH = 768
DIL_OUT = 256
ALIBI_MAX = 8.0
D_FF = 2816
RMS_EPS = 1e-6

LANES = 128
SUBLANES = 8
BF16_SUBLANES = 16
Q_TILE = 128
TILE_SLOTS = 2
GLA_CHUNK = 128
GLA_ROWS_PROMPT = 4
GLA_ROWS_SAMPLE = 4
SAMPLE_PAD = 8
MASK_VALUE = -1e30
VMEM_LIMIT_BYTES = 48 * 1024 * 1024
IN_PROJ_VMEM_LIMIT_BYTES = 56 * 1024 * 1024
MERGE_FFN_VMEM_LIMIT_BYTES = 60 * 1024 * 1024
GLA_VMEM_LIMIT_BYTES = 56 * 1024 * 1024
_LN2 = 0.6931471805599453

_OFF = {}
_o = 0
for _name, _w in (("gq", GLA_DK), ("gk", GLA_DK), ("gv", GLA_DV), ("gr", GLA_DV), ("glr", GLA_GATE_RANK),
                  ("dq", DIL_WIDTH), ("dk", DIL_WIDTH), ("dv", DIL_WIDTH), ("ga", D_MODEL), ("gb", D_MODEL)):
    _OFF[_name] = (_o, _o + _w)
    _o += _w
GLA_COLS = 2 * GLA_DK + 2 * GLA_DV
GLA_SAFE_LOG_DECAY = 20.0
ROW_BLOCK = 512


def _params(*sem):
    return pltpu.CompilerParams(dimension_semantics=sem, vmem_limit_bytes=VMEM_LIMIT_BYTES)


def _dot(a, b):
    return jnp.dot(a, b, preferred_element_type=F32)


def _dot_nt(a, b):
    return lax.dot_general(a, b, (((1,), (1,)), ((), ())), preferred_element_type=F32)


def _dot_tn(a, b):
    return lax.dot_general(a, b, (((0,), (0,)), ((), ())), preferred_element_type=F32)


def _rms(x, g):
    return x * lax.rsqrt(jnp.mean(x * x, axis=-1, keepdims=True) + RMS_EPS) * g


IN_PROJ_COL_CHUNK = 1024


def _split3(x):
    x1 = x.astype(BF16)
    r1 = x - x1.astype(F32)
    x2 = r1.astype(BF16)
    x3 = (r1 - x2.astype(F32)).astype(BF16)
    return x1, x2, x3


def _in_proj_kernel(x_ref, g_ref, wt_ref, w2_ref, gb_ref, *refs, groups, low_rank_col, chunk, n_valid, n_casts):
    cast_in, out_refs = refs[:n_casts], refs[n_casts:]
    for src, dst in zip(cast_in, out_refs[len(out_refs) - n_casts:]):
        dst[...] = src[...].astype(dst.dtype)

    h = _rms(x_ref[...], g_ref[...]).astype(BF16)
    b_ref = out_refs[len(groups)]
    tm = x_ref.shape[0]

    for o_ref, (c_first, n) in zip(out_refs, groups):
        for c0 in range(0, n, IN_PROJ_COL_CHUNK):
            c1 = min(c0 + IN_PROJ_COL_CHUNK, n)
            o_ref[:, c0:c1] = _dot_nt(h, wt_ref[c_first + c0:c_first + c1, :]).astype(o_ref.dtype)

    low_rank = _dot_nt(h, wt_ref[low_rank_col:low_rank_col + LANES, :]).astype(BF16)
    gate = _dot(low_rank, w2_ref[...]) + gb_ref[...]
    log_a = (jnp.minimum(gate, 0.0) - jnp.log(1.0 + jnp.exp(-jnp.abs(gate)))) * (1.0 / GLA_TAU)
    if n_valid < chunk:
        tok = lax.broadcasted_iota(jnp.int32, log_a.shape, 0) & (chunk - 1)
        log_a = jnp.where(tok < n_valid, log_a, 0.0)
    span = min(tm, LANES)
    row = lax.broadcasted_iota(jnp.int32, (span, span), 0)
    col = lax.broadcasted_iota(jnp.int32, (span, span), 1)
    same_chunk = (row & -chunk) == (col & -chunk) if chunk < span else True
    tril = jnp.where((row >= col) & same_chunk, 1.0, 0.0).astype(BF16)
    for r0 in range(0, tm, span):
        parts = _split3(log_a[r0:r0 + span])[:2]
        b_ref[r0:r0 + span, :] = sum(_dot(tril, p) for p in parts)


def _cast_blocks(a, steps):
    count = max(c for c in range(1, steps + 1) if a.shape[0] % (c * BF16_SUBLANES) == 0)
    return a.shape[0] // count, count


def _in_proj(x, g, w_t, w2, gate_b, groups, low_rank_col, out_dtypes, tm, chunk, n_valid, casts=()):
    m, k = x.shape
    steps = m // tm
    assert chunk & (chunk - 1) == 0 and (chunk % LANES == 0 or LANES % chunk == 0) and tm % chunk == 0
    assert all(c % BF16_SUBLANES == 0 and n % LANES == 0 for c, n in groups) and low_rank_col % BF16_SUBLANES == 0
    resident = lambda a: pl.BlockSpec(a.shape, lambda i: (0, 0), pipeline_mode=pl.Buffered(1))
    out_widths = [n for _, n in groups] + [GLA_DK]

    def cast_spec(a):
        rows, count = _cast_blocks(a, steps)
        return pl.BlockSpec((rows, a.shape[1]), lambda i, count=count: (jnp.minimum(i, count - 1), 0))

    return pl.pallas_call(
        functools.partial(_in_proj_kernel, groups=tuple(groups), low_rank_col=low_rank_col,
                          chunk=chunk, n_valid=n_valid, n_casts=len(casts)),
        grid=(steps,),
        in_specs=([pl.BlockSpec((tm, k), lambda i: (i, 0))] + [resident(a) for a in (g, w_t, w2, gate_b)]
                  + [cast_spec(a) for a in casts]),
        out_specs=[pl.BlockSpec((tm, n), lambda i: (i, 0)) for n in out_widths] + [cast_spec(a) for a in casts],
        out_shape=([jax.ShapeDtypeStruct((m, n), dt) for n, dt in zip(out_widths, list(out_dtypes) + [F32])]
                   + [jax.ShapeDtypeStruct(a.shape, BF16) for a in casts]),
        compiler_params=pltpu.CompilerParams(dimension_semantics=("arbitrary",),
                                             vmem_limit_bytes=IN_PROJ_VMEM_LIMIT_BYTES),
        name="in_proj",
    )(x, g, w_t, w2, gate_b, *casts)


N_SAMPLE_IN = 2 + 2 * DIL_GROUPS
N_SAMPLE_OUT = 1 + 2 * DIL_GROUPS


def _gla_kernel(*refs, chunk, n_valid, n_new_sample):
    q_ref, k_ref, v_ref, r_ref, b_ref, ng_ref, s0_ref = refs[:7]
    refs = refs[7:]
    sample_in = ()
    if n_new_sample:
        sample_in, refs = refs[:N_SAMPLE_IN], refs[N_SAMPLE_IN:]
    o_ref, sout_ref = refs[:2]
    refs = refs[2:]
    sample_out = ()
    if n_new_sample:
        sample_out, refs = refs[:N_SAMPLE_OUT], refs[N_SAMPLE_OUT:]
    s_ref, oi_ref, kf_ref, vf_ref = refs

    c = pl.program_id(1)
    rows = q_ref.shape[0]

    @pl.when(c == 0)
    def _():
        s_ref[...] = s0_ref[...]

    for r in range(rows):
        _gla_chunk(r, q_ref, k_ref, v_ref, r_ref, b_ref, ng_ref, o_ref, s_ref, oi_ref, chunk, n_valid)
    if n_new_sample:
        _dil_sample_kernel(*sample_in, *sample_out, n_new=n_new_sample)

    for r in range(rows):
        @pl.when(jnp.min(b_ref[r, chunk - 1:chunk, :]) < -GLA_SAFE_LOG_DECAY)
        def _(r=r):
            _gla_chunk_exact_intra(r, q_ref, k_ref, v_ref, r_ref, b_ref, ng_ref, o_ref, oi_ref, kf_ref, vf_ref,
                                   chunk, n_valid)

    @pl.when(c == pl.num_programs(1) - 1)
    def _():
        sout_ref[...] = s_ref[...]


def _gla_epilogue(o, r, vs, r_ref, ng_ref, o_ref):
    gr = r_ref[r, :, vs].astype(F32)
    o_ref[r, :, vs] = (_rms(o, ng_ref[...]) * (gr * jax.nn.sigmoid(gr))).astype(o_ref.dtype)


def _gla_values(r, vs, v_ref, chunk, n_valid):
    vh = v_ref[r, :, vs]
    if n_valid < chunk:
        tok = lax.broadcasted_iota(jnp.int32, vh.shape, 0)
        vh = jnp.where(tok < n_valid, vh, jnp.zeros_like(vh))
    return vh


def _gla_chunk_exact_intra(r, q_ref, k_ref, v_ref, r_ref, b_ref, ng_ref, o_ref, oi_ref, kf_ref, vf_ref,
                           chunk, n_valid):
    tok = lax.broadcasted_iota(jnp.int32, (chunk, 1), 0)
    for h in range(GLA_HEADS):
        ks = slice(h * GLA_HDK, (h + 1) * GLA_HDK)
        vs = slice(h * GLA_HDV, (h + 1) * GLA_HDV)
        bh = b_ref[r, :, ks]
        qh = q_ref[r, :, ks].astype(F32) * (GLA_HDK ** -0.5)
        kf_ref[...] = k_ref[r, :, ks].astype(F32)
        vf_ref[...] = _gla_values(r, vs, v_ref, chunk, n_valid).astype(F32)

        def eight_keys(i, acc, bh=bh, qh=qh, ks=ks):
            rows8 = pl.ds(pl.multiple_of(i * SUBLANES, SUBLANES), SUBLANES)
            b8, k8, v8 = b_ref[r, rows8, ks], kf_ref[rows8, :], vf_ref[rows8, :]
            for j in range(SUBLANES):
                s = i * SUBLANES + j
                decay = jnp.exp(jnp.where(tok >= s, bh - b8[j:j + 1], MASK_VALUE))
                w = jnp.sum(qh * decay * k8[j:j + 1], axis=-1, keepdims=True)
                acc = acc + w * v8[j:j + 1]
            return acc

        intra = lax.fori_loop(0, chunk // SUBLANES, eight_keys, jnp.zeros((chunk, GLA_HDV), F32))
        _gla_epilogue(oi_ref[r, h] + intra, r, vs, r_ref, ng_ref, o_ref)


def _gla_chunk(r, q_ref, k_ref, v_ref, r_ref, b_ref, ng_ref, o_ref, s_ref, oi_ref, chunk, n_valid):
    row = lax.broadcasted_iota(jnp.int32, (chunk, chunk), 0)
    col = lax.broadcasted_iota(jnp.int32, (chunk, chunk), 1)
    causal = row >= col
    b = b_ref[r]

    for h in range(GLA_HEADS):
        ks = slice(h * GLA_HDK, (h + 1) * GLA_HDK)
        vs = slice(h * GLA_HDV, (h + 1) * GLA_HDV)
        bh = b[:, ks]
        qh = q_ref[r, :, ks].astype(F32) * (GLA_HDK ** -0.5)
        kh = k_ref[r, :, ks].astype(F32)
        vh = _gla_values(r, vs, v_ref, chunk, n_valid)
        qt = (qh * jnp.exp(bh)).astype(BF16)
        kt = (kh * jnp.exp(-bh)).astype(BF16)
        kd = (kh * jnp.exp(bh[chunk - 1:chunk, :] - bh)).astype(BF16)
        s_old = s_ref[r, h]
        scores = jnp.where(causal, _dot_nt(qt, kt), 0.0).astype(BF16)
        o_state = _dot(qt, s_old.astype(BF16))
        oi_ref[r, h] = o_state
        dec = jnp.exp(jnp.broadcast_to(bh[chunk - 1:chunk, :], (GLA_HDK, GLA_HDK)).T)
        s_ref[r, h] = s_old * jnp.concatenate([dec, dec], axis=1) + _dot_tn(kd, vh)
        _gla_epilogue(o_state + _dot(scores, vh), r, vs, r_ref, ng_ref, o_ref)


def _gla(z, b_cum, norm_g, s0, chunk, n_valid, rows, sample=None):
    bsz, t, _ = z.shape
    n_chunks = t // chunk
    state = pl.BlockSpec((rows, GLA_HEADS, GLA_HDK, GLA_HDV), lambda b, c: (b, 0, 0, 0))
    in_specs = [pl.BlockSpec((rows, chunk, GLA_DK), lambda b, c: (b, c, 0)),
                pl.BlockSpec((rows, chunk, GLA_DK), lambda b, c: (b, c, 1)),
                pl.BlockSpec((rows, chunk, GLA_DV), lambda b, c: (b, c, 1)),
                pl.BlockSpec((rows, chunk, GLA_DV), lambda b, c: (b, c, 2)),
                pl.BlockSpec((rows, chunk, GLA_DK), lambda b, c: (b, c, 0)),
                pl.BlockSpec((1, GLA_HDV), lambda b, c: (0, 0)),
                state]
    out_specs = [pl.BlockSpec((rows, chunk, GLA_DV), lambda b, c: (b, c, 0)), state]
    out_shape = [jax.ShapeDtypeStruct((bsz, t, GLA_DV), BF16),
                 jax.ShapeDtypeStruct((bsz, GLA_HEADS, GLA_HDK, GLA_HDV), F32)]
    operands = [z, z, z, z, b_cum, norm_g, s0]
    n_new = 0
    if sample is not None:
        dq, kvn, bufs, n_new = sample
        assert dq.shape[0] == (bsz // rows) * n_chunks
        per_step = lambda a: pl.BlockSpec((1,) + a.shape[1:], lambda b, c: (b * n_chunks + c, 0, 0))
        in_specs += [per_step(a) for a in (dq, kvn, *bufs)]
        operands += [dq, kvn, *bufs]
        o_s = jax.ShapeDtypeStruct((dq.shape[0], dq.shape[1], DIL_OUT), BF16)
        out_specs += [per_step(o_s)] + [per_step(a) for a in bufs]
        out_shape += [o_s] + [jax.ShapeDtypeStruct(a.shape, a.dtype) for a in bufs]
    return pl.pallas_call(
        functools.partial(_gla_kernel, chunk=chunk, n_valid=n_valid, n_new_sample=n_new),
        grid=(bsz // rows, n_chunks),
        in_specs=in_specs,
        out_specs=out_specs,
        out_shape=out_shape,
        scratch_shapes=[pltpu.VMEM((rows, GLA_HEADS, GLA_HDK, GLA_HDV), F32),
                        pltpu.VMEM((rows, GLA_HEADS, chunk, GLA_HDV), F32),
                        pltpu.VMEM((chunk, GLA_HDK), F32),
                        pltpu.VMEM((chunk, GLA_HDV), F32)],
        compiler_params=pltpu.CompilerParams(dimension_semantics=("parallel", "arbitrary"),
                                             vmem_limit_bytes=GLA_VMEM_LIMIT_BYTES),
        name="gla",
    )(*operands)


def _alibi_slope(g, head):
    n = DIL_GROUPS * DIL_HEADS
    return jnp.exp((-ALIBI_MAX * _LN2 / n) * (head + (g * DIL_HEADS + 1.0)))


def _pair_slopes(g, hp, rows):
    head = 2.0 * hp.astype(F32) + jnp.where(rows >= Q_TILE, 1.0, 0.0)
    return _alibi_slope(g, head)


def _dil_prompt_kernel(q0_ref, q1_ref, q2_ref, k0_ref, k1_ref, k2_ref, v0_ref, v1_ref, v2_ref,
                       o_ref, wk0_ref, wv0_ref, wk1_ref, wv1_ref, wk2_ref, wv2_ref,
                       qf_ref, og_ref, ld_ref, bf_ref, br_ref, s_ref, p_ref, m_ref, *, seq):
    hp = pl.program_id(1)
    q_refs = (q0_ref, q1_ref, q2_ref)
    k_refs = (k0_ref, k1_ref, k2_ref)
    v_refs = (v0_ref, v1_ref, v2_ref)
    wk_refs = (wk0_ref, wk1_ref, wk2_ref)
    wv_refs = (wv0_ref, wv1_ref, wv2_ref)

    for g in range(DIL_GROUPS):
        qf_ref[g] = q_refs[g][...].astype(F32) * (DIL_HD ** -0.5)

    for g in range(DIL_GROUPS):
        rate = float(DIL_RATES[g])
        rows = lax.broadcasted_iota(jnp.int32, (2 * Q_TILE, Q_TILE), 0)
        cols = lax.broadcasted_iota(jnp.int32, (2 * Q_TILE, Q_TILE), 1)
        dist = (rows & (Q_TILE - 1)) - cols
        slope = _pair_slopes(g, hp, rows)
        bf_ref[g] = jnp.where(dist >= 0, -slope * rate * dist.astype(F32), MASK_VALUE)
        if g < 2:
            rows = lax.broadcasted_iota(jnp.int32, (2 * Q_TILE, 2 * Q_TILE), 0)
            cols = lax.broadcasted_iota(jnp.int32, (2 * Q_TILE, 2 * Q_TILE), 1)
            dist = (rows & (Q_TILE - 1)) + Q_TILE - cols
            slope = _pair_slopes(g, hp, rows)
            ok = jnp.abs(dist - Q_TILE // 2) <= Q_TILE // 2
            br_ref[g] = jnp.where(ok, -slope * rate * dist.astype(F32), MASK_VALUE)

    lane = lax.broadcasted_iota(jnp.int32, (Q_TILE, LANES), 1)
    lo = lane < DIL_HD

    def rows_of(start, size, rate):
        return pl.ds(start, size) if rate == 1 else pl.ds(start, size, stride=rate)

    def scores_stage(g, q_row, k_row, n_keys, slot):
        rate = DIL_RATES[g]
        q2 = qf_ref[g, rows_of(q_row, Q_TILE, rate), :]
        k2 = k_refs[g][rows_of(k_row, n_keys, rate), :].astype(BF16)
        qs = jnp.concatenate([jnp.where(lo, q2, 0.0), jnp.where(lo, 0.0, q2)], axis=0).astype(BF16)
        s_ref[slot, :, 0:n_keys] = _dot_nt(qs, k2)

    def softmax_stage(n_keys, bias, slot):
        s = s_ref[slot, :, 0:n_keys] + bias
        m = jnp.max(s, axis=-1, keepdims=True)
        p_ref[slot, :, 0:n_keys] = jnp.exp(s - m).astype(BF16)
        m_ref[slot] = jnp.where(lo, m[:Q_TILE], m[Q_TILE:])

    def values_stage(g, q_row, k_row, n_keys, slot):
        rate = DIL_RATES[g]
        v2 = v_refs[g][rows_of(k_row, n_keys, rate), :].astype(BF16)
        vo = jnp.concatenate([v2, jnp.ones((n_keys, LANES), BF16)], axis=1)
        r = _dot(p_ref[slot, :, 0:n_keys], vo)
        o2 = jnp.where(lo, r[:Q_TILE, :LANES], r[Q_TILE:, :LANES])
        l2 = jnp.where(lo, r[:Q_TILE, LANES:], r[Q_TILE:, LANES:])
        og_ref[g, rows_of(q_row, Q_TILE, rate), :] = o2 / l2
        ld_ref[g, rows_of(q_row, Q_TILE, rate), :] = m_ref[slot] + jnp.log(l2)

    tiles = []
    for g in range(DIL_GROUPS):
        rate = DIL_RATES[g]
        span = rate * Q_TILE
        for rho in range(rate):
            tiles.append((g, rho, rho, Q_TILE, ("first", g)))
        for n in range(1, seq // span):
            for rho in range(rate):
                tiles.append((g, rho + n * span, rho + (n - 1) * span, 2 * Q_TILE, ("later", g)))
    sets = [tiles[i:i + TILE_SLOTS] for i in range(0, len(tiles), TILE_SLOTS)]

    def slot_of(set_index, u):
        return (set_index % 2) * TILE_SLOTS + u

    for t in range(len(sets) + 2):
        if 0 <= t - 2 < len(sets):
            for u, (g, q_row, k_row, n_keys, _) in enumerate(sets[t - 2]):
                values_stage(g, q_row, k_row, n_keys, slot_of(t - 2, u))
        if 0 <= t - 1 < len(sets):
            for u, (g, _, _, n_keys, (kind, gb)) in enumerate(sets[t - 1]):
                softmax_stage(n_keys, bf_ref[gb] if kind == "first" else br_ref[gb], slot_of(t - 1, u))
        if t < len(sets):
            for u, (g, q_row, k_row, n_keys, _) in enumerate(sets[t]):
                scores_stage(g, q_row, k_row, n_keys, slot_of(t, u))

    for r0 in range(0, seq, Q_TILE):
        rs = slice(r0, r0 + Q_TILE)
        ld = [ld_ref[g, rs, :] for g in range(DIL_GROUPS)]
        top = jnp.maximum(jnp.maximum(ld[0], ld[1]), ld[2])
        w = [jnp.exp(x - top) for x in ld]
        num = sum(w[g] * og_ref[g, rs, :] for g in range(DIL_GROUPS))
        o_ref[rs, :] = (num / (w[0] + w[1] + w[2])).astype(o_ref.dtype)

    for g in range(DIL_GROUPS):
        keep = wk_refs[g].shape[1]
        for src, dst in ((k_refs[g], wk_refs[g]), (v_refs[g], wv_refs[g])):
            for off in range(0, keep, LANES):
                dst[:, off:off + LANES] = src[seq - keep + off:seq - keep + off + LANES, :].T


def _dil_prompt(dq, kv):
    bsz, seq, _ = dq.shape
    pairs = DIL_HEADS * DIL_HD // LANES
    nblk = DIL_WIDTH // LANES
    keeps = [min(w, seq) for w in DIL_WINDOWS]

    def spec(col0):
        return pl.BlockSpec((None, seq, LANES), lambda b, hp, col0=col0: (b, 0, col0 + hp))

    in_specs = ([spec(g * pairs) for g in range(DIL_GROUPS)]
                + [spec(g * pairs) for g in range(DIL_GROUPS)]
                + [spec(nblk + g * pairs) for g in range(DIL_GROUPS)])
    return pl.pallas_call(
        functools.partial(_dil_prompt_kernel, seq=seq),
        grid=(bsz, pairs),
        in_specs=in_specs,
        out_specs=[pl.BlockSpec((None, seq, LANES), lambda b, hp: (b, 0, hp))]
                  + [pl.BlockSpec((None, LANES, w), lambda b, hp: (b, hp, 0)) for w in keeps for _ in range(2)],
        out_shape=[jax.ShapeDtypeStruct((bsz, seq, DIL_OUT), BF16)]
                  + [jax.ShapeDtypeStruct((bsz, DIL_HEADS * DIL_HD, w), F32) for w in keeps for _ in range(2)],
        scratch_shapes=[pltpu.VMEM((DIL_GROUPS, seq, LANES), F32),
                        pltpu.VMEM((DIL_GROUPS, seq, LANES), F32),
                        pltpu.VMEM((DIL_GROUPS, seq, LANES), F32),
                        pltpu.VMEM((DIL_GROUPS, 2 * Q_TILE, Q_TILE), F32),
                        pltpu.VMEM((2, 2 * Q_TILE, 2 * Q_TILE), F32),
                        pltpu.VMEM((2 * TILE_SLOTS, 2 * Q_TILE, 2 * Q_TILE), F32),
                        pltpu.VMEM((2 * TILE_SLOTS, 2 * Q_TILE, 2 * Q_TILE), BF16),
                        pltpu.VMEM((2 * TILE_SLOTS, Q_TILE, LANES), F32)],
        compiler_params=_params("parallel", "arbitrary"),
        name="dil_prompt",
    )(dq, dq, dq, kv, kv, kv, kv, kv, kv)


def _dil_sample_kernel(q_ref, kvn_ref, kb0_ref, vb0_ref, kb1_ref, vb1_ref, kb2_ref, vb2_ref,
                       o_ref, nk0_ref, nv0_ref, nk1_ref, nv1_ref, nk2_ref, nv2_ref, *, n_new):
    kb_refs = (kb0_ref, kb1_ref, kb2_ref)
    vb_refs = (vb0_ref, vb1_ref, vb2_ref)
    nk_refs = (nk0_ref, nk1_ref, nk2_ref)
    nv_refs = (nv0_ref, nv1_ref, nv2_ref)
    pad = q_ref.shape[1]
    nrow = DIL_HEADS * pad
    width = DIL_HEADS * DIL_HD

    lane_head = jnp.right_shift(lax.broadcasted_iota(jnp.int32, (pad, width), 1), DIL_HD.bit_length() - 1)

    def bias_for(g, dist):
        rate = DIL_RATES[g]
        rows = lax.broadcasted_iota(jnp.int32, dist.shape, 0)
        head = sum(jnp.where(rows >= h * pad, 1.0, 0.0) for h in range(1, DIL_HEADS))
        ok = (dist >= 0) & (dist <= DIL_WINDOWS[g]) & ((dist & (rate - 1)) == 0)
        return jnp.where(ok, -_alibi_slope(g, head) * dist.astype(F32), MASK_VALUE)

    sel_r = lax.broadcasted_iota(jnp.int32, (pad, LANES), 0)
    sel_c = lax.broadcasted_iota(jnp.int32, (pad, LANES), 1)
    place = jnp.where((sel_r < n_new) & (sel_c == sel_r + (LANES - n_new)), 1.0, 0.0).astype(BF16)
    tail_lane = lax.broadcasted_iota(jnp.int32, (width, LANES), 1) >= LANES - n_new

    def shifted(buf_t, new_rows):
        length = buf_t.shape[1]
        rolled = pltpu.roll(buf_t, length - n_new, 1)
        new_t = sum(_dot_tn(part, place) for part in _split3(new_rows))
        last = jnp.where(tail_lane, new_t, rolled[:, length - LANES:])
        return rolled, last

    scores, values, transposed = [], [], []
    for g in range(DIL_GROUPS):
        length = kb_refs[g].shape[2]
        cs = slice(g * width, (g + 1) * width)
        vcs = slice(DIL_WIDTH + g * width, DIL_WIDTH + (g + 1) * width)
        qg = q_ref[0, :, cs].astype(F32) * (DIL_HD ** -0.5)
        qs = jnp.concatenate([jnp.where(lane_head == h, qg, 0.0) for h in range(DIL_HEADS)], axis=0).astype(BF16)
        kb, vb = kb_refs[g][0], vb_refs[g][0]
        kn, vn = kvn_ref[0, :, cs], kvn_ref[0, :, vcs]
        rows = lax.broadcasted_iota(jnp.int32, (nrow, length), 0)
        cols = lax.broadcasted_iota(jnp.int32, (nrow, length), 1)
        dist = length + (rows & (pad - 1)) - cols
        scores.append(_dot(qs, kb.astype(BF16)) + bias_for(g, dist))
        values.append(vb.astype(BF16))
        transposed.append(True)
        rows = lax.broadcasted_iota(jnp.int32, (nrow, pad), 0)
        cols = lax.broadcasted_iota(jnp.int32, (nrow, pad), 1)
        dist = jnp.where(cols < n_new, (rows & (pad - 1)) - cols, -1)
        scores.append(_dot_nt(qs, kn.astype(BF16)) + bias_for(g, dist))
        values.append(vn.astype(BF16))
        transposed.append(False)
        for buf, new, out_ref in ((kb, kn, nk_refs[g]), (vb, vn, nv_refs[g])):
            rolled, last = shifted(buf, new)
            if length > LANES:
                out_ref[0, :, 0:length - LANES] = rolled[:, 0:length - LANES]
            out_ref[0, :, length - LANES:length] = last

    top = functools.reduce(jnp.maximum, [jnp.max(s, axis=-1, keepdims=True) for s in scores])
    probs = [jnp.exp(s - top) for s in scores]
    den = sum(jnp.sum(p, axis=-1, keepdims=True) for p in probs)
    acc = sum((_dot_nt if t else _dot)(p.astype(BF16), v)
              for p, v, t in zip(probs, values, transposed)) / den
    out = sum(jnp.where(lane_head == h, acc[h * pad:(h + 1) * pad], 0.0) for h in range(DIL_HEADS))
    o_ref[0] = out.astype(o_ref.dtype)


FFN_COL_CHUNK = D_FF // 2


def _merge_ffn_kernel(x_ref, oa_ref, ob_ref, ga_ref, gb_ref, pa_ref, pb_ref, wo_ref, n2_ref, nf_ref,
                      wg_ref, wu_ref, wd_ref, y_ref):
    pa = _dot(oa_ref[...], pa_ref[...])
    pb = _dot(ob_ref[...], pb_ref[...])
    merged = jax.nn.sigmoid(ga_ref[...].astype(F32)) * pa + jax.nn.sigmoid(gb_ref[...].astype(F32)) * pb
    x1 = x_ref[...] + _dot(merged.astype(BF16), wo_ref[...])
    h = _rms(x1, n2_ref[...]).astype(BF16)
    acc = x1
    for c0 in range(0, D_FF, FFN_COL_CHUNK):
        cs = slice(c0, c0 + FFN_COL_CHUNK)
        gate = _dot(h, wg_ref[:, cs])
        up = _dot(h, wu_ref[:, cs])
        act = (gate * jax.nn.sigmoid(gate) * up).astype(BF16)
        acc = acc + _dot(act, wd_ref[cs, :])
    y_ref[...] = _rms(acc, nf_ref[...])


def _merge_ffn(x, oa, ob, gates, pa, pb, wo, n2, nf, wg, wu, wd, tm):
    m = x.shape[0]
    row = lambda w: pl.BlockSpec((tm, w), lambda i: (i, 0))
    resident = lambda a: pl.BlockSpec(a.shape, lambda i: (0, 0), pipeline_mode=pl.Buffered(1))
    return pl.pallas_call(
        _merge_ffn_kernel,
        grid=(m // tm,),
        in_specs=[row(D_MODEL), row(GLA_DV), row(DIL_OUT), row(D_MODEL),
                  pl.BlockSpec((tm, D_MODEL), lambda i: (i, 1))]
                 + [resident(a) for a in (pa, pb, wo, n2, nf, wg, wu, wd)],
        out_specs=row(D_MODEL),
        out_shape=jax.ShapeDtypeStruct((m, D_MODEL), F32),
        compiler_params=pltpu.CompilerParams(dimension_semantics=("parallel",),
                                             vmem_limit_bytes=MERGE_FFN_VMEM_LIMIT_BYTES),
        name="merge_ffn",
    )(x, oa, ob, gates, gates, pa, pb, wo, n2, nf, wg, wu, wd)


def _prep_weights(w_in, gla_gate_w2):
    w2 = jnp.concatenate([gla_gate_w2[0], jnp.zeros((LANES - GLA_GATE_RANK, GLA_DK), F32)], axis=0)
    return dict(w_in_t=jnp.transpose(w_in[0]).astype(BF16), w2=w2.astype(BF16))


def _project(x2d, norm1_g, gate_b, wts, tm, chunk, n_valid, casts=()):
    groups = [(0, GLA_COLS), (_OFF["dq"][0], DIL_WIDTH), (_OFF["dk"][0], 2 * DIL_WIDTH), (_OFF["ga"][0], 2 * D_MODEL)]
    return _in_proj(x2d, norm1_g, wts["w_in_t"], wts["w2"], gate_b, groups, _OFF["glr"][0],
                    [BF16, BF16, F32, BF16], tm, chunk, n_valid, casts)


def _tail(x2d, o_a, o_b, z_gates, wts, norm2_g, norm_f_g, tm):
    return _merge_ffn(x2d, o_a, o_b, z_gates, wts["pa"], wts["pb"], wts["wo"], norm2_g,
                      norm_f_g.reshape(1, D_MODEL), wts["wg"], wts["wu"], wts["wd"], tm)


def kernel(x_prompt, x_sample, state_gla, state_win0_k, state_win0_v, state_win1_k, state_win1_v,
           state_win2_k, state_win2_v, norm1_g, w_in, gla_gate_w2, gla_gate_b, gla_norm_g,
           proj_a, proj_b, w_out, norm2_g, w_ffn_gate, w_ffn_up, w_ffn_down, norm_f_g):
    wts = _prep_weights(w_in, gla_gate_w2)
    bp, seq, _ = x_prompt.shape
    bs, n_new, _ = x_sample.shape
    width = DIL_HEADS * DIL_HD

    xp = x_prompt.reshape(bp * seq, D_MODEL)
    later = dict(pa=proj_a[0], pb=proj_b[0], wo=w_out[0], wg=w_ffn_gate[0], wu=w_ffn_up[0], wd=w_ffn_down[0])
    z_gla, z_dq, z_kv, z_gates, b_cum, *cast = _project(xp, norm1_g, gla_gate_b, wts, ROW_BLOCK, GLA_CHUNK,
                                                        GLA_CHUNK, tuple(later.values()))
    wts.update(zip(later.keys(), cast))
    xs = jnp.pad(x_sample, ((0, 0), (0, SAMPLE_PAD - n_new), (0, 0))).reshape(bs * SAMPLE_PAD, D_MODEL)
    zs_gla, zs_dq, zs_kv, zs_gates, bs_cum = _project(xs, norm1_g, gla_gate_b, wts, bs * SAMPLE_PAD,
                                                      SAMPLE_PAD, n_new)

    bufs = [jnp.transpose(a[0], (0, 2, 3, 1)).reshape(bs, width, a.shape[2]) for a in
            (state_win0_k, state_win0_v, state_win1_k, state_win1_v, state_win2_k, state_win2_v)]
    sample = (zs_dq.reshape(bs, SAMPLE_PAD, DIL_WIDTH), zs_kv.reshape(bs, SAMPLE_PAD, 2 * DIL_WIDTH), bufs, n_new)
    s0 = jnp.zeros((bp, GLA_HEADS, GLA_HDK, GLA_HDV), F32)
    o_a, gla_p, os_b, *win_s = _gla(z_gla.reshape(bp, seq, GLA_COLS), b_cum.reshape(bp, seq, GLA_DK), gla_norm_g,
                                    s0, GLA_CHUNK, GLA_CHUNK, GLA_ROWS_PROMPT, sample)

    o_b, *win_p = _dil_prompt(z_dq.reshape(bp, seq, DIL_WIDTH), z_kv.reshape(bp, seq, 2 * DIL_WIDTH))
    y_prompt = _tail(xp, o_a.reshape(bp * seq, GLA_DV), o_b.reshape(bp * seq, DIL_OUT), z_gates, wts,
                     norm2_g, norm_f_g, ROW_BLOCK).reshape(bp, seq, D_MODEL)
    win_p = [jnp.transpose(a.reshape(bp, DIL_HEADS, DIL_HD, a.shape[2]), (0, 3, 1, 2))[None] for a in win_p]

    os_a, gla_s = _gla(zs_gla.reshape(bs, SAMPLE_PAD, GLA_COLS), bs_cum.reshape(bs, SAMPLE_PAD, GLA_DK),
                       gla_norm_g, state_gla[0], SAMPLE_PAD, n_new, GLA_ROWS_SAMPLE)
    ys = _tail(xs, os_a.reshape(bs * SAMPLE_PAD, GLA_DV), os_b.reshape(bs * SAMPLE_PAD, DIL_OUT), zs_gates, wts,
               norm2_g, norm_f_g, bs * SAMPLE_PAD)
    y_sample = ys.reshape(bs, SAMPLE_PAD, D_MODEL)[:, :n_new]
    win_s = [jnp.transpose(a.reshape(bs, DIL_HEADS, DIL_HD, a.shape[2]), (0, 3, 1, 2))[None] for a in win_s]

    return (y_prompt, y_sample, gla_p[None], *win_p, gla_s[None], *win_s)
```

```python
import functools

import jax
import jax.numpy as jnp
from jax import lax
from jax.experimental import pallas as pl
from jax.experimental.pallas import tpu as pltpu

F32 = jnp.float32
BF16 = jnp.bfloat16

D_MODEL = 1024
GLA_HEADS = 4
GLA_DK = 512
GLA_DV = 1024
GLA_HDK = 128
GLA_HDV = 256
GLA_GATE_RANK = 16
GLA_TAU = 16.0
DIL_WINDOWS = (128, 512, 2048)
DIL_RATES = (1, 4, 16)
DIL_GROUPS = 3
DIL_HEADS = 4
DIL_HD = 64
DIL_WIDTH = 768
DIL_OUT = 256
ALIBI_MAX = 8.0
D_FF = 2816
RMS_EPS = 1e-6

LANES = 128
SUBLANES = 8
BF16_SUBLANES = 16
Q_TILE = 128
TILE_SLOTS = 2
GLA_CHUNK = 128
GLA_ROWS_PROMPT = 4
GLA_ROWS_SAMPLE = 4
SAMPLE_PAD = 8
MASK_VALUE = -1e30
VMEM_LIMIT_BYTES = 48 * 1024 * 1024
IN_PROJ_VMEM_LIMIT_BYTES = 60 * 1024 * 1024
MERGE_FFN_VMEM_LIMIT_BYTES = 60 * 1024 * 1024
GLA_VMEM_LIMIT_BYTES = 56 * 1024 * 1024
_LN2 = 0.6931471805599453

_OFF = {}
_o = 0
for _name, _w in (("gq", GLA_DK), ("gk", GLA_DK), ("gv", GLA_DV), ("gr", GLA_DV), ("glr", GLA_GATE_RANK),
                  ("dq", DIL_WIDTH), ("dk", DIL_WIDTH), ("dv", DIL_WIDTH), ("ga", D_MODEL), ("gb", D_MODEL)):
    _OFF[_name] = (_o, _o + _w)
    _o += _w
GLA_COLS = 2 * GLA_DK + 2 * GLA_DV
GLA_SAFE_LOG_DECAY = 20.0
ROW_BLOCK = 512


def _params(*sem):
    return pltpu.CompilerParams(dimension_semantics=sem, vmem_limit_bytes=VMEM_LIMIT_BYTES)


def _dot(a, b):
    return jnp.dot(a, b, preferred_element_type=F32)


def _dot_nt(a, b):
    return lax.dot_general(a, b, (((1,), (1,)), ((), ())), preferred_element_type=F32)


def _dot_tn(a, b):
    return lax.dot_general(a, b, (((0,), (0,)), ((), ())), preferred_element_type=F32)


def _rms(x, g):
    return x * lax.rsqrt(jnp.mean(x * x, axis=-1, keepdims=True) + RMS_EPS) * g


IN_PROJ_COL_CHUNK = 1024


def _split3(x):
    x1 = x.astype(BF16)
    r1 = x - x1.astype(F32)
    x2 = r1.astype(BF16)
    x3 = (r1 - x2.astype(F32)).astype(BF16)
    return x1, x2, x3


def _in_proj_kernel(x_ref, g_ref, wt_ref, w2_ref, gb_ref, *refs, groups, low_rank_col, chunk, n_valid, n_casts,
                    second):
    n_out = len(groups) + 1
    cast_in, refs = refs[:n_casts], refs[n_casts:]
    if second:
        x2_ref, refs = refs[0], refs[1:]
    out_refs, cast_out, out2_refs = refs[:n_out], refs[n_out:n_out + n_casts], refs[n_out + n_casts:]

    if second:
        @pl.when(pl.program_id(0) == 0)
        def _():
            _in_proj_rows(x2_ref, g_ref, wt_ref, w2_ref, gb_ref, out2_refs, groups, low_rank_col, *second)

    for src, dst in zip(cast_in, cast_out):
        dst[...] = src[...].astype(dst.dtype)
    _in_proj_rows(x_ref, g_ref, wt_ref, w2_ref, gb_ref, out_refs, groups, low_rank_col, chunk, n_valid)


def _in_proj_rows(x_ref, g_ref, wt_ref, w2_ref, gb_ref, out_refs, groups, low_rank_col, chunk, n_valid):
    h = _rms(x_ref[...], g_ref[...]).astype(BF16)
    b_ref = out_refs[len(groups)]
    tm = x_ref.shape[0]

    for o_ref, (c_first, n) in zip(out_refs, groups):
        for c0 in range(0, n, IN_PROJ_COL_CHUNK):
            c1 = min(c0 + IN_PROJ_COL_CHUNK, n)
            o_ref[:, c0:c1] = _dot_nt(h, wt_ref[c_first + c0:c_first + c1, :]).astype(o_ref.dtype)

    low_rank = _dot_nt(h, wt_ref[low_rank_col:low_rank_col + LANES, :]).astype(BF16)
    gate = _dot(low_rank, w2_ref[...]) + gb_ref[...]
    log_a = (jnp.minimum(gate, 0.0) - jnp.log(1.0 + jnp.exp(-jnp.abs(gate)))) * (1.0 / GLA_TAU)
    if n_valid < chunk:
        tok = lax.broadcasted_iota(jnp.int32, log_a.shape, 0) & (chunk - 1)
        log_a = jnp.where(tok < n_valid, log_a, 0.0)
    span = min(tm, LANES)
    row = lax.broadcasted_iota(jnp.int32, (span, span), 0)
    col = lax.broadcasted_iota(jnp.int32, (span, span), 1)
    same_chunk = (row & -chunk) == (col & -chunk) if chunk < span else True
    tril = jnp.where((row >= col) & same_chunk, 1.0, 0.0).astype(BF16)
    for r0 in range(0, tm, span):
        parts = _split3(log_a[r0:r0 + span])[:2]
        b_ref[r0:r0 + span, :] = sum(_dot(tril, p) for p in parts)


def _cast_blocks(a, steps):
    count = max(c for c in range(1, steps + 1) if a.shape[0] % (c * BF16_SUBLANES) == 0)
    return a.shape[0] // count, count


def _in_proj(x, g, w_t, w2, gate_b, groups, low_rank_col, out_dtypes, tm, chunk, n_valid, casts=(), second=None):
    m, k = x.shape
    steps = m // tm
    assert chunk & (chunk - 1) == 0 and (chunk % LANES == 0 or LANES % chunk == 0) and tm % chunk == 0
    assert all(c % BF16_SUBLANES == 0 and n % LANES == 0 for c, n in groups) and low_rank_col % BF16_SUBLANES == 0
    resident = lambda a: pl.BlockSpec(a.shape, lambda i: (0, 0), pipeline_mode=pl.Buffered(1))
    out_widths = [n for _, n in groups] + [GLA_DK]

    def cast_spec(a):
        rows, count = _cast_blocks(a, steps)
        return pl.BlockSpec((rows, a.shape[1]), lambda i, count=count: (jnp.minimum(i, count - 1), 0))

    dtypes = list(out_dtypes) + [F32]
    in_specs = ([pl.BlockSpec((tm, k), lambda i: (i, 0))] + [resident(a) for a in (g, w_t, w2, gate_b)]
                + [cast_spec(a) for a in casts])
    out_specs = [pl.BlockSpec((tm, n), lambda i: (i, 0)) for n in out_widths] + [cast_spec(a) for a in casts]
    out_shape = ([jax.ShapeDtypeStruct((m, n), dt) for n, dt in zip(out_widths, dtypes)]
                 + [jax.ShapeDtypeStruct(a.shape, BF16) for a in casts])
    operands = [x, g, w_t, w2, gate_b, *casts]
    second_static = None
    if second is not None:
        x2, chunk2, n_valid2 = second
        second_static = (chunk2, n_valid2)
        in_specs.append(resident(x2))
        operands.append(x2)
        out_specs += [pl.BlockSpec((x2.shape[0], n), lambda i: (0, 0)) for n in out_widths]
        out_shape += [jax.ShapeDtypeStruct((x2.shape[0], n), dt) for n, dt in zip(out_widths, dtypes)]
    return pl.pallas_call(
        functools.partial(_in_proj_kernel, groups=tuple(groups), low_rank_col=low_rank_col,
                          chunk=chunk, n_valid=n_valid, n_casts=len(casts), second=second_static),
        grid=(steps,),
        in_specs=in_specs,
        out_specs=out_specs,
        out_shape=out_shape,
        compiler_params=pltpu.CompilerParams(dimension_semantics=("arbitrary",),
                                             vmem_limit_bytes=IN_PROJ_VMEM_LIMIT_BYTES),
        name="in_proj",
    )(*operands)


N_SAMPLE_IN = 2 + 2 * DIL_GROUPS
N_SAMPLE_OUT = 1 + 2 * DIL_GROUPS


def _gla_kernel(*refs, chunk, n_valid, n_new_sample):
    q_ref, k_ref, v_ref, r_ref, b_ref, ng_ref, s0_ref = refs[:7]
    refs = refs[7:]
    sample_in = ()
    if n_new_sample:
        sample_in, refs = refs[:N_SAMPLE_IN], refs[N_SAMPLE_IN:]
    o_ref, sout_ref = refs[:2]
    refs = refs[2:]
    sample_out = ()
    if n_new_sample:
        sample_out, refs = refs[:N_SAMPLE_OUT], refs[N_SAMPLE_OUT:]
    s_ref, oi_ref, kf_ref, vf_ref = refs

    c = pl.program_id(1)
    rows = q_ref.shape[0]

    @pl.when(c == 0)
    def _():
        s_ref[...] = s0_ref[...]

    for r in range(rows):
        _gla_chunk(r, q_ref, k_ref, v_ref, r_ref, b_ref, ng_ref, o_ref, s_ref, oi_ref, chunk, n_valid)
    if n_new_sample:
        _dil_sample_kernel(*sample_in, *sample_out, n_new=n_new_sample)

    for r in range(rows):
        @pl.when(jnp.min(b_ref[r, chunk - 1:chunk, :]) < -GLA_SAFE_LOG_DECAY)
        def _(r=r):
            _gla_chunk_exact_intra(r, q_ref, k_ref, v_ref, r_ref, b_ref, ng_ref, o_ref, oi_ref, kf_ref, vf_ref,
                                   chunk, n_valid)

    @pl.when(c == pl.num_programs(1) - 1)
    def _():
        sout_ref[...] = s_ref[...]


def _gla_epilogue(o, r, vs, r_ref, ng_ref, o_ref):
    gr = r_ref[r, :, vs].astype(F32)
    o_ref[r, :, vs] = (_rms(o, ng_ref[...]) * (gr * jax.nn.sigmoid(gr))).astype(o_ref.dtype)


def _gla_values(r, vs, v_ref, chunk, n_valid):
    vh = v_ref[r, :, vs]
    if n_valid < chunk:
        tok = lax.broadcasted_iota(jnp.int32, vh.shape, 0)
        vh = jnp.where(tok < n_valid, vh, jnp.zeros_like(vh))
    return vh


def _gla_chunk_exact_intra(r, q_ref, k_ref, v_ref, r_ref, b_ref, ng_ref, o_ref, oi_ref, kf_ref, vf_ref,
                           chunk, n_valid):
    tok = lax.broadcasted_iota(jnp.int32, (chunk, 1), 0)
    for h in range(GLA_HEADS):
        ks = slice(h * GLA_HDK, (h + 1) * GLA_HDK)
        vs = slice(h * GLA_HDV, (h + 1) * GLA_HDV)
        bh = b_ref[r, :, ks]
        qh = q_ref[r, :, ks].astype(F32) * (GLA_HDK ** -0.5)
        kf_ref[...] = k_ref[r, :, ks].astype(F32)
        vf_ref[...] = _gla_values(r, vs, v_ref, chunk, n_valid).astype(F32)

        def eight_keys(i, acc, bh=bh, qh=qh, ks=ks):
            rows8 = pl.ds(pl.multiple_of(i * SUBLANES, SUBLANES), SUBLANES)
            b8, k8, v8 = b_ref[r, rows8, ks], kf_ref[rows8, :], vf_ref[rows8, :]
            for j in range(SUBLANES):
                s = i * SUBLANES + j
                decay = jnp.exp(jnp.where(tok >= s, bh - b8[j:j + 1], MASK_VALUE))
                w = jnp.sum(qh * decay * k8[j:j + 1], axis=-1, keepdims=True)
                acc = acc + w * v8[j:j + 1]
            return acc

        intra = lax.fori_loop(0, chunk // SUBLANES, eight_keys, jnp.zeros((chunk, GLA_HDV), F32))
        _gla_epilogue(oi_ref[r, h] + intra, r, vs, r_ref, ng_ref, o_ref)


def _gla_chunk(r, q_ref, k_ref, v_ref, r_ref, b_ref, ng_ref, o_ref, s_ref, oi_ref, chunk, n_valid):
    row = lax.broadcasted_iota(jnp.int32, (chunk, chunk), 0)
    col = lax.broadcasted_iota(jnp.int32, (chunk, chunk), 1)
    causal = row >= col
    b = b_ref[r]

    for h in range(GLA_HEADS):
        ks = slice(h * GLA_HDK, (h + 1) * GLA_HDK)
        vs = slice(h * GLA_HDV, (h + 1) * GLA_HDV)
        bh = b[:, ks]
        qh = q_ref[r, :, ks].astype(F32) * (GLA_HDK ** -0.5)
        kh = k_ref[r, :, ks].astype(F32)
        vh = _gla_values(r, vs, v_ref, chunk, n_valid)
        qt = (qh * jnp.exp(bh)).astype(BF16)
        kt = (kh * jnp.exp(-bh)).astype(BF16)
        kd = (kh * jnp.exp(bh[chunk - 1:chunk, :] - bh)).astype(BF16)
        s_old = s_ref[r, h]
        scores = jnp.where(causal, _dot_nt(qt, kt), 0.0).astype(BF16)
        o_state = _dot(qt, s_old.astype(BF16))
        oi_ref[r, h] = o_state
        dec = jnp.exp(jnp.broadcast_to(bh[chunk - 1:chunk, :], (GLA_HDK, GLA_HDK)).T)
        s_ref[r, h] = s_old * jnp.concatenate([dec, dec], axis=1) + _dot_tn(kd, vh)
        _gla_epilogue(o_state + _dot(scores, vh), r, vs, r_ref, ng_ref, o_ref)


def _gla(z, b_cum, norm_g, s0, chunk, n_valid, rows, sample=None):
    bsz, t, _ = z.shape
    n_chunks = t // chunk
    state = pl.BlockSpec((rows, GLA_HEADS, GLA_HDK, GLA_HDV), lambda b, c: (b, 0, 0, 0))
    in_specs = [pl.BlockSpec((rows, chunk, GLA_DK), lambda b, c: (b, c, 0)),
                pl.BlockSpec((rows, chunk, GLA_DK), lambda b, c: (b, c, 1)),
                pl.BlockSpec((rows, chunk, GLA_DV), lambda b, c: (b, c, 1)),
                pl.BlockSpec((rows, chunk, GLA_DV), lambda b, c: (b, c, 2)),
                pl.BlockSpec((rows, chunk, GLA_DK), lambda b, c: (b, c, 0)),
                pl.BlockSpec((1, GLA_HDV), lambda b, c: (0, 0)),
                state]
    out_specs = [pl.BlockSpec((rows, chunk, GLA_DV), lambda b, c: (b, c, 0)), state]
    out_shape = [jax.ShapeDtypeStruct((bsz, t, GLA_DV), BF16),
                 jax.ShapeDtypeStruct((bsz, GLA_HEADS, GLA_HDK, GLA_HDV), F32)]
    operands = [z, z, z, z, b_cum, norm_g, s0]
    n_new = 0
    if sample is not None:
        dq, kvn, bufs, n_new = sample
        assert dq.shape[0] == (bsz // rows) * n_chunks
        per_step = lambda a: pl.BlockSpec((1,) + a.shape[1:], lambda b, c: (b * n_chunks + c, 0, 0))
        in_specs += [per_step(a) for a in (dq, kvn, *bufs)]
        operands += [dq, kvn, *bufs]
        o_s = jax.ShapeDtypeStruct((dq.shape[0], dq.shape[1], DIL_OUT), BF16)
        out_specs += [per_step(o_s)] + [per_step(a) for a in bufs]
        out_shape += [o_s] + [jax.ShapeDtypeStruct(a.shape, a.dtype) for a in bufs]
    return pl.pallas_call(
        functools.partial(_gla_kernel, chunk=chunk, n_valid=n_valid, n_new_sample=n_new),
        grid=(bsz // rows, n_chunks),
        in_specs=in_specs,
        out_specs=out_specs,
        out_shape=out_shape,
        scratch_shapes=[pltpu.VMEM((rows, GLA_HEADS, GLA_HDK, GLA_HDV), F32),
                        pltpu.VMEM((rows, GLA_HEADS, chunk, GLA_HDV), F32),
                        pltpu.VMEM((chunk, GLA_HDK), F32),
                        pltpu.VMEM((chunk, GLA_HDV), F32)],
        compiler_params=pltpu.CompilerParams(dimension_semantics=("parallel", "arbitrary"),
                                             vmem_limit_bytes=GLA_VMEM_LIMIT_BYTES),
        name="gla",
    )(*operands)


def _alibi_slope(g, head):
    n = DIL_GROUPS * DIL_HEADS
    return jnp.exp((-ALIBI_MAX * _LN2 / n) * (head + (g * DIL_HEADS + 1.0)))


def _pair_slopes(g, hp, rows):
    head = 2.0 * hp.astype(F32) + jnp.where(rows >= Q_TILE, 1.0, 0.0)
    return _alibi_slope(g, head)


def _dil_prompt_kernel(q0_ref, q1_ref, q2_ref, k0_ref, k1_ref, k2_ref, v0_ref, v1_ref, v2_ref,
                       o_ref, wk0_ref, wv0_ref, wk1_ref, wv1_ref, wk2_ref, wv2_ref,
                       qf_ref, og_ref, ld_ref, bf_ref, br_ref, s_ref, p_ref, m_ref, *, seq):
    hp = pl.program_id(1)
    q_refs = (q0_ref, q1_ref, q2_ref)
    k_refs = (k0_ref, k1_ref, k2_ref)
    v_refs = (v0_ref, v1_ref, v2_ref)
    wk_refs = (wk0_ref, wk1_ref, wk2_ref)
    wv_refs = (wv0_ref, wv1_ref, wv2_ref)

    for g in range(DIL_GROUPS):
        qf_ref[g] = q_refs[g][...].astype(F32) * (DIL_HD ** -0.5)

    for g in range(DIL_GROUPS):
        rate = float(DIL_RATES[g])
        rows = lax.broadcasted_iota(jnp.int32, (2 * Q_TILE, Q_TILE), 0)
        cols = lax.broadcasted_iota(jnp.int32, (2 * Q_TILE, Q_TILE), 1)
        dist = (rows & (Q_TILE - 1)) - cols
        slope = _pair_slopes(g, hp, rows)
        bf_ref[g] = jnp.where(dist >= 0, -slope * rate * dist.astype(F32), MASK_VALUE)
        if g < 2:
            rows = lax.broadcasted_iota(jnp.int32, (2 * Q_TILE, 2 * Q_TILE), 0)
            cols = lax.broadcasted_iota(jnp.int32, (2 * Q_TILE, 2 * Q_TILE), 1)
            dist = (rows & (Q_TILE - 1)) + Q_TILE - cols
            slope = _pair_slopes(g, hp, rows)
            ok = jnp.abs(dist - Q_TILE // 2) <= Q_TILE // 2
            br_ref[g] = jnp.where(ok, -slope * rate * dist.astype(F32), MASK_VALUE)

    lane = lax.broadcasted_iota(jnp.int32, (Q_TILE, LANES), 1)
    lo = lane < DIL_HD

    def rows_of(start, size, rate):
        return pl.ds(start, size) if rate == 1 else pl.ds(start, size, stride=rate)

    def scores_stage(g, q_row, k_row, n_keys, slot):
        rate = DIL_RATES[g]
        q2 = qf_ref[g, rows_of(q_row, Q_TILE, rate), :]
        k2 = k_refs[g][rows_of(k_row, n_keys, rate), :].astype(BF16)
        qs = jnp.concatenate([jnp.where(lo, q2, 0.0), jnp.where(lo, 0.0, q2)], axis=0).astype(BF16)
        s_ref[slot, :, 0:n_keys] = _dot_nt(qs, k2)

    def softmax_stage(n_keys, bias, slot):
        s = s_ref[slot, :, 0:n_keys] + bias
        m = jnp.max(s, axis=-1, keepdims=True)
        p_ref[slot, :, 0:n_keys] = jnp.exp(s - m).astype(BF16)
        m_ref[slot] = jnp.where(lo, m[:Q_TILE], m[Q_TILE:])

    def values_stage(g, q_row, k_row, n_keys, slot):
        rate = DIL_RATES[g]
        v2 = v_refs[g][rows_of(k_row, n_keys, rate), :].astype(BF16)
        vo = jnp.concatenate([v2, jnp.ones((n_keys, LANES), BF16)], axis=1)
        r = _dot(p_ref[slot, :, 0:n_keys], vo)
        o2 = jnp.where(lo, r[:Q_TILE, :LANES], r[Q_TILE:, :LANES])
        l2 = jnp.where(lo, r[:Q_TILE, LANES:], r[Q_TILE:, LANES:])
        og_ref[g, rows_of(q_row, Q_TILE, rate), :] = o2 / l2
        ld_ref[g, rows_of(q_row, Q_TILE, rate), :] = m_ref[slot] + jnp.log(l2)

    tiles = []
    for g in range(DIL_GROUPS):
        rate = DIL_RATES[g]
        span = rate * Q_TILE
        for rho in range(rate):
            tiles.append((g, rho, rho, Q_TILE, ("first", g)))
        for n in range(1, seq // span):
            for rho in range(rate):
                tiles.append((g, rho + n * span, rho + (n - 1) * span, 2 * Q_TILE, ("later", g)))
    sets = [tiles[i:i + TILE_SLOTS] for i in range(0, len(tiles), TILE_SLOTS)]

    def slot_of(set_index, u):
        return (set_index % 2) * TILE_SLOTS + u

    for t in range(len(sets) + 2):
        if 0 <= t - 2 < len(sets):
            for u, (g, q_row, k_row, n_keys, _) in enumerate(sets[t - 2]):
                values_stage(g, q_row, k_row, n_keys, slot_of(t - 2, u))
        if 0 <= t - 1 < len(sets):
            for u, (g, _, _, n_keys, (kind, gb)) in enumerate(sets[t - 1]):
                softmax_stage(n_keys, bf_ref[gb] if kind == "first" else br_ref[gb], slot_of(t - 1, u))
        if t < len(sets):
            for u, (g, q_row, k_row, n_keys, _) in enumerate(sets[t]):
                scores_stage(g, q_row, k_row, n_keys, slot_of(t, u))

    for r0 in range(0, seq, Q_TILE):
        rs = slice(r0, r0 + Q_TILE)
        ld = [ld_ref[g, rs, :] for g in range(DIL_GROUPS)]
        top = jnp.maximum(jnp.maximum(ld[0], ld[1]), ld[2])
        w = [jnp.exp(x - top) for x in ld]
        num = sum(w[g] * og_ref[g, rs, :] for g in range(DIL_GROUPS))
        o_ref[rs, :] = (num / (w[0] + w[1] + w[2])).astype(o_ref.dtype)

    for g in range(DIL_GROUPS):
        keep = wk_refs[g].shape[1]
        for src, dst in ((k_refs[g], wk_refs[g]), (v_refs[g], wv_refs[g])):
            for off in range(0, keep, LANES):
                dst[:, off:off + LANES] = src[seq - keep + off:seq - keep + off + LANES, :].T


def _dil_prompt(dq, kv):
    bsz, seq, _ = dq.shape
    pairs = DIL_HEADS * DIL_HD // LANES
    nblk = DIL_WIDTH // LANES
    keeps = [min(w, seq) for w in DIL_WINDOWS]

    def spec(col0):
        return pl.BlockSpec((None, seq, LANES), lambda b, hp, col0=col0: (b, 0, col0 + hp))

    in_specs = ([spec(g * pairs) for g in range(DIL_GROUPS)]
                + [spec(g * pairs) for g in range(DIL_GROUPS)]
                + [spec(nblk + g * pairs) for g in range(DIL_GROUPS)])
    return pl.pallas_call(
        functools.partial(_dil_prompt_kernel, seq=seq),
        grid=(bsz, pairs),
        in_specs=in_specs,
        out_specs=[pl.BlockSpec((None, seq, LANES), lambda b, hp: (b, 0, hp))]
                  + [pl.BlockSpec((None, LANES, w), lambda b, hp: (b, hp, 0)) for w in keeps for _ in range(2)],
        out_shape=[jax.ShapeDtypeStruct((bsz, seq, DIL_OUT), BF16)]
                  + [jax.ShapeDtypeStruct((bsz, DIL_HEADS * DIL_HD, w), F32) for w in keeps for _ in range(2)],
        scratch_shapes=[pltpu.VMEM((DIL_GROUPS, seq, LANES), F32),
                        pltpu.VMEM((DIL_GROUPS, seq, LANES), F32),
                        pltpu.VMEM((DIL_GROUPS, seq, LANES), F32),
                        pltpu.VMEM((DIL_GROUPS, 2 * Q_TILE, Q_TILE), F32),
                        pltpu.VMEM((2, 2 * Q_TILE, 2 * Q_TILE), F32),
                        pltpu.VMEM((2 * TILE_SLOTS, 2 * Q_TILE, 2 * Q_TILE), F32),
                        pltpu.VMEM((2 * TILE_SLOTS, 2 * Q_TILE, 2 * Q_TILE), BF16),
                        pltpu.VMEM((2 * TILE_SLOTS, Q_TILE, LANES), F32)],
        compiler_params=_params("parallel", "arbitrary"),
        name="dil_prompt",
    )(dq, dq, dq, kv, kv, kv, kv, kv, kv)


def _dil_sample_kernel(q_ref, kvn_ref, kb0_ref, vb0_ref, kb1_ref, vb1_ref, kb2_ref, vb2_ref,
                       o_ref, nk0_ref, nv0_ref, nk1_ref, nv1_ref, nk2_ref, nv2_ref, *, n_new):
    kb_refs = (kb0_ref, kb1_ref, kb2_ref)
    vb_refs = (vb0_ref, vb1_ref, vb2_ref)
    nk_refs = (nk0_ref, nk1_ref, nk2_ref)
    nv_refs = (nv0_ref, nv1_ref, nv2_ref)
    pad = q_ref.shape[1]
    nrow = DIL_HEADS * pad
    width = DIL_HEADS * DIL_HD

    lane_head = jnp.right_shift(lax.broadcasted_iota(jnp.int32, (pad, width), 1), DIL_HD.bit_length() - 1)

    def bias_for(g, dist):
        rate = DIL_RATES[g]
        rows = lax.broadcasted_iota(jnp.int32, dist.shape, 0)
        head = sum(jnp.where(rows >= h * pad, 1.0, 0.0) for h in range(1, DIL_HEADS))
        ok = (dist >= 0) & (dist <= DIL_WINDOWS[g]) & ((dist & (rate - 1)) == 0)
        return jnp.where(ok, -_alibi_slope(g, head) * dist.astype(F32), MASK_VALUE)

    sel_r = lax.broadcasted_iota(jnp.int32, (pad, LANES), 0)
    sel_c = lax.broadcasted_iota(jnp.int32, (pad, LANES), 1)
    place = jnp.where((sel_r < n_new) & (sel_c == sel_r + (LANES - n_new)), 1.0, 0.0).astype(BF16)
    tail_lane = lax.broadcasted_iota(jnp.int32, (width, LANES), 1) >= LANES - n_new

    def shifted(buf_t, new_rows):
        length = buf_t.shape[1]
        rolled = pltpu.roll(buf_t, length - n_new, 1)
        new_t = sum(_dot_tn(part, place) for part in _split3(new_rows))
        last = jnp.where(tail_lane, new_t, rolled[:, length - LANES:])
        return rolled, last

    scores, values, transposed = [], [], []
    for g in range(DIL_GROUPS):
        length = kb_refs[g].shape[2]
        cs = slice(g * width, (g + 1) * width)
        vcs = slice(DIL_WIDTH + g * width, DIL_WIDTH + (g + 1) * width)
        qg = q_ref[0, :, cs].astype(F32) * (DIL_HD ** -0.5)
        qs = jnp.concatenate([jnp.where(lane_head == h, qg, 0.0) for h in range(DIL_HEADS)], axis=0).astype(BF16)
        kb, vb = kb_refs[g][0], vb_refs[g][0]
        kn, vn = kvn_ref[0, :, cs], kvn_ref[0, :, vcs]
        rows = lax.broadcasted_iota(jnp.int32, (nrow, length), 0)
        cols = lax.broadcasted_iota(jnp.int32, (nrow, length), 1)
        dist = length + (rows & (pad - 1)) - cols
        scores.append(_dot(qs, kb.astype(BF16)) + bias_for(g, dist))
        values.append(vb.astype(BF16))
        transposed.append(True)
        rows = lax.broadcasted_iota(jnp.int32, (nrow, pad), 0)
        cols = lax.broadcasted_iota(jnp.int32, (nrow, pad), 1)
        dist = jnp.where(cols < n_new, (rows & (pad - 1)) - cols, -1)
        scores.append(_dot_nt(qs, kn.astype(BF16)) + bias_for(g, dist))
        values.append(vn.astype(BF16))
        transposed.append(False)
        for buf, new, out_ref in ((kb, kn, nk_refs[g]), (vb, vn, nv_refs[g])):
            rolled, last = shifted(buf, new)
            if length > LANES:
                out_ref[0, :, 0:length - LANES] = rolled[:, 0:length - LANES]
            out_ref[0, :, length - LANES:length] = last

    top = functools.reduce(jnp.maximum, [jnp.max(s, axis=-1, keepdims=True) for s in scores])
    probs = [jnp.exp(s - top) for s in scores]
    den = sum(jnp.sum(p, axis=-1, keepdims=True) for p in probs)
    acc = sum((_dot_nt if t else _dot)(p.astype(BF16), v)
              for p, v, t in zip(probs, values, transposed)) / den
    out = sum(jnp.where(lane_head == h, acc[h * pad:(h + 1) * pad], 0.0) for h in range(DIL_HEADS))
    o_ref[0] = out.astype(o_ref.dtype)


FFN_COL_CHUNK = D_FF // 2


def _merge_ffn_kernel(x_ref, oa_ref, ob_ref, ga_ref, gb_ref, pa_ref, pb_ref, wo_ref, n2_ref, nf_ref,
                      wg_ref, wu_ref, wd_ref, *refs):
    weights = (pa_ref, pb_ref, wo_ref, n2_ref, nf_ref, wg_ref, wu_ref, wd_ref)
    if len(refs) > 1:
        x2_ref, oa2_ref, ob2_ref, g2_ref, y_ref, y2_ref = refs

        @pl.when(pl.program_id(0) == 0)
        def _():
            _merge_ffn_rows(x2_ref, oa2_ref, ob2_ref, g2_ref.at[:, 0:D_MODEL], g2_ref.at[:, D_MODEL:2 * D_MODEL],
                            *weights, y2_ref)
    else:
        y_ref, = refs
    _merge_ffn_rows(x_ref, oa_ref, ob_ref, ga_ref, gb_ref, *weights, y_ref)


def _merge_ffn_rows(x_ref, oa_ref, ob_ref, ga_ref, gb_ref, pa_ref, pb_ref, wo_ref, n2_ref, nf_ref,
                    wg_ref, wu_ref, wd_ref, y_ref):
    pa = _dot(oa_ref[...], pa_ref[...])
    pb = _dot(ob_ref[...], pb_ref[...])
    merged = jax.nn.sigmoid(ga_ref[...].astype(F32)) * pa + jax.nn.sigmoid(gb_ref[...].astype(F32)) * pb
    x1 = x_ref[...] + _dot(merged.astype(BF16), wo_ref[...])
    h = _rms(x1, n2_ref[...]).astype(BF16)
    acc = x1
    for c0 in range(0, D_FF, FFN_COL_CHUNK):
        cs = slice(c0, c0 + FFN_COL_CHUNK)
        gate = _dot(h, wg_ref[:, cs])
        up = _dot(h, wu_ref[:, cs])
        act = (gate * jax.nn.sigmoid(gate) * up).astype(BF16)
        acc = acc + _dot(act, wd_ref[cs, :])
    y_ref[...] = _rms(acc, nf_ref[...])


def _merge_ffn(x, oa, ob, gates, pa, pb, wo, n2, nf, wg, wu, wd, tm, second=None):
    m = x.shape[0]
    row = lambda w: pl.BlockSpec((tm, w), lambda i: (i, 0))
    resident = lambda a: pl.BlockSpec(a.shape, lambda i: (0, 0), pipeline_mode=pl.Buffered(1))
    in_specs = ([row(D_MODEL), row(GLA_DV), row(DIL_OUT), row(D_MODEL), pl.BlockSpec((tm, D_MODEL), lambda i: (i, 1))]
                + [resident(a) for a in (pa, pb, wo, n2, nf, wg, wu, wd)])
    out_specs = [row(D_MODEL)]
    out_shape = [jax.ShapeDtypeStruct((m, D_MODEL), F32)]
    operands = [x, oa, ob, gates, gates, pa, pb, wo, n2, nf, wg, wu, wd]
    if second is not None:
        in_specs += [resident(a) for a in second]
        operands += list(second)
        out_specs.append(pl.BlockSpec(second[0].shape, lambda i: (0, 0)))
        out_shape.append(jax.ShapeDtypeStruct(second[0].shape, F32))
    return pl.pallas_call(
        _merge_ffn_kernel,
        grid=(m // tm,),
        in_specs=in_specs,
        out_specs=out_specs,
        out_shape=out_shape,
        compiler_params=pltpu.CompilerParams(dimension_semantics=("arbitrary",),
                                             vmem_limit_bytes=MERGE_FFN_VMEM_LIMIT_BYTES),
        name="merge_ffn",
    )(*operands)


def _prep_weights(w_in, gla_gate_w2):
    w2 = jnp.concatenate([gla_gate_w2[0], jnp.zeros((LANES - GLA_GATE_RANK, GLA_DK), F32)], axis=0)
    return dict(w_in_t=jnp.transpose(w_in[0]).astype(BF16), w2=w2.astype(BF16))


def _project(x2d, norm1_g, gate_b, wts, tm, chunk, n_valid, casts=(), second=None):
    groups = [(0, GLA_COLS), (_OFF["dq"][0], DIL_WIDTH), (_OFF["dk"][0], 2 * DIL_WIDTH), (_OFF["ga"][0], 2 * D_MODEL)]
    return _in_proj(x2d, norm1_g, wts["w_in_t"], wts["w2"], gate_b, groups, _OFF["glr"][0],
                    [BF16, BF16, F32, BF16], tm, chunk, n_valid, casts, second)


def _tail(x2d, o_a, o_b, z_gates, wts, norm2_g, norm_f_g, tm, second):
    return _merge_ffn(x2d, o_a, o_b, z_gates, wts["pa"], wts["pb"], wts["wo"], norm2_g,
                      norm_f_g.reshape(1, D_MODEL), wts["wg"], wts["wu"], wts["wd"], tm, second)


def kernel(x_prompt, x_sample, state_gla, state_win0_k, state_win0_v, state_win1_k, state_win1_v,
           state_win2_k, state_win2_v, norm1_g, w_in, gla_gate_w2, gla_gate_b, gla_norm_g,
           proj_a, proj_b, w_out, norm2_g, w_ffn_gate, w_ffn_up, w_ffn_down, norm_f_g):
    wts = _prep_weights(w_in, gla_gate_w2)
    bp, seq, _ = x_prompt.shape
    bs, n_new, _ = x_sample.shape
    width = DIL_HEADS * DIL_HD

    xp = x_prompt.reshape(bp * seq, D_MODEL)
    xs = jnp.pad(x_sample, ((0, 0), (0, SAMPLE_PAD - n_new), (0, 0))).reshape(bs * SAMPLE_PAD, D_MODEL)
    later = dict(pa=proj_a[0], pb=proj_b[0], wo=w_out[0], wg=w_ffn_gate[0], wu=w_ffn_up[0], wd=w_ffn_down[0])
    z_gla, z_dq, z_kv, z_gates, b_cum, *rest = _project(xp, norm1_g, gla_gate_b, wts, ROW_BLOCK, GLA_CHUNK,
                                                        GLA_CHUNK, tuple(later.values()), (xs, SAMPLE_PAD, n_new))
    wts.update(zip(later.keys(), rest[:len(later)]))
    zs_gla, zs_dq, zs_kv, zs_gates, bs_cum = rest[len(later):]

    bufs = [jnp.transpose(a[0], (0, 2, 3, 1)).reshape(bs, width, a.shape[2]) for a in
            (state_win0_k, state_win0_v, state_win1_k, state_win1_v, state_win2_k, state_win2_v)]
    sample = (zs_dq.reshape(bs, SAMPLE_PAD, DIL_WIDTH), zs_kv.reshape(bs, SAMPLE_PAD, 2 * DIL_WIDTH), bufs, n_new)
    s0 = jnp.zeros((bp, GLA_HEADS, GLA_HDK, GLA_HDV), F32)
    o_a, gla_p, os_b, *win_s = _gla(z_gla.reshape(bp, seq, GLA_COLS), b_cum.reshape(bp, seq, GLA_DK), gla_norm_g,
                                    s0, GLA_CHUNK, GLA_CHUNK, GLA_ROWS_PROMPT, sample)

    o_b, *win_p = _dil_prompt(z_dq.reshape(bp, seq, DIL_WIDTH), z_kv.reshape(bp, seq, 2 * DIL_WIDTH))
    win_p = [jnp.transpose(a.reshape(bp, DIL_HEADS, DIL_HD, a.shape[2]), (0, 3, 1, 2))[None] for a in win_p]

    os_a, gla_s = _gla(zs_gla.reshape(bs, SAMPLE_PAD, GLA_COLS), bs_cum.reshape(bs, SAMPLE_PAD, GLA_DK),
                       gla_norm_g, state_gla[0], SAMPLE_PAD, n_new, GLA_ROWS_SAMPLE)
    sample_rows = (xs, os_a.reshape(bs * SAMPLE_PAD, GLA_DV), os_b.reshape(bs * SAMPLE_PAD, DIL_OUT), zs_gates)
    y_p, ys = _tail(xp, o_a.reshape(bp * seq, GLA_DV), o_b.reshape(bp * seq, DIL_OUT), z_gates, wts,
                    norm2_g, norm_f_g, ROW_BLOCK, sample_rows)
    y_prompt = y_p.reshape(bp, seq, D_MODEL)
    y_sample = ys.reshape(bs, SAMPLE_PAD, D_MODEL)[:, :n_new]
    win_s = [jnp.transpose(a.reshape(bs, DIL_HEADS, DIL_HD, a.shape[2]), (0, 3, 1, 2))[None] for a in win_s]

    return (y_prompt, y_sample, gla_p[None], *win_p, gla_s[None], *win_s)
```

```python
import functools

import jax
import jax.numpy as jnp
from jax import lax
from jax.experimental import pallas as pl
from jax.experimental.pallas import tpu as pltpu

F32 = jnp.float32
BF16 = jnp.bfloat16

D_MODEL = 1024
GLA_HEADS = 4
GLA_DK = 512
GLA_DV = 1024
GLA_HDK = 128
GLA_HDV = 256
GLA_GATE_RANK = 16
GLA_TAU = 16.0
DIL_WINDOWS = (128, 512, 2048)
DIL_RATES = (1, 4, 16)
DIL_GROUPS = 3
DIL_HEADS = 4
DIL_HD = 64
DIL_WIDTH = 768
DIL_OUT = 256
ALIBI_MAX = 8.0
D_FF = 2816
RMS_EPS = 1e-6

LANES = 128
SUBLANES = 8
BF16_SUBLANES = 16
Q_TILE = 128
TILE_SLOTS = 2
GLA_CHUNK = 128
GLA_ROWS_PROMPT = 4
GLA_ROWS_SAMPLE = 4
SAMPLE_PAD = 8
MASK_VALUE = -1e30
VMEM_LIMIT_BYTES = 48 * 1024 * 1024
IN_PROJ_VMEM_LIMIT_BYTES = 60 * 1024 * 1024
MERGE_FFN_VMEM_LIMIT_BYTES = 60 * 1024 * 1024
GLA_VMEM_LIMIT_BYTES = 56 * 1024 * 1024
_LN2 = 0.6931471805599453

_OFF = {}
_o = 0
for _name, _w in (("gq", GLA_DK), ("gk", GLA_DK), ("gv", GLA_DV), ("gr", GLA_DV), ("glr", GLA_GATE_RANK),
                  ("dq", DIL_WIDTH), ("dk", DIL_WIDTH), ("dv", DIL_WIDTH), ("ga", D_MODEL), ("gb", D_MODEL)):
    _OFF[_name] = (_o, _o + _w)
    _o += _w
GLA_COLS = 2 * GLA_DK + 2 * GLA_DV
GLA_SAFE_LOG_DECAY = 20.0
ROW_BLOCK = 512


def _params(*sem):
    return pltpu.CompilerParams(dimension_semantics=sem, vmem_limit_bytes=VMEM_LIMIT_BYTES)


def _dot(a, b):
    return jnp.dot(a, b, preferred_element_type=F32)


def _dot_nt(a, b):
    return lax.dot_general(a, b, (((1,), (1,)), ((), ())), preferred_element_type=F32)


def _dot_tn(a, b):
    return lax.dot_general(a, b, (((0,), (0,)), ((), ())), preferred_element_type=F32)


def _rms(x, g):
    return x * lax.rsqrt(jnp.mean(x * x, axis=-1, keepdims=True) + RMS_EPS) * g


IN_PROJ_COL_CHUNK = 1024
W_STAGE_MAX_ROWS = 256


def _split3(x):
    x1 = x.astype(BF16)
    r1 = x - x1.astype(F32)
    x2 = r1.astype(BF16)
    x3 = (r1 - x2.astype(F32)).astype(BF16)
    return x1, x2, x3


def _in_proj_kernel(x_ref, g_ref, w_hbm, w2_ref, gb_ref, *refs, groups, low_rank_col, chunk, n_valid, n_casts,
                    second):
    n_out = len(groups) + 1
    wt_ref, stage_ref, sem = refs[-3:]
    refs = refs[:-3]
    cast_in, refs = refs[:n_casts], refs[n_casts:]
    if second:
        x2_ref, refs = refs[0], refs[1:]
    out_refs, cast_out, out2_refs = refs[:n_out], refs[n_out:n_out + n_casts], refs[n_out + n_casts:]

    @pl.when(pl.program_id(0) == 0)
    def _():
        rows = stage_ref.shape[1]
        n_chunks = w_hbm.shape[0] // rows

        def chunk_copy(c):
            return pltpu.make_async_copy(w_hbm.at[pl.ds(c * rows, rows), :], stage_ref.at[c % 2], sem.at[c % 2])

        chunk_copy(0).start()
        for c in range(n_chunks):
            if c + 1 < n_chunks:
                chunk_copy(c + 1).start()
            chunk_copy(c).wait()
            wt_ref[c * rows:(c + 1) * rows, :] = stage_ref[c % 2].astype(BF16)

    if second:
        @pl.when(pl.program_id(0) == 0)
        def _():
            _in_proj_rows(x2_ref, g_ref, wt_ref, w2_ref, gb_ref, out2_refs, groups, low_rank_col, *second)

    for src, dst in zip(cast_in, cast_out):
        dst[...] = src[...].astype(dst.dtype)
    _in_proj_rows(x_ref, g_ref, wt_ref, w2_ref, gb_ref, out_refs, groups, low_rank_col, chunk, n_valid)


def _in_proj_rows(x_ref, g_ref, wt_ref, w2_ref, gb_ref, out_refs, groups, low_rank_col, chunk, n_valid):
    h = _rms(x_ref[...], g_ref[...]).astype(BF16)
    b_ref = out_refs[len(groups)]
    tm = x_ref.shape[0]

    for o_ref, (c_first, n) in zip(out_refs, groups):
        for c0 in range(0, n, IN_PROJ_COL_CHUNK):
            c1 = min(c0 + IN_PROJ_COL_CHUNK, n)
            o_ref[:, c0:c1] = _dot_nt(h, wt_ref[c_first + c0:c_first + c1, :]).astype(o_ref.dtype)

    low_rank = _dot_nt(h, wt_ref[low_rank_col:low_rank_col + LANES, :]).astype(BF16)
    gate = _dot(low_rank, w2_ref[...]) + gb_ref[...]
    log_a = (jnp.minimum(gate, 0.0) - jnp.log(1.0 + jnp.exp(-jnp.abs(gate)))) * (1.0 / GLA_TAU)
    if n_valid < chunk:
        tok = lax.broadcasted_iota(jnp.int32, log_a.shape, 0) & (chunk - 1)
        log_a = jnp.where(tok < n_valid, log_a, 0.0)
    span = min(tm, LANES)
    row = lax.broadcasted_iota(jnp.int32, (span, span), 0)
    col = lax.broadcasted_iota(jnp.int32, (span, span), 1)
    same_chunk = (row & -chunk) == (col & -chunk) if chunk < span else True
    tril = jnp.where((row >= col) & same_chunk, 1.0, 0.0).astype(BF16)
    for r0 in range(0, tm, span):
        parts = _split3(log_a[r0:r0 + span])[:2]
        b_ref[r0:r0 + span, :] = sum(_dot(tril, p) for p in parts)


def _cast_blocks(a, steps):
    count = max(c for c in range(1, steps + 1) if a.shape[0] % (c * BF16_SUBLANES) == 0)
    return a.shape[0] // count, count


def _in_proj(x, g, w_t, w2, gate_b, groups, low_rank_col, out_dtypes, tm, chunk, n_valid, casts=(), second=None):
    m, k = x.shape
    steps = m // tm
    assert chunk & (chunk - 1) == 0 and (chunk % LANES == 0 or LANES % chunk == 0) and tm % chunk == 0
    assert all(c % BF16_SUBLANES == 0 and n % LANES == 0 for c, n in groups) and low_rank_col % BF16_SUBLANES == 0
    resident = lambda a: pl.BlockSpec(a.shape, lambda i: (0, 0), pipeline_mode=pl.Buffered(1))
    out_widths = [n for _, n in groups] + [GLA_DK]

    def cast_spec(a):
        rows, count = _cast_blocks(a, steps)
        return pl.BlockSpec((rows, a.shape[1]), lambda i, count=count: (jnp.minimum(i, count - 1), 0))

    dtypes = list(out_dtypes) + [F32]
    n_in = w_t.shape[0]
    stage_rows = max(r for r in range(BF16_SUBLANES, W_STAGE_MAX_ROWS + 1, BF16_SUBLANES) if n_in % r == 0)
    in_specs = ([pl.BlockSpec((tm, k), lambda i: (i, 0)), resident(g), pl.BlockSpec(memory_space=pl.ANY),
                 resident(w2), resident(gate_b)] + [cast_spec(a) for a in casts])
    out_specs = [pl.BlockSpec((tm, n), lambda i: (i, 0)) for n in out_widths] + [cast_spec(a) for a in casts]
    out_shape = ([jax.ShapeDtypeStruct((m, n), dt) for n, dt in zip(out_widths, dtypes)]
                 + [jax.ShapeDtypeStruct(a.shape, BF16) for a in casts])
    operands = [x, g, w_t, w2, gate_b, *casts]
    second_static = None
    if second is not None:
        x2, chunk2, n_valid2 = second
        second_static = (chunk2, n_valid2)
        in_specs.append(resident(x2))
        operands.append(x2)
        out_specs += [pl.BlockSpec((x2.shape[0], n), lambda i: (0, 0)) for n in out_widths]
        out_shape += [jax.ShapeDtypeStruct((x2.shape[0], n), dt) for n, dt in zip(out_widths, dtypes)]
    return pl.pallas_call(
        functools.partial(_in_proj_kernel, groups=tuple(groups), low_rank_col=low_rank_col,
                          chunk=chunk, n_valid=n_valid, n_casts=len(casts), second=second_static),
        grid=(steps,),
        in_specs=in_specs,
        out_specs=out_specs,
        out_shape=out_shape,
        scratch_shapes=[pltpu.VMEM((n_in, k), BF16), pltpu.VMEM((2, stage_rows, k), F32),
                        pltpu.SemaphoreType.DMA((2,))],
        compiler_params=pltpu.CompilerParams(dimension_semantics=("arbitrary",),
                                             vmem_limit_bytes=IN_PROJ_VMEM_LIMIT_BYTES),
        name="in_proj",
    )(*operands)


N_SAMPLE_IN = 2 + 2 * DIL_GROUPS
N_SAMPLE_OUT = 1 + 2 * DIL_GROUPS


def _gla_kernel(*refs, chunk, n_valid, n_new_sample):
    q_ref, k_ref, v_ref, r_ref, b_ref, ng_ref, s0_ref = refs[:7]
    refs = refs[7:]
    sample_in = ()
    if n_new_sample:
        sample_in, refs = refs[:N_SAMPLE_IN], refs[N_SAMPLE_IN:]
    o_ref, sout_ref = refs[:2]
    refs = refs[2:]
    sample_out = ()
    if n_new_sample:
        sample_out, refs = refs[:N_SAMPLE_OUT], refs[N_SAMPLE_OUT:]
    s_ref, oi_ref, kf_ref, vf_ref = refs

    c = pl.program_id(1)
    rows = q_ref.shape[0]

    @pl.when(c == 0)
    def _():
        s_ref[...] = s0_ref[...]

    for r in range(rows):
        _gla_chunk(r, q_ref, k_ref, v_ref, r_ref, b_ref, ng_ref, o_ref, s_ref, oi_ref, chunk, n_valid)
    if n_new_sample:
        _dil_sample_kernel(*sample_in, *sample_out, n_new=n_new_sample)

    for r in range(rows):
        @pl.when(jnp.min(b_ref[r, chunk - 1:chunk, :]) < -GLA_SAFE_LOG_DECAY)
        def _(r=r):
            _gla_chunk_exact_intra(r, q_ref, k_ref, v_ref, r_ref, b_ref, ng_ref, o_ref, oi_ref, kf_ref, vf_ref,
                                   chunk, n_valid)

    @pl.when(c == pl.num_programs(1) - 1)
    def _():
        sout_ref[...] = s_ref[...]


def _gla_epilogue(o, r, vs, r_ref, ng_ref, o_ref):
    gr = r_ref[r, :, vs].astype(F32)
    o_ref[r, :, vs] = (_rms(o, ng_ref[...]) * (gr * jax.nn.sigmoid(gr))).astype(o_ref.dtype)


def _gla_values(r, vs, v_ref, chunk, n_valid):
    vh = v_ref[r, :, vs]
    if n_valid < chunk:
        tok = lax.broadcasted_iota(jnp.int32, vh.shape, 0)
        vh = jnp.where(tok < n_valid, vh, jnp.zeros_like(vh))
    return vh


def _gla_chunk_exact_intra(r, q_ref, k_ref, v_ref, r_ref, b_ref, ng_ref, o_ref, oi_ref, kf_ref, vf_ref,
                           chunk, n_valid):
    tok = lax.broadcasted_iota(jnp.int32, (chunk, 1), 0)
    for h in range(GLA_HEADS):
        ks = slice(h * GLA_HDK, (h + 1) * GLA_HDK)
        vs = slice(h * GLA_HDV, (h + 1) * GLA_HDV)
        bh = b_ref[r, :, ks]
        qh = q_ref[r, :, ks].astype(F32) * (GLA_HDK ** -0.5)
        kf_ref[...] = k_ref[r, :, ks].astype(F32)
        vf_ref[...] = _gla_values(r, vs, v_ref, chunk, n_valid).astype(F32)

        def eight_keys(i, acc, bh=bh, qh=qh, ks=ks):
            rows8 = pl.ds(pl.multiple_of(i * SUBLANES, SUBLANES), SUBLANES)
            b8, k8, v8 = b_ref[r, rows8, ks], kf_ref[rows8, :], vf_ref[rows8, :]
            for j in range(SUBLANES):
                s = i * SUBLANES + j
                decay = jnp.exp(jnp.where(tok >= s, bh - b8[j:j + 1], MASK_VALUE))
                w = jnp.sum(qh * decay * k8[j:j + 1], axis=-1, keepdims=True)
                acc = acc + w * v8[j:j + 1]
            return acc

        intra = lax.fori_loop(0, chunk // SUBLANES, eight_keys, jnp.zeros((chunk, GLA_HDV), F32))
        _gla_epilogue(oi_ref[r, h] + intra, r, vs, r_ref, ng_ref, o_ref)


def _gla_chunk(r, q_ref, k_ref, v_ref, r_ref, b_ref, ng_ref, o_ref, s_ref, oi_ref, chunk, n_valid):
    row = lax.broadcasted_iota(jnp.int32, (chunk, chunk), 0)
    col = lax.broadcasted_iota(jnp.int32, (chunk, chunk), 1)
    causal = row >= col
    b = b_ref[r]

    for h in range(GLA_HEADS):
        ks = slice(h * GLA_HDK, (h + 1) * GLA_HDK)
        vs = slice(h * GLA_HDV, (h + 1) * GLA_HDV)
        bh = b[:, ks]
        qh = q_ref[r, :, ks].astype(F32) * (GLA_HDK ** -0.5)
        kh = k_ref[r, :, ks].astype(F32)
        vh = _gla_values(r, vs, v_ref, chunk, n_valid)
        qt = (qh * jnp.exp(bh)).astype(BF16)
        kt = (kh * jnp.exp(-bh)).astype(BF16)
        kd = (kh * jnp.exp(bh[chunk - 1:chunk, :] - bh)).astype(BF16)
        s_old = s_ref[r, h]
        scores = jnp.where(causal, _dot_nt(qt, kt), 0.0).astype(BF16)
        o_state = _dot(qt, s_old.astype(BF16))
        oi_ref[r, h] = o_state
        dec = jnp.exp(jnp.broadcast_to(bh[chunk - 1:chunk, :], (GLA_HDK, GLA_HDK)).T)
        s_ref[r, h] = s_old * jnp.concatenate([dec, dec], axis=1) + _dot_tn(kd, vh)
        _gla_epilogue(o_state + _dot(scores, vh), r, vs, r_ref, ng_ref, o_ref)


def _gla(z, b_cum, norm_g, s0, chunk, n_valid, rows, sample=None):
    bsz, t, _ = z.shape
    n_chunks = t // chunk
    state = pl.BlockSpec((rows, GLA_HEADS, GLA_HDK, GLA_HDV), lambda b, c: (b, 0, 0, 0))
    in_specs = [pl.BlockSpec((rows, chunk, GLA_DK), lambda b, c: (b, c, 0)),
                pl.BlockSpec((rows, chunk, GLA_DK), lambda b, c: (b, c, 1)),
                pl.BlockSpec((rows, chunk, GLA_DV), lambda b, c: (b, c, 1)),
                pl.BlockSpec((rows, chunk, GLA_DV), lambda b, c: (b, c, 2)),
                pl.BlockSpec((rows, chunk, GLA_DK), lambda b, c: (b, c, 0)),
                pl.BlockSpec((1, GLA_HDV), lambda b, c: (0, 0)),
                state]
    out_specs = [pl.BlockSpec((rows, chunk, GLA_DV), lambda b, c: (b, c, 0)), state]
    out_shape = [jax.ShapeDtypeStruct((bsz, t, GLA_DV), BF16),
                 jax.ShapeDtypeStruct((bsz, GLA_HEADS, GLA_HDK, GLA_HDV), F32)]
    operands = [z, z, z, z, b_cum, norm_g, s0]
    n_new = 0
    if sample is not None:
        dq, kvn, bufs, n_new = sample
        assert dq.shape[0] == (bsz // rows) * n_chunks
        per_step = lambda a: pl.BlockSpec((1,) + a.shape[1:], lambda b, c: (b * n_chunks + c, 0, 0))
        in_specs += [per_step(a) for a in (dq, kvn, *bufs)]
        operands += [dq, kvn, *bufs]
        o_s = jax.ShapeDtypeStruct((dq.shape[0], dq.shape[1], DIL_OUT), BF16)
        out_specs += [per_step(o_s)] + [per_step(a) for a in bufs]
        out_shape += [o_s] + [jax.ShapeDtypeStruct(a.shape, a.dtype) for a in bufs]
    return pl.pallas_call(
        functools.partial(_gla_kernel, chunk=chunk, n_valid=n_valid, n_new_sample=n_new),
        grid=(bsz // rows, n_chunks),
        in_specs=in_specs,
        out_specs=out_specs,
        out_shape=out_shape,
        scratch_shapes=[pltpu.VMEM((rows, GLA_HEADS, GLA_HDK, GLA_HDV), F32),
                        pltpu.VMEM((rows, GLA_HEADS, chunk, GLA_HDV), F32),
                        pltpu.VMEM((chunk, GLA_HDK), F32),
                        pltpu.VMEM((chunk, GLA_HDV), F32)],
        compiler_params=pltpu.CompilerParams(dimension_semantics=("parallel", "arbitrary"),
                                             vmem_limit_bytes=GLA_VMEM_LIMIT_BYTES),
        name="gla",
    )(*operands)


def _alibi_slope(g, head):
    n = DIL_GROUPS * DIL_HEADS
    return jnp.exp((-ALIBI_MAX * _LN2 / n) * (head + (g * DIL_HEADS + 1.0)))


def _pair_slopes(g, hp, rows):
    head = 2.0 * hp.astype(F32) + jnp.where(rows >= Q_TILE, 1.0, 0.0)
    return _alibi_slope(g, head)


def _dil_prompt_kernel(q0_ref, q1_ref, q2_ref, k0_ref, k1_ref, k2_ref, v0_ref, v1_ref, v2_ref,
                       o_ref, wk0_ref, wv0_ref, wk1_ref, wv1_ref, wk2_ref, wv2_ref,
                       qf_ref, og_ref, ld_ref, bf_ref, br_ref, s_ref, p_ref, m_ref, *, seq):
    hp = pl.program_id(1)
    q_refs = (q0_ref, q1_ref, q2_ref)
    k_refs = (k0_ref, k1_ref, k2_ref)
    v_refs = (v0_ref, v1_ref, v2_ref)
    wk_refs = (wk0_ref, wk1_ref, wk2_ref)
    wv_refs = (wv0_ref, wv1_ref, wv2_ref)

    for g in range(DIL_GROUPS):
        qf_ref[g] = q_refs[g][...].astype(F32) * (DIL_HD ** -0.5)

    for g in range(DIL_GROUPS):
        rate = float(DIL_RATES[g])
        rows = lax.broadcasted_iota(jnp.int32, (2 * Q_TILE, Q_TILE), 0)
        cols = lax.broadcasted_iota(jnp.int32, (2 * Q_TILE, Q_TILE), 1)
        dist = (rows & (Q_TILE - 1)) - cols
        slope = _pair_slopes(g, hp, rows)
        bf_ref[g] = jnp.where(dist >= 0, -slope * rate * dist.astype(F32), MASK_VALUE)
        if g < 2:
            rows = lax.broadcasted_iota(jnp.int32, (2 * Q_TILE, 2 * Q_TILE), 0)
            cols = lax.broadcasted_iota(jnp.int32, (2 * Q_TILE, 2 * Q_TILE), 1)
            dist = (rows & (Q_TILE - 1)) + Q_TILE - cols
            slope = _pair_slopes(g, hp, rows)
            ok = jnp.abs(dist - Q_TILE // 2) <= Q_TILE // 2
            br_ref[g] = jnp.where(ok, -slope * rate * dist.astype(F32), MASK_VALUE)

    lane = lax.broadcasted_iota(jnp.int32, (Q_TILE, LANES), 1)
    lo = lane < DIL_HD

    def rows_of(start, size, rate):
        return pl.ds(start, size) if rate == 1 else pl.ds(start, size, stride=rate)

    def scores_stage(g, q_row, k_row, n_keys, slot):
        rate = DIL_RATES[g]
        q2 = qf_ref[g, rows_of(q_row, Q_TILE, rate), :]
        k2 = k_refs[g][rows_of(k_row, n_keys, rate), :].astype(BF16)
        qs = jnp.concatenate([jnp.where(lo, q2, 0.0), jnp.where(lo, 0.0, q2)], axis=0).astype(BF16)
        s_ref[slot, :, 0:n_keys] = _dot_nt(qs, k2)

    def softmax_stage(n_keys, bias, slot):
        s = s_ref[slot, :, 0:n_keys] + bias
        m = jnp.max(s, axis=-1, keepdims=True)
        p_ref[slot, :, 0:n_keys] = jnp.exp(s - m).astype(BF16)
        m_ref[slot] = jnp.where(lo, m[:Q_TILE], m[Q_TILE:])

    def values_stage(g, q_row, k_row, n_keys, slot):
        rate = DIL_RATES[g]
        v2 = v_refs[g][rows_of(k_row, n_keys, rate), :].astype(BF16)
        vo = jnp.concatenate([v2, jnp.ones((n_keys, LANES), BF16)], axis=1)
        r = _dot(p_ref[slot, :, 0:n_keys], vo)
        o2 = jnp.where(lo, r[:Q_TILE, :LANES], r[Q_TILE:, :LANES])
        l2 = jnp.where(lo, r[:Q_TILE, LANES:], r[Q_TILE:, LANES:])
        og_ref[g, rows_of(q_row, Q_TILE, rate), :] = o2 / l2
        ld_ref[g, rows_of(q_row, Q_TILE, rate), :] = m_ref[slot] + jnp.log(l2)

    tiles = []
    for g in range(DIL_GROUPS):
        rate = DIL_RATES[g]
        span = rate * Q_TILE
        for rho in range(rate):
            tiles.append((g, rho, rho, Q_TILE, ("first", g)))
        for n in range(1, seq // span):
            for rho in range(rate):
                tiles.append((g, rho + n * span, rho + (n - 1) * span, 2 * Q_TILE, ("later", g)))
    sets = [tiles[i:i + TILE_SLOTS] for i in range(0, len(tiles), TILE_SLOTS)]

    def slot_of(set_index, u):
        return (set_index % 2) * TILE_SLOTS + u

    for t in range(len(sets) + 2):
        if 0 <= t - 2 < len(sets):
            for u, (g, q_row, k_row, n_keys, _) in enumerate(sets[t - 2]):
                values_stage(g, q_row, k_row, n_keys, slot_of(t - 2, u))
        if 0 <= t - 1 < len(sets):
            for u, (g, _, _, n_keys, (kind, gb)) in enumerate(sets[t - 1]):
                softmax_stage(n_keys, bf_ref[gb] if kind == "first" else br_ref[gb], slot_of(t - 1, u))
        if t < len(sets):
            for u, (g, q_row, k_row, n_keys, _) in enumerate(sets[t]):
                scores_stage(g, q_row, k_row, n_keys, slot_of(t, u))

    for r0 in range(0, seq, Q_TILE):
        rs = slice(r0, r0 + Q_TILE)
        ld = [ld_ref[g, rs, :] for g in range(DIL_GROUPS)]
        top = jnp.maximum(jnp.maximum(ld[0], ld[1]), ld[2])
        w = [jnp.exp(x - top) for x in ld]
        num = sum(w[g] * og_ref[g, rs, :] for g in range(DIL_GROUPS))
        o_ref[rs, :] = (num / (w[0] + w[1] + w[2])).astype(o_ref.dtype)

    for g in range(DIL_GROUPS):
        keep = wk_refs[g].shape[1]
        for src, dst in ((k_refs[g], wk_refs[g]), (v_refs[g], wv_refs[g])):
            for off in range(0, keep, LANES):
                dst[:, off:off + LANES] = src[seq - keep + off:seq - keep + off + LANES, :].T


def _dil_prompt(dq, kv):
    bsz, seq, _ = dq.shape
    pairs = DIL_HEADS * DIL_HD // LANES
    nblk = DIL_WIDTH // LANES
    keeps = [min(w, seq) for w in DIL_WINDOWS]

    def spec(col0):
        return pl.BlockSpec((None, seq, LANES), lambda b, hp, col0=col0: (b, 0, col0 + hp))

    in_specs = ([spec(g * pairs) for g in range(DIL_GROUPS)]
                + [spec(g * pairs) for g in range(DIL_GROUPS)]
                + [spec(nblk + g * pairs) for g in range(DIL_GROUPS)])
    return pl.pallas_call(
        functools.partial(_dil_prompt_kernel, seq=seq),
        grid=(bsz, pairs),
        in_specs=in_specs,
        out_specs=[pl.BlockSpec((None, seq, LANES), lambda b, hp: (b, 0, hp))]
                  + [pl.BlockSpec((None, LANES, w), lambda b, hp: (b, hp, 0)) for w in keeps for _ in range(2)],
        out_shape=[jax.ShapeDtypeStruct((bsz, seq, DIL_OUT), BF16)]
                  + [jax.ShapeDtypeStruct((bsz, DIL_HEADS * DIL_HD, w), F32) for w in keeps for _ in range(2)],
        scratch_shapes=[pltpu.VMEM((DIL_GROUPS, seq, LANES), F32),
                        pltpu.VMEM((DIL_GROUPS, seq, LANES), F32),
                        pltpu.VMEM((DIL_GROUPS, seq, LANES), F32),
                        pltpu.VMEM((DIL_GROUPS, 2 * Q_TILE, Q_TILE), F32),
                        pltpu.VMEM((2, 2 * Q_TILE, 2 * Q_TILE), F32),
                        pltpu.VMEM((2 * TILE_SLOTS, 2 * Q_TILE, 2 * Q_TILE), F32),
                        pltpu.VMEM((2 * TILE_SLOTS, 2 * Q_TILE, 2 * Q_TILE), BF16),
                        pltpu.VMEM((2 * TILE_SLOTS, Q_TILE, LANES), F32)],
        compiler_params=_params("parallel", "arbitrary"),
        name="dil_prompt",
    )(dq, dq, dq, kv, kv, kv, kv, kv, kv)


def _dil_sample_kernel(q_ref, kvn_ref, kb0_ref, vb0_ref, kb1_ref, vb1_ref, kb2_ref, vb2_ref,
                       o_ref, nk0_ref, nv0_ref, nk1_ref, nv1_ref, nk2_ref, nv2_ref, *, n_new):
    kb_refs = (kb0_ref, kb1_ref, kb2_ref)
    vb_refs = (vb0_ref, vb1_ref, vb2_ref)
    nk_refs = (nk0_ref, nk1_ref, nk2_ref)
    nv_refs = (nv0_ref, nv1_ref, nv2_ref)
    pad = q_ref.shape[1]
    nrow = DIL_HEADS * pad
    width = DIL_HEADS * DIL_HD

    lane_head = jnp.right_shift(lax.broadcasted_iota(jnp.int32, (pad, width), 1), DIL_HD.bit_length() - 1)

    def bias_for(g, dist):
        rate = DIL_RATES[g]
        rows = lax.broadcasted_iota(jnp.int32, dist.shape, 0)
        head = sum(jnp.where(rows >= h * pad, 1.0, 0.0) for h in range(1, DIL_HEADS))
        ok = (dist >= 0) & (dist <= DIL_WINDOWS[g]) & ((dist & (rate - 1)) == 0)
        return jnp.where(ok, -_alibi_slope(g, head) * dist.astype(F32), MASK_VALUE)

    sel_r = lax.broadcasted_iota(jnp.int32, (pad, LANES), 0)
    sel_c = lax.broadcasted_iota(jnp.int32, (pad, LANES), 1)
    place = jnp.where((sel_r < n_new) & (sel_c == sel_r + (LANES - n_new)), 1.0, 0.0).astype(BF16)
    tail_lane = lax.broadcasted_iota(jnp.int32, (width, LANES), 1) >= LANES - n_new

    def shifted(buf_t, new_rows):
        length = buf_t.shape[1]
        rolled = pltpu.roll(buf_t, length - n_new, 1)
        new_t = sum(_dot_tn(part, place) for part in _split3(new_rows))
        last = jnp.where(tail_lane, new_t, rolled[:, length - LANES:])
        return rolled, last

    scores, values, transposed = [], [], []
    for g in range(DIL_GROUPS):
        length = kb_refs[g].shape[2]
        cs = slice(g * width, (g + 1) * width)
        vcs = slice(DIL_WIDTH + g * width, DIL_WIDTH + (g + 1) * width)
        qg = q_ref[0, :, cs].astype(F32) * (DIL_HD ** -0.5)
        qs = jnp.concatenate([jnp.where(lane_head == h, qg, 0.0) for h in range(DIL_HEADS)], axis=0).astype(BF16)
        kb, vb = kb_refs[g][0], vb_refs[g][0]
        kn, vn = kvn_ref[0, :, cs], kvn_ref[0, :, vcs]
        rows = lax.broadcasted_iota(jnp.int32, (nrow, length), 0)
        cols = lax.broadcasted_iota(jnp.int32, (nrow, length), 1)
        dist = length + (rows & (pad - 1)) - cols
        scores.append(_dot(qs, kb.astype(BF16)) + bias_for(g, dist))
        values.append(vb.astype(BF16))
        transposed.append(True)
        rows = lax.broadcasted_iota(jnp.int32, (nrow, pad), 0)
        cols = lax.broadcasted_iota(jnp.int32, (nrow, pad), 1)
        dist = jnp.where(cols < n_new, (rows & (pad - 1)) - cols, -1)
        scores.append(_dot_nt(qs, kn.astype(BF16)) + bias_for(g, dist))
        values.append(vn.astype(BF16))
        transposed.append(False)
        for buf, new, out_ref in ((kb, kn, nk_refs[g]), (vb, vn, nv_refs[g])):
            rolled, last = shifted(buf, new)
            if length > LANES:
                out_ref[0, :, 0:length - LANES] = rolled[:, 0:length - LANES]
            out_ref[0, :, length - LANES:length] = last

    top = functools.reduce(jnp.maximum, [jnp.max(s, axis=-1, keepdims=True) for s in scores])
    probs = [jnp.exp(s - top) for s in scores]
    den = sum(jnp.sum(p, axis=-1, keepdims=True) for p in probs)
    acc = sum((_dot_nt if t else _dot)(p.astype(BF16), v)
              for p, v, t in zip(probs, values, transposed)) / den
    out = sum(jnp.where(lane_head == h, acc[h * pad:(h + 1) * pad], 0.0) for h in range(DIL_HEADS))
    o_ref[0] = out.astype(o_ref.dtype)


FFN_COL_CHUNK = D_FF // 2


def _merge_ffn_kernel(x_ref, oa_ref, ob_ref, ga_ref, gb_ref, pa_ref, pb_ref, wo_ref, n2_ref, nf_ref,
                      wg_ref, wu_ref, wd_ref, *refs):
    weights = (pa_ref, pb_ref, wo_ref, n2_ref, nf_ref, wg_ref, wu_ref, wd_ref)
    if len(refs) > 1:
        x2_ref, oa2_ref, ob2_ref, g2_ref, y_ref, y2_ref = refs

        @pl.when(pl.program_id(0) == 0)
        def _():
            _merge_ffn_rows(x2_ref, oa2_ref, ob2_ref, g2_ref.at[:, 0:D_MODEL], g2_ref.at[:, D_MODEL:2 * D_MODEL],
                            *weights, y2_ref)
    else:
        y_ref, = refs
    _merge_ffn_rows(x_ref, oa_ref, ob_ref, ga_ref, gb_ref, *weights, y_ref)


def _merge_ffn_rows(x_ref, oa_ref, ob_ref, ga_ref, gb_ref, pa_ref, pb_ref, wo_ref, n2_ref, nf_ref,
                    wg_ref, wu_ref, wd_ref, y_ref):
    pa = _dot(oa_ref[...], pa_ref[...])
    pb = _dot(ob_ref[...], pb_ref[...])
    merged = jax.nn.sigmoid(ga_ref[...].astype(F32)) * pa + jax.nn.sigmoid(gb_ref[...].astype(F32)) * pb
    x1 = x_ref[...] + _dot(merged.astype(BF16), wo_ref[...])
    h = _rms(x1, n2_ref[...]).astype(BF16)
    acc = x1
    for c0 in range(0, D_FF, FFN_COL_CHUNK):
        cs = slice(c0, c0 + FFN_COL_CHUNK)
        gate = _dot(h, wg_ref[:, cs])
        up = _dot(h, wu_ref[:, cs])
        act = (gate * jax.nn.sigmoid(gate) * up).astype(BF16)
        acc = acc + _dot(act, wd_ref[cs, :])
    y_ref[...] = _rms(acc, nf_ref[...])


def _merge_ffn(x, oa, ob, gates, pa, pb, wo, n2, nf, wg, wu, wd, tm, second=None):
    m = x.shape[0]
    row = lambda w: pl.BlockSpec((tm, w), lambda i: (i, 0))
    resident = lambda a: pl.BlockSpec(a.shape, lambda i: (0, 0), pipeline_mode=pl.Buffered(1))
    in_specs = ([row(D_MODEL), row(GLA_DV), row(DIL_OUT), row(D_MODEL), pl.BlockSpec((tm, D_MODEL), lambda i: (i, 1))]
                + [resident(a) for a in (pa, pb, wo, n2, nf, wg, wu, wd)])
    out_specs = [row(D_MODEL)]
    out_shape = [jax.ShapeDtypeStruct((m, D_MODEL), F32)]
    operands = [x, oa, ob, gates, gates, pa, pb, wo, n2, nf, wg, wu, wd]
    if second is not None:
        in_specs += [resident(a) for a in second]
        operands += list(second)
        out_specs.append(pl.BlockSpec(second[0].shape, lambda i: (0, 0)))
        out_shape.append(jax.ShapeDtypeStruct(second[0].shape, F32))
    return pl.pallas_call(
        _merge_ffn_kernel,
        grid=(m // tm,),
        in_specs=in_specs,
        out_specs=out_specs,
        out_shape=out_shape,
        compiler_params=pltpu.CompilerParams(dimension_semantics=("arbitrary",),
                                             vmem_limit_bytes=MERGE_FFN_VMEM_LIMIT_BYTES),
        name="merge_ffn",
    )(*operands)


def _prep_weights(w_in, gla_gate_w2):
    w2 = jnp.concatenate([gla_gate_w2[0], jnp.zeros((LANES - GLA_GATE_RANK, GLA_DK), F32)], axis=0)
    return dict(w_in_t=jnp.transpose(w_in[0]), w2=w2.astype(BF16))


def _project(x2d, norm1_g, gate_b, wts, tm, chunk, n_valid, casts=(), second=None):
    groups = [(0, GLA_COLS), (_OFF["dq"][0], DIL_WIDTH), (_OFF["dk"][0], 2 * DIL_WIDTH), (_OFF["ga"][0], 2 * D_MODEL)]
    return _in_proj(x2d, norm1_g, wts["w_in_t"], wts["w2"], gate_b, groups, _OFF["glr"][0],
                    [BF16, BF16, F32, BF16], tm, chunk, n_valid, casts, second)


def _tail(x2d, o_a, o_b, z_gates, wts, norm2_g, norm_f_g, tm, second):
    return _merge_ffn(x2d, o_a, o_b, z_gates, wts["pa"], wts["pb"], wts["wo"], norm2_g,
                      norm_f_g.reshape(1, D_MODEL), wts["wg"], wts["wu"], wts["wd"], tm, second)


def kernel(x_prompt, x_sample, state_gla, state_win0_k, state_win0_v, state_win1_k, state_win1_v,
           state_win2_k, state_win2_v, norm1_g, w_in, gla_gate_w2, gla_gate_b, gla_norm_g,
           proj_a, proj_b, w_out, norm2_g, w_ffn_gate, w_ffn_up, w_ffn_down, norm_f_g):
    wts = _prep_weights(w_in, gla_gate_w2)
    bp, seq, _ = x_prompt.shape
    bs, n_new, _ = x_sample.shape
    width = DIL_HEADS * DIL_HD

    xp = x_prompt.reshape(bp * seq, D_MODEL)
    xs = jnp.pad(x_sample, ((0, 0), (0, SAMPLE_PAD - n_new), (0, 0))).reshape(bs * SAMPLE_PAD, D_MODEL)
    later = dict(pa=proj_a[0], pb=proj_b[0], wo=w_out[0], wg=w_ffn_gate[0], wu=w_ffn_up[0], wd=w_ffn_down[0])
    z_gla, z_dq, z_kv, z_gates, b_cum, *rest = _project(xp, norm1_g, gla_gate_b, wts, ROW_BLOCK, GLA_CHUNK,
                                                        GLA_CHUNK, tuple(later.values()), (xs, SAMPLE_PAD, n_new))
    wts.update(zip(later.keys(), rest[:len(later)]))
    zs_gla, zs_dq, zs_kv, zs_gates, bs_cum = rest[len(later):]

    bufs = [jnp.transpose(a[0], (0, 2, 3, 1)).reshape(bs, width, a.shape[2]) for a in
            (state_win0_k, state_win0_v, state_win1_k, state_win1_v, state_win2_k, state_win2_v)]
    sample = (zs_dq.reshape(bs, SAMPLE_PAD, DIL_WIDTH), zs_kv.reshape(bs, SAMPLE_PAD, 2 * DIL_WIDTH), bufs, n_new)
    s0 = jnp.zeros((bp, GLA_HEADS, GLA_HDK, GLA_HDV), F32)
    o_a, gla_p, os_b, *win_s = _gla(z_gla.reshape(bp, seq, GLA_COLS), b_cum.reshape(bp, seq, GLA_DK), gla_norm_g,
                                    s0, GLA_CHUNK, GLA_CHUNK, GLA_ROWS_PROMPT, sample)

    o_b, *win_p = _dil_prompt(z_dq.reshape(bp, seq, DIL_WIDTH), z_kv.reshape(bp, seq, 2 * DIL_WIDTH))
    win_p = [jnp.transpose(a.reshape(bp, DIL_HEADS, DIL_HD, a.shape[2]), (0, 3, 1, 2))[None] for a in win_p]

    os_a, gla_s = _gla(zs_gla.reshape(bs, SAMPLE_PAD, GLA_COLS), bs_cum.reshape(bs, SAMPLE_PAD, GLA_DK),
                       gla_norm_g, state_gla[0], SAMPLE_PAD, n_new, GLA_ROWS_SAMPLE)
    sample_rows = (xs, os_a.reshape(bs * SAMPLE_PAD, GLA_DV), os_b.reshape(bs * SAMPLE_PAD, DIL_OUT), zs_gates)
    y_p, ys = _tail(xp, o_a.reshape(bp * seq, GLA_DV), o_b.reshape(bp * seq, DIL_OUT), z_gates, wts,
                    norm2_g, norm_f_g, ROW_BLOCK, sample_rows)
    y_prompt = y_p.reshape(bp, seq, D_MODEL)
    y_sample = ys.reshape(bs, SAMPLE_PAD, D_MODEL)[:, :n_new]
    win_s = [jnp.transpose(a.reshape(bs, DIL_HEADS, DIL_HD, a.shape[2]), (0, 3, 1, 2))[None] for a in win_s]

    return (y_prompt, y_sample, gla_p[None], *win_p, gla_s[None], *win_s)
```

```python
import functools

import jax
import jax.numpy as jnp
from jax import lax
from jax.experimental import pallas as pl
from jax.experimental.pallas import tpu as pltpu

F32 = jnp.float32
BF16 = jnp.bfloat16

D_MODEL = 1024
GLA_HEADS = 4
GLA_DK = 512
GLA_DV = 1024
GLA_HDK = 128
GLA_HDV = 256
GLA_GATE_RANK = 16
GLA_TAU = 16.0
DIL_WINDOWS = (128, 512, 2048)
DIL_RATES = (1, 4, 16)
DIL_GROUPS = 3
DIL_HEADS = 4
DIL_HD = 64
DIL_WIDTH = 768
DIL_OUT = 256
ALIBI_MAX = 8.0
D_FF = 2816
RMS_EPS = 1e-6

LANES = 128
SUBLANES = 8
BF16_SUBLANES = 16
Q_TILE = 128
TILE_SLOTS = 2
GLA_CHUNK = 128
GLA_ROWS_PROMPT = 4
GLA_ROWS_SAMPLE = 4
SAMPLE_PAD = 8
MASK_VALUE = -1e30
VMEM_LIMIT_BYTES = 48 * 1024 * 1024
IN_PROJ_VMEM_LIMIT_BYTES = 60 * 1024 * 1024
MERGE_FFN_VMEM_LIMIT_BYTES = 60 * 1024 * 1024
GLA_VMEM_LIMIT_BYTES = 56 * 1024 * 1024
_LN2 = 0.6931471805599453

_OFF = {}
_o = 0
for _name, _w in (("gq", GLA_DK), ("gk", GLA_DK), ("gv", GLA_DV), ("gr", GLA_DV), ("glr", GLA_GATE_RANK),
                  ("dq", DIL_WIDTH), ("dk", DIL_WIDTH), ("dv", DIL_WIDTH), ("ga", D_MODEL), ("gb", D_MODEL)):
    _OFF[_name] = (_o, _o + _w)
    _o += _w
GLA_COLS = 2 * GLA_DK + 2 * GLA_DV
GLA_SAFE_LOG_DECAY = 20.0
ROW_BLOCK = 512


def _params(*sem):
    return pltpu.CompilerParams(dimension_semantics=sem, vmem_limit_bytes=VMEM_LIMIT_BYTES)


def _dot(a, b):
    return jnp.dot(a, b, preferred_element_type=F32)


def _dot_nt(a, b):
    return lax.dot_general(a, b, (((1,), (1,)), ((), ())), preferred_element_type=F32)


def _dot_tn(a, b):
    return lax.dot_general(a, b, (((0,), (0,)), ((), ())), preferred_element_type=F32)


def _rms(x, g):
    return x * lax.rsqrt(jnp.mean(x * x, axis=-1, keepdims=True) + RMS_EPS) * g


IN_PROJ_COL_CHUNK = 1024


def _split3(x):
    x1 = x.astype(BF16)
    r1 = x - x1.astype(F32)
    x2 = r1.astype(BF16)
    x3 = (r1 - x2.astype(F32)).astype(BF16)
    return x1, x2, x3


def _in_proj_kernel(x_ref, g_ref, wt_ref, w2_ref, gb_ref, *refs, groups, low_rank_col, chunk, n_valid, n_casts,
                    second):
    n_out = len(groups) + 1
    cast_in, refs = refs[:n_casts], refs[n_casts:]
    if second:
        x2_ref, refs = refs[0], refs[1:]
    out_refs, cast_out, out2_refs = refs[:n_out], refs[n_out:n_out + n_casts], refs[n_out + n_casts:]

    if second:
        @pl.when(pl.program_id(0) == 0)
        def _():
            _in_proj_rows(x2_ref, g_ref, wt_ref, w2_ref, gb_ref, out2_refs, groups, low_rank_col, *second)

    for src, dst in zip(cast_in, cast_out):
        dst[...] = src[...].astype(dst.dtype)
    _in_proj_rows(x_ref, g_ref, wt_ref, w2_ref, gb_ref, out_refs, groups, low_rank_col, chunk, n_valid)


def _in_proj_rows(x_ref, g_ref, wt_ref, w2_ref, gb_ref, out_refs, groups, low_rank_col, chunk, n_valid):
    h = _rms(x_ref[...], g_ref[...]).astype(BF16)
    b_ref = out_refs[len(groups)]
    tm = x_ref.shape[0]

    for o_ref, (c_first, n) in zip(out_refs, groups):
        for c0 in range(0, n, IN_PROJ_COL_CHUNK):
            c1 = min(c0 + IN_PROJ_COL_CHUNK, n)
            o_ref[:, c0:c1] = _dot_nt(h, wt_ref[c_first + c0:c_first + c1, :]).astype(o_ref.dtype)

    low_rank = _dot_nt(h, wt_ref[low_rank_col:low_rank_col + LANES, :]).astype(BF16)
    gate = _dot(low_rank, w2_ref[...]) + gb_ref[...]
    log_a = (jnp.minimum(gate, 0.0) - jnp.log(1.0 + jnp.exp(-jnp.abs(gate)))) * (1.0 / GLA_TAU)
    if n_valid < chunk:
        tok = lax.broadcasted_iota(jnp.int32, log_a.shape, 0) & (chunk - 1)
        log_a = jnp.where(tok < n_valid, log_a, 0.0)
    span = min(tm, LANES)
    row = lax.broadcasted_iota(jnp.int32, (span, span), 0)
    col = lax.broadcasted_iota(jnp.int32, (span, span), 1)
    same_chunk = (row & -chunk) == (col & -chunk) if chunk < span else True
    tril = jnp.where((row >= col) & same_chunk, 1.0, 0.0).astype(BF16)
    for r0 in range(0, tm, span):
        parts = _split3(log_a[r0:r0 + span])[:2]
        b_ref[r0:r0 + span, :] = sum(_dot(tril, p) for p in parts)


def _cast_blocks(a, steps):
    count = max(c for c in range(1, steps + 1) if a.shape[0] % (c * BF16_SUBLANES) == 0)
    return a.shape[0] // count, count


def _in_proj(x, g, w_t, w2, gate_b, groups, low_rank_col, out_dtypes, tm, chunk, n_valid, casts=(), second=None):
    m, k = x.shape
    steps = m // tm
    assert chunk & (chunk - 1) == 0 and (chunk % LANES == 0 or LANES % chunk == 0) and tm % chunk == 0
    assert all(c % BF16_SUBLANES == 0 and n % LANES == 0 for c, n in groups) and low_rank_col % BF16_SUBLANES == 0
    resident = lambda a: pl.BlockSpec(a.shape, lambda i: (0, 0), pipeline_mode=pl.Buffered(1))
    out_widths = [n for _, n in groups] + [GLA_DK]

    def cast_spec(a):
        rows, count = _cast_blocks(a, steps)
        return pl.BlockSpec((rows, a.shape[1]), lambda i, count=count: (jnp.minimum(i, count - 1), 0))

    dtypes = list(out_dtypes) + [F32]
    in_specs = ([pl.BlockSpec((tm, k), lambda i: (i, 0))] + [resident(a) for a in (g, w_t, w2, gate_b)]
                + [cast_spec(a) for a in casts])
    out_specs = [pl.BlockSpec((tm, n), lambda i: (i, 0)) for n in out_widths] + [cast_spec(a) for a in casts]
    out_shape = ([jax.ShapeDtypeStruct((m, n), dt) for n, dt in zip(out_widths, dtypes)]
                 + [jax.ShapeDtypeStruct(a.shape, BF16) for a in casts])
    operands = [x, g, w_t, w2, gate_b, *casts]
    second_static = None
    if second is not None:
        x2, chunk2, n_valid2 = second
        second_static = (chunk2, n_valid2)
        in_specs.append(resident(x2))
        operands.append(x2)
        out_specs += [pl.BlockSpec((x2.shape[0], n), lambda i: (0, 0)) for n in out_widths]
        out_shape += [jax.ShapeDtypeStruct((x2.shape[0], n), dt) for n, dt in zip(out_widths, dtypes)]
    return pl.pallas_call(
        functools.partial(_in_proj_kernel, groups=tuple(groups), low_rank_col=low_rank_col,
                          chunk=chunk, n_valid=n_valid, n_casts=len(casts), second=second_static),
        grid=(steps,),
        in_specs=in_specs,
        out_specs=out_specs,
        out_shape=out_shape,
        compiler_params=pltpu.CompilerParams(dimension_semantics=("arbitrary",),
                                             vmem_limit_bytes=IN_PROJ_VMEM_LIMIT_BYTES),
        name="in_proj",
    )(*operands)


N_SAMPLE_IN = 2 + 2 * DIL_GROUPS
N_SAMPLE_OUT = 1 + 2 * DIL_GROUPS
RING_SLOTS = 3


def _gla_kernel(*refs, chunk, n_valid, n_new_sample, n_steps):
    q_ref, k_ref, v_ref, r_ref, b_ref, ng_ref, s0_ref = refs[:7]
    refs = refs[7:]
    sample_in = ()
    if n_new_sample:
        sample_in, refs = refs[:N_SAMPLE_IN], refs[N_SAMPLE_IN:]
    o_ref, sout_ref = refs[:2]
    refs = refs[2:]
    sample_out = ()
    if n_new_sample:
        sample_out, refs = refs[:N_SAMPLE_OUT], refs[N_SAMPLE_OUT:]
    s_ref, oi_ref, kf_ref, vf_ref = refs[:4]

    c = pl.program_id(1)
    rows = q_ref.shape[0]

    if n_new_sample:
        buf_hbm = sample_in[2:]
        rings, sem = refs[4:4 + len(buf_hbm)], refs[4 + len(buf_hbm)]
        step = pl.program_id(0) * pl.num_programs(1) + c

        def ring_copies(s):
            slot = s % RING_SLOTS
            return [pltpu.make_async_copy(src.at[pl.ds(s, 1)], ring.at[pl.ds(slot, 1)], sem.at[j, slot])
                    for j, (src, ring) in enumerate(zip(buf_hbm, rings))]

        @pl.when(step == 0)
        def _():
            for s in range(RING_SLOTS - 1):
                for cp in ring_copies(s):
                    cp.start()

        @pl.when(step + (RING_SLOTS - 1) < n_steps)
        def _():
            for cp in ring_copies(step + (RING_SLOTS - 1)):
                cp.start()

        for cp in ring_copies(step):
            cp.wait()
        slot = step % RING_SLOTS
        sample_in = tuple(sample_in[:2]) + tuple(ring.at[pl.ds(slot, 1)] for ring in rings)

    @pl.when(c == 0)
    def _():
        s_ref[...] = s0_ref[...]

    for r in range(rows):
        _gla_chunk(r, q_ref, k_ref, v_ref, r_ref, b_ref, ng_ref, o_ref, s_ref, oi_ref, chunk, n_valid)
    if n_new_sample:
        _dil_sample_kernel(*sample_in, *sample_out, n_new=n_new_sample)

    for r in range(rows):
        @pl.when(jnp.min(b_ref[r, chunk - 1:chunk, :]) < -GLA_SAFE_LOG_DECAY)
        def _(r=r):
            _gla_chunk_exact_intra(r, q_ref, k_ref, v_ref, r_ref, b_ref, ng_ref, o_ref, oi_ref, kf_ref, vf_ref,
                                   chunk, n_valid)

    @pl.when(c == pl.num_programs(1) - 1)
    def _():
        sout_ref[...] = s_ref[...]


def _gla_epilogue(o, r, vs, r_ref, ng_ref, o_ref):
    gr = r_ref[r, :, vs].astype(F32)
    o_ref[r, :, vs] = (_rms(o, ng_ref[...]) * (gr * jax.nn.sigmoid(gr))).astype(o_ref.dtype)


def _gla_values(r, vs, v_ref, chunk, n_valid):
    vh = v_ref[r, :, vs]
    if n_valid < chunk:
        tok = lax.broadcasted_iota(jnp.int32, vh.shape, 0)
        vh = jnp.where(tok < n_valid, vh, jnp.zeros_like(vh))
    return vh


def _gla_chunk_exact_intra(r, q_ref, k_ref, v_ref, r_ref, b_ref, ng_ref, o_ref, oi_ref, kf_ref, vf_ref,
                           chunk, n_valid):
    tok = lax.broadcasted_iota(jnp.int32, (chunk, 1), 0)
    for h in range(GLA_HEADS):
        ks = slice(h * GLA_HDK, (h + 1) * GLA_HDK)
        vs = slice(h * GLA_HDV, (h + 1) * GLA_HDV)
        bh = b_ref[r, :, ks]
        qh = q_ref[r, :, ks].astype(F32) * (GLA_HDK ** -0.5)
        kf_ref[...] = k_ref[r, :, ks].astype(F32)
        vf_ref[...] = _gla_values(r, vs, v_ref, chunk, n_valid).astype(F32)

        def eight_keys(i, acc, bh=bh, qh=qh, ks=ks):
            rows8 = pl.ds(pl.multiple_of(i * SUBLANES, SUBLANES), SUBLANES)
            b8, k8, v8 = b_ref[r, rows8, ks], kf_ref[rows8, :], vf_ref[rows8, :]
            for j in range(SUBLANES):
                s = i * SUBLANES + j
                decay = jnp.exp(jnp.where(tok >= s, bh - b8[j:j + 1], MASK_VALUE))
                w = jnp.sum(qh * decay * k8[j:j + 1], axis=-1, keepdims=True)
                acc = acc + w * v8[j:j + 1]
            return acc

        intra = lax.fori_loop(0, chunk // SUBLANES, eight_keys, jnp.zeros((chunk, GLA_HDV), F32))
        _gla_epilogue(oi_ref[r, h] + intra, r, vs, r_ref, ng_ref, o_ref)


def _gla_chunk(r, q_ref, k_ref, v_ref, r_ref, b_ref, ng_ref, o_ref, s_ref, oi_ref, chunk, n_valid):
    row = lax.broadcasted_iota(jnp.int32, (chunk, chunk), 0)
    col = lax.broadcasted_iota(jnp.int32, (chunk, chunk), 1)
    causal = row >= col
    b = b_ref[r]

    for h in range(GLA_HEADS):
        ks = slice(h * GLA_HDK, (h + 1) * GLA_HDK)
        vs = slice(h * GLA_HDV, (h + 1) * GLA_HDV)
        bh = b[:, ks]
        qh = q_ref[r, :, ks].astype(F32) * (GLA_HDK ** -0.5)
        kh = k_ref[r, :, ks].astype(F32)
        vh = _gla_values(r, vs, v_ref, chunk, n_valid)
        qt = (qh * jnp.exp(bh)).astype(BF16)
        kt = (kh * jnp.exp(-bh)).astype(BF16)
        kd = (kh * jnp.exp(bh[chunk - 1:chunk, :] - bh)).astype(BF16)
        s_old = s_ref[r, h]
        scores = jnp.where(causal, _dot_nt(qt, kt), 0.0).astype(BF16)
        o_state = _dot(qt, s_old.astype(BF16))
        oi_ref[r, h] = o_state
        dec = jnp.exp(jnp.broadcast_to(bh[chunk - 1:chunk, :], (GLA_HDK, GLA_HDK)).T)
        s_ref[r, h] = s_old * jnp.concatenate([dec, dec], axis=1) + _dot_tn(kd, vh)
        _gla_epilogue(o_state + _dot(scores, vh), r, vs, r_ref, ng_ref, o_ref)


def _gla(z, b_cum, norm_g, s0, chunk, n_valid, rows, sample=None):
    bsz, t, _ = z.shape
    n_chunks = t // chunk
    state = pl.BlockSpec((rows, GLA_HEADS, GLA_HDK, GLA_HDV), lambda b, c: (b, 0, 0, 0))
    in_specs = [pl.BlockSpec((rows, chunk, GLA_DK), lambda b, c: (b, c, 0)),
                pl.BlockSpec((rows, chunk, GLA_DK), lambda b, c: (b, c, 1)),
                pl.BlockSpec((rows, chunk, GLA_DV), lambda b, c: (b, c, 1)),
                pl.BlockSpec((rows, chunk, GLA_DV), lambda b, c: (b, c, 2)),
                pl.BlockSpec((rows, chunk, GLA_DK), lambda b, c: (b, c, 0)),
                pl.BlockSpec((1, GLA_HDV), lambda b, c: (0, 0)),
                state]
    out_specs = [pl.BlockSpec((rows, chunk, GLA_DV), lambda b, c: (b, c, 0)), state]
    out_shape = [jax.ShapeDtypeStruct((bsz, t, GLA_DV), BF16),
                 jax.ShapeDtypeStruct((bsz, GLA_HEADS, GLA_HDK, GLA_HDV), F32)]
    operands = [z, z, z, z, b_cum, norm_g, s0]
    n_new = 0
    scratch = [pltpu.VMEM((rows, GLA_HEADS, GLA_HDK, GLA_HDV), F32),
               pltpu.VMEM((rows, GLA_HEADS, chunk, GLA_HDV), F32),
               pltpu.VMEM((chunk, GLA_HDK), F32),
               pltpu.VMEM((chunk, GLA_HDV), F32)]
    if sample is not None:
        dq, kvn, bufs, n_new = sample
        assert dq.shape[0] == (bsz // rows) * n_chunks
        assert dq.shape[0] >= RING_SLOTS
        per_step = lambda a: pl.BlockSpec((1,) + a.shape[1:], lambda b, c: (b * n_chunks + c, 0, 0))
        in_specs += [per_step(a) for a in (dq, kvn)] + [pl.BlockSpec(memory_space=pl.ANY) for _ in bufs]
        operands += [dq, kvn, *bufs]
        o_s = jax.ShapeDtypeStruct((dq.shape[0], dq.shape[1], DIL_OUT), BF16)
        out_specs += [per_step(o_s)] + [per_step(a) for a in bufs]
        out_shape += [o_s] + [jax.ShapeDtypeStruct(a.shape, a.dtype) for a in bufs]
        scratch += [pltpu.VMEM((RING_SLOTS,) + a.shape[1:], a.dtype) for a in bufs]
        scratch.append(pltpu.SemaphoreType.DMA((len(bufs), RING_SLOTS)))
    return pl.pallas_call(
        functools.partial(_gla_kernel, chunk=chunk, n_valid=n_valid, n_new_sample=n_new,
                          n_steps=(bsz // rows) * n_chunks),
        grid=(bsz // rows, n_chunks),
        in_specs=in_specs,
        out_specs=out_specs,
        out_shape=out_shape,
        scratch_shapes=scratch,
        compiler_params=pltpu.CompilerParams(dimension_semantics=("arbitrary", "arbitrary"),
                                             vmem_limit_bytes=GLA_VMEM_LIMIT_BYTES),
        name="gla",
    )(*operands)


def _alibi_slope(g, head):
    n = DIL_GROUPS * DIL_HEADS
    return jnp.exp((-ALIBI_MAX * _LN2 / n) * (head + (g * DIL_HEADS + 1.0)))


def _pair_slopes(g, hp, rows):
    head = 2.0 * hp.astype(F32) + jnp.where(rows >= Q_TILE, 1.0, 0.0)
    return _alibi_slope(g, head)


def _dil_prompt_kernel(q0_ref, q1_ref, q2_ref, k0_ref, k1_ref, k2_ref, v0_ref, v1_ref, v2_ref,
                       o_ref, wk0_ref, wv0_ref, wk1_ref, wv1_ref, wk2_ref, wv2_ref,
                       qf_ref, og_ref, ld_ref, bf_ref, br_ref, s_ref, p_ref, m_ref, *, seq):
    hp = pl.program_id(1)
    q_refs = (q0_ref, q1_ref, q2_ref)
    k_refs = (k0_ref, k1_ref, k2_ref)
    v_refs = (v0_ref, v1_ref, v2_ref)
    wk_refs = (wk0_ref, wk1_ref, wk2_ref)
    wv_refs = (wv0_ref, wv1_ref, wv2_ref)

    for g in range(DIL_GROUPS):
        qf_ref[g] = q_refs[g][...].astype(F32) * (DIL_HD ** -0.5)

    for g in range(DIL_GROUPS):
        rate = float(DIL_RATES[g])
        rows = lax.broadcasted_iota(jnp.int32, (2 * Q_TILE, Q_TILE), 0)
        cols = lax.broadcasted_iota(jnp.int32, (2 * Q_TILE, Q_TILE), 1)
        dist = (rows & (Q_TILE - 1)) - cols
        slope = _pair_slopes(g, hp, rows)
        bf_ref[g] = jnp.where(dist >= 0, -slope * rate * dist.astype(F32), MASK_VALUE)
        if g < 2:
            rows = lax.broadcasted_iota(jnp.int32, (2 * Q_TILE, 2 * Q_TILE), 0)
            cols = lax.broadcasted_iota(jnp.int32, (2 * Q_TILE, 2 * Q_TILE), 1)
            dist = (rows & (Q_TILE - 1)) + Q_TILE - cols
            slope = _pair_slopes(g, hp, rows)
            ok = jnp.abs(dist - Q_TILE // 2) <= Q_TILE // 2
            br_ref[g] = jnp.where(ok, -slope * rate * dist.astype(F32), MASK_VALUE)

    lane = lax.broadcasted_iota(jnp.int32, (Q_TILE, LANES), 1)
    lo = lane < DIL_HD

    def rows_of(start, size, rate):
        return pl.ds(start, size) if rate == 1 else pl.ds(start, size, stride=rate)

    def scores_stage(g, q_row, k_row, n_keys, slot):
        rate = DIL_RATES[g]
        q2 = qf_ref[g, rows_of(q_row, Q_TILE, rate), :]
        k2 = k_refs[g][rows_of(k_row, n_keys, rate), :].astype(BF16)
        qs = jnp.concatenate([jnp.where(lo, q2, 0.0), jnp.where(lo, 0.0, q2)], axis=0).astype(BF16)
        s_ref[slot, :, 0:n_keys] = _dot_nt(qs, k2)

    def softmax_stage(n_keys, bias, slot):
        s = s_ref[slot, :, 0:n_keys] + bias
        m = jnp.max(s, axis=-1, keepdims=True)
        p_ref[slot, :, 0:n_keys] = jnp.exp(s - m).astype(BF16)
        m_ref[slot] = jnp.where(lo, m[:Q_TILE], m[Q_TILE:])

    def values_stage(g, q_row, k_row, n_keys, slot):
        rate = DIL_RATES[g]
        v2 = v_refs[g][rows_of(k_row, n_keys, rate), :].astype(BF16)
        vo = jnp.concatenate([v2, jnp.ones((n_keys, LANES), BF16)], axis=1)
        r = _dot(p_ref[slot, :, 0:n_keys], vo)
        o2 = jnp.where(lo, r[:Q_TILE, :LANES], r[Q_TILE:, :LANES])
        l2 = jnp.where(lo, r[:Q_TILE, LANES:], r[Q_TILE:, LANES:])
        og_ref[g, rows_of(q_row, Q_TILE, rate), :] = o2 / l2
        ld_ref[g, rows_of(q_row, Q_TILE, rate), :] = m_ref[slot] + jnp.log(l2)

    tiles = []
    for g in range(DIL_GROUPS):
        rate = DIL_RATES[g]
        span = rate * Q_TILE
        for rho in range(rate):
            tiles.append((g, rho, rho, Q_TILE, ("first", g)))
        for n in range(1, seq // span):
            for rho in range(rate):
                tiles.append((g, rho + n * span, rho + (n - 1) * span, 2 * Q_TILE, ("later", g)))
    sets = [tiles[i:i + TILE_SLOTS] for i in range(0, len(tiles), TILE_SLOTS)]

    def slot_of(set_index, u):
        return (set_index % 2) * TILE_SLOTS + u

    for t in range(len(sets) + 2):
        if 0 <= t - 2 < len(sets):
            for u, (g, q_row, k_row, n_keys, _) in enumerate(sets[t - 2]):
                values_stage(g, q_row, k_row, n_keys, slot_of(t - 2, u))
        if 0 <= t - 1 < len(sets):
            for u, (g, _, _, n_keys, (kind, gb)) in enumerate(sets[t - 1]):
                softmax_stage(n_keys, bf_ref[gb] if kind == "first" else br_ref[gb], slot_of(t - 1, u))
        if t < len(sets):
            for u, (g, q_row, k_row, n_keys, _) in enumerate(sets[t]):
                scores_stage(g, q_row, k_row, n_keys, slot_of(t, u))

    for r0 in range(0, seq, Q_TILE):
        rs = slice(r0, r0 + Q_TILE)
        ld = [ld_ref[g, rs, :] for g in range(DIL_GROUPS)]
        top = jnp.maximum(jnp.maximum(ld[0], ld[1]), ld[2])
        w = [jnp.exp(x - top) for x in ld]
        num = sum(w[g] * og_ref[g, rs, :] for g in range(DIL_GROUPS))
        o_ref[rs, :] = (num / (w[0] + w[1] + w[2])).astype(o_ref.dtype)

    for g in range(DIL_GROUPS):
        keep = wk_refs[g].shape[1]
        for src, dst in ((k_refs[g], wk_refs[g]), (v_refs[g], wv_refs[g])):
            for off in range(0, keep, LANES):
                dst[:, off:off + LANES] = src[seq - keep + off:seq - keep + off + LANES, :].T


def _dil_prompt(dq, kv):
    bsz, seq, _ = dq.shape
    pairs = DIL_HEADS * DIL_HD // LANES
    nblk = DIL_WIDTH // LANES
    keeps = [min(w, seq) for w in DIL_WINDOWS]

    def spec(col0):
        return pl.BlockSpec((None, seq, LANES), lambda b, hp, col0=col0: (b, 0, col0 + hp))

    in_specs = ([spec(g * pairs) for g in range(DIL_GROUPS)]
                + [spec(g * pairs) for g in range(DIL_GROUPS)]
                + [spec(nblk + g * pairs) for g in range(DIL_GROUPS)])
    return pl.pallas_call(
        functools.partial(_dil_prompt_kernel, seq=seq),
        grid=(bsz, pairs),
        in_specs=in_specs,
        out_specs=[pl.BlockSpec((None, seq, LANES), lambda b, hp: (b, 0, hp))]
                  + [pl.BlockSpec((None, LANES, w), lambda b, hp: (b, hp, 0)) for w in keeps for _ in range(2)],
        out_shape=[jax.ShapeDtypeStruct((bsz, seq, DIL_OUT), BF16)]
                  + [jax.ShapeDtypeStruct((bsz, DIL_HEADS * DIL_HD, w), F32) for w in keeps for _ in range(2)],
        scratch_shapes=[pltpu.VMEM((DIL_GROUPS, seq, LANES), F32),
                        pltpu.VMEM((DIL_GROUPS, seq, LANES), F32),
                        pltpu.VMEM((DIL_GROUPS, seq, LANES), F32),
                        pltpu.VMEM((DIL_GROUPS, 2 * Q_TILE, Q_TILE), F32),
                        pltpu.VMEM((2, 2 * Q_TILE, 2 * Q_TILE), F32),
                        pltpu.VMEM((2 * TILE_SLOTS, 2 * Q_TILE, 2 * Q_TILE), F32),
                        pltpu.VMEM((2 * TILE_SLOTS, 2 * Q_TILE, 2 * Q_TILE), BF16),
                        pltpu.VMEM((2 * TILE_SLOTS, Q_TILE, LANES), F32)],
        compiler_params=_params("parallel", "arbitrary"),
        name="dil_prompt",
    )(dq, dq, dq, kv, kv, kv, kv, kv, kv)


def _dil_sample_kernel(q_ref, kvn_ref, kb0_ref, vb0_ref, kb1_ref, vb1_ref, kb2_ref, vb2_ref,
                       o_ref, nk0_ref, nv0_ref, nk1_ref, nv1_ref, nk2_ref, nv2_ref, *, n_new):
    kb_refs = (kb0_ref, kb1_ref, kb2_ref)
    vb_refs = (vb0_ref, vb1_ref, vb2_ref)
    nk_refs = (nk0_ref, nk1_ref, nk2_ref)
    nv_refs = (nv0_ref, nv1_ref, nv2_ref)
    pad = q_ref.shape[1]
    nrow = DIL_HEADS * pad
    width = DIL_HEADS * DIL_HD

    lane_head = jnp.right_shift(lax.broadcasted_iota(jnp.int32, (pad, width), 1), DIL_HD.bit_length() - 1)

    def bias_for(g, dist):
        rate = DIL_RATES[g]
        rows = lax.broadcasted_iota(jnp.int32, dist.shape, 0)
        head = sum(jnp.where(rows >= h * pad, 1.0, 0.0) for h in range(1, DIL_HEADS))
        ok = (dist >= 0) & (dist <= DIL_WINDOWS[g]) & ((dist & (rate - 1)) == 0)
        return jnp.where(ok, -_alibi_slope(g, head) * dist.astype(F32), MASK_VALUE)

    sel_r = lax.broadcasted_iota(jnp.int32, (pad, LANES), 0)
    sel_c = lax.broadcasted_iota(jnp.int32, (pad, LANES), 1)
    place = jnp.where((sel_r < n_new) & (sel_c == sel_r + (LANES - n_new)), 1.0, 0.0).astype(BF16)
    tail_lane = lax.broadcasted_iota(jnp.int32, (width, LANES), 1) >= LANES - n_new

    def shifted(buf_t, new_rows):
        length = buf_t.shape[1]
        rolled = pltpu.roll(buf_t, length - n_new, 1)
        new_t = sum(_dot_tn(part, place) for part in _split3(new_rows))
        last = jnp.where(tail_lane, new_t, rolled[:, length - LANES:])
        return rolled, last

    scores, values, transposed = [], [], []
    for g in range(DIL_GROUPS):
        length = kb_refs[g].shape[2]
        cs = slice(g * width, (g + 1) * width)
        vcs = slice(DIL_WIDTH + g * width, DIL_WIDTH + (g + 1) * width)
        qg = q_ref[0, :, cs].astype(F32) * (DIL_HD ** -0.5)
        qs = jnp.concatenate([jnp.where(lane_head == h, qg, 0.0) for h in range(DIL_HEADS)], axis=0).astype(BF16)
        kb, vb = kb_refs[g][0], vb_refs[g][0]
        kn, vn = kvn_ref[0, :, cs], kvn_ref[0, :, vcs]
        rows = lax.broadcasted_iota(jnp.int32, (nrow, length), 0)
        cols = lax.broadcasted_iota(jnp.int32, (nrow, length), 1)
        dist = length + (rows & (pad - 1)) - cols
        scores.append(_dot(qs, kb.astype(BF16)) + bias_for(g, dist))
        values.append(vb.astype(BF16))
        transposed.append(True)
        rows = lax.broadcasted_iota(jnp.int32, (nrow, pad), 0)
        cols = lax.broadcasted_iota(jnp.int32, (nrow, pad), 1)
        dist = jnp.where(cols < n_new, (rows & (pad - 1)) - cols, -1)
        scores.append(_dot_nt(qs, kn.astype(BF16)) + bias_for(g, dist))
        values.append(vn.astype(BF16))
        transposed.append(False)
        for buf, new, out_ref in ((kb, kn, nk_refs[g]), (vb, vn, nv_refs[g])):
            rolled, last = shifted(buf, new)
            if length > LANES:
                out_ref[0, :, 0:length - LANES] = rolled[:, 0:length - LANES]
            out_ref[0, :, length - LANES:length] = last

    top = functools.reduce(jnp.maximum, [jnp.max(s, axis=-1, keepdims=True) for s in scores])
    probs = [jnp.exp(s - top) for s in scores]
    den = sum(jnp.sum(p, axis=-1, keepdims=True) for p in probs)
    acc = sum((_dot_nt if t else _dot)(p.astype(BF16), v)
              for p, v, t in zip(probs, values, transposed)) / den
    out = sum(jnp.where(lane_head == h, acc[h * pad:(h + 1) * pad], 0.0) for h in range(DIL_HEADS))
    o_ref[0] = out.astype(o_ref.dtype)


FFN_COL_CHUNK = D_FF // 2


def _merge_ffn_kernel(x_ref, oa_ref, ob_ref, ga_ref, gb_ref, pa_ref, pb_ref, wo_ref, n2_ref, nf_ref,
                      wg_ref, wu_ref, wd_ref, *refs):
    weights = (pa_ref, pb_ref, wo_ref, n2_ref, nf_ref, wg_ref, wu_ref, wd_ref)
    if len(refs) > 1:
        x2_ref, oa2_ref, ob2_ref, g2_ref, y_ref, y2_ref = refs

        @pl.when(pl.program_id(0) == 0)
        def _():
            _merge_ffn_rows(x2_ref, oa2_ref, ob2_ref, g2_ref.at[:, 0:D_MODEL], g2_ref.at[:, D_MODEL:2 * D_MODEL],
                            *weights, y2_ref)
    else:
        y_ref, = refs
    _merge_ffn_rows(x_ref, oa_ref, ob_ref, ga_ref, gb_ref, *weights, y_ref)


def _merge_ffn_rows(x_ref, oa_ref, ob_ref, ga_ref, gb_ref, pa_ref, pb_ref, wo_ref, n2_ref, nf_ref,
                    wg_ref, wu_ref, wd_ref, y_ref):
    pa = _dot(oa_ref[...], pa_ref[...])
    pb = _dot(ob_ref[...], pb_ref[...])
    merged = jax.nn.sigmoid(ga_ref[...].astype(F32)) * pa + jax.nn.sigmoid(gb_ref[...].astype(F32)) * pb
    x1 = x_ref[...] + _dot(merged.astype(BF16), wo_ref[...])
    h = _rms(x1, n2_ref[...]).astype(BF16)
    acc = x1
    for c0 in range(0, D_FF, FFN_COL_CHUNK):
        cs = slice(c0, c0 + FFN_COL_CHUNK)
        gate = _dot(h, wg_ref[:, cs])
        up = _dot(h, wu_ref[:, cs])
        act = (gate * jax.nn.sigmoid(gate) * up).astype(BF16)
        acc = acc + _dot(act, wd_ref[cs, :])
    y_ref[...] = _rms(acc, nf_ref[...])


def _merge_ffn(x, oa, ob, gates, pa, pb, wo, n2, nf, wg, wu, wd, tm, second=None):
    m = x.shape[0]
    row = lambda w: pl.BlockSpec((tm, w), lambda i: (i, 0))
    resident = lambda a: pl.BlockSpec(a.shape, lambda i: (0, 0), pipeline_mode=pl.Buffered(1))
    in_specs = ([row(D_MODEL), row(GLA_DV), row(DIL_OUT), row(D_MODEL), pl.BlockSpec((tm, D_MODEL), lambda i: (i, 1))]
                + [resident(a) for a in (pa, pb, wo, n2, nf, wg, wu, wd)])
    out_specs = [row(D_MODEL)]
    out_shape = [jax.ShapeDtypeStruct((m, D_MODEL), F32)]
    operands = [x, oa, ob, gates, gates, pa, pb, wo, n2, nf, wg, wu, wd]
    if second is not None:
        in_specs += [resident(a) for a in second]
        operands += list(second)
        out_specs.append(pl.BlockSpec(second[0].shape, lambda i: (0, 0)))
        out_shape.append(jax.ShapeDtypeStruct(second[0].shape, F32))
    return pl.pallas_call(
        _merge_ffn_kernel,
        grid=(m // tm,),
        in_specs=in_specs,
        out_specs=out_specs,
        out_shape=out_shape,
        compiler_params=pltpu.CompilerParams(dimension_semantics=("arbitrary",),
                                             vmem_limit_bytes=MERGE_FFN_VMEM_LIMIT_BYTES),
        name="merge_ffn",
    )(*operands)


def _prep_weights(w_in, gla_gate_w2):
    w2 = jnp.concatenate([gla_gate_w2[0], jnp.zeros((LANES - GLA_GATE_RANK, GLA_DK), F32)], axis=0)
    return dict(w_in_t=jnp.transpose(w_in[0]).astype(BF16), w2=w2.astype(BF16))


def _project(x2d, norm1_g, gate_b, wts, tm, chunk, n_valid, casts=(), second=None):
    groups = [(0, GLA_COLS), (_OFF["dq"][0], DIL_WIDTH), (_OFF["dk"][0], 2 * DIL_WIDTH), (_OFF["ga"][0], 2 * D_MODEL)]
    return _in_proj(x2d, norm1_g, wts["w_in_t"], wts["w2"], gate_b, groups, _OFF["glr"][0],
                    [BF16, BF16, F32, BF16], tm, chunk, n_valid, casts, second)


def _tail(x2d, o_a, o_b, z_gates, wts, norm2_g, norm_f_g, tm, second):
    return _merge_ffn(x2d, o_a, o_b, z_gates, wts["pa"], wts["pb"], wts["wo"], norm2_g,
                      norm_f_g.reshape(1, D_MODEL), wts["wg"], wts["wu"], wts["wd"], tm, second)


def kernel(x_prompt, x_sample, state_gla, state_win0_k, state_win0_v, state_win1_k, state_win1_v,
           state_win2_k, state_win2_v, norm1_g, w_in, gla_gate_w2, gla_gate_b, gla_norm_g,
           proj_a, proj_b, w_out, norm2_g, w_ffn_gate, w_ffn_up, w_ffn_down, norm_f_g):
    wts = _prep_weights(w_in, gla_gate_w2)
    bp, seq, _ = x_prompt.shape
    bs, n_new, _ = x_sample.shape
    width = DIL_HEADS * DIL_HD

    xp = x_prompt.reshape(bp * seq, D_MODEL)
    xs = jnp.pad(x_sample, ((0, 0), (0, SAMPLE_PAD - n_new), (0, 0))).reshape(bs * SAMPLE_PAD, D_MODEL)
    later = dict(pa=proj_a[0], pb=proj_b[0], wo=w_out[0], wg=w_ffn_gate[0], wu=w_ffn_up[0], wd=w_ffn_down[0])
    z_gla, z_dq, z_kv, z_gates, b_cum, *rest = _project(xp, norm1_g, gla_gate_b, wts, ROW_BLOCK, GLA_CHUNK,
                                                        GLA_CHUNK, tuple(later.values()), (xs, SAMPLE_PAD, n_new))
    wts.update(zip(later.keys(), rest[:len(later)]))
    zs_gla, zs_dq, zs_kv, zs_gates, bs_cum = rest[len(later):]

    bufs = [jnp.transpose(a[0], (0, 2, 3, 1)).reshape(bs, width, a.shape[2]) for a in
            (state_win0_k, state_win0_v, state_win1_k, state_win1_v, state_win2_k, state_win2_v)]
    sample = (zs_dq.reshape(bs, SAMPLE_PAD, DIL_WIDTH), zs_kv.reshape(bs, SAMPLE_PAD, 2 * DIL_WIDTH), bufs, n_new)
    s0 = jnp.zeros((bp, GLA_HEADS, GLA_HDK, GLA_HDV), F32)
    o_a, gla_p, os_b, *win_s = _gla(z_gla.reshape(bp, seq, GLA_COLS), b_cum.reshape(bp, seq, GLA_DK), gla_norm_g,
                                    s0, GLA_CHUNK, GLA_CHUNK, GLA_ROWS_PROMPT, sample)

    o_b, *win_p = _dil_prompt(z_dq.reshape(bp, seq, DIL_WIDTH), z_kv.reshape(bp, seq, 2 * DIL_WIDTH))
    win_p = [jnp.transpose(a.reshape(bp, DIL_HEADS, DIL_HD, a.shape[2]), (0, 3, 1, 2))[None] for a in win_p]

    os_a, gla_s = _gla(zs_gla.reshape(bs, SAMPLE_PAD, GLA_COLS), bs_cum.reshape(bs, SAMPLE_PAD, GLA_DK),
                       gla_norm_g, state_gla[0], SAMPLE_PAD, n_new, GLA_ROWS_SAMPLE)
    sample_rows = (xs, os_a.reshape(bs * SAMPLE_PAD, GLA_DV), os_b.reshape(bs * SAMPLE_PAD, DIL_OUT), zs_gates)
    y_p, ys = _tail(xp, o_a.reshape(bp * seq, GLA_DV), o_b.reshape(bp * seq, DIL_OUT), z_gates, wts,
                    norm2_g, norm_f_g, ROW_BLOCK, sample_rows)
    y_prompt = y_p.reshape(bp, seq, D_MODEL)
    y_sample = ys.reshape(bs, SAMPLE_PAD, D_MODEL)[:, :n_new]
    win_s = [jnp.transpose(a.reshape(bs, DIL_HEADS, DIL_HD, a.shape[2]), (0, 3, 1, 2))[None] for a in win_s]

    return (y_prompt, y_sample, gla_p[None], *win_p, gla_s[None], *win_s)
```

```python
import functools

import jax
import jax.numpy as jnp
from jax import lax
from jax.experimental import pallas as pl
from jax.experimental.pallas import tpu as pltpu

F32 = jnp.float32
BF16 = jnp.bfloat16

D_MODEL = 1024
GLA_HEADS = 4
GLA_DK = 512
GLA_DV = 1024
GLA_HDK = 128
GLA_HDV = 256
GLA_GATE_RANK = 16
GLA_TAU = 16.0
DIL_WINDOWS = (128, 512, 2048)
DIL_RATES = (1, 4, 16)
DIL_GROUPS = 3
DIL_HEADS = 4
DIL_HD = 64
DIL_WIDTH = 768
DIL_OUT = 256
ALIBI_MAX = 8.0
D_FF = 2816
RMS_EPS = 1e-6

LANES = 128
SUBLANES = 8
BF16_SUBLANES = 16
Q_TILE = 128
TILE_SLOTS = 2
GLA_CHUNK = 128
GLA_ROWS_PROMPT = 4
GLA_ROWS_SAMPLE = 4
SAMPLE_PAD = 8
MASK_VALUE = -1e30
VMEM_LIMIT_BYTES = 48 * 1024 * 1024
IN_PROJ_VMEM_LIMIT_BYTES = 60 * 1024 * 1024
MERGE_FFN_VMEM_LIMIT_BYTES = 60 * 1024 * 1024
GLA_VMEM_LIMIT_BYTES = 60 * 1024 * 1024
_LN2 = 0.6931471805599453

_OFF = {}
_o = 0
for _name, _w in (("gq", GLA_DK), ("gk", GLA_DK), ("gv", GLA_DV), ("gr", GLA_DV), ("glr", GLA_GATE_RANK),
                  ("dq", DIL_WIDTH), ("dk", DIL_WIDTH), ("dv", DIL_WIDTH), ("ga", D_MODEL), ("gb", D_MODEL)):
    _OFF[_name] = (_o, _o + _w)
    _o += _w
GLA_COLS = 2 * GLA_DK + 2 * GLA_DV
GLA_SAFE_LOG_DECAY = 20.0
ROW_BLOCK = 512


def _params(*sem):
    return pltpu.CompilerParams(dimension_semantics=sem, vmem_limit_bytes=VMEM_LIMIT_BYTES)


def _dot(a, b):
    return jnp.dot(a, b, preferred_element_type=F32)


def _dot_nt(a, b):
    return lax.dot_general(a, b, (((1,), (1,)), ((), ())), preferred_element_type=F32)


def _dot_tn(a, b):
    return lax.dot_general(a, b, (((0,), (0,)), ((), ())), preferred_element_type=F32)


def _rms(x, g):
    return x * lax.rsqrt(jnp.mean(x * x, axis=-1, keepdims=True) + RMS_EPS) * g


IN_PROJ_COL_CHUNK = 1024


def _split3(x):
    x1 = x.astype(BF16)
    r1 = x - x1.astype(F32)
    x2 = r1.astype(BF16)
    x3 = (r1 - x2.astype(F32)).astype(BF16)
    return x1, x2, x3


def _in_proj_kernel(x_ref, g_ref, wt_ref, w2_ref, gb_ref, *refs, groups, low_rank_col, chunk, n_valid, n_casts,
                    second):
    n_out = len(groups) + 1
    cast_in, refs = refs[:n_casts], refs[n_casts:]
    if second:
        x2_ref, refs = refs[0], refs[1:]
    out_refs, cast_out, out2_refs = refs[:n_out], refs[n_out:n_out + n_casts], refs[n_out + n_casts:]

    if second:
        @pl.when(pl.program_id(0) == 0)
        def _():
            _in_proj_rows(x2_ref, g_ref, wt_ref, w2_ref, gb_ref, out2_refs, groups, low_rank_col, *second)

    for src, dst in zip(cast_in, cast_out):
        dst[...] = src[...].astype(dst.dtype)
    _in_proj_rows(x_ref, g_ref, wt_ref, w2_ref, gb_ref, out_refs, groups, low_rank_col, chunk, n_valid)


def _in_proj_rows(x_ref, g_ref, wt_ref, w2_ref, gb_ref, out_refs, groups, low_rank_col, chunk, n_valid):
    h = _rms(x_ref[...], g_ref[...]).astype(BF16)
    b_ref = out_refs[len(groups)]
    tm = x_ref.shape[0]

    for o_ref, (c_first, n) in zip(out_refs, groups):
        for c0 in range(0, n, IN_PROJ_COL_CHUNK):
            c1 = min(c0 + IN_PROJ_COL_CHUNK, n)
            o_ref[:, c0:c1] = _dot_nt(h, wt_ref[c_first + c0:c_first + c1, :]).astype(o_ref.dtype)

    low_rank = _dot_nt(h, wt_ref[low_rank_col:low_rank_col + LANES, :]).astype(BF16)
    gate = _dot(low_rank, w2_ref[...]) + gb_ref[...]
    log_a = (jnp.minimum(gate, 0.0) - jnp.log(1.0 + jnp.exp(-jnp.abs(gate)))) * (1.0 / GLA_TAU)
    if n_valid < chunk:
        tok = lax.broadcasted_iota(jnp.int32, log_a.shape, 0) & (chunk - 1)
        log_a = jnp.where(tok < n_valid, log_a, 0.0)
    span = min(tm, LANES)
    row = lax.broadcasted_iota(jnp.int32, (span, span), 0)
    col = lax.broadcasted_iota(jnp.int32, (span, span), 1)
    same_chunk = (row & -chunk) == (col & -chunk) if chunk < span else True
    tril = jnp.where((row >= col) & same_chunk, 1.0, 0.0).astype(BF16)
    for r0 in range(0, tm, span):
        parts = _split3(log_a[r0:r0 + span])[:2]
        b_ref[r0:r0 + span, :] = sum(_dot(tril, p) for p in parts)


def _cast_blocks(a, steps):
    count = max(c for c in range(1, steps + 1) if a.shape[0] % (c * BF16_SUBLANES) == 0)
    return a.shape[0] // count, count


def _in_proj(x, g, w_t, w2, gate_b, groups, low_rank_col, out_dtypes, tm, chunk, n_valid, casts=(), second=None):
    m, k = x.shape
    steps = m // tm
    assert chunk & (chunk - 1) == 0 and (chunk % LANES == 0 or LANES % chunk == 0) and tm % chunk == 0
    assert all(c % BF16_SUBLANES == 0 and n % LANES == 0 for c, n in groups) and low_rank_col % BF16_SUBLANES == 0
    resident = lambda a: pl.BlockSpec(a.shape, lambda i: (0, 0), pipeline_mode=pl.Buffered(1))
    out_widths = [n for _, n in groups] + [GLA_DK]

    def cast_spec(a):
        rows, count = _cast_blocks(a, steps)
        return pl.BlockSpec((rows, a.shape[1]), lambda i, count=count: (jnp.minimum(i, count - 1), 0))

    dtypes = list(out_dtypes) + [F32]
    in_specs = ([pl.BlockSpec((tm, k), lambda i: (i, 0))] + [resident(a) for a in (g, w_t, w2, gate_b)]
                + [cast_spec(a) for a in casts])
    out_specs = [pl.BlockSpec((tm, n), lambda i: (i, 0)) for n in out_widths] + [cast_spec(a) for a in casts]
    out_shape = ([jax.ShapeDtypeStruct((m, n), dt) for n, dt in zip(out_widths, dtypes)]
                 + [jax.ShapeDtypeStruct(a.shape, BF16) for a in casts])
    operands = [x, g, w_t, w2, gate_b, *casts]
    second_static = None
    if second is not None:
        x2, chunk2, n_valid2 = second
        second_static = (chunk2, n_valid2)
        in_specs.append(resident(x2))
        operands.append(x2)
        out_specs += [pl.BlockSpec((x2.shape[0], n), lambda i: (0, 0)) for n in out_widths]
        out_shape += [jax.ShapeDtypeStruct((x2.shape[0], n), dt) for n, dt in zip(out_widths, dtypes)]
    return pl.pallas_call(
        functools.partial(_in_proj_kernel, groups=tuple(groups), low_rank_col=low_rank_col,
                          chunk=chunk, n_valid=n_valid, n_casts=len(casts), second=second_static),
        grid=(steps,),
        in_specs=in_specs,
        out_specs=out_specs,
        out_shape=out_shape,
        compiler_params=pltpu.CompilerParams(dimension_semantics=("arbitrary",),
                                             vmem_limit_bytes=IN_PROJ_VMEM_LIMIT_BYTES),
        name="in_proj",
    )(*operands)


N_SAMPLE_IN = 2 + 2 * DIL_GROUPS
N_SAMPLE_OUT = 1 + 2 * DIL_GROUPS
RING_SLOTS = 3


def _gla_kernel(*refs, chunk, n_valid, n_new_sample, n_steps):
    sample_in = ()
    if n_new_sample:
        z_hbm, b_hbm, ng_ref, s0_ref = refs[:4]
        sample_in, refs = refs[4:4 + N_SAMPLE_IN], refs[4 + N_SAMPLE_IN:]
    else:
        q_ref, k_ref, v_ref, r_ref, b_ref, ng_ref, s0_ref = refs[:7]
        refs = refs[7:]
    o_ref, sout_ref = refs[:2]
    refs = refs[2:]
    sample_out = ()
    if n_new_sample:
        sample_out, refs = refs[:N_SAMPLE_OUT], refs[N_SAMPLE_OUT:]
    s_ref, oi_ref, kf_ref, vf_ref = refs[:4]

    c = pl.program_id(1)
    rows = s_ref.shape[0]

    if n_new_sample:
        buf_hbm = sample_in[2:]
        n_buf = len(buf_hbm)
        rings, z_ring, b_ring, sem = refs[4:4 + n_buf], refs[4 + n_buf], refs[5 + n_buf], refs[6 + n_buf]
        n_chunks = pl.num_programs(1)
        step = pl.program_id(0) * n_chunks + c

        def ring_copies(s):
            slot = s % RING_SLOTS
            block = (pl.ds((s // n_chunks) * rows, rows), pl.ds((s % n_chunks) * chunk, chunk))
            return ([pltpu.make_async_copy(src.at[pl.ds(s, 1)], ring.at[pl.ds(slot, 1)], sem.at[j, slot])
                     for j, (src, ring) in enumerate(zip(buf_hbm, rings))]
                    + [pltpu.make_async_copy(z_hbm.at[block], z_ring.at[slot], sem.at[n_buf, slot]),
                       pltpu.make_async_copy(b_hbm.at[block], b_ring.at[slot], sem.at[n_buf + 1, slot])])

        @pl.when(step == 0)
        def _():
            for s in range(RING_SLOTS - 1):
                for cp in ring_copies(s):
                    cp.start()

        @pl.when(step + (RING_SLOTS - 1) < n_steps)
        def _():
            for cp in ring_copies(step + (RING_SLOTS - 1)):
                cp.start()

        for cp in ring_copies(step):
            cp.wait()
        slot = step % RING_SLOTS
        sample_in = tuple(sample_in[:2]) + tuple(ring.at[pl.ds(slot, 1)] for ring in rings)
        z_now, b_ref = z_ring.at[slot], b_ring.at[slot]
        q_ref, k_ref = z_now.at[:, :, 0:GLA_DK], z_now.at[:, :, GLA_DK:2 * GLA_DK]
        v_ref, r_ref = z_now.at[:, :, 2 * GLA_DK:2 * GLA_DK + GLA_DV], z_now.at[:, :, 2 * GLA_DK + GLA_DV:GLA_COLS]

    @pl.when(c == 0)
    def _():
        s_ref[...] = s0_ref[...]

    for r in range(rows):
        _gla_chunk(r, q_ref, k_ref, v_ref, r_ref, b_ref, ng_ref, o_ref, s_ref, oi_ref, chunk, n_valid)
    if n_new_sample:
        _dil_sample_kernel(*sample_in, *sample_out, n_new=n_new_sample)

    for r in range(rows):
        @pl.when(jnp.min(b_ref[r, chunk - 1:chunk, :]) < -GLA_SAFE_LOG_DECAY)
        def _(r=r):
            _gla_chunk_exact_intra(r, q_ref, k_ref, v_ref, r_ref, b_ref, ng_ref, o_ref, oi_ref, kf_ref, vf_ref,
                                   chunk, n_valid)

    @pl.when(c == pl.num_programs(1) - 1)
    def _():
        sout_ref[...] = s_ref[...]


def _gla_epilogue(o, r, vs, r_ref, ng_ref, o_ref):
    gr = r_ref[r, :, vs].astype(F32)
    o_ref[r, :, vs] = (_rms(o, ng_ref[...]) * (gr * jax.nn.sigmoid(gr))).astype(o_ref.dtype)


def _gla_values(r, vs, v_ref, chunk, n_valid):
    vh = v_ref[r, :, vs]
    if n_valid < chunk:
        tok = lax.broadcasted_iota(jnp.int32, vh.shape, 0)
        vh = jnp.where(tok < n_valid, vh, jnp.zeros_like(vh))
    return vh


def _gla_chunk_exact_intra(r, q_ref, k_ref, v_ref, r_ref, b_ref, ng_ref, o_ref, oi_ref, kf_ref, vf_ref,
                           chunk, n_valid):
    tok = lax.broadcasted_iota(jnp.int32, (chunk, 1), 0)
    for h in range(GLA_HEADS):
        ks = slice(h * GLA_HDK, (h + 1) * GLA_HDK)
        vs = slice(h * GLA_HDV, (h + 1) * GLA_HDV)
        bh = b_ref[r, :, ks]
        qh = q_ref[r, :, ks].astype(F32) * (GLA_HDK ** -0.5)
        kf_ref[...] = k_ref[r, :, ks].astype(F32)
        vf_ref[...] = _gla_values(r, vs, v_ref, chunk, n_valid).astype(F32)

        def eight_keys(i, acc, bh=bh, qh=qh, ks=ks):
            rows8 = pl.ds(pl.multiple_of(i * SUBLANES, SUBLANES), SUBLANES)
            b8, k8, v8 = b_ref[r, rows8, ks], kf_ref[rows8, :], vf_ref[rows8, :]
            for j in range(SUBLANES):
                s = i * SUBLANES + j
                decay = jnp.exp(jnp.where(tok >= s, bh - b8[j:j + 1], MASK_VALUE))
                w = jnp.sum(qh * decay * k8[j:j + 1], axis=-1, keepdims=True)
                acc = acc + w * v8[j:j + 1]
            return acc

        intra = lax.fori_loop(0, chunk // SUBLANES, eight_keys, jnp.zeros((chunk, GLA_HDV), F32))
        _gla_epilogue(oi_ref[r, h] + intra, r, vs, r_ref, ng_ref, o_ref)


def _gla_chunk(r, q_ref, k_ref, v_ref, r_ref, b_ref, ng_ref, o_ref, s_ref, oi_ref, chunk, n_valid):
    row = lax.broadcasted_iota(jnp.int32, (chunk, chunk), 0)
    col = lax.broadcasted_iota(jnp.int32, (chunk, chunk), 1)
    causal = row >= col
    b = b_ref[r]

    for h in range(GLA_HEADS):
        ks = slice(h * GLA_HDK, (h + 1) * GLA_HDK)
        vs = slice(h * GLA_HDV, (h + 1) * GLA_HDV)
        bh = b[:, ks]
        qh = q_ref[r, :, ks].astype(F32) * (GLA_HDK ** -0.5)
        kh = k_ref[r, :, ks].astype(F32)
        vh = _gla_values(r, vs, v_ref, chunk, n_valid)
        qt = (qh * jnp.exp(bh)).astype(BF16)
        kt = (kh * jnp.exp(-bh)).astype(BF16)
        kd = (kh * jnp.exp(bh[chunk - 1:chunk, :] - bh)).astype(BF16)
        s_old = s_ref[r, h]
        scores = jnp.where(causal, _dot_nt(qt, kt), 0.0).astype(BF16)
        o_state = _dot(qt, s_old.astype(BF16))
        oi_ref[r, h] = o_state
        dec = jnp.exp(jnp.broadcast_to(bh[chunk - 1:chunk, :], (GLA_HDK, GLA_HDK)).T)
        s_ref[r, h] = s_old * jnp.concatenate([dec, dec], axis=1) + _dot_tn(kd, vh)
        _gla_epilogue(o_state + _dot(scores, vh), r, vs, r_ref, ng_ref, o_ref)


def _gla(z, b_cum, norm_g, s0, chunk, n_valid, rows, sample=None):
    bsz, t, _ = z.shape
    n_chunks = t // chunk
    state = pl.BlockSpec((rows, GLA_HEADS, GLA_HDK, GLA_HDV), lambda b, c: (b, 0, 0, 0))
    in_specs = [pl.BlockSpec((rows, chunk, GLA_DK), lambda b, c: (b, c, 0)),
                pl.BlockSpec((rows, chunk, GLA_DK), lambda b, c: (b, c, 1)),
                pl.BlockSpec((rows, chunk, GLA_DV), lambda b, c: (b, c, 1)),
                pl.BlockSpec((rows, chunk, GLA_DV), lambda b, c: (b, c, 2)),
                pl.BlockSpec((rows, chunk, GLA_DK), lambda b, c: (b, c, 0)),
                pl.BlockSpec((1, GLA_HDV), lambda b, c: (0, 0)),
                state]
    out_specs = [pl.BlockSpec((rows, chunk, GLA_DV), lambda b, c: (b, c, 0)), state]
    out_shape = [jax.ShapeDtypeStruct((bsz, t, GLA_DV), BF16),
                 jax.ShapeDtypeStruct((bsz, GLA_HEADS, GLA_HDK, GLA_HDV), F32)]
    operands = [z, z, z, z, b_cum, norm_g, s0]
    n_new = 0
    scratch = [pltpu.VMEM((rows, GLA_HEADS, GLA_HDK, GLA_HDV), F32),
               pltpu.VMEM((rows, GLA_HEADS, chunk, GLA_HDV), F32),
               pltpu.VMEM((chunk, GLA_HDK), F32),
               pltpu.VMEM((chunk, GLA_HDV), F32)]
    if sample is not None:
        dq, kvn, bufs, n_new = sample
        assert dq.shape[0] == (bsz // rows) * n_chunks
        assert dq.shape[0] >= RING_SLOTS
        per_step = lambda a: pl.BlockSpec((1,) + a.shape[1:], lambda b, c: (b * n_chunks + c, 0, 0))
        in_hbm = pl.BlockSpec(memory_space=pl.ANY)
        in_specs = ([in_hbm, in_hbm] + in_specs[5:] + [per_step(a) for a in (dq, kvn)] + [in_hbm for _ in bufs])
        operands = [z, b_cum, norm_g, s0, dq, kvn, *bufs]
        o_s = jax.ShapeDtypeStruct((dq.shape[0], dq.shape[1], DIL_OUT), BF16)
        out_specs += [per_step(o_s)] + [per_step(a) for a in bufs]
        out_shape += [o_s] + [jax.ShapeDtypeStruct(a.shape, a.dtype) for a in bufs]
        scratch += [pltpu.VMEM((RING_SLOTS,) + a.shape[1:], a.dtype) for a in bufs]
        scratch += [pltpu.VMEM((RING_SLOTS, rows, chunk, GLA_COLS), z.dtype),
                    pltpu.VMEM((RING_SLOTS, rows, chunk, GLA_DK), b_cum.dtype)]
        scratch.append(pltpu.SemaphoreType.DMA((len(bufs) + 2, RING_SLOTS)))
    return pl.pallas_call(
        functools.partial(_gla_kernel, chunk=chunk, n_valid=n_valid, n_new_sample=n_new,
                          n_steps=(bsz // rows) * n_chunks),
        grid=(bsz // rows, n_chunks),
        in_specs=in_specs,
        out_specs=out_specs,
        out_shape=out_shape,
        scratch_shapes=scratch,
        compiler_params=pltpu.CompilerParams(dimension_semantics=("arbitrary", "arbitrary"),
                                             vmem_limit_bytes=GLA_VMEM_LIMIT_BYTES),
        name="gla",
    )(*operands)


def _alibi_slope(g, head):
    n = DIL_GROUPS * DIL_HEADS
    return jnp.exp((-ALIBI_MAX * _LN2 / n) * (head + (g * DIL_HEADS + 1.0)))


def _pair_slopes(g, hp, rows):
    head = 2.0 * hp.astype(F32) + jnp.where(rows >= Q_TILE, 1.0, 0.0)
    return _alibi_slope(g, head)


def _dil_prompt_kernel(q0_ref, q1_ref, q2_ref, k0_ref, k1_ref, k2_ref, v0_ref, v1_ref, v2_ref,
                       o_ref, wk0_ref, wv0_ref, wk1_ref, wv1_ref, wk2_ref, wv2_ref,
                       qf_ref, og_ref, ld_ref, bf_ref, br_ref, s_ref, p_ref, m_ref, *, seq):
    hp = pl.program_id(1)
    q_refs = (q0_ref, q1_ref, q2_ref)
    k_refs = (k0_ref, k1_ref, k2_ref)
    v_refs = (v0_ref, v1_ref, v2_ref)
    wk_refs = (wk0_ref, wk1_ref, wk2_ref)
    wv_refs = (wv0_ref, wv1_ref, wv2_ref)

    for g in range(DIL_GROUPS):
        qf_ref[g] = q_refs[g][...].astype(F32) * (DIL_HD ** -0.5)

    for g in range(DIL_GROUPS):
        rate = float(DIL_RATES[g])
        rows = lax.broadcasted_iota(jnp.int32, (2 * Q_TILE, Q_TILE), 0)
        cols = lax.broadcasted_iota(jnp.int32, (2 * Q_TILE, Q_TILE), 1)
        dist = (rows & (Q_TILE - 1)) - cols
        slope = _pair_slopes(g, hp, rows)
        bf_ref[g] = jnp.where(dist >= 0, -slope * rate * dist.astype(F32), MASK_VALUE)
        if g < 2:
            rows = lax.broadcasted_iota(jnp.int32, (2 * Q_TILE, 2 * Q_TILE), 0)
            cols = lax.broadcasted_iota(jnp.int32, (2 * Q_TILE, 2 * Q_TILE), 1)
            dist = (rows & (Q_TILE - 1)) + Q_TILE - cols
            slope = _pair_slopes(g, hp, rows)
            ok = jnp.abs(dist - Q_TILE // 2) <= Q_TILE // 2
            br_ref[g] = jnp.where(ok, -slope * rate * dist.astype(F32), MASK_VALUE)

    lane = lax.broadcasted_iota(jnp.int32, (Q_TILE, LANES), 1)
    lo = lane < DIL_HD

    def rows_of(start, size, rate):
        return pl.ds(start, size) if rate == 1 else pl.ds(start, size, stride=rate)

    def scores_stage(g, q_row, k_row, n_keys, slot):
        rate = DIL_RATES[g]
        q2 = qf_ref[g, rows_of(q_row, Q_TILE, rate), :]
        k2 = k_refs[g][rows_of(k_row, n_keys, rate), :].astype(BF16)
        qs = jnp.concatenate([jnp.where(lo, q2, 0.0), jnp.where(lo, 0.0, q2)], axis=0).astype(BF16)
        s_ref[slot, :, 0:n_keys] = _dot_nt(qs, k2)

    def softmax_stage(n_keys, bias, slot):
        s = s_ref[slot, :, 0:n_keys] + bias
        m = jnp.max(s, axis=-1, keepdims=True)
        p_ref[slot, :, 0:n_keys] = jnp.exp(s - m).astype(BF16)
        m_ref[slot] = jnp.where(lo, m[:Q_TILE], m[Q_TILE:])

    def values_stage(g, q_row, k_row, n_keys, slot):
        rate = DIL_RATES[g]
        v2 = v_refs[g][rows_of(k_row, n_keys, rate), :].astype(BF16)
        vo = jnp.concatenate([v2, jnp.ones((n_keys, LANES), BF16)], axis=1)
        r = _dot(p_ref[slot, :, 0:n_keys], vo)
        o2 = jnp.where(lo, r[:Q_TILE, :LANES], r[Q_TILE:, :LANES])
        l2 = jnp.where(lo, r[:Q_TILE, LANES:], r[Q_TILE:, LANES:])
        og_ref[g, rows_of(q_row, Q_TILE, rate), :] = o2 / l2
        ld_ref[g, rows_of(q_row, Q_TILE, rate), :] = m_ref[slot] + jnp.log(l2)

    tiles = []
    for g in range(DIL_GROUPS):
        rate = DIL_RATES[g]
        span = rate * Q_TILE
        for rho in range(rate):
            tiles.append((g, rho, rho, Q_TILE, ("first", g)))
        for n in range(1, seq // span):
            for rho in range(rate):
                tiles.append((g, rho + n * span, rho + (n - 1) * span, 2 * Q_TILE, ("later", g)))
    sets = [tiles[i:i + TILE_SLOTS] for i in range(0, len(tiles), TILE_SLOTS)]

    def slot_of(set_index, u):
        return (set_index % 2) * TILE_SLOTS + u

    for t in range(len(sets) + 2):
        if 0 <= t - 2 < len(sets):
            for u, (g, q_row, k_row, n_keys, _) in enumerate(sets[t - 2]):
                values_stage(g, q_row, k_row, n_keys, slot_of(t - 2, u))
        if 0 <= t - 1 < len(sets):
            for u, (g, _, _, n_keys, (kind, gb)) in enumerate(sets[t - 1]):
                softmax_stage(n_keys, bf_ref[gb] if kind == "first" else br_ref[gb], slot_of(t - 1, u))
        if t < len(sets):
            for u, (g, q_row, k_row, n_keys, _) in enumerate(sets[t]):
                scores_stage(g, q_row, k_row, n_keys, slot_of(t, u))

    for r0 in range(0, seq, Q_TILE):
        rs = slice(r0, r0 + Q_TILE)
        ld = [ld_ref[g, rs, :] for g in range(DIL_GROUPS)]
        top = jnp.maximum(jnp.maximum(ld[0], ld[1]), ld[2])
        w = [jnp.exp(x - top) for x in ld]
        num = sum(w[g] * og_ref[g, rs, :] for g in range(DIL_GROUPS))
        o_ref[rs, :] = (num / (w[0] + w[1] + w[2])).astype(o_ref.dtype)

    for g in range(DIL_GROUPS):
        keep = wk_refs[g].shape[1]
        for src, dst in ((k_refs[g], wk_refs[g]), (v_refs[g], wv_refs[g])):
            for off in range(0, keep, LANES):
                dst[:, off:off + LANES] = src[seq - keep + off:seq - keep + off + LANES, :].T


def _dil_prompt(dq, kv):
    bsz, seq, _ = dq.shape
    pairs = DIL_HEADS * DIL_HD // LANES
    nblk = DIL_WIDTH // LANES
    keeps = [min(w, seq) for w in DIL_WINDOWS]

    def spec(col0):
        return pl.BlockSpec((None, seq, LANES), lambda b, hp, col0=col0: (b, 0, col0 + hp))

    in_specs = ([spec(g * pairs) for g in range(DIL_GROUPS)]
                + [spec(g * pairs) for g in range(DIL_GROUPS)]
                + [spec(nblk + g * pairs) for g in range(DIL_GROUPS)])
    return pl.pallas_call(
        functools.partial(_dil_prompt_kernel, seq=seq),
        grid=(bsz, pairs),
        in_specs=in_specs,
        out_specs=[pl.BlockSpec((None, seq, LANES), lambda b, hp: (b, 0, hp))]
                  + [pl.BlockSpec((None, LANES, w), lambda b, hp: (b, hp, 0)) for w in keeps for _ in range(2)],
        out_shape=[jax.ShapeDtypeStruct((bsz, seq, DIL_OUT), BF16)]
                  + [jax.ShapeDtypeStruct((bsz, DIL_HEADS * DIL_HD, w), F32) for w in keeps for _ in range(2)],
        scratch_shapes=[pltpu.VMEM((DIL_GROUPS, seq, LANES), F32),
                        pltpu.VMEM((DIL_GROUPS, seq, LANES), F32),
                        pltpu.VMEM((DIL_GROUPS, seq, LANES), F32),
                        pltpu.VMEM((DIL_GROUPS, 2 * Q_TILE, Q_TILE), F32),
                        pltpu.VMEM((2, 2 * Q_TILE, 2 * Q_TILE), F32),
                        pltpu.VMEM((2 * TILE_SLOTS, 2 * Q_TILE, 2 * Q_TILE), F32),
                        pltpu.VMEM((2 * TILE_SLOTS, 2 * Q_TILE, 2 * Q_TILE), BF16),
                        pltpu.VMEM((2 * TILE_SLOTS, Q_TILE, LANES), F32)],
        compiler_params=_params("parallel", "arbitrary"),
        name="dil_prompt",
    )(dq, dq, dq, kv, kv, kv, kv, kv, kv)


def _dil_sample_kernel(q_ref, kvn_ref, kb0_ref, vb0_ref, kb1_ref, vb1_ref, kb2_ref, vb2_ref,
                       o_ref, nk0_ref, nv0_ref, nk1_ref, nv1_ref, nk2_ref, nv2_ref, *, n_new):
    kb_refs = (kb0_ref, kb1_ref, kb2_ref)
    vb_refs = (vb0_ref, vb1_ref, vb2_ref)
    nk_refs = (nk0_ref, nk1_ref, nk2_ref)
    nv_refs = (nv0_ref, nv1_ref, nv2_ref)
    pad = q_ref.shape[1]
    nrow = DIL_HEADS * pad
    width = DIL_HEADS * DIL_HD

    lane_head = jnp.right_shift(lax.broadcasted_iota(jnp.int32, (pad, width), 1), DIL_HD.bit_length() - 1)

    def bias_for(g, dist):
        rate = DIL_RATES[g]
        rows = lax.broadcasted_iota(jnp.int32, dist.shape, 0)
        head = sum(jnp.where(rows >= h * pad, 1.0, 0.0) for h in range(1, DIL_HEADS))
        ok = (dist >= 0) & (dist <= DIL_WINDOWS[g]) & ((dist & (rate - 1)) == 0)
        return jnp.where(ok, -_alibi_slope(g, head) * dist.astype(F32), MASK_VALUE)

    sel_r = lax.broadcasted_iota(jnp.int32, (pad, LANES), 0)
    sel_c = lax.broadcasted_iota(jnp.int32, (pad, LANES), 1)
    place = jnp.where((sel_r < n_new) & (sel_c == sel_r + (LANES - n_new)), 1.0, 0.0).astype(BF16)
    tail_lane = lax.broadcasted_iota(jnp.int32, (width, LANES), 1) >= LANES - n_new

    def shifted(buf_t, new_rows):
        length = buf_t.shape[1]
        rolled = pltpu.roll(buf_t, length - n_new, 1)
        new_t = sum(_dot_tn(part, place) for part in _split3(new_rows))
        last = jnp.where(tail_lane, new_t, rolled[:, length - LANES:])
        return rolled, last

    scores, values, transposed = [], [], []
    for g in range(DIL_GROUPS):
        length = kb_refs[g].shape[2]
        cs = slice(g * width, (g + 1) * width)
        vcs = slice(DIL_WIDTH + g * width, DIL_WIDTH + (g + 1) * width)
        qg = q_ref[0, :, cs].astype(F32) * (DIL_HD ** -0.5)
        qs = jnp.concatenate([jnp.where(lane_head == h, qg, 0.0) for h in range(DIL_HEADS)], axis=0).astype(BF16)
        kb, vb = kb_refs[g][0], vb_refs[g][0]
        kn, vn = kvn_ref[0, :, cs], kvn_ref[0, :, vcs]
        rows = lax.broadcasted_iota(jnp.int32, (nrow, length), 0)
        cols = lax.broadcasted_iota(jnp.int32, (nrow, length), 1)
        dist = length + (rows & (pad - 1)) - cols
        scores.append(_dot(qs, kb.astype(BF16)) + bias_for(g, dist))
        values.append(vb.astype(BF16))
        transposed.append(True)
        rows = lax.broadcasted_iota(jnp.int32, (nrow, pad), 0)
        cols = lax.broadcasted_iota(jnp.int32, (nrow, pad), 1)
        dist = jnp.where(cols < n_new, (rows & (pad - 1)) - cols, -1)
        scores.append(_dot_nt(qs, kn.astype(BF16)) + bias_for(g, dist))
        values.append(vn.astype(BF16))
        transposed.append(False)
        for buf, new, out_ref in ((kb, kn, nk_refs[g]), (vb, vn, nv_refs[g])):
            rolled, last = shifted(buf, new)
            if length > LANES:
                out_ref[0, :, 0:length - LANES] = rolled[:, 0:length - LANES]
            out_ref[0, :, length - LANES:length] = last

    top = functools.reduce(jnp.maximum, [jnp.max(s, axis=-1, keepdims=True) for s in scores])
    probs = [jnp.exp(s - top) for s in scores]
    den = sum(jnp.sum(p, axis=-1, keepdims=True) for p in probs)
    acc = sum((_dot_nt if t else _dot)(p.astype(BF16), v)
              for p, v, t in zip(probs, values, transposed)) / den
    out = sum(jnp.where(lane_head == h, acc[h * pad:(h + 1) * pad], 0.0) for h in range(DIL_HEADS))
    o_ref[0] = out.astype(o_ref.dtype)


FFN_COL_CHUNK = D_FF // 2


def _merge_ffn_kernel(x_ref, oa_ref, ob_ref, ga_ref, gb_ref, pa_ref, pb_ref, wo_ref, n2_ref, nf_ref,
                      wg_ref, wu_ref, wd_ref, *refs):
    weights = (pa_ref, pb_ref, wo_ref, n2_ref, nf_ref, wg_ref, wu_ref, wd_ref)
    if len(refs) > 1:
        x2_ref, oa2_ref, ob2_ref, g2_ref, y_ref, y2_ref = refs

        @pl.when(pl.program_id(0) == 0)
        def _():
            _merge_ffn_rows(x2_ref, oa2_ref, ob2_ref, g2_ref.at[:, 0:D_MODEL], g2_ref.at[:, D_MODEL:2 * D_MODEL],
                            *weights, y2_ref)
    else:
        y_ref, = refs
    _merge_ffn_rows(x_ref, oa_ref, ob_ref, ga_ref, gb_ref, *weights, y_ref)


def _merge_ffn_rows(x_ref, oa_ref, ob_ref, ga_ref, gb_ref, pa_ref, pb_ref, wo_ref, n2_ref, nf_ref,
                    wg_ref, wu_ref, wd_ref, y_ref):
    pa = _dot(oa_ref[...], pa_ref[...])
    pb = _dot(ob_ref[...], pb_ref[...])
    merged = jax.nn.sigmoid(ga_ref[...].astype(F32)) * pa + jax.nn.sigmoid(gb_ref[...].astype(F32)) * pb
    x1 = x_ref[...] + _dot(merged.astype(BF16), wo_ref[...])
    h = _rms(x1, n2_ref[...]).astype(BF16)
    acc = x1
    for c0 in range(0, D_FF, FFN_COL_CHUNK):
        cs = slice(c0, c0 + FFN_COL_CHUNK)
        gate = _dot(h, wg_ref[:, cs])
        up = _dot(h, wu_ref[:, cs])
        act = (gate * jax.nn.sigmoid(gate) * up).astype(BF16)
        acc = acc + _dot(act, wd_ref[cs, :])
    y_ref[...] = _rms(acc, nf_ref[...])


def _merge_ffn(x, oa, ob, gates, pa, pb, wo, n2, nf, wg, wu, wd, tm, second=None):
    m = x.shape[0]
    row = lambda w: pl.BlockSpec((tm, w), lambda i: (i, 0))
    resident = lambda a: pl.BlockSpec(a.shape, lambda i: (0, 0), pipeline_mode=pl.Buffered(1))
    in_specs = ([row(D_MODEL), row(GLA_DV), row(DIL_OUT), row(D_MODEL), pl.BlockSpec((tm, D_MODEL), lambda i: (i, 1))]
                + [resident(a) for a in (pa, pb, wo, n2, nf, wg, wu, wd)])
    out_specs = [row(D_MODEL)]
    out_shape = [jax.ShapeDtypeStruct((m, D_MODEL), F32)]
    operands = [x, oa, ob, gates, gates, pa, pb, wo, n2, nf, wg, wu, wd]
    if second is not None:
        in_specs += [resident(a) for a in second]
        operands += list(second)
        out_specs.append(pl.BlockSpec(second[0].shape, lambda i: (0, 0)))
        out_shape.append(jax.ShapeDtypeStruct(second[0].shape, F32))
    return pl.pallas_call(
        _merge_ffn_kernel,
        grid=(m // tm,),
        in_specs=in_specs,
        out_specs=out_specs,
        out_shape=out_shape,
        compiler_params=pltpu.CompilerParams(dimension_semantics=("arbitrary",),
                                             vmem_limit_bytes=MERGE_FFN_VMEM_LIMIT_BYTES),
        name="merge_ffn",
    )(*operands)


def _prep_weights(w_in, gla_gate_w2):
    w2 = jnp.concatenate([gla_gate_w2[0], jnp.zeros((LANES - GLA_GATE_RANK, GLA_DK), F32)], axis=0)
    return dict(w_in_t=jnp.transpose(w_in[0]).astype(BF16), w2=w2.astype(BF16))


def _project(x2d, norm1_g, gate_b, wts, tm, chunk, n_valid, casts=(), second=None):
    groups = [(0, GLA_COLS), (_OFF["dq"][0], DIL_WIDTH), (_OFF["dk"][0], 2 * DIL_WIDTH), (_OFF["ga"][0], 2 * D_MODEL)]
    return _in_proj(x2d, norm1_g, wts["w_in_t"], wts["w2"], gate_b, groups, _OFF["glr"][0],
                    [BF16, BF16, F32, BF16], tm, chunk, n_valid, casts, second)


def _tail(x2d, o_a, o_b, z_gates, wts, norm2_g, norm_f_g, tm, second):
    return _merge_ffn(x2d, o_a, o_b, z_gates, wts["pa"], wts["pb"], wts["wo"], norm2_g,
                      norm_f_g.reshape(1, D_MODEL), wts["wg"], wts["wu"], wts["wd"], tm, second)


def kernel(x_prompt, x_sample, state_gla, state_win0_k, state_win0_v, state_win1_k, state_win1_v,
           state_win2_k, state_win2_v, norm1_g, w_in, gla_gate_w2, gla_gate_b, gla_norm_g,
           proj_a, proj_b, w_out, norm2_g, w_ffn_gate, w_ffn_up, w_ffn_down, norm_f_g):
    wts = _prep_weights(w_in, gla_gate_w2)
    bp, seq, _ = x_prompt.shape
    bs, n_new, _ = x_sample.shape
    width = DIL_HEADS * DIL_HD

    xp = x_prompt.reshape(bp * seq, D_MODEL)
    xs = jnp.pad(x_sample, ((0, 0), (0, SAMPLE_PAD - n_new), (0, 0))).reshape(bs * SAMPLE_PAD, D_MODEL)
    later = dict(pa=proj_a[0], pb=proj_b[0], wo=w_out[0], wg=w_ffn_gate[0], wu=w_ffn_up[0], wd=w_ffn_down[0])
    z_gla, z_dq, z_kv, z_gates, b_cum, *rest = _project(xp, norm1_g, gla_gate_b, wts, ROW_BLOCK, GLA_CHUNK,
                                                        GLA_CHUNK, tuple(later.values()), (xs, SAMPLE_PAD, n_new))
    wts.update(zip(later.keys(), rest[:len(later)]))
    zs_gla, zs_dq, zs_kv, zs_gates, bs_cum = rest[len(later):]

    bufs = [jnp.transpose(a[0], (0, 2, 3, 1)).reshape(bs, width, a.shape[2]) for a in
            (state_win0_k, state_win0_v, state_win1_k, state_win1_v, state_win2_k, state_win2_v)]
    sample = (zs_dq.reshape(bs, SAMPLE_PAD, DIL_WIDTH), zs_kv.reshape(bs, SAMPLE_PAD, 2 * DIL_WIDTH), bufs, n_new)
    s0 = jnp.zeros((bp, GLA_HEADS, GLA_HDK, GLA_HDV), F32)
    o_a, gla_p, os_b, *win_s = _gla(z_gla.reshape(bp, seq, GLA_COLS), b_cum.reshape(bp, seq, GLA_DK), gla_norm_g,
                                    s0, GLA_CHUNK, GLA_CHUNK, GLA_ROWS_PROMPT, sample)

    o_b, *win_p = _dil_prompt(z_dq.reshape(bp, seq, DIL_WIDTH), z_kv.reshape(bp, seq, 2 * DIL_WIDTH))
    win_p = [jnp.transpose(a.reshape(bp, DIL_HEADS, DIL_HD, a.shape[2]), (0, 3, 1, 2))[None] for a in win_p]

    os_a, gla_s = _gla(zs_gla.reshape(bs, SAMPLE_PAD, GLA_COLS), bs_cum.reshape(bs, SAMPLE_PAD, GLA_DK),
                       gla_norm_g, state_gla[0], SAMPLE_PAD, n_new, GLA_ROWS_SAMPLE)
    sample_rows = (xs, os_a.reshape(bs * SAMPLE_PAD, GLA_DV), os_b.reshape(bs * SAMPLE_PAD, DIL_OUT), zs_gates)
    y_p, ys = _tail(xp, o_a.reshape(bp * seq, GLA_DV), o_b.reshape(bp * seq, DIL_OUT), z_gates, wts,
                    norm2_g, norm_f_g, ROW_BLOCK, sample_rows)
    y_prompt = y_p.reshape(bp, seq, D_MODEL)
    y_sample = ys.reshape(bs, SAMPLE_PAD, D_MODEL)[:, :n_new]
    win_s = [jnp.transpose(a.reshape(bs, DIL_HEADS, DIL_HD, a.shape[2]), (0, 3, 1, 2))[None] for a in win_s]

    return (y_prompt, y_sample, gla_p[None], *win_p, gla_s[None], *win_s)
```

```python
import functools

import jax
import jax.numpy as jnp
from jax import lax
from jax.experimental import pallas as pl
from jax.experimental.pallas import tpu as pltpu

F32 = jnp.float32
BF16 = jnp.bfloat16

D_MODEL = 1024
GLA_HEADS = 4
GLA_DK = 512
GLA_DV = 1024
GLA_HDK = 128
GLA_HDV = 256
GLA_GATE_RANK = 16
GLA_TAU = 16.0
DIL_WINDOWS = (128, 512, 2048)
DIL_RATES = (1, 4, 16)
DIL_GROUPS = 3
DIL_HEADS = 4
DIL_HD = 64
DIL_WIDTH = 768
DIL_OUT = 256
ALIBI_MAX = 8.0
D_FF = 2816
RMS_EPS = 1e-6

LANES = 128
SUBLANES = 8
BF16_SUBLANES = 16
Q_TILE = 128
TILE_SLOTS = 2
GLA_CHUNK = 128
GLA_ROWS_PROMPT = 4
GLA_ROWS_SAMPLE = 4
SAMPLE_PAD = 8
MASK_VALUE = -1e30
VMEM_LIMIT_BYTES = 48 * 1024 * 1024
IN_PROJ_VMEM_LIMIT_BYTES = 60 * 1024 * 1024
MERGE_FFN_VMEM_LIMIT_BYTES = 60 * 1024 * 1024
GLA_VMEM_LIMIT_BYTES = 60 * 1024 * 1024
_LN2 = 0.6931471805599453

_OFF = {}
_o = 0
for _name, _w in (("gq", GLA_DK), ("gk", GLA_DK), ("gv", GLA_DV), ("gr", GLA_DV), ("glr", GLA_GATE_RANK),
                  ("dq", DIL_WIDTH), ("dk", DIL_WIDTH), ("dv", DIL_WIDTH), ("ga", D_MODEL), ("gb", D_MODEL)):
    _OFF[_name] = (_o, _o + _w)
    _o += _w
GLA_COLS = 2 * GLA_DK + 2 * GLA_DV
GLA_SAFE_LOG_DECAY = 20.0
ROW_BLOCK = 512


def _params(*sem):
    return pltpu.CompilerParams(dimension_semantics=sem, vmem_limit_bytes=VMEM_LIMIT_BYTES)


def _dot(a, b):
    return jnp.dot(a, b, preferred_element_type=F32)


def _dot_nt(a, b):
    return lax.dot_general(a, b, (((1,), (1,)), ((), ())), preferred_element_type=F32)


def _dot_tn(a, b):
    return lax.dot_general(a, b, (((0,), (0,)), ((), ())), preferred_element_type=F32)


def _rms(x, g):
    return x * lax.rsqrt(jnp.mean(x * x, axis=-1, keepdims=True) + RMS_EPS) * g


IN_PROJ_COL_CHUNK = 1024


def _split3(x):
    x1 = x.astype(BF16)
    r1 = x - x1.astype(F32)
    x2 = r1.astype(BF16)
    x3 = (r1 - x2.astype(F32)).astype(BF16)
    return x1, x2, x3


def _in_proj_kernel(x_ref, g_ref, wt_ref, w2_ref, gb_ref, *refs, groups, low_rank_col, chunk, n_valid, n_casts,
                    second):
    n_out = len(groups) + 1
    cast_in, refs = refs[:n_casts], refs[n_casts:]
    if second:
        x2_ref, refs = refs[0], refs[1:]
    out_refs, cast_out, out2_refs = refs[:n_out], refs[n_out:n_out + n_casts], refs[n_out + n_casts:]

    if second:
        @pl.when(pl.program_id(0) == 0)
        def _():
            _in_proj_rows(x2_ref, g_ref, wt_ref, w2_ref, gb_ref, out2_refs, groups, low_rank_col, *second)

    for src, dst in zip(cast_in, cast_out):
        dst[...] = src[...].astype(dst.dtype)
    _in_proj_rows(x_ref, g_ref, wt_ref, w2_ref, gb_ref, out_refs, groups, low_rank_col, chunk, n_valid)


def _in_proj_rows(x_ref, g_ref, wt_ref, w2_ref, gb_ref, out_refs, groups, low_rank_col, chunk, n_valid):
    h = _rms(x_ref[...], g_ref[...]).astype(BF16)
    b_ref = out_refs[len(groups)]
    tm = x_ref.shape[0]

    for o_ref, (c_first, n) in zip(out_refs, groups):
        for c0 in range(0, n, IN_PROJ_COL_CHUNK):
            c1 = min(c0 + IN_PROJ_COL_CHUNK, n)
            o_ref[:, c0:c1] = _dot_nt(h, wt_ref[c_first + c0:c_first + c1, :]).astype(o_ref.dtype)

    low_rank = _dot_nt(h, wt_ref[low_rank_col:low_rank_col + LANES, :]).astype(BF16)
    gate = _dot(low_rank, w2_ref[...]) + gb_ref[...]
    log_a = (jnp.minimum(gate, 0.0) - jnp.log(1.0 + jnp.exp(-jnp.abs(gate)))) * (1.0 / GLA_TAU)
    if n_valid < chunk:
        tok = lax.broadcasted_iota(jnp.int32, log_a.shape, 0) & (chunk - 1)
        log_a = jnp.where(tok < n_valid, log_a, 0.0)
    span = min(tm, LANES)
    row = lax.broadcasted_iota(jnp.int32, (span, span), 0)
    col = lax.broadcasted_iota(jnp.int32, (span, span), 1)
    same_chunk = (row & -chunk) == (col & -chunk) if chunk < span else True
    tril = jnp.where((row >= col) & same_chunk, 1.0, 0.0).astype(BF16)
    for r0 in range(0, tm, span):
        parts = _split3(log_a[r0:r0 + span])[:2]
        b_ref[r0:r0 + span, :] = sum(_dot(tril, p) for p in parts)


def _cast_blocks(a, steps):
    count = max(c for c in range(1, steps + 1) if a.shape[0] % (c * BF16_SUBLANES) == 0)
    return a.shape[0] // count, count


def _in_proj(x, g, w_t, w2, gate_b, groups, low_rank_col, out_dtypes, tm, chunk, n_valid, casts=(), second=None):
    m, k = x.shape
    steps = m // tm
    assert chunk & (chunk - 1) == 0 and (chunk % LANES == 0 or LANES % chunk == 0) and tm % chunk == 0
    assert all(c % BF16_SUBLANES == 0 and n % LANES == 0 for c, n in groups) and low_rank_col % BF16_SUBLANES == 0
    resident = lambda a: pl.BlockSpec(a.shape, lambda i: (0, 0), pipeline_mode=pl.Buffered(1))
    out_widths = [n for _, n in groups] + [GLA_DK]

    def cast_spec(a):
        rows, count = _cast_blocks(a, steps)
        return pl.BlockSpec((rows, a.shape[1]), lambda i, count=count: (jnp.minimum(i, count - 1), 0))

    dtypes = list(out_dtypes) + [F32]
    in_specs = ([pl.BlockSpec((tm, k), lambda i: (i, 0))] + [resident(a) for a in (g, w_t, w2, gate_b)]
                + [cast_spec(a) for a in casts])
    out_specs = [pl.BlockSpec((tm, n), lambda i: (i, 0)) for n in out_widths] + [cast_spec(a) for a in casts]
    out_shape = ([jax.ShapeDtypeStruct((m, n), dt) for n, dt in zip(out_widths, dtypes)]
                 + [jax.ShapeDtypeStruct(a.shape, BF16) for a in casts])
    operands = [x, g, w_t, w2, gate_b, *casts]
    second_static = None
    if second is not None:
        x2, chunk2, n_valid2 = second
        second_static = (chunk2, n_valid2)
        in_specs.append(resident(x2))
        operands.append(x2)
        out_specs += [pl.BlockSpec((x2.shape[0], n), lambda i: (0, 0)) for n in out_widths]
        out_shape += [jax.ShapeDtypeStruct((x2.shape[0], n), dt) for n, dt in zip(out_widths, dtypes)]
    return pl.pallas_call(
        functools.partial(_in_proj_kernel, groups=tuple(groups), low_rank_col=low_rank_col,
                          chunk=chunk, n_valid=n_valid, n_casts=len(casts), second=second_static),
        grid=(steps,),
        in_specs=in_specs,
        out_specs=out_specs,
        out_shape=out_shape,
        compiler_params=pltpu.CompilerParams(dimension_semantics=("arbitrary",),
                                             vmem_limit_bytes=IN_PROJ_VMEM_LIMIT_BYTES),
        name="in_proj",
    )(*operands)


N_SAMPLE_IN = 2 + 2 * DIL_GROUPS
N_SAMPLE_OUT = 1 + 2 * DIL_GROUPS
RING_SLOTS = 3


def _gla_kernel(*refs, chunk, n_valid, n_new_sample, n_steps, zero_state):
    sample_in = ()
    if n_new_sample:
        n_fixed = 3 if zero_state else 4
        z_hbm, b_hbm, ng_ref = refs[:3]
        s0_ref = None if zero_state else refs[3]
        sample_in, refs = refs[n_fixed:n_fixed + N_SAMPLE_IN], refs[n_fixed + N_SAMPLE_IN:]
    else:
        q_ref, k_ref, v_ref, r_ref, b_ref, ng_ref, s0_ref = refs[:7]
        refs = refs[7:]
    o_ref, sout_ref = refs[:2]
    refs = refs[2:]
    sample_out = ()
    if n_new_sample:
        sample_out, refs = refs[:N_SAMPLE_OUT], refs[N_SAMPLE_OUT:]
    s_ref, oi_ref, kf_ref, vf_ref = refs[:4]

    c = pl.program_id(1)
    rows = s_ref.shape[0]

    if n_new_sample:
        buf_hbm = sample_in[2:]
        n_buf = len(buf_hbm)
        rings, z_ring, b_ring, sem = refs[4:4 + n_buf], refs[4 + n_buf], refs[5 + n_buf], refs[6 + n_buf]
        n_chunks = pl.num_programs(1)
        step = pl.program_id(0) * n_chunks + c

        def ring_copies(s):
            slot = s % RING_SLOTS
            block = (pl.ds((s // n_chunks) * rows, rows), pl.ds((s % n_chunks) * chunk, chunk))
            return ([pltpu.make_async_copy(src.at[pl.ds(s, 1)], ring.at[pl.ds(slot, 1)], sem.at[j, slot])
                     for j, (src, ring) in enumerate(zip(buf_hbm, rings))]
                    + [pltpu.make_async_copy(z_hbm.at[block], z_ring.at[slot], sem.at[n_buf, slot]),
                       pltpu.make_async_copy(b_hbm.at[block], b_ring.at[slot], sem.at[n_buf + 1, slot])])

        @pl.when(step == 0)
        def _():
            for s in range(RING_SLOTS - 1):
                for cp in ring_copies(s):
                    cp.start()

        @pl.when(step + (RING_SLOTS - 1) < n_steps)
        def _():
            for cp in ring_copies(step + (RING_SLOTS - 1)):
                cp.start()

        for cp in ring_copies(step):
            cp.wait()
        slot = step % RING_SLOTS
        sample_in = tuple(sample_in[:2]) + tuple(ring.at[pl.ds(slot, 1)] for ring in rings)
        z_now, b_ref = z_ring.at[slot], b_ring.at[slot]
        q_ref, k_ref = z_now.at[:, :, 0:GLA_DK], z_now.at[:, :, GLA_DK:2 * GLA_DK]
        v_ref, r_ref = z_now.at[:, :, 2 * GLA_DK:2 * GLA_DK + GLA_DV], z_now.at[:, :, 2 * GLA_DK + GLA_DV:GLA_COLS]

    @pl.when(c == 0)
    def _():
        s_ref[...] = jnp.zeros(s_ref.shape, F32) if s0_ref is None else s0_ref[...]

    for r in range(rows):
        _gla_chunk(r, q_ref, k_ref, v_ref, r_ref, b_ref, ng_ref, o_ref, s_ref, oi_ref, chunk, n_valid)
    if n_new_sample:
        _dil_sample_kernel(*sample_in, *sample_out, n_new=n_new_sample)

    for r in range(rows):
        @pl.when(jnp.min(b_ref[r, chunk - 1:chunk, :]) < -GLA_SAFE_LOG_DECAY)
        def _(r=r):
            _gla_chunk_exact_intra(r, q_ref, k_ref, v_ref, r_ref, b_ref, ng_ref, o_ref, oi_ref, kf_ref, vf_ref,
                                   chunk, n_valid)

    @pl.when(c == pl.num_programs(1) - 1)
    def _():
        sout_ref[...] = s_ref[...]


def _gla_epilogue(o, r, vs, r_ref, ng_ref, o_ref):
    gr = r_ref[r, :, vs].astype(F32)
    o_ref[r, :, vs] = (_rms(o, ng_ref[...]) * (gr * jax.nn.sigmoid(gr))).astype(o_ref.dtype)


def _gla_values(r, vs, v_ref, chunk, n_valid):
    vh = v_ref[r, :, vs]
    if n_valid < chunk:
        tok = lax.broadcasted_iota(jnp.int32, vh.shape, 0)
        vh = jnp.where(tok < n_valid, vh, jnp.zeros_like(vh))
    return vh


def _gla_chunk_exact_intra(r, q_ref, k_ref, v_ref, r_ref, b_ref, ng_ref, o_ref, oi_ref, kf_ref, vf_ref,
                           chunk, n_valid):
    tok = lax.broadcasted_iota(jnp.int32, (chunk, 1), 0)
    for h in range(GLA_HEADS):
        ks = slice(h * GLA_HDK, (h + 1) * GLA_HDK)
        vs = slice(h * GLA_HDV, (h + 1) * GLA_HDV)
        bh = b_ref[r, :, ks]
        qh = q_ref[r, :, ks].astype(F32) * (GLA_HDK ** -0.5)
        kf_ref[...] = k_ref[r, :, ks].astype(F32)
        vf_ref[...] = _gla_values(r, vs, v_ref, chunk, n_valid).astype(F32)

        def eight_keys(i, acc, bh=bh, qh=qh, ks=ks):
            rows8 = pl.ds(pl.multiple_of(i * SUBLANES, SUBLANES), SUBLANES)
            b8, k8, v8 = b_ref[r, rows8, ks], kf_ref[rows8, :], vf_ref[rows8, :]
            for j in range(SUBLANES):
                s = i * SUBLANES + j
                decay = jnp.exp(jnp.where(tok >= s, bh - b8[j:j + 1], MASK_VALUE))
                w = jnp.sum(qh * decay * k8[j:j + 1], axis=-1, keepdims=True)
                acc = acc + w * v8[j:j + 1]
            return acc

        intra = lax.fori_loop(0, chunk // SUBLANES, eight_keys, jnp.zeros((chunk, GLA_HDV), F32))
        _gla_epilogue(oi_ref[r, h] + intra, r, vs, r_ref, ng_ref, o_ref)


def _gla_chunk(r, q_ref, k_ref, v_ref, r_ref, b_ref, ng_ref, o_ref, s_ref, oi_ref, chunk, n_valid):
    row = lax.broadcasted_iota(jnp.int32, (chunk, chunk), 0)
    col = lax.broadcasted_iota(jnp.int32, (chunk, chunk), 1)
    causal = row >= col
    b = b_ref[r]

    for h in range(GLA_HEADS):
        ks = slice(h * GLA_HDK, (h + 1) * GLA_HDK)
        vs = slice(h * GLA_HDV, (h + 1) * GLA_HDV)
        bh = b[:, ks]
        qh = q_ref[r, :, ks].astype(F32) * (GLA_HDK ** -0.5)
        kh = k_ref[r, :, ks].astype(F32)
        vh = _gla_values(r, vs, v_ref, chunk, n_valid)
        qt = (qh * jnp.exp(bh)).astype(BF16)
        kt = (kh * jnp.exp(-bh)).astype(BF16)
        kd = (kh * jnp.exp(bh[chunk - 1:chunk, :] - bh)).astype(BF16)
        s_old = s_ref[r, h]
        scores = jnp.where(causal, _dot_nt(qt, kt), 0.0).astype(BF16)
        o_state = _dot(qt, s_old.astype(BF16))
        oi_ref[r, h] = o_state
        dec = jnp.exp(jnp.broadcast_to(bh[chunk - 1:chunk, :], (GLA_HDK, GLA_HDK)).T)
        s_ref[r, h] = s_old * jnp.concatenate([dec, dec], axis=1) + _dot_tn(kd, vh)
        _gla_epilogue(o_state + _dot(scores, vh), r, vs, r_ref, ng_ref, o_ref)


def _gla(z, b_cum, norm_g, s0, chunk, n_valid, rows, sample=None):
    bsz, t, _ = z.shape
    n_chunks = t // chunk
    state = pl.BlockSpec((rows, GLA_HEADS, GLA_HDK, GLA_HDV), lambda b, c: (b, 0, 0, 0))
    in_specs = [pl.BlockSpec((rows, chunk, GLA_DK), lambda b, c: (b, c, 0)),
                pl.BlockSpec((rows, chunk, GLA_DK), lambda b, c: (b, c, 1)),
                pl.BlockSpec((rows, chunk, GLA_DV), lambda b, c: (b, c, 1)),
                pl.BlockSpec((rows, chunk, GLA_DV), lambda b, c: (b, c, 2)),
                pl.BlockSpec((rows, chunk, GLA_DK), lambda b, c: (b, c, 0)),
                pl.BlockSpec((1, GLA_HDV), lambda b, c: (0, 0)),
                state]
    out_specs = [pl.BlockSpec((rows, chunk, GLA_DV), lambda b, c: (b, c, 0)), state]
    out_shape = [jax.ShapeDtypeStruct((bsz, t, GLA_DV), BF16),
                 jax.ShapeDtypeStruct((bsz, GLA_HEADS, GLA_HDK, GLA_HDV), F32)]
    operands = [z, z, z, z, b_cum, norm_g, s0]
    n_new = 0
    scratch = [pltpu.VMEM((rows, GLA_HEADS, GLA_HDK, GLA_HDV), F32),
               pltpu.VMEM((rows, GLA_HEADS, chunk, GLA_HDV), F32),
               pltpu.VMEM((chunk, GLA_HDK), F32),
               pltpu.VMEM((chunk, GLA_HDV), F32)]
    if sample is not None:
        dq, kvn, bufs, n_new = sample
        assert dq.shape[0] == (bsz // rows) * n_chunks
        assert dq.shape[0] >= RING_SLOTS
        per_step = lambda a: pl.BlockSpec((1,) + a.shape[1:], lambda b, c: (b * n_chunks + c, 0, 0))
        in_hbm = pl.BlockSpec(memory_space=pl.ANY)
        initial = [] if s0 is None else [(state, s0)]
        in_specs = ([in_hbm, in_hbm, in_specs[5]] + [spec for spec, _ in initial]
                    + [per_step(a) for a in (dq, kvn)] + [in_hbm for _ in bufs])
        operands = [z, b_cum, norm_g] + [a for _, a in initial] + [dq, kvn, *bufs]
        o_s = jax.ShapeDtypeStruct((dq.shape[0], dq.shape[1], DIL_OUT), BF16)
        out_specs += [per_step(o_s)] + [per_step(a) for a in bufs]
        out_shape += [o_s] + [jax.ShapeDtypeStruct(a.shape, a.dtype) for a in bufs]
        scratch += [pltpu.VMEM((RING_SLOTS,) + a.shape[1:], a.dtype) for a in bufs]
        scratch += [pltpu.VMEM((RING_SLOTS, rows, chunk, GLA_COLS), z.dtype),
                    pltpu.VMEM((RING_SLOTS, rows, chunk, GLA_DK), b_cum.dtype)]
        scratch.append(pltpu.SemaphoreType.DMA((len(bufs) + 2, RING_SLOTS)))
    return pl.pallas_call(
        functools.partial(_gla_kernel, chunk=chunk, n_valid=n_valid, n_new_sample=n_new,
                          n_steps=(bsz // rows) * n_chunks, zero_state=s0 is None),
        grid=(bsz // rows, n_chunks),
        in_specs=in_specs,
        out_specs=out_specs,
        out_shape=out_shape,
        scratch_shapes=scratch,
        compiler_params=pltpu.CompilerParams(dimension_semantics=("arbitrary", "arbitrary"),
                                             vmem_limit_bytes=GLA_VMEM_LIMIT_BYTES),
        name="gla",
    )(*operands)


def _alibi_slope(g, head):
    n = DIL_GROUPS * DIL_HEADS
    return jnp.exp((-ALIBI_MAX * _LN2 / n) * (head + (g * DIL_HEADS + 1.0)))


def _pair_slopes(g, hp, rows):
    head = 2.0 * hp.astype(F32) + jnp.where(rows >= Q_TILE, 1.0, 0.0)
    return _alibi_slope(g, head)


def _dil_prompt_kernel(q0_ref, q1_ref, q2_ref, k0_ref, k1_ref, k2_ref, v0_ref, v1_ref, v2_ref,
                       o_ref, wk0_ref, wv0_ref, wk1_ref, wv1_ref, wk2_ref, wv2_ref,
                       qf_ref, og_ref, ld_ref, bf_ref, br_ref, s_ref, p_ref, m_ref, *, seq):
    hp = pl.program_id(1)
    q_refs = (q0_ref, q1_ref, q2_ref)
    k_refs = (k0_ref, k1_ref, k2_ref)
    v_refs = (v0_ref, v1_ref, v2_ref)
    wk_refs = (wk0_ref, wk1_ref, wk2_ref)
    wv_refs = (wv0_ref, wv1_ref, wv2_ref)

    for g in range(DIL_GROUPS):
        qf_ref[g] = q_refs[g][...].astype(F32) * (DIL_HD ** -0.5)

    for g in range(DIL_GROUPS):
        rate = float(DIL_RATES[g])
        rows = lax.broadcasted_iota(jnp.int32, (2 * Q_TILE, Q_TILE), 0)
        cols = lax.broadcasted_iota(jnp.int32, (2 * Q_TILE, Q_TILE), 1)
        dist = (rows & (Q_TILE - 1)) - cols
        slope = _pair_slopes(g, hp, rows)
        bf_ref[g] = jnp.where(dist >= 0, -slope * rate * dist.astype(F32), MASK_VALUE)
        if g < 2:
            rows = lax.broadcasted_iota(jnp.int32, (2 * Q_TILE, 2 * Q_TILE), 0)
            cols = lax.broadcasted_iota(jnp.int32, (2 * Q_TILE, 2 * Q_TILE), 1)
            dist = (rows & (Q_TILE - 1)) + Q_TILE - cols
            slope = _pair_slopes(g, hp, rows)
            ok = jnp.abs(dist - Q_TILE // 2) <= Q_TILE // 2
            br_ref[g] = jnp.where(ok, -slope * rate * dist.astype(F32), MASK_VALUE)

    lane = lax.broadcasted_iota(jnp.int32, (Q_TILE, LANES), 1)
    lo = lane < DIL_HD

    def rows_of(start, size, rate):
        return pl.ds(start, size) if rate == 1 else pl.ds(start, size, stride=rate)

    def scores_stage(g, q_row, k_row, n_keys, slot):
        rate = DIL_RATES[g]
        q2 = qf_ref[g, rows_of(q_row, Q_TILE, rate), :]
        k2 = k_refs[g][rows_of(k_row, n_keys, rate), :].astype(BF16)
        qs = jnp.concatenate([jnp.where(lo, q2, 0.0), jnp.where(lo, 0.0, q2)], axis=0).astype(BF16)
        s_ref[slot, :, 0:n_keys] = _dot_nt(qs, k2)

    def softmax_stage(n_keys, bias, slot):
        s = s_ref[slot, :, 0:n_keys] + bias
        m = jnp.max(s, axis=-1, keepdims=True)
        p_ref[slot, :, 0:n_keys] = jnp.exp(s - m).astype(BF16)
        m_ref[slot] = jnp.where(lo, m[:Q_TILE], m[Q_TILE:])

    def values_stage(g, q_row, k_row, n_keys, slot):
        rate = DIL_RATES[g]
        v2 = v_refs[g][rows_of(k_row, n_keys, rate), :].astype(BF16)
        vo = jnp.concatenate([v2, jnp.ones((n_keys, LANES), BF16)], axis=1)
        r = _dot(p_ref[slot, :, 0:n_keys], vo)
        o2 = jnp.where(lo, r[:Q_TILE, :LANES], r[Q_TILE:, :LANES])
        l2 = jnp.where(lo, r[:Q_TILE, LANES:], r[Q_TILE:, LANES:])
        og_ref[g, rows_of(q_row, Q_TILE, rate), :] = o2 / l2
        ld_ref[g, rows_of(q_row, Q_TILE, rate), :] = m_ref[slot] + jnp.log(l2)

    tiles = []
    for g in range(DIL_GROUPS):
        rate = DIL_RATES[g]
        span = rate * Q_TILE
        for rho in range(rate):
            tiles.append((g, rho, rho, Q_TILE, ("first", g)))
        for n in range(1, seq // span):
            for rho in range(rate):
                tiles.append((g, rho + n * span, rho + (n - 1) * span, 2 * Q_TILE, ("later", g)))
    sets = [tiles[i:i + TILE_SLOTS] for i in range(0, len(tiles), TILE_SLOTS)]

    def slot_of(set_index, u):
        return (set_index % 2) * TILE_SLOTS + u

    for t in range(len(sets) + 2):
        if 0 <= t - 2 < len(sets):
            for u, (g, q_row, k_row, n_keys, _) in enumerate(sets[t - 2]):
                values_stage(g, q_row, k_row, n_keys, slot_of(t - 2, u))
        if 0 <= t - 1 < len(sets):
            for u, (g, _, _, n_keys, (kind, gb)) in enumerate(sets[t - 1]):
                softmax_stage(n_keys, bf_ref[gb] if kind == "first" else br_ref[gb], slot_of(t - 1, u))
        if t < len(sets):
            for u, (g, q_row, k_row, n_keys, _) in enumerate(sets[t]):
                scores_stage(g, q_row, k_row, n_keys, slot_of(t, u))

    for r0 in range(0, seq, Q_TILE):
        rs = slice(r0, r0 + Q_TILE)
        ld = [ld_ref[g, rs, :] for g in range(DIL_GROUPS)]
        top = jnp.maximum(jnp.maximum(ld[0], ld[1]), ld[2])
        w = [jnp.exp(x - top) for x in ld]
        num = sum(w[g] * og_ref[g, rs, :] for g in range(DIL_GROUPS))
        o_ref[rs, :] = (num / (w[0] + w[1] + w[2])).astype(o_ref.dtype)

    for g in range(DIL_GROUPS):
        keep = wk_refs[g].shape[1]
        for src, dst in ((k_refs[g], wk_refs[g]), (v_refs[g], wv_refs[g])):
            for off in range(0, keep, LANES):
                dst[:, off:off + LANES] = src[seq - keep + off:seq - keep + off + LANES, :].T


def _dil_prompt(dq, kv):
    bsz, seq, _ = dq.shape
    pairs = DIL_HEADS * DIL_HD // LANES
    nblk = DIL_WIDTH // LANES
    keeps = [min(w, seq) for w in DIL_WINDOWS]

    def spec(col0):
        return pl.BlockSpec((None, seq, LANES), lambda b, hp, col0=col0: (b, 0, col0 + hp))

    in_specs = ([spec(g * pairs) for g in range(DIL_GROUPS)]
                + [spec(g * pairs) for g in range(DIL_GROUPS)]
                + [spec(nblk + g * pairs) for g in range(DIL_GROUPS)])
    return pl.pallas_call(
        functools.partial(_dil_prompt_kernel, seq=seq),
        grid=(bsz, pairs),
        in_specs=in_specs,
        out_specs=[pl.BlockSpec((None, seq, LANES), lambda b, hp: (b, 0, hp))]
                  + [pl.BlockSpec((None, LANES, w), lambda b, hp: (b, hp, 0)) for w in keeps for _ in range(2)],
        out_shape=[jax.ShapeDtypeStruct((bsz, seq, DIL_OUT), BF16)]
                  + [jax.ShapeDtypeStruct((bsz, DIL_HEADS * DIL_HD, w), F32) for w in keeps for _ in range(2)],
        scratch_shapes=[pltpu.VMEM((DIL_GROUPS, seq, LANES), F32),
                        pltpu.VMEM((DIL_GROUPS, seq, LANES), F32),
                        pltpu.VMEM((DIL_GROUPS, seq, LANES), F32),
                        pltpu.VMEM((DIL_GROUPS, 2 * Q_TILE, Q_TILE), F32),
                        pltpu.VMEM((2, 2 * Q_TILE, 2 * Q_TILE), F32),
                        pltpu.VMEM((2 * TILE_SLOTS, 2 * Q_TILE, 2 * Q_TILE), F32),
                        pltpu.VMEM((2 * TILE_SLOTS, 2 * Q_TILE, 2 * Q_TILE), BF16),
                        pltpu.VMEM((2 * TILE_SLOTS, Q_TILE, LANES), F32)],
        compiler_params=_params("parallel", "arbitrary"),
        name="dil_prompt",
    )(dq, dq, dq, kv, kv, kv, kv, kv, kv)


def _dil_sample_kernel(q_ref, kvn_ref, kb0_ref, vb0_ref, kb1_ref, vb1_ref, kb2_ref, vb2_ref,
                       o_ref, nk0_ref, nv0_ref, nk1_ref, nv1_ref, nk2_ref, nv2_ref, *, n_new):
    kb_refs = (kb0_ref, kb1_ref, kb2_ref)
    vb_refs = (vb0_ref, vb1_ref, vb2_ref)
    nk_refs = (nk0_ref, nk1_ref, nk2_ref)
    nv_refs = (nv0_ref, nv1_ref, nv2_ref)
    pad = q_ref.shape[1]
    nrow = DIL_HEADS * pad
    width = DIL_HEADS * DIL_HD

    lane_head = jnp.right_shift(lax.broadcasted_iota(jnp.int32, (pad, width), 1), DIL_HD.bit_length() - 1)

    def bias_for(g, dist):
        rate = DIL_RATES[g]
        rows = lax.broadcasted_iota(jnp.int32, dist.shape, 0)
        head = sum(jnp.where(rows >= h * pad, 1.0, 0.0) for h in range(1, DIL_HEADS))
        ok = (dist >= 0) & (dist <= DIL_WINDOWS[g]) & ((dist & (rate - 1)) == 0)
        return jnp.where(ok, -_alibi_slope(g, head) * dist.astype(F32), MASK_VALUE)

    sel_r = lax.broadcasted_iota(jnp.int32, (pad, LANES), 0)
    sel_c = lax.broadcasted_iota(jnp.int32, (pad, LANES), 1)
    place = jnp.where((sel_r < n_new) & (sel_c == sel_r + (LANES - n_new)), 1.0, 0.0).astype(BF16)
    tail_lane = lax.broadcasted_iota(jnp.int32, (width, LANES), 1) >= LANES - n_new

    def shifted(buf_t, new_rows):
        length = buf_t.shape[1]
        rolled = pltpu.roll(buf_t, length - n_new, 1)
        new_t = sum(_dot_tn(part, place) for part in _split3(new_rows))
        last = jnp.where(tail_lane, new_t, rolled[:, length - LANES:])
        return rolled, last

    scores, values, transposed = [], [], []
    for g in range(DIL_GROUPS):
        length = kb_refs[g].shape[2]
        cs = slice(g * width, (g + 1) * width)
        vcs = slice(DIL_WIDTH + g * width, DIL_WIDTH + (g + 1) * width)
        qg = q_ref[0, :, cs].astype(F32) * (DIL_HD ** -0.5)
        qs = jnp.concatenate([jnp.where(lane_head == h, qg, 0.0) for h in range(DIL_HEADS)], axis=0).astype(BF16)
        kb, vb = kb_refs[g][0], vb_refs[g][0]
        kn, vn = kvn_ref[0, :, cs], kvn_ref[0, :, vcs]
        rows = lax.broadcasted_iota(jnp.int32, (nrow, length), 0)
        cols = lax.broadcasted_iota(jnp.int32, (nrow, length), 1)
        dist = length + (rows & (pad - 1)) - cols
        scores.append(_dot(qs, kb.astype(BF16)) + bias_for(g, dist))
        values.append(vb.astype(BF16))
        transposed.append(True)
        rows = lax.broadcasted_iota(jnp.int32, (nrow, pad), 0)
        cols = lax.broadcasted_iota(jnp.int32, (nrow, pad), 1)
        dist = jnp.where(cols < n_new, (rows & (pad - 1)) - cols, -1)
        scores.append(_dot_nt(qs, kn.astype(BF16)) + bias_for(g, dist))
        values.append(vn.astype(BF16))
        transposed.append(False)
        for buf, new, out_ref in ((kb, kn, nk_refs[g]), (vb, vn, nv_refs[g])):
            rolled, last = shifted(buf, new)
            if length > LANES:
                out_ref[0, :, 0:length - LANES] = rolled[:, 0:length - LANES]
            out_ref[0, :, length - LANES:length] = last

    top = functools.reduce(jnp.maximum, [jnp.max(s, axis=-1, keepdims=True) for s in scores])
    probs = [jnp.exp(s - top) for s in scores]
    den = sum(jnp.sum(p, axis=-1, keepdims=True) for p in probs)
    acc = sum((_dot_nt if t else _dot)(p.astype(BF16), v)
              for p, v, t in zip(probs, values, transposed)) / den
    out = sum(jnp.where(lane_head == h, acc[h * pad:(h + 1) * pad], 0.0) for h in range(DIL_HEADS))
    o_ref[0] = out.astype(o_ref.dtype)


FFN_COL_CHUNK = D_FF // 2


def _merge_ffn_kernel(x_ref, oa_ref, ob_ref, ga_ref, gb_ref, pa_ref, pb_ref, wo_ref, n2_ref, nf_ref,
                      wg_ref, wu_ref, wd_ref, *refs):
    weights = (pa_ref, pb_ref, wo_ref, n2_ref, nf_ref, wg_ref, wu_ref, wd_ref)
    if len(refs) > 1:
        x2_ref, oa2_ref, ob2_ref, g2_ref, y_ref, y2_ref = refs

        @pl.when(pl.program_id(0) == 0)
        def _():
            _merge_ffn_rows(x2_ref, oa2_ref, ob2_ref, g2_ref.at[:, 0:D_MODEL], g2_ref.at[:, D_MODEL:2 * D_MODEL],
                            *weights, y2_ref)
    else:
        y_ref, = refs
    _merge_ffn_rows(x_ref, oa_ref, ob_ref, ga_ref, gb_ref, *weights, y_ref)


def _merge_ffn_rows(x_ref, oa_ref, ob_ref, ga_ref, gb_ref, pa_ref, pb_ref, wo_ref, n2_ref, nf_ref,
                    wg_ref, wu_ref, wd_ref, y_ref):
    pa = _dot(oa_ref[...], pa_ref[...])
    pb = _dot(ob_ref[...], pb_ref[...])
    merged = jax.nn.sigmoid(ga_ref[...].astype(F32)) * pa + jax.nn.sigmoid(gb_ref[...].astype(F32)) * pb
    x1 = x_ref[...] + _dot(merged.astype(BF16), wo_ref[...])
    h = _rms(x1, n2_ref[...]).astype(BF16)
    acc = x1
    for c0 in range(0, D_FF, FFN_COL_CHUNK):
        cs = slice(c0, c0 + FFN_COL_CHUNK)
        gate = _dot(h, wg_ref[:, cs])
        up = _dot(h, wu_ref[:, cs])
        act = (gate * jax.nn.sigmoid(gate) * up).astype(BF16)
        acc = acc + _dot(act, wd_ref[cs, :])
    y_ref[...] = _rms(acc, nf_ref[...])


def _merge_ffn(x, oa, ob, gates, pa, pb, wo, n2, nf, wg, wu, wd, tm, second=None):
    m = x.shape[0]
    row = lambda w: pl.BlockSpec((tm, w), lambda i: (i, 0))
    resident = lambda a: pl.BlockSpec(a.shape, lambda i: (0, 0), pipeline_mode=pl.Buffered(1))
    in_specs = ([row(D_MODEL), row(GLA_DV), row(DIL_OUT), row(D_MODEL), pl.BlockSpec((tm, D_MODEL), lambda i: (i, 1))]
                + [resident(a) for a in (pa, pb, wo, n2, nf, wg, wu, wd)])
    out_specs = [row(D_MODEL)]
    out_shape = [jax.ShapeDtypeStruct((m, D_MODEL), F32)]
    operands = [x, oa, ob, gates, gates, pa, pb, wo, n2, nf, wg, wu, wd]
    if second is not None:
        in_specs += [resident(a) for a in second]
        operands += list(second)
        out_specs.append(pl.BlockSpec(second[0].shape, lambda i: (0, 0)))
        out_shape.append(jax.ShapeDtypeStruct(second[0].shape, F32))
    return pl.pallas_call(
        _merge_ffn_kernel,
        grid=(m // tm,),
        in_specs=in_specs,
        out_specs=out_specs,
        out_shape=out_shape,
        compiler_params=pltpu.CompilerParams(dimension_semantics=("arbitrary",),
                                             vmem_limit_bytes=MERGE_FFN_VMEM_LIMIT_BYTES),
        name="merge_ffn",
    )(*operands)


def _prep_weights(w_in, gla_gate_w2):
    w2 = jnp.concatenate([gla_gate_w2[0], jnp.zeros((LANES - GLA_GATE_RANK, GLA_DK), F32)], axis=0)
    return dict(w_in_t=jnp.transpose(w_in[0]).astype(BF16), w2=w2.astype(BF16))


def _project(x2d, norm1_g, gate_b, wts, tm, chunk, n_valid, casts=(), second=None):
    groups = [(0, GLA_COLS), (_OFF["dq"][0], DIL_WIDTH), (_OFF["dk"][0], 2 * DIL_WIDTH), (_OFF["ga"][0], 2 * D_MODEL)]
    return _in_proj(x2d, norm1_g, wts["w_in_t"], wts["w2"], gate_b, groups, _OFF["glr"][0],
                    [BF16, BF16, F32, BF16], tm, chunk, n_valid, casts, second)


def _tail(x2d, o_a, o_b, z_gates, wts, norm2_g, norm_f_g, tm, second):
    return _merge_ffn(x2d, o_a, o_b, z_gates, wts["pa"], wts["pb"], wts["wo"], norm2_g,
                      norm_f_g.reshape(1, D_MODEL), wts["wg"], wts["wu"], wts["wd"], tm, second)


def kernel(x_prompt, x_sample, state_gla, state_win0_k, state_win0_v, state_win1_k, state_win1_v,
           state_win2_k, state_win2_v, norm1_g, w_in, gla_gate_w2, gla_gate_b, gla_norm_g,
           proj_a, proj_b, w_out, norm2_g, w_ffn_gate, w_ffn_up, w_ffn_down, norm_f_g):
    wts = _prep_weights(w_in, gla_gate_w2)
    bp, seq, _ = x_prompt.shape
    bs, n_new, _ = x_sample.shape
    width = DIL_HEADS * DIL_HD

    xp = x_prompt.reshape(bp * seq, D_MODEL)
    xs = jnp.pad(x_sample, ((0, 0), (0, SAMPLE_PAD - n_new), (0, 0))).reshape(bs * SAMPLE_PAD, D_MODEL)
    later = dict(pa=proj_a[0], pb=proj_b[0], wo=w_out[0], wg=w_ffn_gate[0], wu=w_ffn_up[0], wd=w_ffn_down[0])
    z_gla, z_dq, z_kv, z_gates, b_cum, *rest = _project(xp, norm1_g, gla_gate_b, wts, ROW_BLOCK, GLA_CHUNK,
                                                        GLA_CHUNK, tuple(later.values()), (xs, SAMPLE_PAD, n_new))
    wts.update(zip(later.keys(), rest[:len(later)]))
    zs_gla, zs_dq, zs_kv, zs_gates, bs_cum = rest[len(later):]

    bufs = [jnp.transpose(a[0], (0, 2, 3, 1)).reshape(bs, width, a.shape[2]) for a in
            (state_win0_k, state_win0_v, state_win1_k, state_win1_v, state_win2_k, state_win2_v)]
    sample = (zs_dq.reshape(bs, SAMPLE_PAD, DIL_WIDTH), zs_kv.reshape(bs, SAMPLE_PAD, 2 * DIL_WIDTH), bufs, n_new)
    o_a, gla_p, os_b, *win_s = _gla(z_gla.reshape(bp, seq, GLA_COLS), b_cum.reshape(bp, seq, GLA_DK), gla_norm_g,
                                    None, GLA_CHUNK, GLA_CHUNK, GLA_ROWS_PROMPT, sample)

    o_b, *win_p = _dil_prompt(z_dq.reshape(bp, seq, DIL_WIDTH), z_kv.reshape(bp, seq, 2 * DIL_WIDTH))
    win_p = [jnp.transpose(a.reshape(bp, DIL_HEADS, DIL_HD, a.shape[2]), (0, 3, 1, 2))[None] for a in win_p]

    os_a, gla_s = _gla(zs_gla.reshape(bs, SAMPLE_PAD, GLA_COLS), bs_cum.reshape(bs, SAMPLE_PAD, GLA_DK),
                       gla_norm_g, state_gla[0], SAMPLE_PAD, n_new, GLA_ROWS_SAMPLE)
    sample_rows = (xs, os_a.reshape(bs * SAMPLE_PAD, GLA_DV), os_b.reshape(bs * SAMPLE_PAD, DIL_OUT), zs_gates)
    y_p, ys = _tail(xp, o_a.reshape(bp * seq, GLA_DV), o_b.reshape(bp * seq, DIL_OUT), z_gates, wts,
                    norm2_g, norm_f_g, ROW_BLOCK, sample_rows)
    y_prompt = y_p.reshape(bp, seq, D_MODEL)
    y_sample = ys.reshape(bs, SAMPLE_PAD, D_MODEL)[:, :n_new]
    win_s = [jnp.transpose(a.reshape(bs, DIL_HEADS, DIL_HD, a.shape[2]), (0, 3, 1, 2))[None] for a in win_s]

    return (y_prompt, y_sample, gla_p[None], *win_p, gla_s[None], *win_s)
```

```python
import functools

import jax
import jax.numpy as jnp
from jax import lax
from jax.experimental import pallas as pl
from jax.experimental.pallas import tpu as pltpu

F32 = jnp.float32
BF16 = jnp.bfloat16

D_MODEL = 1024
GLA_HEADS = 4
GLA_DK = 512
GLA_DV = 1024
GLA_HDK = 128
GLA_HDV = 256
GLA_GATE_RANK = 16
GLA_TAU = 16.0
DIL_WINDOWS = (128, 512, 2048)
DIL_RATES = (1, 4, 16)
DIL_GROUPS = 3
DIL_HEADS = 4
DIL_HD = 64
DIL_WIDTH = 768
DIL_OUT = 256
ALIBI_MAX = 8.0
D_FF = 2816
RMS_EPS = 1e-6

LANES = 128
SUBLANES = 8
BF16_SUBLANES = 16
Q_TILE = 128
TILE_SLOTS = 2
SOFTMAX_ROWS = 32
GLA_CHUNK = 128
GLA_ROWS_PROMPT = 4
GLA_ROWS_SAMPLE = 4
SAMPLE_PAD = 8
MASK_VALUE = -1e30
VMEM_LIMIT_BYTES = 48 * 1024 * 1024
IN_PROJ_VMEM_LIMIT_BYTES = 60 * 1024 * 1024
MERGE_FFN_VMEM_LIMIT_BYTES = 60 * 1024 * 1024
GLA_VMEM_LIMIT_BYTES = 60 * 1024 * 1024
_LN2 = 0.6931471805599453

_OFF = {}
_o = 0
for _name, _w in (("gq", GLA_DK), ("gk", GLA_DK), ("gv", GLA_DV), ("gr", GLA_DV), ("glr", GLA_GATE_RANK),
                  ("dq", DIL_WIDTH), ("dk", DIL_WIDTH), ("dv", DIL_WIDTH), ("ga", D_MODEL), ("gb", D_MODEL)):
    _OFF[_name] = (_o, _o + _w)
    _o += _w
GLA_COLS = 2 * GLA_DK + 2 * GLA_DV
GLA_SAFE_LOG_DECAY = 20.0
ROW_BLOCK = 512


def _params(*sem):
    return pltpu.CompilerParams(dimension_semantics=sem, vmem_limit_bytes=VMEM_LIMIT_BYTES)


def _dot(a, b):
    return jnp.dot(a, b, preferred_element_type=F32)


def _dot_nt(a, b):
    return lax.dot_general(a, b, (((1,), (1,)), ((), ())), preferred_element_type=F32)


def _dot_tn(a, b):
    return lax.dot_general(a, b, (((0,), (0,)), ((), ())), preferred_element_type=F32)


def _rms(x, g):
    return x * lax.rsqrt(jnp.mean(x * x, axis=-1, keepdims=True) + RMS_EPS) * g


IN_PROJ_COL_CHUNK = 1024


def _split3(x):
    x1 = x.astype(BF16)
    r1 = x - x1.astype(F32)
    x2 = r1.astype(BF16)
    x3 = (r1 - x2.astype(F32)).astype(BF16)
    return x1, x2, x3


def _in_proj_kernel(x_ref, g_ref, wt_ref, w2_ref, gb_ref, *refs, groups, low_rank_col, chunk, n_valid, n_casts,
                    second):
    n_out = len(groups) + 1
    cast_in, refs = refs[:n_casts], refs[n_casts:]
    if second:
        x2_ref, refs = refs[0], refs[1:]
    out_refs, cast_out, out2_refs = refs[:n_out], refs[n_out:n_out + n_casts], refs[n_out + n_casts:]

    if second:
        @pl.when(pl.program_id(0) == 0)
        def _():
            _in_proj_rows(x2_ref, g_ref, wt_ref, w2_ref, gb_ref, out2_refs, groups, low_rank_col, *second)

    for src, dst in zip(cast_in, cast_out):
        dst[...] = src[...].astype(dst.dtype)
    _in_proj_rows(x_ref, g_ref, wt_ref, w2_ref, gb_ref, out_refs, groups, low_rank_col, chunk, n_valid)


def _in_proj_rows(x_ref, g_ref, wt_ref, w2_ref, gb_ref, out_refs, groups, low_rank_col, chunk, n_valid):
    h = _rms(x_ref[...], g_ref[...]).astype(BF16)
    b_ref = out_refs[len(groups)]
    tm = x_ref.shape[0]

    for o_ref, (c_first, n) in zip(out_refs, groups):
        for c0 in range(0, n, IN_PROJ_COL_CHUNK):
            c1 = min(c0 + IN_PROJ_COL_CHUNK, n)
            o_ref[:, c0:c1] = _dot_nt(h, wt_ref[c_first + c0:c_first + c1, :]).astype(o_ref.dtype)

    low_rank = _dot_nt(h, wt_ref[low_rank_col:low_rank_col + LANES, :]).astype(BF16)
    gate = _dot(low_rank, w2_ref[...]) + gb_ref[...]
    log_a = (jnp.minimum(gate, 0.0) - jnp.log(1.0 + jnp.exp(-jnp.abs(gate)))) * (1.0 / GLA_TAU)
    if n_valid < chunk:
        tok = lax.broadcasted_iota(jnp.int32, log_a.shape, 0) & (chunk - 1)
        log_a = jnp.where(tok < n_valid, log_a, 0.0)
    span = min(tm, LANES)
    row = lax.broadcasted_iota(jnp.int32, (span, span), 0)
    col = lax.broadcasted_iota(jnp.int32, (span, span), 1)
    same_chunk = (row & -chunk) == (col & -chunk) if chunk < span else True
    tril = jnp.where((row >= col) & same_chunk, 1.0, 0.0).astype(BF16)
    for r0 in range(0, tm, span):
        parts = _split3(log_a[r0:r0 + span])[:2]
        b_ref[r0:r0 + span, :] = sum(_dot(tril, p) for p in parts)


def _cast_blocks(a, steps):
    count = max(c for c in range(1, steps + 1) if a.shape[0] % (c * BF16_SUBLANES) == 0)
    return a.shape[0] // count, count


def _in_proj(x, g, w_t, w2, gate_b, groups, low_rank_col, out_dtypes, tm, chunk, n_valid, casts=(), second=None):
    m, k = x.shape
    steps = m // tm
    assert chunk & (chunk - 1) == 0 and (chunk % LANES == 0 or LANES % chunk == 0) and tm % chunk == 0
    assert all(c % BF16_SUBLANES == 0 and n % LANES == 0 for c, n in groups) and low_rank_col % BF16_SUBLANES == 0
    resident = lambda a: pl.BlockSpec(a.shape, lambda i: (0, 0), pipeline_mode=pl.Buffered(1))
    out_widths = [n for _, n in groups] + [GLA_DK]

    def cast_spec(a):
        rows, count = _cast_blocks(a, steps)
        return pl.BlockSpec((rows, a.shape[1]), lambda i, count=count: (jnp.minimum(i, count - 1), 0))

    dtypes = list(out_dtypes) + [F32]
    in_specs = ([pl.BlockSpec((tm, k), lambda i: (i, 0))] + [resident(a) for a in (g, w_t, w2, gate_b)]
                + [cast_spec(a) for a in casts])
    out_specs = [pl.BlockSpec((tm, n), lambda i: (i, 0)) for n in out_widths] + [cast_spec(a) for a in casts]
    out_shape = ([jax.ShapeDtypeStruct((m, n), dt) for n, dt in zip(out_widths, dtypes)]
                 + [jax.ShapeDtypeStruct(a.shape, BF16) for a in casts])
    operands = [x, g, w_t, w2, gate_b, *casts]
    second_static = None
    if second is not None:
        x2, chunk2, n_valid2 = second
        second_static = (chunk2, n_valid2)
        in_specs.append(resident(x2))
        operands.append(x2)
        out_specs += [pl.BlockSpec((x2.shape[0], n), lambda i: (0, 0)) for n in out_widths]
        out_shape += [jax.ShapeDtypeStruct((x2.shape[0], n), dt) for n, dt in zip(out_widths, dtypes)]
    return pl.pallas_call(
        functools.partial(_in_proj_kernel, groups=tuple(groups), low_rank_col=low_rank_col,
                          chunk=chunk, n_valid=n_valid, n_casts=len(casts), second=second_static),
        grid=(steps,),
        in_specs=in_specs,
        out_specs=out_specs,
        out_shape=out_shape,
        compiler_params=pltpu.CompilerParams(dimension_semantics=("arbitrary",),
                                             vmem_limit_bytes=IN_PROJ_VMEM_LIMIT_BYTES),
        name="in_proj",
    )(*operands)


N_SAMPLE_IN = 2 + 2 * DIL_GROUPS
N_SAMPLE_OUT = 1 + 2 * DIL_GROUPS
RING_SLOTS = 3


def _gla_kernel(*refs, chunk, n_valid, n_new_sample, n_steps, zero_state):
    sample_in = ()
    if n_new_sample:
        n_fixed = 3 if zero_state else 4
        z_hbm, b_hbm, ng_ref = refs[:3]
        s0_ref = None if zero_state else refs[3]
        sample_in, refs = refs[n_fixed:n_fixed + N_SAMPLE_IN], refs[n_fixed + N_SAMPLE_IN:]
    else:
        q_ref, k_ref, v_ref, r_ref, b_ref, ng_ref, s0_ref = refs[:7]
        refs = refs[7:]
    o_ref, sout_ref = refs[:2]
    refs = refs[2:]
    sample_out = ()
    if n_new_sample:
        sample_out, refs = refs[:N_SAMPLE_OUT], refs[N_SAMPLE_OUT:]
    s_ref, oi_ref, kf_ref, vf_ref = refs[:4]

    c = pl.program_id(1)
    rows = s_ref.shape[0]

    if n_new_sample:
        buf_hbm = sample_in[2:]
        n_buf = len(buf_hbm)
        rings, z_ring, b_ring, sem = refs[4:4 + n_buf], refs[4 + n_buf], refs[5 + n_buf], refs[6 + n_buf]
        n_chunks = pl.num_programs(1)
        step = pl.program_id(0) * n_chunks + c

        def ring_copies(s):
            slot = s % RING_SLOTS
            block = (pl.ds((s // n_chunks) * rows, rows), pl.ds((s % n_chunks) * chunk, chunk))
            return ([pltpu.make_async_copy(src.at[pl.ds(s, 1)], ring.at[pl.ds(slot, 1)], sem.at[j, slot])
                     for j, (src, ring) in enumerate(zip(buf_hbm, rings))]
                    + [pltpu.make_async_copy(z_hbm.at[block], z_ring.at[slot], sem.at[n_buf, slot]),
                       pltpu.make_async_copy(b_hbm.at[block], b_ring.at[slot], sem.at[n_buf + 1, slot])])

        @pl.when(step == 0)
        def _():
            for s in range(RING_SLOTS - 1):
                for cp in ring_copies(s):
                    cp.start()

        @pl.when(step + (RING_SLOTS - 1) < n_steps)
        def _():
            for cp in ring_copies(step + (RING_SLOTS - 1)):
                cp.start()

        for cp in ring_copies(step):
            cp.wait()
        slot = step % RING_SLOTS
        sample_in = tuple(sample_in[:2]) + tuple(ring.at[pl.ds(slot, 1)] for ring in rings)
        z_now, b_ref = z_ring.at[slot], b_ring.at[slot]
        q_ref, k_ref = z_now.at[:, :, 0:GLA_DK], z_now.at[:, :, GLA_DK:2 * GLA_DK]
        v_ref, r_ref = z_now.at[:, :, 2 * GLA_DK:2 * GLA_DK + GLA_DV], z_now.at[:, :, 2 * GLA_DK + GLA_DV:GLA_COLS]

    @pl.when(c == 0)
    def _():
        s_ref[...] = jnp.zeros(s_ref.shape, F32) if s0_ref is None else s0_ref[...]

    for r in range(rows):
        _gla_chunk(r, q_ref, k_ref, v_ref, r_ref, b_ref, ng_ref, o_ref, s_ref, oi_ref, chunk, n_valid)
    if n_new_sample:
        _dil_sample_kernel(*sample_in, *sample_out, n_new=n_new_sample)

    for r in range(rows):
        @pl.when(jnp.min(b_ref[r, chunk - 1:chunk, :]) < -GLA_SAFE_LOG_DECAY)
        def _(r=r):
            _gla_chunk_exact_intra(r, q_ref, k_ref, v_ref, r_ref, b_ref, ng_ref, o_ref, oi_ref, kf_ref, vf_ref,
                                   chunk, n_valid)

    @pl.when(c == pl.num_programs(1) - 1)
    def _():
        sout_ref[...] = s_ref[...]


def _gla_epilogue(o, r, vs, r_ref, ng_ref, o_ref):
    gr = r_ref[r, :, vs].astype(F32)
    o_ref[r, :, vs] = (_rms(o, ng_ref[...]) * (gr * jax.nn.sigmoid(gr))).astype(o_ref.dtype)


def _gla_values(r, vs, v_ref, chunk, n_valid):
    vh = v_ref[r, :, vs]
    if n_valid < chunk:
        tok = lax.broadcasted_iota(jnp.int32, vh.shape, 0)
        vh = jnp.where(tok < n_valid, vh, jnp.zeros_like(vh))
    return vh


def _gla_chunk_exact_intra(r, q_ref, k_ref, v_ref, r_ref, b_ref, ng_ref, o_ref, oi_ref, kf_ref, vf_ref,
                           chunk, n_valid):
    tok = lax.broadcasted_iota(jnp.int32, (chunk, 1), 0)
    for h in range(GLA_HEADS):
        ks = slice(h * GLA_HDK, (h + 1) * GLA_HDK)
        vs = slice(h * GLA_HDV, (h + 1) * GLA_HDV)
        bh = b_ref[r, :, ks]
        qh = q_ref[r, :, ks].astype(F32) * (GLA_HDK ** -0.5)
        kf_ref[...] = k_ref[r, :, ks].astype(F32)
        vf_ref[...] = _gla_values(r, vs, v_ref, chunk, n_valid).astype(F32)

        def eight_keys(i, acc, bh=bh, qh=qh, ks=ks):
            rows8 = pl.ds(pl.multiple_of(i * SUBLANES, SUBLANES), SUBLANES)
            b8, k8, v8 = b_ref[r, rows8, ks], kf_ref[rows8, :], vf_ref[rows8, :]
            for j in range(SUBLANES):
                s = i * SUBLANES + j
                decay = jnp.exp(jnp.where(tok >= s, bh - b8[j:j + 1], MASK_VALUE))
                w = jnp.sum(qh * decay * k8[j:j + 1], axis=-1, keepdims=True)
                acc = acc + w * v8[j:j + 1]
            return acc

        intra = lax.fori_loop(0, chunk // SUBLANES, eight_keys, jnp.zeros((chunk, GLA_HDV), F32))
        _gla_epilogue(oi_ref[r, h] + intra, r, vs, r_ref, ng_ref, o_ref)


def _gla_chunk(r, q_ref, k_ref, v_ref, r_ref, b_ref, ng_ref, o_ref, s_ref, oi_ref, chunk, n_valid):
    row = lax.broadcasted_iota(jnp.int32, (chunk, chunk), 0)
    col = lax.broadcasted_iota(jnp.int32, (chunk, chunk), 1)
    causal = row >= col
    b = b_ref[r]

    for h in range(GLA_HEADS):
        ks = slice(h * GLA_HDK, (h + 1) * GLA_HDK)
        vs = slice(h * GLA_HDV, (h + 1) * GLA_HDV)
        bh = b[:, ks]
        qh = q_ref[r, :, ks].astype(F32) * (GLA_HDK ** -0.5)
        kh = k_ref[r, :, ks].astype(F32)
        vh = _gla_values(r, vs, v_ref, chunk, n_valid)
        qt = (qh * jnp.exp(bh)).astype(BF16)
        kt = (kh * jnp.exp(-bh)).astype(BF16)
        kd = (kh * jnp.exp(bh[chunk - 1:chunk, :] - bh)).astype(BF16)
        s_old = s_ref[r, h]
        scores = jnp.where(causal, _dot_nt(qt, kt), 0.0).astype(BF16)
        o_state = _dot(qt, s_old.astype(BF16))
        oi_ref[r, h] = o_state
        dec = jnp.exp(jnp.broadcast_to(bh[chunk - 1:chunk, :], (GLA_HDK, GLA_HDK)).T)
        s_ref[r, h] = s_old * jnp.concatenate([dec, dec], axis=1) + _dot_tn(kd, vh)
        _gla_epilogue(o_state + _dot(scores, vh), r, vs, r_ref, ng_ref, o_ref)


def _gla(z, b_cum, norm_g, s0, chunk, n_valid, rows, sample=None):
    bsz, t, _ = z.shape
    n_chunks = t // chunk
    state = pl.BlockSpec((rows, GLA_HEADS, GLA_HDK, GLA_HDV), lambda b, c: (b, 0, 0, 0))
    in_specs = [pl.BlockSpec((rows, chunk, GLA_DK), lambda b, c: (b, c, 0)),
                pl.BlockSpec((rows, chunk, GLA_DK), lambda b, c: (b, c, 1)),
                pl.BlockSpec((rows, chunk, GLA_DV), lambda b, c: (b, c, 1)),
                pl.BlockSpec((rows, chunk, GLA_DV), lambda b, c: (b, c, 2)),
                pl.BlockSpec((rows, chunk, GLA_DK), lambda b, c: (b, c, 0)),
                pl.BlockSpec((1, GLA_HDV), lambda b, c: (0, 0)),
                state]
    out_specs = [pl.BlockSpec((rows, chunk, GLA_DV), lambda b, c: (b, c, 0)), state]
    out_shape = [jax.ShapeDtypeStruct((bsz, t, GLA_DV), BF16),
                 jax.ShapeDtypeStruct((bsz, GLA_HEADS, GLA_HDK, GLA_HDV), F32)]
    operands = [z, z, z, z, b_cum, norm_g, s0]
    n_new = 0
    scratch = [pltpu.VMEM((rows, GLA_HEADS, GLA_HDK, GLA_HDV), F32),
               pltpu.VMEM((rows, GLA_HEADS, chunk, GLA_HDV), F32),
               pltpu.VMEM((chunk, GLA_HDK), F32),
               pltpu.VMEM((chunk, GLA_HDV), F32)]
    if sample is not None:
        dq, kvn, bufs, n_new = sample
        assert dq.shape[0] == (bsz // rows) * n_chunks
        assert dq.shape[0] >= RING_SLOTS
        per_step = lambda a: pl.BlockSpec((1,) + a.shape[1:], lambda b, c: (b * n_chunks + c, 0, 0))
        in_hbm = pl.BlockSpec(memory_space=pl.ANY)
        initial = [] if s0 is None else [(state, s0)]
        in_specs = ([in_hbm, in_hbm, in_specs[5]] + [spec for spec, _ in initial]
                    + [per_step(a) for a in (dq, kvn)] + [in_hbm for _ in bufs])
        operands = [z, b_cum, norm_g] + [a for _, a in initial] + [dq, kvn, *bufs]
        o_s = jax.ShapeDtypeStruct((dq.shape[0], dq.shape[1], DIL_OUT), BF16)
        out_specs += [per_step(o_s)] + [per_step(a) for a in bufs]
        out_shape += [o_s] + [jax.ShapeDtypeStruct(a.shape, a.dtype) for a in bufs]
        scratch += [pltpu.VMEM((RING_SLOTS,) + a.shape[1:], a.dtype) for a in bufs]
        scratch += [pltpu.VMEM((RING_SLOTS, rows, chunk, GLA_COLS), z.dtype),
                    pltpu.VMEM((RING_SLOTS, rows, chunk, GLA_DK), b_cum.dtype)]
        scratch.append(pltpu.SemaphoreType.DMA((len(bufs) + 2, RING_SLOTS)))
    return pl.pallas_call(
        functools.partial(_gla_kernel, chunk=chunk, n_valid=n_valid, n_new_sample=n_new,
                          n_steps=(bsz // rows) * n_chunks, zero_state=s0 is None),
        grid=(bsz // rows, n_chunks),
        in_specs=in_specs,
        out_specs=out_specs,
        out_shape=out_shape,
        scratch_shapes=scratch,
        compiler_params=pltpu.CompilerParams(dimension_semantics=("arbitrary", "arbitrary"),
                                             vmem_limit_bytes=GLA_VMEM_LIMIT_BYTES),
        name="gla",
    )(*operands)


def _alibi_slope(g, head):
    n = DIL_GROUPS * DIL_HEADS
    return jnp.exp((-ALIBI_MAX * _LN2 / n) * (head + (g * DIL_HEADS + 1.0)))


def _pair_slopes(g, hp, rows):
    head = 2.0 * hp.astype(F32) + jnp.where(rows >= Q_TILE, 1.0, 0.0)
    return _alibi_slope(g, head)


def _dil_prompt_kernel(q0_ref, q1_ref, q2_ref, k0_ref, k1_ref, k2_ref, v0_ref, v1_ref, v2_ref,
                       o_ref, wk0_ref, wv0_ref, wk1_ref, wv1_ref, wk2_ref, wv2_ref,
                       qf_ref, og_ref, ld_ref, bf_ref, br_ref, s_ref, p_ref, m_ref, *, seq):
    hp = pl.program_id(1)
    q_refs = (q0_ref, q1_ref, q2_ref)
    k_refs = (k0_ref, k1_ref, k2_ref)
    v_refs = (v0_ref, v1_ref, v2_ref)
    wk_refs = (wk0_ref, wk1_ref, wk2_ref)
    wv_refs = (wv0_ref, wv1_ref, wv2_ref)

    for g in range(DIL_GROUPS):
        qf_ref[g] = q_refs[g][...].astype(F32) * (DIL_HD ** -0.5)

    for g in range(DIL_GROUPS):
        rate = float(DIL_RATES[g])
        rows = lax.broadcasted_iota(jnp.int32, (2 * Q_TILE, Q_TILE), 0)
        cols = lax.broadcasted_iota(jnp.int32, (2 * Q_TILE, Q_TILE), 1)
        dist = (rows & (Q_TILE - 1)) - cols
        slope = _pair_slopes(g, hp, rows)
        bf_ref[g] = jnp.where(dist >= 0, -slope * rate * dist.astype(F32), MASK_VALUE)
        if g < 2:
            rows = lax.broadcasted_iota(jnp.int32, (2 * Q_TILE, 2 * Q_TILE), 0)
            cols = lax.broadcasted_iota(jnp.int32, (2 * Q_TILE, 2 * Q_TILE), 1)
            dist = (rows & (Q_TILE - 1)) + Q_TILE - cols
            slope = _pair_slopes(g, hp, rows)
            ok = jnp.abs(dist - Q_TILE // 2) <= Q_TILE // 2
            br_ref[g] = jnp.where(ok, -slope * rate * dist.astype(F32), MASK_VALUE)

    lane = lax.broadcasted_iota(jnp.int32, (Q_TILE, LANES), 1)
    lo = lane < DIL_HD
    lo_rows = lax.broadcasted_iota(jnp.int32, (SOFTMAX_ROWS, LANES), 1) < DIL_HD

    def rows_of(start, size, rate):
        return pl.ds(start, size) if rate == 1 else pl.ds(start, size, stride=rate)

    def scores_stage(g, q_row, k_row, n_keys, slot):
        rate = DIL_RATES[g]
        q2 = qf_ref[g, rows_of(q_row, Q_TILE, rate), :]
        k2 = k_refs[g][rows_of(k_row, n_keys, rate), :].astype(BF16)
        qs = jnp.concatenate([jnp.where(lo, q2, 0.0), jnp.where(lo, 0.0, q2)], axis=0).astype(BF16)
        s_ref[slot, :, 0:n_keys] = _dot_nt(qs, k2)

    def softmax_stage(n_keys, bias_ref, gb, slot):
        for c in range(0, Q_TILE, SOFTMAX_ROWS):
            ms = []
            for r0 in (c, Q_TILE + c):
                s = s_ref[slot, r0:r0 + SOFTMAX_ROWS, 0:n_keys] + bias_ref[gb, r0:r0 + SOFTMAX_ROWS, :]
                m = jnp.max(s, axis=-1, keepdims=True)
                p_ref[slot, r0:r0 + SOFTMAX_ROWS, 0:n_keys] = jnp.exp(s - m).astype(BF16)
                ms.append(m)
            m_ref[slot, c:c + SOFTMAX_ROWS, :] = jnp.where(lo_rows, ms[0], ms[1])

    def values_stage(g, q_row, k_row, n_keys, slot):
        rate = DIL_RATES[g]
        v2 = v_refs[g][rows_of(k_row, n_keys, rate), :].astype(BF16)
        vo = jnp.concatenate([v2, jnp.ones((n_keys, LANES), BF16)], axis=1)
        r = _dot(p_ref[slot, :, 0:n_keys], vo)
        o2 = jnp.where(lo, r[:Q_TILE, :LANES], r[Q_TILE:, :LANES])
        l2 = jnp.where(lo, r[:Q_TILE, LANES:], r[Q_TILE:, LANES:])
        og_ref[g, rows_of(q_row, Q_TILE, rate), :] = o2 / l2
        ld_ref[g, rows_of(q_row, Q_TILE, rate), :] = m_ref[slot] + jnp.log(l2)

    tiles = []
    for g in range(DIL_GROUPS):
        rate = DIL_RATES[g]
        span = rate * Q_TILE
        for rho in range(rate):
            tiles.append((g, rho, rho, Q_TILE, ("first", g)))
        for n in range(1, seq // span):
            for rho in range(rate):
                tiles.append((g, rho + n * span, rho + (n - 1) * span, 2 * Q_TILE, ("later", g)))
    sets = [tiles[i:i + TILE_SLOTS] for i in range(0, len(tiles), TILE_SLOTS)]

    def slot_of(set_index, u):
        return (set_index % 2) * TILE_SLOTS + u

    for t in range(len(sets) + 2):
        if 0 <= t - 2 < len(sets):
            for u, (g, q_row, k_row, n_keys, _) in enumerate(sets[t - 2]):
                values_stage(g, q_row, k_row, n_keys, slot_of(t - 2, u))
        if 0 <= t - 1 < len(sets):
            for u, (g, _, _, n_keys, (kind, gb)) in enumerate(sets[t - 1]):
                softmax_stage(n_keys, bf_ref if kind == "first" else br_ref, gb, slot_of(t - 1, u))
        if t < len(sets):
            for u, (g, q_row, k_row, n_keys, _) in enumerate(sets[t]):
                scores_stage(g, q_row, k_row, n_keys, slot_of(t, u))

    for r0 in range(0, seq, Q_TILE):
        rs = slice(r0, r0 + Q_TILE)
        ld = [ld_ref[g, rs, :] for g in range(DIL_GROUPS)]
        top = jnp.maximum(jnp.maximum(ld[0], ld[1]), ld[2])
        w = [jnp.exp(x - top) for x in ld]
        num = sum(w[g] * og_ref[g, rs, :] for g in range(DIL_GROUPS))
        o_ref[rs, :] = (num / (w[0] + w[1] + w[2])).astype(o_ref.dtype)

    for g in range(DIL_GROUPS):
        keep = wk_refs[g].shape[1]
        for src, dst in ((k_refs[g], wk_refs[g]), (v_refs[g], wv_refs[g])):
            for off in range(0, keep, LANES):
                dst[:, off:off + LANES] = src[seq - keep + off:seq - keep + off + LANES, :].T


def _dil_prompt(dq, kv):
    bsz, seq, _ = dq.shape
    pairs = DIL_HEADS * DIL_HD // LANES
    nblk = DIL_WIDTH // LANES
    keeps = [min(w, seq) for w in DIL_WINDOWS]

    def spec(col0):
        return pl.BlockSpec((None, seq, LANES), lambda b, hp, col0=col0: (b, 0, col0 + hp))

    in_specs = ([spec(g * pairs) for g in range(DIL_GROUPS)]
                + [spec(g * pairs) for g in range(DIL_GROUPS)]
                + [spec(nblk + g * pairs) for g in range(DIL_GROUPS)])
    return pl.pallas_call(
        functools.partial(_dil_prompt_kernel, seq=seq),
        grid=(bsz, pairs),
        in_specs=in_specs,
        out_specs=[pl.BlockSpec((None, seq, LANES), lambda b, hp: (b, 0, hp))]
                  + [pl.BlockSpec((None, LANES, w), lambda b, hp: (b, hp, 0)) for w in keeps for _ in range(2)],
        out_shape=[jax.ShapeDtypeStruct((bsz, seq, DIL_OUT), BF16)]
                  + [jax.ShapeDtypeStruct((bsz, DIL_HEADS * DIL_HD, w), F32) for w in keeps for _ in range(2)],
        scratch_shapes=[pltpu.VMEM((DIL_GROUPS, seq, LANES), F32),
                        pltpu.VMEM((DIL_GROUPS, seq, LANES), F32),
                        pltpu.VMEM((DIL_GROUPS, seq, LANES), F32),
                        pltpu.VMEM((DIL_GROUPS, 2 * Q_TILE, Q_TILE), F32),
                        pltpu.VMEM((2, 2 * Q_TILE, 2 * Q_TILE), F32),
                        pltpu.VMEM((2 * TILE_SLOTS, 2 * Q_TILE, 2 * Q_TILE), F32),
                        pltpu.VMEM((2 * TILE_SLOTS, 2 * Q_TILE, 2 * Q_TILE), BF16),
                        pltpu.VMEM((2 * TILE_SLOTS, Q_TILE, LANES), F32)],
        compiler_params=_params("parallel", "arbitrary"),
        name="dil_prompt",
    )(dq, dq, dq, kv, kv, kv, kv, kv, kv)


def _dil_sample_kernel(q_ref, kvn_ref, kb0_ref, vb0_ref, kb1_ref, vb1_ref, kb2_ref, vb2_ref,
                       o_ref, nk0_ref, nv0_ref, nk1_ref, nv1_ref, nk2_ref, nv2_ref, *, n_new):
    kb_refs = (kb0_ref, kb1_ref, kb2_ref)
    vb_refs = (vb0_ref, vb1_ref, vb2_ref)
    nk_refs = (nk0_ref, nk1_ref, nk2_ref)
    nv_refs = (nv0_ref, nv1_ref, nv2_ref)
    pad = q_ref.shape[1]
    nrow = DIL_HEADS * pad
    width = DIL_HEADS * DIL_HD

    lane_head = jnp.right_shift(lax.broadcasted_iota(jnp.int32, (pad, width), 1), DIL_HD.bit_length() - 1)

    def bias_for(g, dist):
        rate = DIL_RATES[g]
        rows = lax.broadcasted_iota(jnp.int32, dist.shape, 0)
        head = sum(jnp.where(rows >= h * pad, 1.0, 0.0) for h in range(1, DIL_HEADS))
        ok = (dist >= 0) & (dist <= DIL_WINDOWS[g]) & ((dist & (rate - 1)) == 0)
        return jnp.where(ok, -_alibi_slope(g, head) * dist.astype(F32), MASK_VALUE)

    sel_r = lax.broadcasted_iota(jnp.int32, (pad, LANES), 0)
    sel_c = lax.broadcasted_iota(jnp.int32, (pad, LANES), 1)
    place = jnp.where((sel_r < n_new) & (sel_c == sel_r + (LANES - n_new)), 1.0, 0.0).astype(BF16)
    tail_lane = lax.broadcasted_iota(jnp.int32, (width, LANES), 1) >= LANES - n_new

    def shifted(buf_t, new_rows):
        length = buf_t.shape[1]
        rolled = pltpu.roll(buf_t, length - n_new, 1)
        new_t = sum(_dot_tn(part, place) for part in _split3(new_rows))
        last = jnp.where(tail_lane, new_t, rolled[:, length - LANES:])
        return rolled, last

    scores, values, transposed = [], [], []
    for g in range(DIL_GROUPS):
        length = kb_refs[g].shape[2]
        cs = slice(g * width, (g + 1) * width)
        vcs = slice(DIL_WIDTH + g * width, DIL_WIDTH + (g + 1) * width)
        qg = q_ref[0, :, cs].astype(F32) * (DIL_HD ** -0.5)
        qs = jnp.concatenate([jnp.where(lane_head == h, qg, 0.0) for h in range(DIL_HEADS)], axis=0).astype(BF16)
        kb, vb = kb_refs[g][0], vb_refs[g][0]
        kn, vn = kvn_ref[0, :, cs], kvn_ref[0, :, vcs]
        rows = lax.broadcasted_iota(jnp.int32, (nrow, length), 0)
        cols = lax.broadcasted_iota(jnp.int32, (nrow, length), 1)
        dist = length + (rows & (pad - 1)) - cols
        scores.append(_dot(qs, kb.astype(BF16)) + bias_for(g, dist))
        values.append(vb.astype(BF16))
        transposed.append(True)
        rows = lax.broadcasted_iota(jnp.int32, (nrow, pad), 0)
        cols = lax.broadcasted_iota(jnp.int32, (nrow, pad), 1)
        dist = jnp.where(cols < n_new, (rows & (pad - 1)) - cols, -1)
        scores.append(_dot_nt(qs, kn.astype(BF16)) + bias_for(g, dist))
        values.append(vn.astype(BF16))
        transposed.append(False)
        for buf, new, out_ref in ((kb, kn, nk_refs[g]), (vb, vn, nv_refs[g])):
            rolled, last = shifted(buf, new)
            if length > LANES:
                out_ref[0, :, 0:length - LANES] = rolled[:, 0:length - LANES]
            out_ref[0, :, length - LANES:length] = last

    top = functools.reduce(jnp.maximum, [jnp.max(s, axis=-1, keepdims=True) for s in scores])
    probs = [jnp.exp(s - top) for s in scores]
    den = sum(jnp.sum(p, axis=-1, keepdims=True) for p in probs)
    acc = sum((_dot_nt if t else _dot)(p.astype(BF16), v)
              for p, v, t in zip(probs, values, transposed)) / den
    out = sum(jnp.where(lane_head == h, acc[h * pad:(h + 1) * pad], 0.0) for h in range(DIL_HEADS))
    o_ref[0] = out.astype(o_ref.dtype)


FFN_COL_CHUNK = D_FF // 2


def _merge_ffn_kernel(x_ref, oa_ref, ob_ref, ga_ref, gb_ref, pa_ref, pb_ref, wo_ref, n2_ref, nf_ref,
                      wg_ref, wu_ref, wd_ref, *refs):
    weights = (pa_ref, pb_ref, wo_ref, n2_ref, nf_ref, wg_ref, wu_ref, wd_ref)
    if len(refs) > 1:
        x2_ref, oa2_ref, ob2_ref, g2_ref, y_ref, y2_ref = refs

        @pl.when(pl.program_id(0) == 0)
        def _():
            _merge_ffn_rows(x2_ref, oa2_ref, ob2_ref, g2_ref.at[:, 0:D_MODEL], g2_ref.at[:, D_MODEL:2 * D_MODEL],
                            *weights, y2_ref)
    else:
        y_ref, = refs
    _merge_ffn_rows(x_ref, oa_ref, ob_ref, ga_ref, gb_ref, *weights, y_ref)


def _merge_ffn_rows(x_ref, oa_ref, ob_ref, ga_ref, gb_ref, pa_ref, pb_ref, wo_ref, n2_ref, nf_ref,
                    wg_ref, wu_ref, wd_ref, y_ref):
    pa = _dot(oa_ref[...], pa_ref[...])
    pb = _dot(ob_ref[...], pb_ref[...])
    merged = jax.nn.sigmoid(ga_ref[...].astype(F32)) * pa + jax.nn.sigmoid(gb_ref[...].astype(F32)) * pb
    x1 = x_ref[...] + _dot(merged.astype(BF16), wo_ref[...])
    h = _rms(x1, n2_ref[...]).astype(BF16)
    acc = x1
    for c0 in range(0, D_FF, FFN_COL_CHUNK):
        cs = slice(c0, c0 + FFN_COL_CHUNK)
        gate = _dot(h, wg_ref[:, cs])
        up = _dot(h, wu_ref[:, cs])
        act = (gate * jax.nn.sigmoid(gate) * up).astype(BF16)
        acc = acc + _dot(act, wd_ref[cs, :])
    y_ref[...] = _rms(acc, nf_ref[...])


def _merge_ffn(x, oa, ob, gates, pa, pb, wo, n2, nf, wg, wu, wd, tm, second=None):
    m = x.shape[0]
    row = lambda w: pl.BlockSpec((tm, w), lambda i: (i, 0))
    resident = lambda a: pl.BlockSpec(a.shape, lambda i: (0, 0), pipeline_mode=pl.Buffered(1))
    in_specs = ([row(D_MODEL), row(GLA_DV), row(DIL_OUT), row(D_MODEL), pl.BlockSpec((tm, D_MODEL), lambda i: (i, 1))]
                + [resident(a) for a in (pa, pb, wo, n2, nf, wg, wu, wd)])
    out_specs = [row(D_MODEL)]
    out_shape = [jax.ShapeDtypeStruct((m, D_MODEL), F32)]
    operands = [x, oa, ob, gates, gates, pa, pb, wo, n2, nf, wg, wu, wd]
    if second is not None:
        in_specs += [resident(a) for a in second]
        operands += list(second)
        out_specs.append(pl.BlockSpec(second[0].shape, lambda i: (0, 0)))
        out_shape.append(jax.ShapeDtypeStruct(second[0].shape, F32))
    return pl.pallas_call(
        _merge_ffn_kernel,
        grid=(m // tm,),
        in_specs=in_specs,
        out_specs=out_specs,
        out_shape=out_shape,
        compiler_params=pltpu.CompilerParams(dimension_semantics=("arbitrary",),
                                             vmem_limit_bytes=MERGE_FFN_VMEM_LIMIT_BYTES),
        name="merge_ffn",
    )(*operands)


def _prep_weights(w_in, gla_gate_w2):
    w2 = jnp.concatenate([gla_gate_w2[0], jnp.zeros((LANES - GLA_GATE_RANK, GLA_DK), F32)], axis=0)
    return dict(w_in_t=jnp.transpose(w_in[0]).astype(BF16), w2=w2.astype(BF16))


def _project(x2d, norm1_g, gate_b, wts, tm, chunk, n_valid, casts=(), second=None):
    groups = [(0, GLA_COLS), (_OFF["dq"][0], DIL_WIDTH), (_OFF["dk"][0], 2 * DIL_WIDTH), (_OFF["ga"][0], 2 * D_MODEL)]
    return _in_proj(x2d, norm1_g, wts["w_in_t"], wts["w2"], gate_b, groups, _OFF["glr"][0],
                    [BF16, BF16, F32, BF16], tm, chunk, n_valid, casts, second)


def _tail(x2d, o_a, o_b, z_gates, wts, norm2_g, norm_f_g, tm, second):
    return _merge_ffn(x2d, o_a, o_b, z_gates, wts["pa"], wts["pb"], wts["wo"], norm2_g,
                      norm_f_g.reshape(1, D_MODEL), wts["wg"], wts["wu"], wts["wd"], tm, second)


def kernel(x_prompt, x_sample, state_gla, state_win0_k, state_win0_v, state_win1_k, state_win1_v,
           state_win2_k, state_win2_v, norm1_g, w_in, gla_gate_w2, gla_gate_b, gla_norm_g,
           proj_a, proj_b, w_out, norm2_g, w_ffn_gate, w_ffn_up, w_ffn_down, norm_f_g):
    wts = _prep_weights(w_in, gla_gate_w2)
    bp, seq, _ = x_prompt.shape
    bs, n_new, _ = x_sample.shape
    width = DIL_HEADS * DIL_HD

    xp = x_prompt.reshape(bp * seq, D_MODEL)
    xs = jnp.pad(x_sample, ((0, 0), (0, SAMPLE_PAD - n_new), (0, 0))).reshape(bs * SAMPLE_PAD, D_MODEL)
    later = dict(pa=proj_a[0], pb=proj_b[0], wo=w_out[0], wg=w_ffn_gate[0], wu=w_ffn_up[0], wd=w_ffn_down[0])
    z_gla, z_dq, z_kv, z_gates, b_cum, *rest = _project(xp, norm1_g, gla_gate_b, wts, ROW_BLOCK, GLA_CHUNK,
                                                        GLA_CHUNK, tuple(later.values()), (xs, SAMPLE_PAD, n_new))
    wts.update(zip(later.keys(), rest[:len(later)]))
    zs_gla, zs_dq, zs_kv, zs_gates, bs_cum = rest[len(later):]

    bufs = [jnp.transpose(a[0], (0, 2, 3, 1)).reshape(bs, width, a.shape[2]) for a in
            (state_win0_k, state_win0_v, state_win1_k, state_win1_v, state_win2_k, state_win2_v)]
    sample = (zs_dq.reshape(bs, SAMPLE_PAD, DIL_WIDTH), zs_kv.reshape(bs, SAMPLE_PAD, 2 * DIL_WIDTH), bufs, n_new)
    o_a, gla_p, os_b, *win_s = _gla(z_gla.reshape(bp, seq, GLA_COLS), b_cum.reshape(bp, seq, GLA_DK), gla_norm_g,
                                    None, GLA_CHUNK, GLA_CHUNK, GLA_ROWS_PROMPT, sample)

    o_b, *win_p = _dil_prompt(z_dq.reshape(bp, seq, DIL_WIDTH), z_kv.reshape(bp, seq, 2 * DIL_WIDTH))
    win_p = [jnp.transpose(a.reshape(bp, DIL_HEADS, DIL_HD, a.shape[2]), (0, 3, 1, 2))[None] for a in win_p]

    os_a, gla_s = _gla(zs_gla.reshape(bs, SAMPLE_PAD, GLA_COLS), bs_cum.reshape(bs, SAMPLE_PAD, GLA_DK),
                       gla_norm_g, state_gla[0], SAMPLE_PAD, n_new, GLA_ROWS_SAMPLE)
    sample_rows = (xs, os_a.reshape(bs * SAMPLE_PAD, GLA_DV), os_b.reshape(bs * SAMPLE_PAD, DIL_OUT), zs_gates)
    y_p, ys = _tail(xp, o_a.reshape(bp * seq, GLA_DV), o_b.reshape(bp * seq, DIL_OUT), z_gates, wts,
                    norm2_g, norm_f_g, ROW_BLOCK, sample_rows)
    y_prompt = y_p.reshape(bp, seq, D_MODEL)
    y_sample = ys.reshape(bs, SAMPLE_PAD, D_MODEL)[:, :n_new]
    win_s = [jnp.transpose(a.reshape(bs, DIL_HEADS, DIL_HD, a.shape[2]), (0, 3, 1, 2))[None] for a in win_s]

    return (y_prompt, y_sample, gla_p[None], *win_p, gla_s[None], *win_s)
```
